```python
import math
import jax, jax.numpy as jnp
from jax import lax
import numpy as np

D_MODEL = 1024
BATCH = 2
SEQ = 8192
DEPTH = 4

CTX_LEN = 256
GRID_W = 64
EPS = 1e-6

BR_WIDTH = D_MODEL // 2
N_BRANCH = 3

DA_SUB = 64
DA_VDIM = 2 * DA_SUB
DA_HEADS = BR_WIDTH // DA_VDIM
DA_QK = DA_HEADS * 2 * DA_SUB
BLOCK_Q = 128
ROPE_BASE = 10000.0
ROPE_NF = DA_SUB // 4

HY_WIDTH = BR_WIDTH
HY_SHORT = 3
HY_BANDS = 16
HY_EMB = 1 + 2 * HY_BANDS
HY_HIDDEN = 64
HY_TARGET = 1e-2
HY_FAST_DECAY = 0.3
HY_SLOW_DECAY = 1.5
HY_MIN_DECAY = math.log(HY_TARGET) / HY_SLOW_DECAY
HY_MAX_DECAY = math.log(HY_TARGET) / HY_FAST_DECAY

GM_WIDTH = BR_WIDTH
GM_GROUPS = 8
GM_CHUNK = 128

COL_K = 0
COL_V = COL_K + DA_QK
COL_Q = COL_V + BR_WIDTH
COL_GA = COL_Q + DA_QK
COL_HY = COL_GA + BR_WIDTH
COL_GB = COL_HY + 3 * HY_WIDTH
COL_GM = COL_GB + BR_WIDTH
COL_GC = COL_GM + 2 * GM_WIDTH
COL_MG = COL_GC + BR_WIDTH
COL_END = COL_MG + N_BRANCH * D_MODEL

kernel_name = "hybrid_diffattn_hyena_gmlp_prefix_dit"


def rms_norm(x, g):
    xf = x.astype(jnp.float32)
    y = xf * lax.rsqrt(jnp.mean(xf * xf, axis=-1, keepdims=True) + EPS)
    return (y * g.astype(jnp.float32)).astype(x.dtype)


def layer_norm(x, g, b):
    xf = x.astype(jnp.float32)
    mu = jnp.mean(xf, axis=-1, keepdims=True)
    var = jnp.mean(jnp.square(xf - mu), axis=-1, keepdims=True)
    y = (xf - mu) * lax.rsqrt(var + EPS)
    return (y * g.astype(jnp.float32) + b.astype(jnp.float32)).astype(x.dtype)


def modulation(cond, w, b):
    m = jax.nn.silu(cond) @ w + b
    return jnp.split(m, 3, axis=-1)


def axial_rope(row, col):
    inv = ROPE_BASE ** (-jnp.arange(ROPE_NF, dtype=jnp.float32) / ROPE_NF)
    ang = jnp.stack([row[:, None] * inv, col[:, None] * inv], axis=1)
    return jnp.cos(ang), jnp.sin(ang)


def apply_rope(x, cos, sin):
    xr = x.reshape(*x.shape[:-1], 2, 2, ROPE_NF)
    x1, x2 = xr[..., 0, :], xr[..., 1, :]
    cs, sn = cos.astype(x.dtype), sin.astype(x.dtype)
    out = jnp.stack([x1 * cs - x2 * sn, x2 * cs + x1 * sn], axis=-2)
    return out.reshape(x.shape)


def qk_heads(zs):
    b, n, _ = zs.shape
    return zs.reshape(b, n, DA_HEADS, 2, DA_SUB).transpose(0, 2, 3, 1, 4)


def v_heads(zs):
    b, n, _ = zs.shape
    return zs.reshape(b, n, DA_HEADS, DA_VDIM).transpose(0, 2, 1, 3)


def diff_attend(q, k, v, lam):
    b, h, _, n, _ = q.shape
    nb = n // BLOCK_Q
    qb = q.reshape(b, h, 2, nb, BLOCK_Q, DA_SUB).transpose(3, 0, 1, 2, 4, 5)
    scale = DA_SUB ** -0.5

    def one(qi):
        s = jnp.einsum('bhmqd,bhmkd->bhmqk', qi, k).astype(jnp.float32) * scale
        p = jax.nn.softmax(s, axis=-1)
        w = p[:, :, 0] - lam * p[:, :, 1]
        return jnp.einsum('bhqk,bhkd->bhqd', w.astype(v.dtype), v)

    o = lax.map(one, qb)
    return o.transpose(1, 2, 0, 3, 4).reshape(b, h, n, DA_VDIM)


def diff_head_out(o, g, lam_init):
    o = rms_norm(o, g) * (1.0 - lam_init)
    b, _, n, _ = o.shape
    return o.transpose(0, 2, 1, 3).reshape(b, n, BR_WIDTH)


def hyena_filter(L, w1, b1, w2, b2, w3, freq):
    f32 = jnp.float32
    t = jnp.linspace(0.0, 1.0, L, dtype=f32)[:, None]
    wpos = (2.0 * math.pi / L) * jnp.arange(L, dtype=f32)[:, None]
    bands = jnp.linspace(1e-4, HY_BANDS - 1, HY_BANDS, dtype=f32)[None, :]
    z = jnp.concatenate([t, jnp.cos(bands * wpos), -jnp.sin(bands * wpos)], axis=-1)
    hid = jnp.sin(freq[0].astype(f32) * (z @ w1.astype(f32) + b1.astype(f32)))
    hid = jnp.sin(freq[1].astype(f32) * (hid @ w2.astype(f32) + b2.astype(f32)))
    h = (hid @ w3.astype(f32)).reshape(L, 2, HY_WIDTH)
    deltas = jnp.abs(jnp.linspace(HY_MIN_DECAY, HY_MAX_DECAY, HY_WIDTH, dtype=f32))
    return h * jnp.exp(-t * deltas)[:, None, :]


def long_conv_bidir(u, h, bias):
    L = u.shape[1]
    f32 = jnp.float32
    k = jnp.concatenate([h[:, 0], jnp.zeros((1, HY_WIDTH), f32), h[:0:-1, 1]], axis=0)
    k = k * lax.rsqrt(jnp.sum(k * k, axis=0, keepdims=True))
    kf = jnp.fft.rfft(k, n=2 * L, axis=0)
    uf = jnp.fft.rfft(u.astype(f32), n=2 * L, axis=1)
    y = jnp.fft.irfft(uf * kf[None], n=2 * L, axis=1)[:, :L]
    return (y + u.astype(f32) * bias.astype(f32)).astype(u.dtype)


def short_conv(z, w, b):
    n = z.shape[1]
    pad = HY_SHORT // 2
    zp = jnp.pad(z, ((0, 0), (pad, pad), (0, 0)))
    return sum(zp[:, j:j + n] * w[j] for j in range(HY_SHORT)) + b


def hyena_branch(zb, sw, sb, filt, bias):
    zb = short_conv(zb, sw, sb)
    x0, x1, v = jnp.split(zb, 3, axis=-1)
    return x0 * long_conv_bidir(x1 * v, filt, bias)


def gmlp_branch(zg, ln_g, ln_b, ws, bs):
    u, v = jnp.split(jax.nn.gelu(zg, approximate=False), 2, axis=-1)
    v = layer_norm(v, ln_g, ln_b)
    b, n, _ = v.shape
    vr = v.reshape(b, n // GM_CHUNK, GM_CHUNK, GM_GROUPS, GM_WIDTH // GM_GROUPS)
    vm = jnp.einsum('gpq,bcqgd->bcpgd', ws, vr) + bs.T[:, :, None]
    return u * vm.reshape(b, n, GM_WIDTH)


def merge_branches(z, ys, wb, wo):
    out = None
    for i, (y, g0) in enumerate(zip(ys, (COL_GA, COL_GB, COL_GC))):
        gated = y * jax.nn.silu(z[..., g0:g0 + BR_WIDTH])
        sel = jax.nn.sigmoid(z[..., COL_MG + i * D_MODEL:COL_MG + (i + 1) * D_MODEL])
        term = sel * (gated @ wb[i])
        out = term if out is None else out + term
    return out @ wo


def setup_inputs(seed: int = 0) -> dict:
    key = jax.random.key(seed)
    ks = jax.random.split(key, 26)
    f32 = jnp.float32
    nrm = lambda k, shape, s: jax.random.normal(k, shape, f32) * s
    D = D_MODEL
    return {
        "x": nrm(ks[0], (BATCH, SEQ, D), 1.0),
        "c": nrm(ks[1], (BATCH, D), 1.0),
        "ctx": nrm(ks[2], (BATCH, CTX_LEN, D), 1.0),
        "c_ctx": nrm(ks[3], (D,), 1.0),
        "ada_w": nrm(ks[4], (DEPTH, D, 3 * D), 0.5 * D ** -0.5),
        "ada_b": nrm(ks[5], (DEPTH, 3 * D), 0.01),
        "norm_pre": 1.0 + nrm(ks[6], (DEPTH, D), 0.05),
        "norm_post": 1.0 + nrm(ks[7], (DEPTH, D), 0.05),
        "w_in": nrm(ks[8], (DEPTH, D, COL_END), D ** -0.5),
        "da_lambda": nrm(ks[9], (DEPTH, 4, DA_SUB), 0.1),
        "da_subln": 1.0 + nrm(ks[10], (DEPTH, DA_VDIM), 0.05),
        "hy_short_w": nrm(ks[11], (DEPTH, HY_SHORT, 3 * HY_WIDTH), HY_SHORT ** -0.5),
        "hy_short_b": nrm(ks[12], (DEPTH, 3 * HY_WIDTH), 0.02),
        "hy_f_w1": nrm(ks[13], (DEPTH, HY_EMB, HY_HIDDEN), HY_EMB ** -0.5),
        "hy_f_b1": nrm(ks[14], (DEPTH, HY_HIDDEN), 0.1),
        "hy_f_w2": nrm(ks[15], (DEPTH, HY_HIDDEN, HY_HIDDEN), HY_HIDDEN ** -0.5),
        "hy_f_b2": nrm(ks[16], (DEPTH, HY_HIDDEN), 0.1),
        "hy_f_w3": nrm(ks[17], (DEPTH, HY_HIDDEN, 2 * HY_WIDTH), HY_HIDDEN ** -0.5),
        "hy_f_freq": 1.0 + nrm(ks[18], (DEPTH, 2, HY_HIDDEN), 0.1),
        "hy_bias": nrm(ks[19], (DEPTH, HY_WIDTH), 0.5),
        "gm_ln_g": 1.0 + nrm(ks[20], (DEPTH, GM_WIDTH), 0.05),
        "gm_ln_b": nrm(ks[21], (DEPTH, GM_WIDTH), 0.02),
        "gm_ws": nrm(ks[22], (DEPTH, GM_GROUPS, GM_CHUNK, GM_CHUNK), 0.5 * GM_CHUNK ** -0.5),
        "gm_bs": 1.0 + nrm(ks[23], (DEPTH, GM_GROUPS, GM_CHUNK), 0.1),
        "w_branch": nrm(ks[24], (DEPTH, N_BRANCH, BR_WIDTH, D), BR_WIDTH ** -0.5),
        "w_out": nrm(ks[25], (DEPTH, D, D), D ** -0.5),
    }


def reference(x, c, ctx, c_ctx, ada_w, ada_b, norm_pre, norm_post, w_in, da_lambda, da_subln,
              hy_short_w, hy_short_b, hy_f_w1, hy_f_b1, hy_f_w2, hy_f_b2, hy_f_w3, hy_f_freq, hy_bias,
              gm_ln_g, gm_ln_b, gm_ws, gm_bs, w_branch, w_out):
    n = x.shape[1]
    n_ctx = ctx.shape[1]
    rows = n // GRID_W
    row = jnp.repeat(jnp.arange(rows, dtype=jnp.float32), GRID_W)
    col = jnp.tile(jnp.arange(GRID_W, dtype=jnp.float32), rows)
    cos, sin = axial_rope(row, col)
    xc = ctx
    for l in range(DEPTH):
        last = l == DEPTH - 1
        lam_init = 0.8 - 0.6 * math.exp(-0.3 * l)
        lp = da_lambda[l].astype(jnp.float32)
        lam = jnp.exp(jnp.sum(lp[0] * lp[1])) - jnp.exp(jnp.sum(lp[2] * lp[3])) + lam_init

        sh, sc, gt = modulation(c, ada_w[l], ada_b[l])
        shc, scc, gtc = modulation(c_ctx, ada_w[l], ada_b[l])
        h = rms_norm(x, norm_pre[l]) * (1.0 + sc[:, None]) + sh[:, None]
        hc = rms_norm(xc, norm_pre[l]) * (1.0 + scc) + shc

        z = h @ w_in[l]
        zc = hc @ (w_in[l][:, :COL_Q] if last else w_in[l])
        kc = qk_heads(zc[..., COL_K:COL_V])
        vc = v_heads(zc[..., COL_V:COL_Q])

        q = apply_rope(qk_heads(z[..., COL_Q:COL_GA]), cos, sin)
        k = apply_rope(qk_heads(z[..., COL_K:COL_V]), cos, sin)
        v = v_heads(z[..., COL_V:COL_Q])
        k_all = jnp.concatenate([k, kc], axis=3)
        v_all = jnp.concatenate([v, vc], axis=2)
        y_a = diff_head_out(diff_attend(q, k_all, v_all, lam), da_subln[l], lam_init)
        filt = hyena_filter(n, hy_f_w1[l], hy_f_b1[l], hy_f_w2[l], hy_f_b2[l], hy_f_w3[l], hy_f_freq[l])
        y_b = hyena_branch(z[..., COL_HY:COL_GB], hy_short_w[l], hy_short_b[l], filt, hy_bias[l])
        y_c = gmlp_branch(z[..., COL_GM:COL_GC], gm_ln_g[l], gm_ln_b[l], gm_ws[l], gm_bs[l])
        out = merge_branches(z, (y_a, y_b, y_c), w_branch[l], w_out[l])
        x_new = x + gt[:, None] * rms_norm(out, norm_post[l])

        if not last:
            qc = qk_heads(zc[..., COL_Q:COL_GA])
            yc_a = diff_head_out(diff_attend(qc, kc, vc, lam), da_subln[l], lam_init)
            filt_c = hyena_filter(n_ctx, hy_f_w1[l], hy_f_b1[l], hy_f_w2[l], hy_f_b2[l], hy_f_w3[l], hy_f_freq[l])
            yc_b = hyena_branch(zc[..., COL_HY:COL_GB], hy_short_w[l], hy_short_b[l], filt_c, hy_bias[l])
            yc_c = gmlp_branch(zc[..., COL_GM:COL_GC], gm_ln_g[l], gm_ln_b[l], gm_ws[l], gm_bs[l])
            outc = merge_branches(zc, (yc_a, yc_b, yc_c), w_branch[l], w_out[l])
            xc = xc + gtc * rms_norm(outc, norm_post[l])
        x = x_new
    return x
```

```python
import functools
import math

import jax
import jax.numpy as jnp
import numpy as np
from jax import lax
from jax.experimental import pallas as pl
from jax.experimental.pallas import tpu as pltpu

F32 = jnp.float32
BF16 = jnp.bfloat16

D_MODEL = 1024
DEPTH = 4
GRID_W = 64
EPS = 1e-6
BR_WIDTH = 512
DA_SUB = 64
DA_VDIM = 128
DA_HEADS = 4
ROPE_BASE = 10000.0
ROPE_NF = 16
HY_WIDTH = 512
HY_BANDS = 16
HY_EMB = 33
HY_HIDDEN = 64
HY_MIN_DECAY = math.log(1e-2) / 1.5
HY_MAX_DECAY = math.log(1e-2) / 0.3
GM_GROUPS = 8
GM_CHUNK = 128

LANES = 128
DFT_R = 128

_REF_COLS = dict(K=(0, 512), V=(512, 512), Q=(1024, 512), GA=(1536, 512), HY=(2048, 1536),
                 GB=(3584, 512), GM=(4096, 1024), GC=(5120, 512), MG=(5632, 3072))
_NEW_ORDER = ("MG", "HY", "K", "GM", "V", "Q", "GA", "GB", "GC")
COL = {}
_off = 0
for _name in _NEW_ORDER:
    COL[_name] = _off
    _off += _REF_COLS[_name][1]
COL_END = _off
for _name in _NEW_ORDER:
    assert COL[_name] % _REF_COLS[_name][1] == 0

VMEM_LIMIT = 48 * 1024 * 1024


def _params(sem):
    return pltpu.CompilerParams(dimension_semantics=sem, vmem_limit_bytes=VMEM_LIMIT)


def _dot(a, b):
    return jnp.dot(a, b, preferred_element_type=F32)


def _dot_hi(a, b):
    return jnp.dot(a, b, preferred_element_type=F32, precision=lax.Precision.HIGHEST)


def _mod_kernel(c_ref, w_ref, b_ref, o_ref):
    cond = c_ref[...]
    s = cond * jax.nn.sigmoid(cond)
    o_ref[0] = _dot_hi(s, w_ref[0]) + b_ref[0]


def modulation_all(cond8, ada_w, ada_b):
    tn = 1024
    return pl.pallas_call(
        _mod_kernel,
        grid=(DEPTH, 3 * D_MODEL // tn),
        in_specs=[pl.BlockSpec((8, D_MODEL), lambda l, j: (0, 0)),
                  pl.BlockSpec((1, D_MODEL, tn), lambda l, j: (l, 0, j)),
                  pl.BlockSpec((1, 1, tn), lambda l, j: (l, 0, j))],
        out_specs=pl.BlockSpec((1, 8, tn), lambda l, j: (l, 0, j)),
        out_shape=jax.ShapeDtypeStruct((DEPTH, 8, 3 * D_MODEL), F32),
        compiler_params=_params(("arbitrary", "arbitrary")),
        name="modulation",
    )(cond8, ada_w, ada_b.reshape(DEPTH, 1, 3 * D_MODEL))


def _inproj_kernel(x_ref, mod_ref, g_ref, w_ref, o_ref, h_ref):
    @pl.when(pl.program_id(2) == 0)
    def _():
        x = x_ref[0]
        y = x * lax.rsqrt(jnp.mean(x * x, axis=-1, keepdims=True) + EPS) * g_ref[...]
        sh = mod_ref[0, :, 0:D_MODEL]
        sc = mod_ref[0, :, D_MODEL:2 * D_MODEL]
        h_ref[...] = (y * (1.0 + sc) + sh).astype(BF16)

    o_ref[0] = _dot(h_ref[...], w_ref[...]).astype(BF16)


def in_projection(x, mod_rows, row_of_batch, g, w):
    bsz, n, _ = x.shape
    tm = min(n, 512)
    tn = COL_END // 4
    return pl.pallas_call(
        _inproj_kernel,
        grid=(bsz, n // tm, COL_END // tn),
        in_specs=[pl.BlockSpec((1, tm, D_MODEL), lambda b, i, j: (b, i, 0)),
                  pl.BlockSpec((1, 1, 3 * D_MODEL), lambda b, i, j: (row_of_batch(b), 0, 0)),
                  pl.BlockSpec((1, D_MODEL), lambda b, i, j: (0, 0)),
                  pl.BlockSpec((D_MODEL, tn), lambda b, i, j: (0, j))],
        out_specs=pl.BlockSpec((1, tm, tn), lambda b, i, j: (b, i, j)),
        out_shape=jax.ShapeDtypeStruct((bsz, n, COL_END), BF16),
        scratch_shapes=[pltpu.VMEM((tm, D_MODEL), BF16)],
        compiler_params=_params(("arbitrary", "arbitrary", "arbitrary")),
        name="in_projection",
    )(x, mod_rows, g.reshape(1, D_MODEL), w)


def _rope_kernel(zq_ref, zk_ref, cos_ref, sin_ref, q_ref, k_ref):
    lane = lax.broadcasted_iota(jnp.int32, cos_ref.shape, 1)
    low = (lane % (2 * ROPE_NF)) < ROPE_NF
    cs = cos_ref[...]
    sn = sin_ref[...]

    def rot(x):
        partner = jnp.where(low, pltpu.roll(x, LANES - ROPE_NF, axis=1), pltpu.roll(x, ROPE_NF, axis=1))
        return (x * cs + partner * sn).astype(BF16)

    q_ref[0] = rot(zq_ref[0].astype(F32))
    k_ref[0] = rot(zk_ref[0].astype(F32))


def rope_tables(n):
    pos = jnp.arange(n)
    row = (pos // GRID_W).astype(F32)
    col = (pos % GRID_W).astype(F32)
    inv = ROPE_BASE ** (-jnp.arange(ROPE_NF, dtype=F32) / ROPE_NF)
    ar = row[:, None] * inv
    ac = col[:, None] * inv
    cos64 = jnp.concatenate([jnp.cos(ar), jnp.cos(ar), jnp.cos(ac), jnp.cos(ac)], axis=1)
    sin64 = jnp.concatenate([-jnp.sin(ar), jnp.sin(ar), -jnp.sin(ac), jnp.sin(ac)], axis=1)
    return jnp.tile(cos64, (1, 2)), jnp.tile(sin64, (1, 2))


def rope_qk(z, cos_t, sin_t):
    bsz, n, _ = z.shape
    tr = 1024
    qb, kb = COL["Q"] // LANES, COL["K"] // LANES
    out = jax.ShapeDtypeStruct((bsz, n, DA_HEADS * LANES), BF16)
    return pl.pallas_call(
        _rope_kernel,
        grid=(bsz, n // tr, DA_HEADS),
        in_specs=[pl.BlockSpec((1, tr, LANES), lambda b, i, h: (b, i, qb + h)),
                  pl.BlockSpec((1, tr, LANES), lambda b, i, h: (b, i, kb + h)),
                  pl.BlockSpec((tr, LANES), lambda b, i, h: (i, 0)),
                  pl.BlockSpec((tr, LANES), lambda b, i, h: (i, 0))],
        out_specs=[pl.BlockSpec((1, tr, LANES), lambda b, i, h: (b, i, h)),
                   pl.BlockSpec((1, tr, LANES), lambda b, i, h: (b, i, h))],
        out_shape=[out, out],
        compiler_params=_params(("arbitrary", "arbitrary", "arbitrary")),
        name="rope_qk",
    )(z, z, cos_t, sin_t)


def _attn_kernel(*refs, lam_init, n_lat, ck):
    if n_lat:
        lam_ref, g_ref, q_ref, kc_ref, vc_ref, k_ref, v_ref, o_ref = refs
    else:
        lam_ref, g_ref, q_ref, kc_ref, vc_ref, o_ref = refs
    tq = q_ref.shape[1]
    lp = lam_ref[...]
    lam = (jnp.exp(jnp.sum(lp[0:1] * lp[1:2], axis=1, keepdims=True))
           - jnp.exp(jnp.sum(lp[2:3] * lp[3:4], axis=1, keepdims=True)) + lam_init)

    q = q_ref[0] * jnp.asarray(DA_SUB ** -0.5, BF16)
    lane = lax.broadcasted_iota(jnp.int32, q.shape, 1)
    zero = jnp.zeros_like(q)
    qq = jnp.concatenate([jnp.where(lane < DA_SUB, q, zero), jnp.where(lane >= DA_SUB, q, zero)], axis=0)

    def step(k, v, carry):
        m, l, acc = carry
        s = lax.dot_general(qq, k, (((1,), (1,)), ((), ())), preferred_element_type=F32)
        m_new = jnp.maximum(m, jnp.max(s, axis=1, keepdims=True))
        alpha = jnp.exp(m - m_new)
        p = jnp.exp(s - m_new)
        l = alpha * l + jnp.sum(p, axis=1, keepdims=True)
        acc = alpha * acc + _dot(p.astype(BF16), v)
        return m_new, l, acc

    carry = (jnp.full((2 * tq, 1), -jnp.inf, F32), jnp.zeros((2 * tq, 1), F32),
             jnp.zeros((2 * tq, DA_VDIM), F32))
    carry = step(kc_ref[0], vc_ref[0], carry)
    if n_lat:
        def body(c, carry):
            start = pl.multiple_of(c * ck, ck)
            return step(k_ref[0, pl.ds(start, ck), :], v_ref[0, pl.ds(start, ck), :], carry)
        carry = lax.fori_loop(0, n_lat // ck, body, carry)
    _, l, acc = carry
    o = acc / l
    d = o[:tq] - lam * o[tq:]
    y = d * lax.rsqrt(jnp.mean(d * d, axis=-1, keepdims=True) + EPS) * g_ref[...]
    o_ref[0] = (y * (1.0 - lam_init)).astype(BF16)


def diff_attention(q_arr, q_col, zc, lat, lam_p, subln, lam_init):
    bsz, nq, _ = q_arr.shape
    n_ctx = zc.shape[1]
    tq = 128
    kcb, vcb = COL["K"] // LANES, COL["V"] // LANES
    in_specs = [pl.BlockSpec((4, DA_SUB), lambda b, h, i: (0, 0)),
                pl.BlockSpec((1, DA_VDIM), lambda b, h, i: (0, 0)),
                pl.BlockSpec((1, tq, LANES), lambda b, h, i: (b, i, q_col + h)),
                pl.BlockSpec((1, n_ctx, LANES), lambda b, h, i: (b, 0, kcb + h)),
                pl.BlockSpec((1, n_ctx, LANES), lambda b, h, i: (b, 0, vcb + h))]
    args = [lam_p, subln.reshape(1, DA_VDIM), q_arr, zc, zc]
    n_lat = 0
    if lat is not None:
        k_arr, k_col, v_arr, v_col = lat
        n_lat = k_arr.shape[1]
        in_specs += [pl.BlockSpec((1, n_lat, LANES), lambda b, h, i: (b, 0, k_col + h)),
                     pl.BlockSpec((1, n_lat, LANES), lambda b, h, i: (b, 0, v_col + h))]
        args += [k_arr, v_arr]
    return pl.pallas_call(
        functools.partial(_attn_kernel, lam_init=lam_init, n_lat=n_lat, ck=1024),
        grid=(bsz, DA_HEADS, nq // tq),
        in_specs=in_specs,
        out_specs=pl.BlockSpec((1, tq, LANES), lambda b, h, i: (b, i, h)),
        out_shape=jax.ShapeDtypeStruct((bsz, nq, BR_WIDTH), BF16),
        compiler_params=_params(("arbitrary", "arbitrary", "arbitrary")),
        name="diff_attention",
    )(*args)


def hyena_positions(L):
    j = jnp.arange(2 * L)
    pos = jnp.where(j < L, j, jnp.where(j == L, 0, 2 * L - j))
    t = jnp.linspace(0.0, 1.0, L, dtype=F32)[pos][:, None]
    wpos = ((2.0 * math.pi / L) * jnp.arange(L, dtype=F32))[pos][:, None]
    bands = jnp.linspace(1e-4, HY_BANDS - 1, HY_BANDS, dtype=F32)[None, :]
    mask = (j != L).astype(F32)[:, None]
    pad = jnp.zeros((2 * L, LANES - HY_EMB - 1), F32)
    return jnp.concatenate([t, jnp.cos(bands * wpos), -jnp.sin(bands * wpos), pad, mask], axis=-1)


def _filter_kernel(feat_ref, w1_ref, b1_ref, w2_ref, b2_ref, w3_ref, fr_ref, dl_ref, k_ref, ss_ref):
    feat = feat_ref[...]
    hid = jnp.sin(fr_ref[0:1, :] * (_dot_hi(feat, w1_ref[...]) + b1_ref[...]))
    hid = jnp.sin(fr_ref[1:2, :] * (_dot_hi(hid, w2_ref[...]) + b2_ref[...]))
    h = _dot_hi(hid, w3_ref[...])
    t = feat[:, 0:1]
    mask = feat[:, LANES - 1:LANES]
    k = h * jnp.exp(-t * dl_ref[...]) * mask
    k_ref[...] = k

    @pl.when(pl.program_id(0) == 0)
    def _():
        ss_ref[...] = jnp.zeros_like(ss_ref)

    ss_ref[...] += jnp.sum(k * k, axis=0, keepdims=True)


def hyena_filter_taps(feat, w1, b1, w2, b2, w3, freq):
    two_l = feat.shape[0]
    tr = min(two_l // 2, 1024)
    nb = two_l // tr
    w1p = jnp.zeros((LANES, HY_HIDDEN), F32).at[:HY_EMB].set(w1)
    deltas = jnp.abs(jnp.linspace(HY_MIN_DECAY, HY_MAX_DECAY, HY_WIDTH, dtype=F32))[None, :]
    const = lambda i: (0, 0)
    return pl.pallas_call(
        _filter_kernel,
        grid=(nb,),
        in_specs=[pl.BlockSpec((tr, LANES), lambda i: (i, 0)),
                  pl.BlockSpec((LANES, HY_HIDDEN), const),
                  pl.BlockSpec((1, HY_HIDDEN), const),
                  pl.BlockSpec((HY_HIDDEN, HY_HIDDEN), const),
                  pl.BlockSpec((1, HY_HIDDEN), const),
                  pl.BlockSpec((HY_HIDDEN, HY_WIDTH), lambda i: (0, (2 * i) // nb)),
                  pl.BlockSpec((2, HY_HIDDEN), const),
                  pl.BlockSpec((1, HY_WIDTH), const)],
        out_specs=[pl.BlockSpec((tr, HY_WIDTH), lambda i: (i, 0)),
                   pl.BlockSpec((1, HY_WIDTH), const)],
        out_shape=[jax.ShapeDtypeStruct((two_l, HY_WIDTH), F32),
                   jax.ShapeDtypeStruct((1, HY_WIDTH), F32)],
        compiler_params=_params(("arbitrary",)),
        name="hyena_filter",
    )(feat, w1p, b1.reshape(1, -1), w2, b2.reshape(1, -1), w3, freq, deltas)


def _short_conv_kernel(z_ref, prev_ref, next_ref, w_ref, b_ref, x0_ref, uv_ref):
    i = pl.program_id(1)
    last = pl.num_programs(1) - 1
    z = z_ref[0].astype(F32)
    tr = z.shape[0]
    halo = prev_ref.shape[1]
    before = jnp.where(i == 0, 0.0, prev_ref[0, halo - 1:halo, :].astype(F32))
    after = jnp.where(i == last, 0.0, next_ref[0, 0:1, :].astype(F32))
    row = lax.broadcasted_iota(jnp.int32, z.shape, 0)
    zm = jnp.where(row == 0, before, pltpu.roll(z, 1, axis=0))
    zp = jnp.where(row == tr - 1, after, pltpu.roll(z, tr - 1, axis=0))
    y = zm * w_ref[0:1, :] + z * w_ref[1:2, :] + zp * w_ref[2:3, :] + b_ref[...]
    x0_ref[0] = y[:, 0:HY_WIDTH].astype(BF16)
    uv_ref[0] = (y[:, HY_WIDTH:2 * HY_WIDTH] * y[:, 2 * HY_WIDTH:]).astype(BF16)


def hyena_short_conv(z, sw, sb):
    bsz, n, _ = z.shape
    tr = min(n, 512)
    halo = 16
    w3c = 3 * HY_WIDTH
    cb = COL["HY"] // w3c
    nh = n // halo
    out = jax.ShapeDtypeStruct((bsz, n, HY_WIDTH), BF16)
    return pl.pallas_call(
        _short_conv_kernel,
        grid=(bsz, n // tr),
        in_specs=[pl.BlockSpec((1, tr, w3c), lambda b, i: (b, i, cb)),
                  pl.BlockSpec((1, halo, w3c), lambda b, i: (b, jnp.maximum(i * (tr // halo) - 1, 0), cb)),
                  pl.BlockSpec((1, halo, w3c), lambda b, i: (b, jnp.minimum((i + 1) * (tr // halo), nh - 1), cb)),
                  pl.BlockSpec((3, w3c), lambda b, i: (0, 0)),
                  pl.BlockSpec((1, w3c), lambda b, i: (0, 0))],
        out_specs=[pl.BlockSpec((1, tr, HY_WIDTH), lambda b, i: (b, i, 0)),
                   pl.BlockSpec((1, tr, HY_WIDTH), lambda b, i: (b, i, 0))],
        out_shape=[out, out],
        compiler_params=_params(("arbitrary", "arbitrary")),
        name="hyena_short_conv",
    )(z, z, z, sw, sb.reshape(1, w3c))


def dft_tables():
    r = DFT_R
    n_fft = r * r
    idx = jnp.arange(r, dtype=jnp.int32)
    ang = (2.0 * math.pi / r) * ((idx[:, None] * idx[None, :]) % r).astype(F32)
    f_re, f_im = jnp.cos(ang), -jnp.sin(ang)
    k1 = idx[:, None, None]
    k2 = idx[None, :, None]
    n2 = idx[None, None, :]
    m = (n2 * (r * k2 + k1)) % n_fft
    ang = (2.0 * math.pi / n_fft) * m.astype(F32)
    g_re, g_im = jnp.cos(ang), -jnp.sin(ang)
    gh_re = jnp.transpose(g_re, (0, 2, 1))
    gh_im = -jnp.transpose(g_im, (0, 2, 1))
    return tuple(a.astype(BF16) for a in (f_re, f_im, g_re, g_im, gh_re, gh_im))


def _fft1_kernel(*refs, n_in):
    if n_in == 2:
        x_ref, fre_ref, fim_ref, are_ref, aim_ref = refs
        x0 = x_ref[0]
        x1 = x_ref[1]
        fre, fim = fre_ref[...], fim_ref[...]
        are_ref[...] = (_dot(fre, x0) - _dot(fim, x1)).astype(BF16)
        aim_ref[...] = (_dot(fre, x1) + _dot(fim, x0)).astype(BF16)
    else:
        x_ref, fre_ref, fim_ref, are_ref, aim_ref = refs
        x = x_ref[...].astype(BF16)
        are_ref[...] = _dot(fre_ref[...], x).astype(BF16)
        aim_ref[...] = _dot(fim_ref[...], x).astype(BF16)


def fft_level1(x, f_re, f_im):
    r = DFT_R
    cols = x.shape[-1]
    tc = 8192
    if x.ndim == 3:
        n_in, kdim = 2, x.shape[1]
        x_spec = pl.BlockSpec((2, kdim, tc), lambda j: (0, 0, j))
    else:
        n_in, kdim = 1, x.shape[0]
        x_spec = pl.BlockSpec((kdim, tc), lambda j: (0, j))
    out = jax.ShapeDtypeStruct((r, cols), BF16)
    return pl.pallas_call(
        functools.partial(_fft1_kernel, n_in=n_in),
        grid=(cols // tc,),
        in_specs=[x_spec,
                  pl.BlockSpec((r, kdim), lambda j: (0, 0)),
                  pl.BlockSpec((r, kdim), lambda j: (0, 0))],
        out_specs=[pl.BlockSpec((r, tc), lambda j: (0, j)), pl.BlockSpec((r, tc), lambda j: (0, j))],
        out_shape=[out, out],
        compiler_params=_params(("arbitrary",)),
        name="fft_level1",
    )(x, f_re[:, :kdim], f_im[:, :kdim])


def _fft2_filter_kernel(are_ref, aim_ref, gre_ref, gim_ref, sc_ref, kre_ref, kim_ref):
    sc = sc_ref[...]
    for j in range(are_ref.shape[0]):
        a_re, a_im = are_ref[j], aim_ref[j]
        g_re, g_im = gre_ref[j], gim_ref[j]
        kre_ref[j] = (_dot(g_re, a_re) - _dot(g_im, a_im)) * sc
        kim_ref[j] = (_dot(g_re, a_im) + _dot(g_im, a_re)) * sc


def fft_level2_filter(a_re, a_im, g_re, g_im, scale):
    r = DFT_R
    c = a_re.shape[-1]
    tk = 8
    blk_a = pl.BlockSpec((tk, r, c), lambda i: (i, 0, 0))
    blk_g = pl.BlockSpec((tk, r, r), lambda i: (i, 0, 0))
    out = jax.ShapeDtypeStruct((r, r, c), F32)
    return pl.pallas_call(
        _fft2_filter_kernel,
        grid=(r // tk,),
        in_specs=[blk_a, blk_a, blk_g, blk_g, pl.BlockSpec((1, c), lambda i: (0, 0))],
        out_specs=[blk_a, blk_a],
        out_shape=[out, out],
        compiler_params=_params(("arbitrary",)),
        name="fft_level2_filter",
    )(a_re, a_im, g_re, g_im, scale)


def _fft2_conv_kernel(are_ref, aim_ref, gre_ref, gim_ref, hre_ref, him_ref, kre_ref, kim_ref, bre_ref, bim_ref):
    for j in range(are_ref.shape[0]):
        a_re, a_im = are_ref[j], aim_ref[j]
        g_re, g_im = gre_ref[j], gim_ref[j]
        x_re = _dot(g_re, a_re) - _dot(g_im, a_im)
        x_im = _dot(g_re, a_im) + _dot(g_im, a_re)
        k_re, k_im = kre_ref[j], kim_ref[j]
        y_re = (x_re * k_re - x_im * k_im).astype(BF16)
        y_im = (x_re * k_im + x_im * k_re).astype(BF16)
        h_re, h_im = hre_ref[j], him_ref[j]
        bre_ref[j] = (_dot(h_re, y_re) - _dot(h_im, y_im)).astype(BF16)
        bim_ref[j] = (_dot(h_re, y_im) + _dot(h_im, y_re)).astype(BF16)


def fft_level2_conv(a_re, a_im, g_re, g_im, gh_re, gh_im, kf_re, kf_im):
    r = DFT_R
    c = a_re.shape[-1]
    tk = 8
    blk_a = pl.BlockSpec((tk, r, c), lambda i: (i, 0, 0))
    blk_g = pl.BlockSpec((tk, r, r), lambda i: (i, 0, 0))
    out = jax.ShapeDtypeStruct((r, r, c), BF16)
    return pl.pallas_call(
        _fft2_conv_kernel,
        grid=(r // tk,),
        in_specs=[blk_a, blk_a, blk_g, blk_g, blk_g, blk_g, blk_a, blk_a],
        out_specs=[blk_a, blk_a],
        out_shape=[out, out],
        compiler_params=_params(("arbitrary",)),
        name="fft_level2_conv",
    )(a_re, a_im, g_re, g_im, gh_re, gh_im, kf_re, kf_im)


def _ifft1_kernel(bre_ref, bim_ref, fre_ref, fim_ref, uv_ref, x0_ref, bias_ref, o_ref):
    fre, fim = fre_ref[...], fim_ref[...]
    b_re, b_im = bre_ref[...], bim_ref[...]
    y0 = _dot(fre, b_re) + _dot(fim, b_im)
    y1 = _dot(fre, b_im) - _dot(fim, b_re)
    bias = bias_ref[...]
    o_ref[0] = ((y0 + uv_ref[0].astype(F32) * bias) * x0_ref[0].astype(F32)).astype(BF16)
    o_ref[1] = ((y1 + uv_ref[1].astype(F32) * bias) * x0_ref[1].astype(F32)).astype(BF16)


def ifft_level1(b_re, b_im, f_re, f_im, uv, x0, bias_t):
    r = DFT_R
    half = r // 2
    cols = b_re.shape[-1]
    tc = 8192
    blk_b = pl.BlockSpec((r, tc), lambda j: (0, j))
    blk_f = pl.BlockSpec((half, r), lambda j: (0, 0))
    blk_x = pl.BlockSpec((2, half, tc), lambda j: (0, 0, j))
    return pl.pallas_call(
        _ifft1_kernel,
        grid=(cols // tc,),
        in_specs=[blk_b, blk_b, blk_f, blk_f, blk_x, blk_x, pl.BlockSpec((1, tc), lambda j: (0, 0))],
        out_specs=blk_x,
        out_shape=jax.ShapeDtypeStruct((2, half, cols), BF16),
        compiler_params=_params(("arbitrary",)),
        name="ifft_level1",
    )(b_re, b_im, f_re[:half], f_im[:half], uv, x0, bias_t)


def hyena_long_conv(uv, x0, taps, sumsq, bias, tables):
    f_re, f_im, g_re, g_im, gh_re, gh_im = tables
    r = DFT_R
    bsz, L, c = uv.shape
    assert bsz == 2 and 2 * L == r * r
    scale = lax.rsqrt(sumsq) * (1.0 / (r * r))
    ka_re, ka_im = fft_level1(taps.reshape(r, r * c), f_re, f_im)
    kf_re, kf_im = fft_level2_filter(ka_re.reshape(r, r, c), ka_im.reshape(r, r, c), g_re, g_im, scale)
    uv2 = uv.reshape(2, r // 2, r * c)
    a_re, a_im = fft_level1(uv2, f_re, f_im)
    b_re, b_im = fft_level2_conv(a_re.reshape(r, r, c), a_im.reshape(r, r, c), g_re, g_im, gh_re, gh_im,
                                 kf_re, kf_im)
    bias_t = jnp.tile(bias.reshape(1, c), (1, 8192 // c))
    y = ifft_level1(b_re.reshape(r, r * c), b_im.reshape(r, r * c), f_re, f_im, uv2,
                    x0.reshape(2, r // 2, r * c), bias_t)
    return y.reshape(2, L, c)


def dft_small_tables(n_fft):
    idx = jnp.arange(n_fft, dtype=jnp.int32)
    ang = (2.0 * math.pi / n_fft) * ((idx[:, None] * idx[None, :]) % n_fft).astype(F32)
    return jnp.cos(ang).astype(BF16), (-jnp.sin(ang)).astype(BF16)


def _conv_small_kernel(taps_ref, ss_ref, fre_ref, fim_ref, uv_ref, x0_ref, bias_ref, o_ref):
    two_l = taps_ref.shape[0]
    L = two_l // 2
    fre, fim = fre_ref[...], fim_ref[...]
    scale = lax.rsqrt(ss_ref[...]) * (1.0 / two_l)
    taps = taps_ref[...].astype(BF16)
    k_re = _dot(fre, taps) * scale
    k_im = _dot(fim, taps) * scale
    u0, u1 = uv_ref[0], uv_ref[1]
    fre_l, fim_l = fre[:, :L], fim[:, :L]
    x_re = _dot(fre_l, u0) - _dot(fim_l, u1)
    x_im = _dot(fre_l, u1) + _dot(fim_l, u0)
    y_re = (x_re * k_re - x_im * k_im).astype(BF16)
    y_im = (x_re * k_im + x_im * k_re).astype(BF16)
    fre_t, fim_t = fre[:L, :], fim[:L, :]
    y0 = _dot(fre_t, y_re) + _dot(fim_t, y_im)
    y1 = _dot(fre_t, y_im) - _dot(fim_t, y_re)
    bias = bias_ref[...]
    o_ref[0] = ((y0 + u0.astype(F32) * bias) * x0_ref[0].astype(F32)).astype(BF16)
    o_ref[1] = ((y1 + u1.astype(F32) * bias) * x0_ref[1].astype(F32)).astype(BF16)


def hyena_long_conv_small(uv, x0, taps, sumsq, bias, tables):
    bsz, L, c = uv.shape
    assert bsz == 2
    f_re, f_im = tables
    full = lambda shape: pl.BlockSpec(shape, lambda i: (0,) * len(shape))
    return pl.pallas_call(
        _conv_small_kernel,
        grid=(1,),
        in_specs=[full((2 * L, c)), full((1, c)), full((2 * L, 2 * L)), full((2 * L, 2 * L)),
                  full((2, L, c)), full((2, L, c)), full((1, c))],
        out_specs=full((2, L, c)),
        out_shape=jax.ShapeDtypeStruct((2, L, c), BF16),
        compiler_params=_params(("arbitrary",)),
        name="hyena_conv_small",
    )(taps, sumsq, f_re, f_im, uv, x0, bias.reshape(1, c))


def _gmlp_kernel(z_ref, g_ref, b_ref, ws_ref, bs_ref, o_ref):
    zg = z_ref[0].astype(F32)
    gl = 0.5 * zg * (1.0 + lax.erf(zg * (2.0 ** -0.5)))
    w = BR_WIDTH
    u = gl[:, :w]
    v = gl[:, w:]
    mu = jnp.mean(v, axis=-1, keepdims=True)
    var = jnp.mean(jnp.square(v - mu), axis=-1, keepdims=True)
    v = ((v - mu) * lax.rsqrt(var + EPS) * g_ref[...] + b_ref[...]).astype(BF16)
    gw = w // GM_GROUPS
    lane = lax.broadcasted_iota(jnp.int32, (GM_CHUNK, LANES), 1)
    first = lane < gw
    for ci in range(zg.shape[0] // GM_CHUNK):
        rows = slice(ci * GM_CHUNK, (ci + 1) * GM_CHUNK)
        tiles = []
        for t in range(w // LANES):
            vt = v[rows, t * LANES:(t + 1) * LANES]
            tiles.append(jnp.where(first, _dot(ws_ref[2 * t], vt), _dot(ws_ref[2 * t + 1], vt)))
        vm = jnp.concatenate(tiles, axis=1) + bs_ref[...]
        o_ref[0, rows, :] = (u[rows] * vm).astype(BF16)


def gmlp(z, ln_g, ln_b, ws, bs):
    bsz, n, _ = z.shape
    tr = min(n, 512)
    w = BR_WIDTH
    cb = COL["GM"] // (2 * w)
    bs_full = jnp.repeat(bs.T, w // GM_GROUPS, axis=1)
    return pl.pallas_call(
        _gmlp_kernel,
        grid=(bsz, n // tr),
        in_specs=[pl.BlockSpec((1, tr, 2 * w), lambda b, i: (b, i, cb)),
                  pl.BlockSpec((1, w), lambda b, i: (0, 0)),
                  pl.BlockSpec((1, w), lambda b, i: (0, 0)),
                  pl.BlockSpec((GM_GROUPS, GM_CHUNK, GM_CHUNK), lambda b, i: (0, 0, 0)),
                  pl.BlockSpec((GM_CHUNK, w), lambda b, i: (0, 0))],
        out_specs=pl.BlockSpec((1, tr, w), lambda b, i: (b, i, 0)),
        out_shape=jax.ShapeDtypeStruct((bsz, n, w), BF16),
        compiler_params=_params(("arbitrary", "arbitrary")),
        name="gmlp",
    )(z, ln_g.reshape(1, w), ln_b.reshape(1, w), ws.astype(BF16), bs_full)


def _merge_kernel(ya_ref, yb_ref, yc_ref, ga_ref, gb_ref, gc_ref, mg_ref, x_ref, mod_ref, np_ref,
                  wb_ref, wo_ref, o_ref):
    acc = None
    for i, (y_ref, g_ref) in enumerate(((ya_ref, ga_ref), (yb_ref, gb_ref), (yc_ref, gc_ref))):
        g = g_ref[0].astype(F32)
        gated = (y_ref[0].astype(F32) * (g * jax.nn.sigmoid(g))).astype(BF16)
        sel = jax.nn.sigmoid(mg_ref[0, :, i * D_MODEL:(i + 1) * D_MODEL].astype(F32))
        term = sel * _dot(gated, wb_ref[i])
        acc = term if acc is None else acc + term
    out = _dot(acc.astype(BF16), wo_ref[...])
    r = out * lax.rsqrt(jnp.mean(out * out, axis=-1, keepdims=True) + EPS) * np_ref[...]
    gt = mod_ref[0, :, 2 * D_MODEL:]
    o_ref[0] = x_ref[0] + gt * r


def merge_out(ya, yb, yc, z, x, mod_rows, row_of_batch, npost, wb, wo):
    bsz, n, _ = x.shape
    tm = min(n, 512)
    w = BR_WIDTH
    yspec = pl.BlockSpec((1, tm, w), lambda b, i: (b, i, 0))

    def zspec(name):
        cb = COL[name] // w
        return pl.BlockSpec((1, tm, w), lambda b, i: (b, i, cb))

    return pl.pallas_call(
        _merge_kernel,
        grid=(bsz, n // tm),
        in_specs=[yspec, yspec, yspec, zspec("GA"), zspec("GB"), zspec("GC"),
                  pl.BlockSpec((1, tm, 3 * D_MODEL), lambda b, i: (b, i, 0)),
                  pl.BlockSpec((1, tm, D_MODEL), lambda b, i: (b, i, 0)),
                  pl.BlockSpec((1, 1, 3 * D_MODEL), lambda b, i: (row_of_batch(b), 0, 0)),
                  pl.BlockSpec((1, D_MODEL), lambda b, i: (0, 0)),
                  pl.BlockSpec((3, w, D_MODEL), lambda b, i: (0, 0, 0)),
                  pl.BlockSpec((D_MODEL, D_MODEL), lambda b, i: (0, 0))],
        out_specs=pl.BlockSpec((1, tm, D_MODEL), lambda b, i: (b, i, 0)),
        out_shape=jax.ShapeDtypeStruct((bsz, n, D_MODEL), F32),
        compiler_params=_params(("arbitrary", "arbitrary")),
        name="merge_out",
    )(ya, yb, yc, z, z, z, z, x, mod_rows, npost.reshape(1, D_MODEL), wb, wo)


def _permute_cols(w):
    return jnp.concatenate([w[:, _REF_COLS[nm][0]:_REF_COLS[nm][0] + _REF_COLS[nm][1]] for nm in _NEW_ORDER],
                           axis=1)


def kernel(x, c, ctx, c_ctx, ada_w, ada_b, norm_pre, norm_post, w_in, da_lambda, da_subln, hy_short_w,
           hy_short_b, hy_f_w1, hy_f_b1, hy_f_w2, hy_f_b2, hy_f_w3, hy_f_freq, hy_bias, gm_ln_g, gm_ln_b,
           gm_ws, gm_bs, w_branch, w_out):
    bsz, n, _ = x.shape
    n_ctx = ctx.shape[1]
    assert bsz == 2 and 2 * n == DFT_R * DFT_R

    cond8 = jnp.zeros((8, D_MODEL), F32).at[0:bsz].set(c).at[bsz].set(c_ctx)
    mod = modulation_all(cond8, ada_w, ada_b)
    lat_row = lambda b: b
    ctx_row = lambda b: bsz

    cos_t, sin_t = rope_tables(n)
    feat = hyena_positions(n)
    feat_c = hyena_positions(n_ctx)
    tables = dft_tables()
    tables_c = dft_small_tables(2 * n_ctx)
    kb, vb, qb = COL["K"] // LANES, COL["V"] // LANES, COL["Q"] // LANES

    xc = ctx
    for l in range(DEPTH):
        last = l == DEPTH - 1
        lam_init = 0.8 - 0.6 * math.exp(-0.3 * l)
        mod_rows = mod[l].reshape(8, 1, 3 * D_MODEL)
        w_l = _permute_cols(w_in[l]).astype(BF16)
        wb_l = w_branch[l].astype(BF16)
        wo_l = w_out[l].astype(BF16)
        filt_w = (hy_f_w1[l], hy_f_b1[l], hy_f_w2[l], hy_f_b2[l], hy_f_w3[l], hy_f_freq[l])

        z = in_projection(x, mod_rows, lat_row, norm_pre[l], w_l)
        zc = in_projection(xc, mod_rows, ctx_row, norm_pre[l], w_l)

        qr, kr = rope_qk(z, cos_t, sin_t)
        y_a = diff_attention(qr, 0, zc, (kr, 0, z, vb), da_lambda[l], da_subln[l], lam_init)
        taps, sumsq = hyena_filter_taps(feat, *filt_w)
        x0, uv = hyena_short_conv(z, hy_short_w[l], hy_short_b[l])
        y_b = hyena_long_conv(uv, x0, taps, sumsq, hy_bias[l], tables)
        y_c = gmlp(z, gm_ln_g[l], gm_ln_b[l], gm_ws[l], gm_bs[l])
        x_new = merge_out(y_a, y_b, y_c, z, x, mod_rows, lat_row, norm_post[l], wb_l, wo_l)

        if not last:
            yc_a = diff_attention(zc, qb, zc, None, da_lambda[l], da_subln[l], lam_init)
            taps_c, sumsq_c = hyena_filter_taps(feat_c, *filt_w)
            x0c, uvc = hyena_short_conv(zc, hy_short_w[l], hy_short_b[l])
            yc_b = hyena_long_conv_small(uvc, x0c, taps_c, sumsq_c, hy_bias[l], tables_c)
            yc_c = gmlp(zc, gm_ln_g[l], gm_ln_b[l], gm_ws[l], gm_bs[l])
            xc = merge_out(yc_a, yc_b, yc_c, zc, xc, mod_rows, ctx_row, norm_post[l], wb_l, wo_l)
        x = x_new
    return x
```

```python
import functools
import math

import jax
import jax.numpy as jnp
import numpy as np
from jax import lax
from jax.experimental import pallas as pl
from jax.experimental.pallas import tpu as pltpu

F32 = jnp.float32
BF16 = jnp.bfloat16

D_MODEL = 1024
DEPTH = 4
GRID_W = 64
EPS = 1e-6
BR_WIDTH = 512
DA_SUB = 64
DA_VDIM = 128
DA_HEADS = 4
ROPE_BASE = 10000.0
ROPE_NF = 16
HY_WIDTH = 512
HY_BANDS = 16
HY_EMB = 33
HY_HIDDEN = 64
HY_MIN_DECAY = math.log(1e-2) / 1.5
HY_MAX_DECAY = math.log(1e-2) / 0.3
GM_GROUPS = 8
GM_CHUNK = 128

LANES = 128
DFT_R = 128
QK_SCALE = DA_SUB ** -0.5 * math.log2(math.e)

_REF_COLS = dict(K=(0, 512), V=(512, 512), Q=(1024, 512), GA=(1536, 512), HY=(2048, 1536),
                 GB=(3584, 512), GM=(4096, 1024), GC=(5120, 512), MG=(5632, 3072))
_NEW_ORDER = ("MG", "HY", "K", "GM", "V", "Q", "GA", "GB", "GC")
COL = {}
_off = 0
for _name in _NEW_ORDER:
    COL[_name] = _off
    _off += _REF_COLS[_name][1]
COL_END = _off
for _name in _NEW_ORDER:
    assert COL[_name] % _REF_COLS[_name][1] == 0

VMEM_LIMIT = 48 * 1024 * 1024


def _params(sem):
    return pltpu.CompilerParams(dimension_semantics=sem, vmem_limit_bytes=VMEM_LIMIT)


def _dot(a, b):
    return jnp.dot(a, b, preferred_element_type=F32)


def _dot_hi(a, b):
    return jnp.dot(a, b, preferred_element_type=F32, precision=lax.Precision.HIGHEST)


def _mod_kernel(c_ref, w_ref, b_ref, o_ref):
    cond = c_ref[...]
    s = cond * jax.nn.sigmoid(cond)
    o_ref[0] = _dot_hi(s, w_ref[0]) + b_ref[0]


def modulation_all(cond8, ada_w, ada_b):
    tn = 1024
    return pl.pallas_call(
        _mod_kernel,
        grid=(DEPTH, 3 * D_MODEL // tn),
        in_specs=[pl.BlockSpec((8, D_MODEL), lambda l, j: (0, 0)),
                  pl.BlockSpec((1, D_MODEL, tn), lambda l, j: (l, 0, j)),
                  pl.BlockSpec((1, 1, tn), lambda l, j: (l, 0, j))],
        out_specs=pl.BlockSpec((1, 8, tn), lambda l, j: (l, 0, j)),
        out_shape=jax.ShapeDtypeStruct((DEPTH, 8, 3 * D_MODEL), F32),
        compiler_params=_params(("arbitrary", "arbitrary")),
        name="modulation",
    )(cond8, ada_w, ada_b.reshape(DEPTH, 1, 3 * D_MODEL))


def _inproj_kernel(x_ref, mod_ref, g_ref, w_ref, o_ref, h_ref):
    @pl.when(pl.program_id(2) == 0)
    def _():
        x = x_ref[0]
        y = x * lax.rsqrt(jnp.mean(x * x, axis=-1, keepdims=True) + EPS) * g_ref[...]
        sh = mod_ref[0, :, 0:D_MODEL]
        sc = mod_ref[0, :, D_MODEL:2 * D_MODEL]
        h_ref[...] = (y * (1.0 + sc) + sh).astype(BF16)

    o_ref[0] = _dot(h_ref[...], w_ref[...]).astype(BF16)


def in_projection(x, mod_rows, row_of_batch, g, w):
    bsz, n, _ = x.shape
    tm = min(n, 512)
    tn = COL_END // 4
    return pl.pallas_call(
        _inproj_kernel,
        grid=(bsz, n // tm, COL_END // tn),
        in_specs=[pl.BlockSpec((1, tm, D_MODEL), lambda b, i, j: (b, i, 0)),
                  pl.BlockSpec((1, 1, 3 * D_MODEL), lambda b, i, j: (row_of_batch(b), 0, 0)),
                  pl.BlockSpec((1, D_MODEL), lambda b, i, j: (0, 0)),
                  pl.BlockSpec((D_MODEL, tn), lambda b, i, j: (0, j))],
        out_specs=pl.BlockSpec((1, tm, tn), lambda b, i, j: (b, i, j)),
        out_shape=jax.ShapeDtypeStruct((bsz, n, COL_END), BF16),
        scratch_shapes=[pltpu.VMEM((tm, D_MODEL), BF16)],
        compiler_params=_params(("arbitrary", "arbitrary", "arbitrary")),
        name="in_projection",
    )(x, mod_rows, g.reshape(1, D_MODEL), w)


def _rope_kernel(zq_ref, zk_ref, cos_ref, sin_ref, q_ref, k_ref):
    lane = lax.broadcasted_iota(jnp.int32, cos_ref.shape, 1)
    low = (lane % (2 * ROPE_NF)) < ROPE_NF
    cs = cos_ref[...]
    sn = sin_ref[...]

    def rot(x):
        partner = jnp.where(low, pltpu.roll(x, LANES - ROPE_NF, axis=1), pltpu.roll(x, ROPE_NF, axis=1))
        return (x * cs + partner * sn).astype(BF16)

    q_ref[0] = rot(zq_ref[0].astype(F32) * QK_SCALE)
    k_ref[0] = rot(zk_ref[0].astype(F32))


def rope_tables(n):
    pos = jnp.arange(n)
    row = (pos // GRID_W).astype(F32)
    col = (pos % GRID_W).astype(F32)
    inv = ROPE_BASE ** (-jnp.arange(ROPE_NF, dtype=F32) / ROPE_NF)
    ar = row[:, None] * inv
    ac = col[:, None] * inv
    cos64 = jnp.concatenate([jnp.cos(ar), jnp.cos(ar), jnp.cos(ac), jnp.cos(ac)], axis=1)
    sin64 = jnp.concatenate([-jnp.sin(ar), jnp.sin(ar), -jnp.sin(ac), jnp.sin(ac)], axis=1)
    return jnp.tile(cos64, (1, 2)), jnp.tile(sin64, (1, 2))


def rope_qk(z, cos_t, sin_t):
    bsz, n, _ = z.shape
    tr = 1024
    qb, kb = COL["Q"] // LANES, COL["K"] // LANES
    out = jax.ShapeDtypeStruct((bsz, n, DA_HEADS * LANES), BF16)
    return pl.pallas_call(
        _rope_kernel,
        grid=(bsz, n // tr, DA_HEADS),
        in_specs=[pl.BlockSpec((1, tr, LANES), lambda b, i, h: (b, i, qb + h)),
                  pl.BlockSpec((1, tr, LANES), lambda b, i, h: (b, i, kb + h)),
                  pl.BlockSpec((tr, LANES), lambda b, i, h: (i, 0)),
                  pl.BlockSpec((tr, LANES), lambda b, i, h: (i, 0))],
        out_specs=[pl.BlockSpec((1, tr, LANES), lambda b, i, h: (b, i, h)),
                   pl.BlockSpec((1, tr, LANES), lambda b, i, h: (b, i, h))],
        out_shape=[out, out],
        compiler_params=_params(("arbitrary", "arbitrary", "arbitrary")),
        name="rope_qk",
    )(z, z, cos_t, sin_t)


def _attn_kernel(*refs, lam_init, n_lat, ck, prescaled):
    if n_lat:
        lam_ref, g_ref, q_ref, kc_ref, vc_ref, k_ref, v_ref, o_ref, vcx_ref, vx_ref = refs
    else:
        lam_ref, g_ref, q_ref, kc_ref, vc_ref, o_ref, vcx_ref = refs
    tq = q_ref.shape[1]
    n_ctx = kc_ref.shape[1]

    @pl.when(pl.program_id(2) == 0)
    def _():
        def ones_col(rows):
            return (lax.broadcasted_iota(jnp.int32, (rows, LANES), 1) == 0).astype(BF16)
        vcx_ref[:, 0:DA_VDIM] = vc_ref[0]
        vcx_ref[:, DA_VDIM:] = ones_col(n_ctx)
        if n_lat:
            vx_ref[:, 0:DA_VDIM] = v_ref[0]
            vx_ref[:, DA_VDIM:] = ones_col(n_lat)

    lp = lam_ref[...]
    lam = (jnp.exp(jnp.sum(lp[0:1] * lp[1:2], axis=1, keepdims=True))
           - jnp.exp(jnp.sum(lp[2:3] * lp[3:4], axis=1, keepdims=True)) + lam_init)

    q = q_ref[0]
    if not prescaled:
        q = (q.astype(F32) * QK_SCALE).astype(BF16)
    lane = lax.broadcasted_iota(jnp.int32, q.shape, 1)
    zero = jnp.zeros_like(q)
    qq = jnp.concatenate([jnp.where(lane < DA_SUB, q, zero), jnp.where(lane >= DA_SUB, q, zero)], axis=0)

    chunks = [(lambda: kc_ref[0], lambda: vcx_ref[...])]
    for c in range(n_lat // ck):
        chunks.append((lambda c=c: k_ref[0, c * ck:(c + 1) * ck, :], lambda c=c: vx_ref[c * ck:(c + 1) * ck, :]))

    def scores(c):
        return lax.dot_general(qq, chunks[c][0](), (((1,), (1,)), ((), ())), preferred_element_type=F32)

    m = jnp.full((2 * tq, 1), -jnp.inf, F32)
    acc = jnp.zeros((2 * tq, 2 * DA_VDIM), F32)
    pending = None
    s_next = scores(0)
    for c in range(len(chunks)):
        s = s_next
        if c + 1 < len(chunks):
            s_next = scores(c + 1)
        m_new = jnp.maximum(m, jnp.max(s, axis=1, keepdims=True))
        alpha = jnp.exp2(m - m_new)
        p = jnp.exp2(s - m_new).astype(BF16)
        m = m_new
        if pending is not None:
            p_prev, alpha_prev, c_prev = pending
            acc = acc * alpha_prev + _dot(p_prev, chunks[c_prev][1]())
        pending = (p, alpha, c)
    p_prev, alpha_prev, c_prev = pending
    acc = acc * alpha_prev + _dot(p_prev, chunks[c_prev][1]())

    o = acc[:, 0:DA_VDIM] / acc[:, DA_VDIM:DA_VDIM + 1]
    d = o[:tq] - lam * o[tq:]
    y = d * lax.rsqrt(jnp.mean(d * d, axis=-1, keepdims=True) + EPS) * g_ref[...]
    o_ref[0] = (y * (1.0 - lam_init)).astype(BF16)


def diff_attention(q_arr, q_col, zc, lat, lam_p, subln, lam_init, prescaled):
    bsz, nq, _ = q_arr.shape
    n_ctx = zc.shape[1]
    tq = 128
    kcb, vcb = COL["K"] // LANES, COL["V"] // LANES
    in_specs = [pl.BlockSpec((4, DA_SUB), lambda b, h, i: (0, 0)),
                pl.BlockSpec((1, DA_VDIM), lambda b, h, i: (0, 0)),
                pl.BlockSpec((1, tq, LANES), lambda b, h, i: (b, i, q_col + h)),
                pl.BlockSpec((1, n_ctx, LANES), lambda b, h, i: (b, 0, kcb + h)),
                pl.BlockSpec((1, n_ctx, LANES), lambda b, h, i: (b, 0, vcb + h))]
    args = [lam_p, subln.reshape(1, DA_VDIM), q_arr, zc, zc]
    scratch = [pltpu.VMEM((n_ctx, 2 * DA_VDIM), BF16)]
    n_lat = 0
    if lat is not None:
        k_arr, k_col, v_arr, v_col = lat
        n_lat = k_arr.shape[1]
        in_specs += [pl.BlockSpec((1, n_lat, LANES), lambda b, h, i: (b, 0, k_col + h)),
                     pl.BlockSpec((1, n_lat, LANES), lambda b, h, i: (b, 0, v_col + h))]
        args += [k_arr, v_arr]
        scratch.append(pltpu.VMEM((n_lat, 2 * DA_VDIM), BF16))
    return pl.pallas_call(
        functools.partial(_attn_kernel, lam_init=lam_init, n_lat=n_lat, ck=1024, prescaled=prescaled),
        grid=(bsz, DA_HEADS, nq // tq),
        in_specs=in_specs,
        out_specs=pl.BlockSpec((1, tq, LANES), lambda b, h, i: (b, i, h)),
        out_shape=jax.ShapeDtypeStruct((bsz, nq, BR_WIDTH), BF16),
        scratch_shapes=scratch,
        compiler_params=_params(("arbitrary", "arbitrary", "arbitrary")),
        name="diff_attention",
    )(*args)


def hyena_positions(L):
    j = jnp.arange(2 * L)
    pos = jnp.where(j < L, j, jnp.where(j == L, 0, 2 * L - j))
    t = jnp.linspace(0.0, 1.0, L, dtype=F32)[pos][:, None]
    wpos = ((2.0 * math.pi / L) * jnp.arange(L, dtype=F32))[pos][:, None]
    bands = jnp.linspace(1e-4, HY_BANDS - 1, HY_BANDS, dtype=F32)[None, :]
    mask = (j != L).astype(F32)[:, None]
    pad = jnp.zeros((2 * L, LANES - HY_EMB - 1), F32)
    return jnp.concatenate([t, jnp.cos(bands * wpos), -jnp.sin(bands * wpos), pad, mask], axis=-1)


def _filter_kernel(feat_ref, w1_ref, b1_ref, w2_ref, b2_ref, w3_ref, fr_ref, dl_ref, k_ref, ss_ref):
    feat = feat_ref[...]
    hid = jnp.sin(fr_ref[0:1, :] * (_dot_hi(feat, w1_ref[...]) + b1_ref[...]))
    hid = jnp.sin(fr_ref[1:2, :] * (_dot_hi(hid, w2_ref[...]) + b2_ref[...]))
    h = _dot_hi(hid, w3_ref[...])
    t = feat[:, 0:1]
    mask = feat[:, LANES - 1:LANES]
    k = h * jnp.exp(-t * dl_ref[...]) * mask
    k_ref[...] = k

    @pl.when(pl.program_id(0) == 0)
    def _():
        ss_ref[...] = jnp.zeros_like(ss_ref)

    ss_ref[...] += jnp.sum(k * k, axis=0, keepdims=True)


def hyena_filter_taps(feat, w1, b1, w2, b2, w3, freq):
    two_l = feat.shape[0]
    tr = min(two_l // 2, 1024)
    nb = two_l // tr
    w1p = jnp.zeros((LANES, HY_HIDDEN), F32).at[:HY_EMB].set(w1)
    deltas = jnp.abs(jnp.linspace(HY_MIN_DECAY, HY_MAX_DECAY, HY_WIDTH, dtype=F32))[None, :]
    const = lambda i: (0, 0)
    return pl.pallas_call(
        _filter_kernel,
        grid=(nb,),
        in_specs=[pl.BlockSpec((tr, LANES), lambda i: (i, 0)),
                  pl.BlockSpec((LANES, HY_HIDDEN), const),
                  pl.BlockSpec((1, HY_HIDDEN), const),
                  pl.BlockSpec((HY_HIDDEN, HY_HIDDEN), const),
                  pl.BlockSpec((1, HY_HIDDEN), const),
                  pl.BlockSpec((HY_HIDDEN, HY_WIDTH), lambda i: (0, (2 * i) // nb)),
                  pl.BlockSpec((2, HY_HIDDEN), const),
                  pl.BlockSpec((1, HY_WIDTH), const)],
        out_specs=[pl.BlockSpec((tr, HY_WIDTH), lambda i: (i, 0)),
                   pl.BlockSpec((1, HY_WIDTH), const)],
        out_shape=[jax.ShapeDtypeStruct((two_l, HY_WIDTH), F32),
                   jax.ShapeDtypeStruct((1, HY_WIDTH), F32)],
        compiler_params=_params(("arbitrary",)),
        name="hyena_filter",
    )(feat, w1p, b1.reshape(1, -1), w2, b2.reshape(1, -1), w3, freq, deltas)


def _short_conv_kernel(z_ref, prev_ref, next_ref, w_ref, b_ref, x0_ref, uv_ref):
    i = pl.program_id(1)
    last = pl.num_programs(1) - 1
    z = z_ref[0].astype(F32)
    tr = z.shape[0]
    halo = prev_ref.shape[1]
    before = jnp.where(i == 0, 0.0, prev_ref[0, halo - 1:halo, :].astype(F32))
    after = jnp.where(i == last, 0.0, next_ref[0, 0:1, :].astype(F32))
    row = lax.broadcasted_iota(jnp.int32, z.shape, 0)
    zm = jnp.where(row == 0, before, pltpu.roll(z, 1, axis=0))
    zp = jnp.where(row == tr - 1, after, pltpu.roll(z, tr - 1, axis=0))
    y = zm * w_ref[0:1, :] + z * w_ref[1:2, :] + zp * w_ref[2:3, :] + b_ref[...]
    x0_ref[0] = y[:, 0:HY_WIDTH].astype(BF16)
    uv_ref[0] = (y[:, HY_WIDTH:2 * HY_WIDTH] * y[:, 2 * HY_WIDTH:]).astype(BF16)


def hyena_short_conv(z, sw, sb):
    bsz, n, _ = z.shape
    tr = min(n, 512)
    halo = 16
    w3c = 3 * HY_WIDTH
    cb = COL["HY"] // w3c
    nh = n // halo
    out = jax.ShapeDtypeStruct((bsz, n, HY_WIDTH), BF16)
    return pl.pallas_call(
        _short_conv_kernel,
        grid=(bsz, n // tr),
        in_specs=[pl.BlockSpec((1, tr, w3c), lambda b, i: (b, i, cb)),
                  pl.BlockSpec((1, halo, w3c), lambda b, i: (b, jnp.maximum(i * (tr // halo) - 1, 0), cb)),
                  pl.BlockSpec((1, halo, w3c), lambda b, i: (b, jnp.minimum((i + 1) * (tr // halo), nh - 1), cb)),
                  pl.BlockSpec((3, w3c), lambda b, i: (0, 0)),
                  pl.BlockSpec((1, w3c), lambda b, i: (0, 0))],
        out_specs=[pl.BlockSpec((1, tr, HY_WIDTH), lambda b, i: (b, i, 0)),
                   pl.BlockSpec((1, tr, HY_WIDTH), lambda b, i: (b, i, 0))],
        out_shape=[out, out],
        compiler_params=_params(("arbitrary", "arbitrary")),
        name="hyena_short_conv",
    )(z, z, z, sw, sb.reshape(1, w3c))


def dft_tables():
    r = DFT_R
    n_fft = r * r
    idx = jnp.arange(r, dtype=jnp.int32)
    ang = (2.0 * math.pi / r) * ((idx[:, None] * idx[None, :]) % r).astype(F32)
    f_re, f_im = jnp.cos(ang), -jnp.sin(ang)
    k1 = idx[:, None, None]
    k2 = idx[None, :, None]
    n2 = idx[None, None, :]
    m = (n2 * (r * k2 + k1)) % n_fft
    ang = (2.0 * math.pi / n_fft) * m.astype(F32)
    g_re, g_im = jnp.cos(ang), -jnp.sin(ang)
    gh_re = jnp.transpose(g_re, (0, 2, 1))
    gh_im = -jnp.transpose(g_im, (0, 2, 1))
    return tuple(a.astype(BF16) for a in (f_re, f_im, g_re, g_im, gh_re, gh_im))


def _fft1_kernel(*refs, n_in):
    if n_in == 2:
        x_ref, fre_ref, fim_ref, are_ref, aim_ref = refs
        x0 = x_ref[0]
        x1 = x_ref[1]
        fre, fim = fre_ref[...], fim_ref[...]
        are_ref[...] = (_dot(fre, x0) - _dot(fim, x1)).astype(BF16)
        aim_ref[...] = (_dot(fre, x1) + _dot(fim, x0)).astype(BF16)
    else:
        x_ref, fre_ref, fim_ref, are_ref, aim_ref = refs
        x = x_ref[...].astype(BF16)
        are_ref[...] = _dot(fre_ref[...], x).astype(BF16)
        aim_ref[...] = _dot(fim_ref[...], x).astype(BF16)


def fft_level1(x, f_re, f_im):
    r = DFT_R
    cols = x.shape[-1]
    tc = 8192
    if x.ndim == 3:
        n_in, kdim = 2, x.shape[1]
        x_spec = pl.BlockSpec((2, kdim, tc), lambda j: (0, 0, j))
    else:
        n_in, kdim = 1, x.shape[0]
        x_spec = pl.BlockSpec((kdim, tc), lambda j: (0, j))
    out = jax.ShapeDtypeStruct((r, cols), BF16)
    return pl.pallas_call(
        functools.partial(_fft1_kernel, n_in=n_in),
        grid=(cols // tc,),
        in_specs=[x_spec,
                  pl.BlockSpec((r, kdim), lambda j: (0, 0)),
                  pl.BlockSpec((r, kdim), lambda j: (0, 0))],
        out_specs=[pl.BlockSpec((r, tc), lambda j: (0, j)), pl.BlockSpec((r, tc), lambda j: (0, j))],
        out_shape=[out, out],
        compiler_params=_params(("arbitrary",)),
        name="fft_level1",
    )(x, f_re[:, :kdim], f_im[:, :kdim])


def _fft2_filter_kernel(are_ref, aim_ref, gre_ref, gim_ref, sc_ref, kre_ref, kim_ref):
    sc = sc_ref[...]
    for j in range(are_ref.shape[0]):
        a_re, a_im = are_ref[j], aim_ref[j]
        g_re, g_im = gre_ref[j], gim_ref[j]
        kre_ref[j] = (_dot(g_re, a_re) - _dot(g_im, a_im)) * sc
        kim_ref[j] = (_dot(g_re, a_im) + _dot(g_im, a_re)) * sc


def fft_level2_filter(a_re, a_im, g_re, g_im, scale):
    r = DFT_R
    c = a_re.shape[-1]
    tk = 8
    blk_a = pl.BlockSpec((tk, r, c), lambda i: (i, 0, 0))
    blk_g = pl.BlockSpec((tk, r, r), lambda i: (i, 0, 0))
    out = jax.ShapeDtypeStruct((r, r, c), F32)
    return pl.pallas_call(
        _fft2_filter_kernel,
        grid=(r // tk,),
        in_specs=[blk_a, blk_a, blk_g, blk_g, pl.BlockSpec((1, c), lambda i: (0, 0))],
        out_specs=[blk_a, blk_a],
        out_shape=[out, out],
        compiler_params=_params(("arbitrary",)),
        name="fft_level2_filter",
    )(a_re, a_im, g_re, g_im, scale)


def _fft2_conv_kernel(are_ref, aim_ref, gre_ref, gim_ref, hre_ref, him_ref, kre_ref, kim_ref, bre_ref, bim_ref):
    for j in range(are_ref.shape[0]):
        a_re, a_im = are_ref[j], aim_ref[j]
        g_re, g_im = gre_ref[j], gim_ref[j]
        x_re = _dot(g_re, a_re) - _dot(g_im, a_im)
        x_im = _dot(g_re, a_im) + _dot(g_im, a_re)
        k_re, k_im = kre_ref[j], kim_ref[j]
        y_re = (x_re * k_re - x_im * k_im).astype(BF16)
        y_im = (x_re * k_im + x_im * k_re).astype(BF16)
        h_re, h_im = hre_ref[j], him_ref[j]
        bre_ref[j] = (_dot(h_re, y_re) - _dot(h_im, y_im)).astype(BF16)
        bim_ref[j] = (_dot(h_re, y_im) + _dot(h_im, y_re)).astype(BF16)


def fft_level2_conv(a_re, a_im, g_re, g_im, gh_re, gh_im, kf_re, kf_im):
    r = DFT_R
    c = a_re.shape[-1]
    tk = 8
    blk_a = pl.BlockSpec((tk, r, c), lambda i: (i, 0, 0))
    blk_g = pl.BlockSpec((tk, r, r), lambda i: (i, 0, 0))
    out = jax.ShapeDtypeStruct((r, r, c), BF16)
    return pl.pallas_call(
        _fft2_conv_kernel,
        grid=(r // tk,),
        in_specs=[blk_a, blk_a, blk_g, blk_g, blk_g, blk_g, blk_a, blk_a],
        out_specs=[blk_a, blk_a],
        out_shape=[out, out],
        compiler_params=_params(("arbitrary",)),
        name="fft_level2_conv",
    )(a_re, a_im, g_re, g_im, gh_re, gh_im, kf_re, kf_im)


def _ifft1_kernel(bre_ref, bim_ref, fre_ref, fim_ref, uv_ref, x0_ref, bias_ref, o_ref):
    fre, fim = fre_ref[...], fim_ref[...]
    b_re, b_im = bre_ref[...], bim_ref[...]
    y0 = _dot(fre, b_re) + _dot(fim, b_im)
    y1 = _dot(fre, b_im) - _dot(fim, b_re)
    bias = bias_ref[...]
    o_ref[0] = ((y0 + uv_ref[0].astype(F32) * bias) * x0_ref[0].astype(F32)).astype(BF16)
    o_ref[1] = ((y1 + uv_ref[1].astype(F32) * bias) * x0_ref[1].astype(F32)).astype(BF16)


def ifft_level1(b_re, b_im, f_re, f_im, uv, x0, bias_t):
    r = DFT_R
    half = r // 2
    cols = b_re.shape[-1]
    tc = 8192
    blk_b = pl.BlockSpec((r, tc), lambda j: (0, j))
    blk_f = pl.BlockSpec((half, r), lambda j: (0, 0))
    blk_x = pl.BlockSpec((2, half, tc), lambda j: (0, 0, j))
    return pl.pallas_call(
        _ifft1_kernel,
        grid=(cols // tc,),
        in_specs=[blk_b, blk_b, blk_f, blk_f, blk_x, blk_x, pl.BlockSpec((1, tc), lambda j: (0, 0))],
        out_specs=blk_x,
        out_shape=jax.ShapeDtypeStruct((2, half, cols), BF16),
        compiler_params=_params(("arbitrary",)),
        name="ifft_level1",
    )(b_re, b_im, f_re[:half], f_im[:half], uv, x0, bias_t)


def hyena_long_conv(uv, x0, taps, sumsq, bias, tables):
    f_re, f_im, g_re, g_im, gh_re, gh_im = tables
    r = DFT_R
    bsz, L, c = uv.shape
    assert bsz == 2 and 2 * L == r * r
    scale = lax.rsqrt(sumsq) * (1.0 / (r * r))
    ka_re, ka_im = fft_level1(taps.reshape(r, r * c), f_re, f_im)
    kf_re, kf_im = fft_level2_filter(ka_re.reshape(r, r, c), ka_im.reshape(r, r, c), g_re, g_im, scale)
    uv2 = uv.reshape(2, r // 2, r * c)
    a_re, a_im = fft_level1(uv2, f_re, f_im)
    b_re, b_im = fft_level2_conv(a_re.reshape(r, r, c), a_im.reshape(r, r, c), g_re, g_im, gh_re, gh_im,
                                 kf_re, kf_im)
    bias_t = jnp.tile(bias.reshape(1, c), (1, 8192 // c))
    y = ifft_level1(b_re.reshape(r, r * c), b_im.reshape(r, r * c), f_re, f_im, uv2,
                    x0.reshape(2, r // 2, r * c), bias_t)
    return y.reshape(2, L, c)


def dft_small_tables(n_fft):
    idx = jnp.arange(n_fft, dtype=jnp.int32)
    ang = (2.0 * math.pi / n_fft) * ((idx[:, None] * idx[None, :]) % n_fft).astype(F32)
    return jnp.cos(ang).astype(BF16), (-jnp.sin(ang)).astype(BF16)


def _conv_small_kernel(taps_ref, ss_ref, fre_ref, fim_ref, uv_ref, x0_ref, bias_ref, o_ref):
    two_l = taps_ref.shape[0]
    L = two_l // 2
    fre, fim = fre_ref[...], fim_ref[...]
    scale = lax.rsqrt(ss_ref[...]) * (1.0 / two_l)
    taps = taps_ref[...].astype(BF16)
    k_re = _dot(fre, taps) * scale
    k_im = _dot(fim, taps) * scale
    u0, u1 = uv_ref[0], uv_ref[1]
    fre_l, fim_l = fre[:, :L], fim[:, :L]
    x_re = _dot(fre_l, u0) - _dot(fim_l, u1)
    x_im = _dot(fre_l, u1) + _dot(fim_l, u0)
    y_re = (x_re * k_re - x_im * k_im).astype(BF16)
    y_im = (x_re * k_im + x_im * k_re).astype(BF16)
    fre_t, fim_t = fre[:L, :], fim[:L, :]
    y0 = _dot(fre_t, y_re) + _dot(fim_t, y_im)
    y1 = _dot(fre_t, y_im) - _dot(fim_t, y_re)
    bias = bias_ref[...]
    o_ref[0] = ((y0 + u0.astype(F32) * bias) * x0_ref[0].astype(F32)).astype(BF16)
    o_ref[1] = ((y1 + u1.astype(F32) * bias) * x0_ref[1].astype(F32)).astype(BF16)


def hyena_long_conv_small(uv, x0, taps, sumsq, bias, tables):
    bsz, L, c = uv.shape
    assert bsz == 2
    f_re, f_im = tables
    full = lambda shape: pl.BlockSpec(shape, lambda i: (0,) * len(shape))
    return pl.pallas_call(
        _conv_small_kernel,
        grid=(1,),
        in_specs=[full((2 * L, c)), full((1, c)), full((2 * L, 2 * L)), full((2 * L, 2 * L)),
                  full((2, L, c)), full((2, L, c)), full((1, c))],
        out_specs=full((2, L, c)),
        out_shape=jax.ShapeDtypeStruct((2, L, c), BF16),
        compiler_params=_params(("arbitrary",)),
        name="hyena_conv_small",
    )(taps, sumsq, f_re, f_im, uv, x0, bias.reshape(1, c))


def _gmlp_kernel(z_ref, g_ref, b_ref, ws_ref, bs_ref, o_ref):
    zg = z_ref[0].astype(F32)
    gl = 0.5 * zg * (1.0 + lax.erf(zg * (2.0 ** -0.5)))
    w = BR_WIDTH
    u = gl[:, :w]
    v = gl[:, w:]
    mu = jnp.mean(v, axis=-1, keepdims=True)
    var = jnp.mean(jnp.square(v - mu), axis=-1, keepdims=True)
    v = ((v - mu) * lax.rsqrt(var + EPS) * g_ref[...] + b_ref[...]).astype(BF16)
    gw = w // GM_GROUPS
    lane = lax.broadcasted_iota(jnp.int32, (GM_CHUNK, LANES), 1)
    first = lane < gw
    for ci in range(zg.shape[0] // GM_CHUNK):
        rows = slice(ci * GM_CHUNK, (ci + 1) * GM_CHUNK)
        tiles = []
        for t in range(w // LANES):
            vt = v[rows, t * LANES:(t + 1) * LANES]
            tiles.append(jnp.where(first, _dot(ws_ref[2 * t], vt), _dot(ws_ref[2 * t + 1], vt)))
        vm = jnp.concatenate(tiles, axis=1) + bs_ref[...]
        o_ref[0, rows, :] = (u[rows] * vm).astype(BF16)


def gmlp(z, ln_g, ln_b, ws, bs):
    bsz, n, _ = z.shape
    tr = min(n, 512)
    w = BR_WIDTH
    cb = COL["GM"] // (2 * w)
    bs_full = jnp.repeat(bs.T, w // GM_GROUPS, axis=1)
    return pl.pallas_call(
        _gmlp_kernel,
        grid=(bsz, n // tr),
        in_specs=[pl.BlockSpec((1, tr, 2 * w), lambda b, i: (b, i, cb)),
                  pl.BlockSpec((1, w), lambda b, i: (0, 0)),
                  pl.BlockSpec((1, w), lambda b, i: (0, 0)),
                  pl.BlockSpec((GM_GROUPS, GM_CHUNK, GM_CHUNK), lambda b, i: (0, 0, 0)),
                  pl.BlockSpec((GM_CHUNK, w), lambda b, i: (0, 0))],
        out_specs=pl.BlockSpec((1, tr, w), lambda b, i: (b, i, 0)),
        out_shape=jax.ShapeDtypeStruct((bsz, n, w), BF16),
        compiler_params=_params(("arbitrary", "arbitrary")),
        name="gmlp",
    )(z, ln_g.reshape(1, w), ln_b.reshape(1, w), ws.astype(BF16), bs_full)


def _merge_kernel(ya_ref, yb_ref, yc_ref, ga_ref, gb_ref, gc_ref, mg_ref, x_ref, mod_ref, np_ref,
                  wb_ref, wo_ref, o_ref):
    acc = None
    for i, (y_ref, g_ref) in enumerate(((ya_ref, ga_ref), (yb_ref, gb_ref), (yc_ref, gc_ref))):
        g = g_ref[0].astype(F32)
        gated = (y_ref[0].astype(F32) * (g * jax.nn.sigmoid(g))).astype(BF16)
        sel = jax.nn.sigmoid(mg_ref[0, :, i * D_MODEL:(i + 1) * D_MODEL].astype(F32))
        term = sel * _dot(gated, wb_ref[i])
        acc = term if acc is None else acc + term
    out = _dot(acc.astype(BF16), wo_ref[...])
    r = out * lax.rsqrt(jnp.mean(out * out, axis=-1, keepdims=True) + EPS) * np_ref[...]
    gt = mod_ref[0, :, 2 * D_MODEL:]
    o_ref[0] = x_ref[0] + gt * r


def merge_out(ya, yb, yc, z, x, mod_rows, row_of_batch, npost, wb, wo):
    bsz, n, _ = x.shape
    tm = min(n, 512)
    w = BR_WIDTH
    yspec = pl.BlockSpec((1, tm, w), lambda b, i: (b, i, 0))

    def zspec(name):
        cb = COL[name] // w
        return pl.BlockSpec((1, tm, w), lambda b, i: (b, i, cb))

    return pl.pallas_call(
        _merge_kernel,
        grid=(bsz, n // tm),
        in_specs=[yspec, yspec, yspec, zspec("GA"), zspec("GB"), zspec("GC"),
                  pl.BlockSpec((1, tm, 3 * D_MODEL), lambda b, i: (b, i, 0)),
                  pl.BlockSpec((1, tm, D_MODEL), lambda b, i: (b, i, 0)),
                  pl.BlockSpec((1, 1, 3 * D_MODEL), lambda b, i: (row_of_batch(b), 0, 0)),
                  pl.BlockSpec((1, D_MODEL), lambda b, i: (0, 0)),
                  pl.BlockSpec((3, w, D_MODEL), lambda b, i: (0, 0, 0)),
                  pl.BlockSpec((D_MODEL, D_MODEL), lambda b, i: (0, 0))],
        out_specs=pl.BlockSpec((1, tm, D_MODEL), lambda b, i: (b, i, 0)),
        out_shape=jax.ShapeDtypeStruct((bsz, n, D_MODEL), F32),
        compiler_params=_params(("arbitrary", "arbitrary")),
        name="merge_out",
    )(ya, yb, yc, z, z, z, z, x, mod_rows, npost.reshape(1, D_MODEL), wb, wo)


def _permute_cols(w):
    return jnp.concatenate([w[:, _REF_COLS[nm][0]:_REF_COLS[nm][0] + _REF_COLS[nm][1]] for nm in _NEW_ORDER],
                           axis=1)


def kernel(x, c, ctx, c_ctx, ada_w, ada_b, norm_pre, norm_post, w_in, da_lambda, da_subln, hy_short_w,
           hy_short_b, hy_f_w1, hy_f_b1, hy_f_w2, hy_f_b2, hy_f_w3, hy_f_freq, hy_bias, gm_ln_g, gm_ln_b,
           gm_ws, gm_bs, w_branch, w_out):
    bsz, n, _ = x.shape
    n_ctx = ctx.shape[1]
    assert bsz == 2 and 2 * n == DFT_R * DFT_R

    cond8 = jnp.zeros((8, D_MODEL), F32).at[0:bsz].set(c).at[bsz].set(c_ctx)
    mod = modulation_all(cond8, ada_w, ada_b)
    lat_row = lambda b: b
    ctx_row = lambda b: bsz

    cos_t, sin_t = rope_tables(n)
    feat = hyena_positions(n)
    feat_c = hyena_positions(n_ctx)
    tables = dft_tables()
    tables_c = dft_small_tables(2 * n_ctx)
    kb, vb, qb = COL["K"] // LANES, COL["V"] // LANES, COL["Q"] // LANES

    xc = ctx
    for l in range(DEPTH):
        last = l == DEPTH - 1
        lam_init = 0.8 - 0.6 * math.exp(-0.3 * l)
        mod_rows = mod[l].reshape(8, 1, 3 * D_MODEL)
        w_l = _permute_cols(w_in[l]).astype(BF16)
        wb_l = w_branch[l].astype(BF16)
        wo_l = w_out[l].astype(BF16)
        filt_w = (hy_f_w1[l], hy_f_b1[l], hy_f_w2[l], hy_f_b2[l], hy_f_w3[l], hy_f_freq[l])

        z = in_projection(x, mod_rows, lat_row, norm_pre[l], w_l)
        zc = in_projection(xc, mod_rows, ctx_row, norm_pre[l], w_l)

        qr, kr = rope_qk(z, cos_t, sin_t)
        y_a = diff_attention(qr, 0, zc, (kr, 0, z, vb), da_lambda[l], da_subln[l], lam_init, True)
        taps, sumsq = hyena_filter_taps(feat, *filt_w)
        x0, uv = hyena_short_conv(z, hy_short_w[l], hy_short_b[l])
        y_b = hyena_long_conv(uv, x0, taps, sumsq, hy_bias[l], tables)
        y_c = gmlp(z, gm_ln_g[l], gm_ln_b[l], gm_ws[l], gm_bs[l])
        x_new = merge_out(y_a, y_b, y_c, z, x, mod_rows, lat_row, norm_post[l], wb_l, wo_l)

        if not last:
            yc_a = diff_attention(zc, qb, zc, None, da_lambda[l], da_subln[l], lam_init, False)
            taps_c, sumsq_c = hyena_filter_taps(feat_c, *filt_w)
            x0c, uvc = hyena_short_conv(zc, hy_short_w[l], hy_short_b[l])
            yc_b = hyena_long_conv_small(uvc, x0c, taps_c, sumsq_c, hy_bias[l], tables_c)
            yc_c = gmlp(zc, gm_ln_g[l], gm_ln_b[l], gm_ws[l], gm_bs[l])
            xc = merge_out(yc_a, yc_b, yc_c, zc, xc, mod_rows, ctx_row, norm_post[l], wb_l, wo_l)
        x = x_new
    return x
```

```python
import functools
import math

import jax
import jax.numpy as jnp
import numpy as np
from jax import lax
from jax.experimental import pallas as pl
from jax.experimental.pallas import tpu as pltpu

F32 = jnp.float32
BF16 = jnp.bfloat16

D_MODEL = 1024
DEPTH = 4
GRID_W = 64
EPS = 1e-6
BR_WIDTH = 512
DA_SUB = 64
DA_VDIM = 128
DA_HEADS = 4
ROPE_BASE = 10000.0
ROPE_NF = 16
HY_WIDTH = 512
HY_BANDS = 16
HY_EMB = 33
HY_HIDDEN = 64
HY_MIN_DECAY = math.log(1e-2) / 1.5
HY_MAX_DECAY = math.log(1e-2) / 0.3
GM_GROUPS = 8
GM_CHUNK = 128

LANES = 128
DFT_R = 128
QK_SCALE = DA_SUB ** -0.5 * math.log2(math.e)

_REF_COLS = dict(K=(0, 512), V=(512, 512), Q=(1024, 512), GA=(1536, 512), HY=(2048, 1536),
                 GB=(3584, 512), GM=(4096, 1024), GC=(5120, 512), MG=(5632, 3072))
_NEW_ORDER = ("MG", "HY", "K", "GM", "V", "Q", "GA", "GB", "GC")
COL = {}
_off = 0
for _name in _NEW_ORDER:
    COL[_name] = _off
    _off += _REF_COLS[_name][1]
COL_END = _off
for _name in _NEW_ORDER:
    assert COL[_name] % _REF_COLS[_name][1] == 0

VMEM_LIMIT = 48 * 1024 * 1024


def _params(sem):
    return pltpu.CompilerParams(dimension_semantics=sem, vmem_limit_bytes=VMEM_LIMIT)


def _dot(a, b):
    return jnp.dot(a, b, preferred_element_type=F32)


def _dot_hi(a, b):
    return jnp.dot(a, b, preferred_element_type=F32, precision=lax.Precision.HIGHEST)


def _mod_kernel(c_ref, w_ref, b_ref, o_ref):
    cond = c_ref[...]
    s = cond * jax.nn.sigmoid(cond)
    o_ref[0] = _dot_hi(s, w_ref[0]) + b_ref[0]


def modulation_all(cond8, ada_w, ada_b):
    tn = 1024
    return pl.pallas_call(
        _mod_kernel,
        grid=(DEPTH, 3 * D_MODEL // tn),
        in_specs=[pl.BlockSpec((8, D_MODEL), lambda l, j: (0, 0)),
                  pl.BlockSpec((1, D_MODEL, tn), lambda l, j: (l, 0, j)),
                  pl.BlockSpec((1, 1, tn), lambda l, j: (l, 0, j))],
        out_specs=pl.BlockSpec((1, 8, tn), lambda l, j: (l, 0, j)),
        out_shape=jax.ShapeDtypeStruct((DEPTH, 8, 3 * D_MODEL), F32),
        compiler_params=_params(("arbitrary", "arbitrary")),
        name="modulation",
    )(cond8, ada_w, ada_b.reshape(DEPTH, 1, 3 * D_MODEL))


def _inproj_kernel(x_ref, mod_ref, g_ref, w_ref, o_ref, h_ref):
    @pl.when(pl.program_id(2) == 0)
    def _():
        x = x_ref[0]
        y = x * lax.rsqrt(jnp.mean(x * x, axis=-1, keepdims=True) + EPS) * g_ref[...]
        sh = mod_ref[0, :, 0:D_MODEL]
        sc = mod_ref[0, :, D_MODEL:2 * D_MODEL]
        h_ref[...] = (y * (1.0 + sc) + sh).astype(BF16)

    o_ref[0] = _dot(h_ref[...], w_ref[...]).astype(BF16)


def in_projection(x, mod_rows, row_of_batch, g, w):
    bsz, n, _ = x.shape
    tm = min(n, 1024)
    tn = COL_END // 4
    return pl.pallas_call(
        _inproj_kernel,
        grid=(bsz, n // tm, COL_END // tn),
        in_specs=[pl.BlockSpec((1, tm, D_MODEL), lambda b, i, j: (b, i, 0)),
                  pl.BlockSpec((1, 1, 3 * D_MODEL), lambda b, i, j: (row_of_batch(b), 0, 0)),
                  pl.BlockSpec((1, D_MODEL), lambda b, i, j: (0, 0)),
                  pl.BlockSpec((D_MODEL, tn), lambda b, i, j: (0, j))],
        out_specs=pl.BlockSpec((1, tm, tn), lambda b, i, j: (b, i, j)),
        out_shape=jax.ShapeDtypeStruct((bsz, n, COL_END), BF16),
        scratch_shapes=[pltpu.VMEM((tm, D_MODEL), BF16)],
        compiler_params=_params(("arbitrary", "arbitrary", "arbitrary")),
        name="in_projection",
    )(x, mod_rows, g.reshape(1, D_MODEL), w)


def _rope_kernel(zq_ref, zk_ref, cos_ref, sin_ref, q_ref, k_ref):
    lane = lax.broadcasted_iota(jnp.int32, cos_ref.shape, 1)
    low = (lane % (2 * ROPE_NF)) < ROPE_NF
    cs = cos_ref[...]
    sn = sin_ref[...]

    def rot(x):
        partner = jnp.where(low, pltpu.roll(x, LANES - ROPE_NF, axis=1), pltpu.roll(x, ROPE_NF, axis=1))
        return (x * cs + partner * sn).astype(BF16)

    for h in range(DA_HEADS):
        cols = slice(h * LANES, (h + 1) * LANES)
        q_ref[0, :, cols] = rot(zq_ref[0, :, cols].astype(F32) * QK_SCALE)
        k_ref[0, :, cols] = rot(zk_ref[0, :, cols].astype(F32))


def rope_tables(n):
    pos = jnp.arange(n)
    row = (pos // GRID_W).astype(F32)
    col = (pos % GRID_W).astype(F32)
    inv = ROPE_BASE ** (-jnp.arange(ROPE_NF, dtype=F32) / ROPE_NF)
    ar = row[:, None] * inv
    ac = col[:, None] * inv
    cos64 = jnp.concatenate([jnp.cos(ar), jnp.cos(ar), jnp.cos(ac), jnp.cos(ac)], axis=1)
    sin64 = jnp.concatenate([-jnp.sin(ar), jnp.sin(ar), -jnp.sin(ac), jnp.sin(ac)], axis=1)
    return jnp.tile(cos64, (1, 2)), jnp.tile(sin64, (1, 2))


def rope_qk(z, cos_t, sin_t):
    bsz, n, _ = z.shape
    tr = 1024
    w = DA_HEADS * LANES
    qb, kb = COL["Q"] // w, COL["K"] // w
    out = jax.ShapeDtypeStruct((bsz, n, w), BF16)
    return pl.pallas_call(
        _rope_kernel,
        grid=(bsz, n // tr),
        in_specs=[pl.BlockSpec((1, tr, w), lambda b, i: (b, i, qb)),
                  pl.BlockSpec((1, tr, w), lambda b, i: (b, i, kb)),
                  pl.BlockSpec((tr, LANES), lambda b, i: (i, 0)),
                  pl.BlockSpec((tr, LANES), lambda b, i: (i, 0))],
        out_specs=[pl.BlockSpec((1, tr, w), lambda b, i: (b, i, 0)),
                   pl.BlockSpec((1, tr, w), lambda b, i: (b, i, 0))],
        out_shape=[out, out],
        compiler_params=_params(("arbitrary", "arbitrary")),
        name="rope_qk",
    )(z, z, cos_t, sin_t)


def _attn_kernel(*refs, lam_init, n_lat, ck, prescaled):
    if n_lat:
        lam_ref, g_ref, q_ref, kc_ref, vc_ref, k_ref, v_ref, o_ref, vcx_ref, vx_ref = refs
    else:
        lam_ref, g_ref, q_ref, kc_ref, vc_ref, o_ref, vcx_ref = refs
    tq = q_ref.shape[1]
    n_ctx = kc_ref.shape[1]

    @pl.when(pl.program_id(2) == 0)
    def _():
        def ones_col(rows):
            return (lax.broadcasted_iota(jnp.int32, (rows, LANES), 1) == 0).astype(BF16)
        vcx_ref[:, 0:DA_VDIM] = vc_ref[0]
        vcx_ref[:, DA_VDIM:] = ones_col(n_ctx)
        if n_lat:
            vx_ref[:, 0:DA_VDIM] = v_ref[0]
            vx_ref[:, DA_VDIM:] = ones_col(n_lat)

    lp = lam_ref[...]
    lam = (jnp.exp(jnp.sum(lp[0:1] * lp[1:2], axis=1, keepdims=True))
           - jnp.exp(jnp.sum(lp[2:3] * lp[3:4], axis=1, keepdims=True)) + lam_init)

    q = q_ref[0]
    if not prescaled:
        q = (q.astype(F32) * QK_SCALE).astype(BF16)
    lane = lax.broadcasted_iota(jnp.int32, q.shape, 1)
    zero = jnp.zeros_like(q)
    qq = jnp.concatenate([jnp.where(lane < DA_SUB, q, zero), jnp.where(lane >= DA_SUB, q, zero)], axis=0)

    chunks = [(lambda: kc_ref[0], lambda: vcx_ref[...])]
    for c in range(n_lat // ck):
        chunks.append((lambda c=c: k_ref[0, c * ck:(c + 1) * ck, :], lambda c=c: vx_ref[c * ck:(c + 1) * ck, :]))

    def scores(c):
        return lax.dot_general(qq, chunks[c][0](), (((1,), (1,)), ((), ())), preferred_element_type=F32)

    m = jnp.full((2 * tq, 1), -jnp.inf, F32)
    acc = jnp.zeros((2 * tq, 2 * DA_VDIM), F32)
    pending = None
    s_next = scores(0)
    for c in range(len(chunks)):
        s = s_next
        if c + 1 < len(chunks):
            s_next = scores(c + 1)
        m_new = jnp.maximum(m, jnp.max(s, axis=1, keepdims=True))
        alpha = jnp.exp2(m - m_new)
        p = jnp.exp2(s - m_new).astype(BF16)
        m = m_new
        if pending is not None:
            p_prev, alpha_prev, c_prev = pending
            acc = acc * alpha_prev + _dot(p_prev, chunks[c_prev][1]())
        pending = (p, alpha, c)
    p_prev, alpha_prev, c_prev = pending
    acc = acc * alpha_prev + _dot(p_prev, chunks[c_prev][1]())

    o = acc[:, 0:DA_VDIM] / acc[:, DA_VDIM:DA_VDIM + 1]
    d = o[:tq] - lam * o[tq:]
    y = d * lax.rsqrt(jnp.mean(d * d, axis=-1, keepdims=True) + EPS) * g_ref[...]
    o_ref[0] = (y * (1.0 - lam_init)).astype(BF16)


def diff_attention(q_arr, q_col, zc, lat, lam_p, subln, lam_init, prescaled):
    bsz, nq, _ = q_arr.shape
    n_ctx = zc.shape[1]
    tq = 128
    kcb, vcb = COL["K"] // LANES, COL["V"] // LANES
    in_specs = [pl.BlockSpec((4, DA_SUB), lambda b, h, i: (0, 0)),
                pl.BlockSpec((1, DA_VDIM), lambda b, h, i: (0, 0)),
                pl.BlockSpec((1, tq, LANES), lambda b, h, i: (b, i, q_col + h)),
                pl.BlockSpec((1, n_ctx, LANES), lambda b, h, i: (b, 0, kcb + h)),
                pl.BlockSpec((1, n_ctx, LANES), lambda b, h, i: (b, 0, vcb + h))]
    args = [lam_p, subln.reshape(1, DA_VDIM), q_arr, zc, zc]
    scratch = [pltpu.VMEM((n_ctx, 2 * DA_VDIM), BF16)]
    n_lat = 0
    if lat is not None:
        k_arr, k_col, v_arr, v_col = lat
        n_lat = k_arr.shape[1]
        in_specs += [pl.BlockSpec((1, n_lat, LANES), lambda b, h, i: (b, 0, k_col + h)),
                     pl.BlockSpec((1, n_lat, LANES), lambda b, h, i: (b, 0, v_col + h))]
        args += [k_arr, v_arr]
        scratch.append(pltpu.VMEM((n_lat, 2 * DA_VDIM), BF16))
    return pl.pallas_call(
        functools.partial(_attn_kernel, lam_init=lam_init, n_lat=n_lat, ck=1024, prescaled=prescaled),
        grid=(bsz, DA_HEADS, nq // tq),
        in_specs=in_specs,
        out_specs=pl.BlockSpec((1, tq, LANES), lambda b, h, i: (b, i, h)),
        out_shape=jax.ShapeDtypeStruct((bsz, nq, BR_WIDTH), BF16),
        scratch_shapes=scratch,
        compiler_params=_params(("arbitrary", "arbitrary", "arbitrary")),
        name="diff_attention",
    )(*args)


def hyena_positions(L):
    t = jnp.linspace(0.0, 1.0, L, dtype=F32)[:, None]
    wpos = ((2.0 * math.pi / L) * jnp.arange(L, dtype=F32))[:, None]
    bands = jnp.linspace(1e-4, HY_BANDS - 1, HY_BANDS, dtype=F32)[None, :]
    fwd = jnp.concatenate([t, jnp.cos(bands * wpos), -jnp.sin(bands * wpos)], axis=-1)
    emb = jnp.concatenate([fwd, fwd[0:1], jnp.flip(fwd[1:], axis=0)], axis=0)
    mask = (jnp.arange(2 * L) != L).astype(F32)[:, None]
    pad = jnp.zeros((2 * L, LANES - HY_EMB - 1), F32)
    return jnp.concatenate([emb, pad, mask], axis=-1)


def _filter_kernel(feat_ref, w1_ref, b1_ref, w2_ref, b2_ref, w3_ref, fr_ref, dl_ref, k_ref, ss_ref):
    feat = feat_ref[...]
    hid = jnp.sin(fr_ref[0:1, :] * (_dot_hi(feat, w1_ref[...]) + b1_ref[...]))
    hid = jnp.sin(fr_ref[1:2, :] * (_dot_hi(hid, w2_ref[...]) + b2_ref[...]))
    h = _dot_hi(hid, w3_ref[...])
    t = feat[:, 0:1]
    mask = feat[:, LANES - 1:LANES]
    k = h * jnp.exp(-t * dl_ref[...]) * mask
    k_ref[...] = k

    @pl.when(pl.program_id(0) == 0)
    def _():
        ss_ref[...] = jnp.zeros_like(ss_ref)

    ss_ref[...] += jnp.sum(k * k, axis=0, keepdims=True)


def hyena_filter_taps(feat, w1, b1, w2, b2, w3, freq):
    two_l = feat.shape[0]
    tr = min(two_l // 2, 1024)
    nb = two_l // tr
    w1p = jnp.zeros((LANES, HY_HIDDEN), F32).at[:HY_EMB].set(w1)
    deltas = jnp.abs(jnp.linspace(HY_MIN_DECAY, HY_MAX_DECAY, HY_WIDTH, dtype=F32))[None, :]
    const = lambda i: (0, 0)
    return pl.pallas_call(
        _filter_kernel,
        grid=(nb,),
        in_specs=[pl.BlockSpec((tr, LANES), lambda i: (i, 0)),
                  pl.BlockSpec((LANES, HY_HIDDEN), const),
                  pl.BlockSpec((1, HY_HIDDEN), const),
                  pl.BlockSpec((HY_HIDDEN, HY_HIDDEN), const),
                  pl.BlockSpec((1, HY_HIDDEN), const),
                  pl.BlockSpec((HY_HIDDEN, HY_WIDTH), lambda i: (0, (2 * i) // nb)),
                  pl.BlockSpec((2, HY_HIDDEN), const),
                  pl.BlockSpec((1, HY_WIDTH), const)],
        out_specs=[pl.BlockSpec((tr, HY_WIDTH), lambda i: (i, 0)),
                   pl.BlockSpec((1, HY_WIDTH), const)],
        out_shape=[jax.ShapeDtypeStruct((two_l, HY_WIDTH), F32),
                   jax.ShapeDtypeStruct((1, HY_WIDTH), F32)],
        compiler_params=_params(("arbitrary",)),
        name="hyena_filter",
    )(feat, w1p, b1.reshape(1, -1), w2, b2.reshape(1, -1), w3, freq, deltas)


def _short_conv_kernel(z_ref, prev_ref, next_ref, w_ref, b_ref, x0_ref, uv_ref):
    i = pl.program_id(1)
    last = pl.num_programs(1) - 1
    z = z_ref[0].astype(F32)
    tr = z.shape[0]
    halo = prev_ref.shape[1]
    before = jnp.where(i == 0, 0.0, prev_ref[0, halo - 1:halo, :].astype(F32))
    after = jnp.where(i == last, 0.0, next_ref[0, 0:1, :].astype(F32))
    row = lax.broadcasted_iota(jnp.int32, z.shape, 0)
    zm = jnp.where(row == 0, before, pltpu.roll(z, 1, axis=0))
    zp = jnp.where(row == tr - 1, after, pltpu.roll(z, tr - 1, axis=0))
    y = zm * w_ref[0:1, :] + z * w_ref[1:2, :] + zp * w_ref[2:3, :] + b_ref[...]
    x0_ref[0] = y[:, 0:HY_WIDTH].astype(BF16)
    uv_ref[0] = (y[:, HY_WIDTH:2 * HY_WIDTH] * y[:, 2 * HY_WIDTH:]).astype(BF16)


def hyena_short_conv(z, sw, sb):
    bsz, n, _ = z.shape
    tr = min(n, 512)
    halo = 16
    w3c = 3 * HY_WIDTH
    cb = COL["HY"] // w3c
    nh = n // halo
    out = jax.ShapeDtypeStruct((bsz, n, HY_WIDTH), BF16)
    return pl.pallas_call(
        _short_conv_kernel,
        grid=(bsz, n // tr),
        in_specs=[pl.BlockSpec((1, tr, w3c), lambda b, i: (b, i, cb)),
                  pl.BlockSpec((1, halo, w3c), lambda b, i: (b, jnp.maximum(i * (tr // halo) - 1, 0), cb)),
                  pl.BlockSpec((1, halo, w3c), lambda b, i: (b, jnp.minimum((i + 1) * (tr // halo), nh - 1), cb)),
                  pl.BlockSpec((3, w3c), lambda b, i: (0, 0)),
                  pl.BlockSpec((1, w3c), lambda b, i: (0, 0))],
        out_specs=[pl.BlockSpec((1, tr, HY_WIDTH), lambda b, i: (b, i, 0)),
                   pl.BlockSpec((1, tr, HY_WIDTH), lambda b, i: (b, i, 0))],
        out_shape=[out, out],
        compiler_params=_params(("arbitrary", "arbitrary")),
        name="hyena_short_conv",
    )(z, z, z, sw, sb.reshape(1, w3c))


def dft_tables():
    r = DFT_R
    n_fft = r * r
    idx = jnp.arange(r, dtype=jnp.int32)
    prod = idx[:, None] * idx[None, :]
    ang = (2.0 * math.pi / r) * (prod % r).astype(F32)
    f_re, f_im = jnp.cos(ang), -jnp.sin(ang)
    ang = (2.0 * math.pi / n_fft) * prod.astype(F32)
    t_re, t_im = jnp.cos(ang), -jnp.sin(ang)

    def cmul(a_re, a_im, b_re, b_im):
        return a_re * b_re - a_im * b_im, a_re * b_im + a_im * b_re

    g_re, g_im = cmul(f_re[None, :, :], f_im[None, :, :], t_re[:, None, :], t_im[:, None, :])
    gh_re, gh_im = cmul(f_re[None, :, :], -f_im[None, :, :], t_re[:, :, None], -t_im[:, :, None])
    return tuple(a.astype(BF16) for a in (f_re, f_im, g_re, g_im, gh_re, gh_im))


def _fft1_kernel(*refs, n_in):
    if n_in == 2:
        x_ref, fre_ref, fim_ref, are_ref, aim_ref = refs
        x0 = x_ref[0]
        x1 = x_ref[1]
        fre, fim = fre_ref[...], fim_ref[...]
        are_ref[...] = (_dot(fre, x0) - _dot(fim, x1)).astype(BF16)
        aim_ref[...] = (_dot(fre, x1) + _dot(fim, x0)).astype(BF16)
    else:
        x_ref, fre_ref, fim_ref, are_ref, aim_ref = refs
        x = x_ref[...].astype(BF16)
        are_ref[...] = _dot(fre_ref[...], x).astype(BF16)
        aim_ref[...] = _dot(fim_ref[...], x).astype(BF16)


def fft_level1(x, f_re, f_im):
    r = DFT_R
    cols = x.shape[-1]
    tc = 8192
    if x.ndim == 3:
        n_in, kdim = 2, x.shape[1]
        x_spec = pl.BlockSpec((2, kdim, tc), lambda j: (0, 0, j))
    else:
        n_in, kdim = 1, x.shape[0]
        x_spec = pl.BlockSpec((kdim, tc), lambda j: (0, j))
    out = jax.ShapeDtypeStruct((r, cols), BF16)
    return pl.pallas_call(
        functools.partial(_fft1_kernel, n_in=n_in),
        grid=(cols // tc,),
        in_specs=[x_spec,
                  pl.BlockSpec((r, kdim), lambda j: (0, 0)),
                  pl.BlockSpec((r, kdim), lambda j: (0, 0))],
        out_specs=[pl.BlockSpec((r, tc), lambda j: (0, j)), pl.BlockSpec((r, tc), lambda j: (0, j))],
        out_shape=[out, out],
        compiler_params=_params(("arbitrary",)),
        name="fft_level1",
    )(x, f_re[:, :kdim], f_im[:, :kdim])


def _fft2_filter_kernel(are_ref, aim_ref, gre_ref, gim_ref, sc_ref, kre_ref, kim_ref):
    sc = sc_ref[...]
    for j in range(are_ref.shape[0]):
        a_re, a_im = are_ref[j], aim_ref[j]
        g_re, g_im = gre_ref[j], gim_ref[j]
        kre_ref[j] = (_dot(g_re, a_re) - _dot(g_im, a_im)) * sc
        kim_ref[j] = (_dot(g_re, a_im) + _dot(g_im, a_re)) * sc


def fft_level2_filter(a_re, a_im, g_re, g_im, scale):
    r = DFT_R
    c = a_re.shape[-1]
    tk = 8
    blk_a = pl.BlockSpec((tk, r, c), lambda i: (i, 0, 0))
    blk_g = pl.BlockSpec((tk, r, r), lambda i: (i, 0, 0))
    out = jax.ShapeDtypeStruct((r, r, c), F32)
    return pl.pallas_call(
        _fft2_filter_kernel,
        grid=(r // tk,),
        in_specs=[blk_a, blk_a, blk_g, blk_g, pl.BlockSpec((1, c), lambda i: (0, 0))],
        out_specs=[blk_a, blk_a],
        out_shape=[out, out],
        compiler_params=_params(("arbitrary",)),
        name="fft_level2_filter",
    )(a_re, a_im, g_re, g_im, scale)


def _fft2_conv_kernel(are_ref, aim_ref, gre_ref, gim_ref, hre_ref, him_ref, kre_ref, kim_ref, bre_ref, bim_ref):
    for j in range(are_ref.shape[0]):
        a_re, a_im = are_ref[j], aim_ref[j]
        g_re, g_im = gre_ref[j], gim_ref[j]
        x_re = _dot(g_re, a_re) - _dot(g_im, a_im)
        x_im = _dot(g_re, a_im) + _dot(g_im, a_re)
        k_re, k_im = kre_ref[j], kim_ref[j]
        y_re = (x_re * k_re - x_im * k_im).astype(BF16)
        y_im = (x_re * k_im + x_im * k_re).astype(BF16)
        h_re, h_im = hre_ref[j], him_ref[j]
        bre_ref[j] = (_dot(h_re, y_re) - _dot(h_im, y_im)).astype(BF16)
        bim_ref[j] = (_dot(h_re, y_im) + _dot(h_im, y_re)).astype(BF16)


def fft_level2_conv(a_re, a_im, g_re, g_im, gh_re, gh_im, kf_re, kf_im):
    r = DFT_R
    c = a_re.shape[-1]
    tk = 8
    blk_a = pl.BlockSpec((tk, r, c), lambda i: (i, 0, 0))
    blk_g = pl.BlockSpec((tk, r, r), lambda i: (i, 0, 0))
    out = jax.ShapeDtypeStruct((r, r, c), BF16)
    return pl.pallas_call(
        _fft2_conv_kernel,
        grid=(r // tk,),
        in_specs=[blk_a, blk_a, blk_g, blk_g, blk_g, blk_g, blk_a, blk_a],
        out_specs=[blk_a, blk_a],
        out_shape=[out, out],
        compiler_params=_params(("arbitrary",)),
        name="fft_level2_conv",
    )(a_re, a_im, g_re, g_im, gh_re, gh_im, kf_re, kf_im)


def _ifft1_kernel(bre_ref, bim_ref, fre_ref, fim_ref, uv_ref, x0_ref, bias_ref, o_ref):
    fre, fim = fre_ref[...], fim_ref[...]
    b_re, b_im = bre_ref[...], bim_ref[...]
    y0 = _dot(fre, b_re) + _dot(fim, b_im)
    y1 = _dot(fre, b_im) - _dot(fim, b_re)
    bias = bias_ref[...]
    o_ref[0] = ((y0 + uv_ref[0].astype(F32) * bias) * x0_ref[0].astype(F32)).astype(BF16)
    o_ref[1] = ((y1 + uv_ref[1].astype(F32) * bias) * x0_ref[1].astype(F32)).astype(BF16)


def ifft_level1(b_re, b_im, f_re, f_im, uv, x0, bias_t):
    r = DFT_R
    half = r // 2
    cols = b_re.shape[-1]
    tc = 8192
    blk_b = pl.BlockSpec((r, tc), lambda j: (0, j))
    blk_f = pl.BlockSpec((half, r), lambda j: (0, 0))
    blk_x = pl.BlockSpec((2, half, tc), lambda j: (0, 0, j))
    return pl.pallas_call(
        _ifft1_kernel,
        grid=(cols // tc,),
        in_specs=[blk_b, blk_b, blk_f, blk_f, blk_x, blk_x, pl.BlockSpec((1, tc), lambda j: (0, 0))],
        out_specs=blk_x,
        out_shape=jax.ShapeDtypeStruct((2, half, cols), BF16),
        compiler_params=_params(("arbitrary",)),
        name="ifft_level1",
    )(b_re, b_im, f_re[:half], f_im[:half], uv, x0, bias_t)


def hyena_long_conv(uv, x0, taps, sumsq, bias, tables):
    f_re, f_im, g_re, g_im, gh_re, gh_im = tables
    r = DFT_R
    bsz, L, c = uv.shape
    assert bsz == 2 and 2 * L == r * r
    scale = lax.rsqrt(sumsq) * (1.0 / (r * r))
    ka_re, ka_im = fft_level1(taps.reshape(r, r * c), f_re, f_im)
    kf_re, kf_im = fft_level2_filter(ka_re.reshape(r, r, c), ka_im.reshape(r, r, c), g_re, g_im, scale)
    uv2 = uv.reshape(2, r // 2, r * c)
    a_re, a_im = fft_level1(uv2, f_re, f_im)
    b_re, b_im = fft_level2_conv(a_re.reshape(r, r, c), a_im.reshape(r, r, c), g_re, g_im, gh_re, gh_im,
                                 kf_re, kf_im)
    bias_t = jnp.tile(bias.reshape(1, c), (1, 8192 // c))
    y = ifft_level1(b_re.reshape(r, r * c), b_im.reshape(r, r * c), f_re, f_im, uv2,
                    x0.reshape(2, r // 2, r * c), bias_t)
    return y.reshape(2, L, c)


def dft_small_tables(n_fft):
    idx = jnp.arange(n_fft, dtype=jnp.int32)
    ang = (2.0 * math.pi / n_fft) * ((idx[:, None] * idx[None, :]) % n_fft).astype(F32)
    return jnp.cos(ang).astype(BF16), (-jnp.sin(ang)).astype(BF16)


def _conv_small_kernel(taps_ref, ss_ref, fre_ref, fim_ref, uv_ref, x0_ref, bias_ref, o_ref):
    two_l = taps_ref.shape[0]
    L = two_l // 2
    fre, fim = fre_ref[...], fim_ref[...]
    scale = lax.rsqrt(ss_ref[...]) * (1.0 / two_l)
    taps = taps_ref[...].astype(BF16)
    k_re = _dot(fre, taps) * scale
    k_im = _dot(fim, taps) * scale
    u0, u1 = uv_ref[0], uv_ref[1]
    fre_l, fim_l = fre[:, :L], fim[:, :L]
    x_re = _dot(fre_l, u0) - _dot(fim_l, u1)
    x_im = _dot(fre_l, u1) + _dot(fim_l, u0)
    y_re = (x_re * k_re - x_im * k_im).astype(BF16)
    y_im = (x_re * k_im + x_im * k_re).astype(BF16)
    fre_t, fim_t = fre[:L, :], fim[:L, :]
    y0 = _dot(fre_t, y_re) + _dot(fim_t, y_im)
    y1 = _dot(fre_t, y_im) - _dot(fim_t, y_re)
    bias = bias_ref[...]
    o_ref[0] = ((y0 + u0.astype(F32) * bias) * x0_ref[0].astype(F32)).astype(BF16)
    o_ref[1] = ((y1 + u1.astype(F32) * bias) * x0_ref[1].astype(F32)).astype(BF16)


def hyena_long_conv_small(uv, x0, taps, sumsq, bias, tables):
    bsz, L, c = uv.shape
    assert bsz == 2
    f_re, f_im = tables
    full = lambda shape: pl.BlockSpec(shape, lambda i: (0,) * len(shape))
    return pl.pallas_call(
        _conv_small_kernel,
        grid=(1,),
        in_specs=[full((2 * L, c)), full((1, c)), full((2 * L, 2 * L)), full((2 * L, 2 * L)),
                  full((2, L, c)), full((2, L, c)), full((1, c))],
        out_specs=full((2, L, c)),
        out_shape=jax.ShapeDtypeStruct((2, L, c), BF16),
        compiler_params=_params(("arbitrary",)),
        name="hyena_conv_small",
    )(taps, sumsq, f_re, f_im, uv, x0, bias.reshape(1, c))


def _gmlp_kernel(z_ref, g_ref, b_ref, ws_ref, bs_ref, o_ref):
    zg = z_ref[0].astype(F32)
    gl = 0.5 * zg * (1.0 + lax.erf(zg * (2.0 ** -0.5)))
    w = BR_WIDTH
    u = gl[:, :w]
    v = gl[:, w:]
    mu = jnp.mean(v, axis=-1, keepdims=True)
    var = jnp.mean(jnp.square(v - mu), axis=-1, keepdims=True)
    v = ((v - mu) * lax.rsqrt(var + EPS) * g_ref[...] + b_ref[...]).astype(BF16)
    gw = w // GM_GROUPS
    lane = lax.broadcasted_iota(jnp.int32, (GM_CHUNK, LANES), 1)
    first = lane < gw
    for ci in range(zg.shape[0] // GM_CHUNK):
        rows = slice(ci * GM_CHUNK, (ci + 1) * GM_CHUNK)
        tiles = []
        for t in range(w // LANES):
            vt = v[rows, t * LANES:(t + 1) * LANES]
            tiles.append(jnp.where(first, _dot(ws_ref[2 * t], vt), _dot(ws_ref[2 * t + 1], vt)))
        vm = jnp.concatenate(tiles, axis=1) + bs_ref[...]
        o_ref[0, rows, :] = (u[rows] * vm).astype(BF16)


def gmlp(z, ln_g, ln_b, ws, bs):
    bsz, n, _ = z.shape
    tr = min(n, 512)
    w = BR_WIDTH
    cb = COL["GM"] // (2 * w)
    bs_full = jnp.repeat(bs.T, w // GM_GROUPS, axis=1)
    return pl.pallas_call(
        _gmlp_kernel,
        grid=(bsz, n // tr),
        in_specs=[pl.BlockSpec((1, tr, 2 * w), lambda b, i: (b, i, cb)),
                  pl.BlockSpec((1, w), lambda b, i: (0, 0)),
                  pl.BlockSpec((1, w), lambda b, i: (0, 0)),
                  pl.BlockSpec((GM_GROUPS, GM_CHUNK, GM_CHUNK), lambda b, i: (0, 0, 0)),
                  pl.BlockSpec((GM_CHUNK, w), lambda b, i: (0, 0))],
        out_specs=pl.BlockSpec((1, tr, w), lambda b, i: (b, i, 0)),
        out_shape=jax.ShapeDtypeStruct((bsz, n, w), BF16),
        compiler_params=_params(("arbitrary", "arbitrary")),
        name="gmlp",
    )(z, ln_g.reshape(1, w), ln_b.reshape(1, w), ws.astype(BF16), bs_full)


def _merge_kernel(ya_ref, yb_ref, yc_ref, ga_ref, gb_ref, gc_ref, mg_ref, x_ref, mod_ref, np_ref,
                  wb_ref, wo_ref, o_ref):
    acc = None
    for i, (y_ref, g_ref) in enumerate(((ya_ref, ga_ref), (yb_ref, gb_ref), (yc_ref, gc_ref))):
        g = g_ref[0].astype(F32)
        gated = (y_ref[0].astype(F32) * (g * jax.nn.sigmoid(g))).astype(BF16)
        sel = jax.nn.sigmoid(mg_ref[0, :, i * D_MODEL:(i + 1) * D_MODEL].astype(F32))
        term = sel * _dot(gated, wb_ref[i])
        acc = term if acc is None else acc + term
    out = _dot(acc.astype(BF16), wo_ref[...])
    r = out * lax.rsqrt(jnp.mean(out * out, axis=-1, keepdims=True) + EPS) * np_ref[...]
    gt = mod_ref[0, :, 2 * D_MODEL:]
    o_ref[0] = x_ref[0] + gt * r


def merge_out(ya, yb, yc, z, x, mod_rows, row_of_batch, npost, wb, wo):
    bsz, n, _ = x.shape
    tm = min(n, 512)
    w = BR_WIDTH
    yspec = pl.BlockSpec((1, tm, w), lambda b, i: (b, i, 0))

    def zspec(name):
        cb = COL[name] // w
        return pl.BlockSpec((1, tm, w), lambda b, i: (b, i, cb))

    return pl.pallas_call(
        _merge_kernel,
        grid=(bsz, n // tm),
        in_specs=[yspec, yspec, yspec, zspec("GA"), zspec("GB"), zspec("GC"),
                  pl.BlockSpec((1, tm, 3 * D_MODEL), lambda b, i: (b, i, 0)),
                  pl.BlockSpec((1, tm, D_MODEL), lambda b, i: (b, i, 0)),
                  pl.BlockSpec((1, 1, 3 * D_MODEL), lambda b, i: (row_of_batch(b), 0, 0)),
                  pl.BlockSpec((1, D_MODEL), lambda b, i: (0, 0)),
                  pl.BlockSpec((3, w, D_MODEL), lambda b, i: (0, 0, 0)),
                  pl.BlockSpec((D_MODEL, D_MODEL), lambda b, i: (0, 0))],
        out_specs=pl.BlockSpec((1, tm, D_MODEL), lambda b, i: (b, i, 0)),
        out_shape=jax.ShapeDtypeStruct((bsz, n, D_MODEL), F32),
        compiler_params=_params(("arbitrary", "arbitrary")),
        name="merge_out",
    )(ya, yb, yc, z, z, z, z, x, mod_rows, npost.reshape(1, D_MODEL), wb, wo)


def _permute_cols(w):
    return jnp.concatenate([w[:, _REF_COLS[nm][0]:_REF_COLS[nm][0] + _REF_COLS[nm][1]] for nm in _NEW_ORDER],
                           axis=1)


def kernel(x, c, ctx, c_ctx, ada_w, ada_b, norm_pre, norm_post, w_in, da_lambda, da_subln, hy_short_w,
           hy_short_b, hy_f_w1, hy_f_b1, hy_f_w2, hy_f_b2, hy_f_w3, hy_f_freq, hy_bias, gm_ln_g, gm_ln_b,
           gm_ws, gm_bs, w_branch, w_out):
    bsz, n, _ = x.shape
    n_ctx = ctx.shape[1]
    assert bsz == 2 and 2 * n == DFT_R * DFT_R

    cond8 = jnp.zeros((8, D_MODEL), F32).at[0:bsz].set(c).at[bsz].set(c_ctx)
    mod = modulation_all(cond8, ada_w, ada_b)
    lat_row = lambda b: b
    ctx_row = lambda b: bsz

    cos_t, sin_t = rope_tables(n)
    feat = hyena_positions(n)
    feat_c = hyena_positions(n_ctx)
    tables = dft_tables()
    tables_c = dft_small_tables(2 * n_ctx)
    kb, vb, qb = COL["K"] // LANES, COL["V"] // LANES, COL["Q"] // LANES

    xc = ctx
    for l in range(DEPTH):
        last = l == DEPTH - 1
        lam_init = 0.8 - 0.6 * math.exp(-0.3 * l)
        mod_rows = mod[l].reshape(8, 1, 3 * D_MODEL)
        w_l = _permute_cols(w_in[l]).astype(BF16)
        wb_l = w_branch[l].astype(BF16)
        wo_l = w_out[l].astype(BF16)
        filt_w = (hy_f_w1[l], hy_f_b1[l], hy_f_w2[l], hy_f_b2[l], hy_f_w3[l], hy_f_freq[l])

        z = in_projection(x, mod_rows, lat_row, norm_pre[l], w_l)
        zc = in_projection(xc, mod_rows, ctx_row, norm_pre[l], w_l)

        qr, kr = rope_qk(z, cos_t, sin_t)
        y_a = diff_attention(qr, 0, zc, (kr, 0, z, vb), da_lambda[l], da_subln[l], lam_init, True)
        taps, sumsq = hyena_filter_taps(feat, *filt_w)
        x0, uv = hyena_short_conv(z, hy_short_w[l], hy_short_b[l])
        y_b = hyena_long_conv(uv, x0, taps, sumsq, hy_bias[l], tables)
        y_c = gmlp(z, gm_ln_g[l], gm_ln_b[l], gm_ws[l], gm_bs[l])
        x_new = merge_out(y_a, y_b, y_c, z, x, mod_rows, lat_row, norm_post[l], wb_l, wo_l)

        if not last:
            yc_a = diff_attention(zc, qb, zc, None, da_lambda[l], da_subln[l], lam_init, False)
            taps_c, sumsq_c = hyena_filter_taps(feat_c, *filt_w)
            x0c, uvc = hyena_short_conv(zc, hy_short_w[l], hy_short_b[l])
            yc_b = hyena_long_conv_small(uvc, x0c, taps_c, sumsq_c, hy_bias[l], tables_c)
            yc_c = gmlp(zc, gm_ln_g[l], gm_ln_b[l], gm_ws[l], gm_bs[l])
            xc = merge_out(yc_a, yc_b, yc_c, zc, xc, mod_rows, ctx_row, norm_post[l], wb_l, wo_l)
        x = x_new
    return x
```

```python
import functools
import math

import jax
import jax.numpy as jnp
import numpy as np
from jax import lax
from jax.experimental import pallas as pl
from jax.experimental.pallas import tpu as pltpu

F32 = jnp.float32
BF16 = jnp.bfloat16

D_MODEL = 1024
DEPTH = 4
GRID_W = 64
EPS = 1e-6
BR_WIDTH = 512
DA_SUB = 64
DA_VDIM = 128
DA_HEADS = 4
ROPE_BASE = 10000.0
ROPE_NF = 16
HY_WIDTH = 512
HY_BANDS = 16
HY_EMB = 33
HY_HIDDEN = 64
HY_MIN_DECAY = math.log(1e-2) / 1.5
HY_MAX_DECAY = math.log(1e-2) / 0.3
GM_GROUPS = 8
GM_CHUNK = 128

LANES = 128
DFT_R = 128
QK_SCALE = DA_SUB ** -0.5 * math.log2(math.e)

_REF_COLS = dict(K=(0, 512), V=(512, 512), Q=(1024, 512), GA=(1536, 512), HY=(2048, 1536),
                 GB=(3584, 512), GM=(4096, 1024), GC=(5120, 512), MG=(5632, 3072))
_NEW_ORDER = ("MG", "HY", "K", "GM", "V", "Q", "GA", "GB", "GC")
COL = {}
_off = 0
for _name in _NEW_ORDER:
    COL[_name] = _off
    _off += _REF_COLS[_name][1]
COL_END = _off
for _name in _NEW_ORDER:
    assert COL[_name] % _REF_COLS[_name][1] == 0

VMEM_LIMIT = 48 * 1024 * 1024


def _params(sem):
    return pltpu.CompilerParams(dimension_semantics=sem, vmem_limit_bytes=VMEM_LIMIT)


def _dot(a, b):
    return jnp.dot(a, b, preferred_element_type=F32)


def _dot_hi(a, b):
    return jnp.dot(a, b, preferred_element_type=F32, precision=lax.Precision.HIGHEST)


def _mod_kernel(c_ref, w_ref, b_ref, o_ref):
    cond = c_ref[...]
    s = cond * jax.nn.sigmoid(cond)
    o_ref[0] = _dot_hi(s, w_ref[0]) + b_ref[0]


def modulation_all(cond8, ada_w, ada_b):
    tn = 1024
    return pl.pallas_call(
        _mod_kernel,
        grid=(DEPTH, 3 * D_MODEL // tn),
        in_specs=[pl.BlockSpec((8, D_MODEL), lambda l, j: (0, 0)),
                  pl.BlockSpec((1, D_MODEL, tn), lambda l, j: (l, 0, j)),
                  pl.BlockSpec((1, 1, tn), lambda l, j: (l, 0, j))],
        out_specs=pl.BlockSpec((1, 8, tn), lambda l, j: (l, 0, j)),
        out_shape=jax.ShapeDtypeStruct((DEPTH, 8, 3 * D_MODEL), F32),
        compiler_params=_params(("arbitrary", "arbitrary")),
        name="modulation",
    )(cond8, ada_w, ada_b.reshape(DEPTH, 1, 3 * D_MODEL))


def _inproj_kernel(x_ref, mod_ref, g_ref, w_ref, o_ref, h_ref):
    @pl.when(pl.program_id(2) == 0)
    def _():
        x = x_ref[0]
        y = x * lax.rsqrt(jnp.mean(x * x, axis=-1, keepdims=True) + EPS) * g_ref[...]
        sh = mod_ref[0, :, 0:D_MODEL]
        sc = mod_ref[0, :, D_MODEL:2 * D_MODEL]
        h_ref[...] = (y * (1.0 + sc) + sh).astype(BF16)

    o_ref[0] = _dot(h_ref[...], w_ref[...]).astype(BF16)


def in_projection(x, mod_rows, row_of_batch, g, w):
    bsz, n, _ = x.shape
    tm = min(n, 1024)
    tn = COL_END // 4
    return pl.pallas_call(
        _inproj_kernel,
        grid=(bsz, n // tm, COL_END // tn),
        in_specs=[pl.BlockSpec((1, tm, D_MODEL), lambda b, i, j: (b, i, 0)),
                  pl.BlockSpec((1, 1, 3 * D_MODEL), lambda b, i, j: (row_of_batch(b), 0, 0)),
                  pl.BlockSpec((1, D_MODEL), lambda b, i, j: (0, 0)),
                  pl.BlockSpec((D_MODEL, tn), lambda b, i, j: (0, j))],
        out_specs=pl.BlockSpec((1, tm, tn), lambda b, i, j: (b, i, j)),
        out_shape=jax.ShapeDtypeStruct((bsz, n, COL_END), BF16),
        scratch_shapes=[pltpu.VMEM((tm, D_MODEL), BF16)],
        compiler_params=_params(("arbitrary", "arbitrary", "arbitrary")),
        name="in_projection",
    )(x, mod_rows, g.reshape(1, D_MODEL), w)


def _rope_kernel(zq_ref, zk_ref, cos_ref, sin_ref, q_ref, k_ref):
    lane = lax.broadcasted_iota(jnp.int32, cos_ref.shape, 1)
    low = (lane % (2 * ROPE_NF)) < ROPE_NF
    cs = cos_ref[...]
    sn = sin_ref[...]

    def rot(x):
        partner = jnp.where(low, pltpu.roll(x, LANES - ROPE_NF, axis=1), pltpu.roll(x, ROPE_NF, axis=1))
        return (x * cs + partner * sn).astype(BF16)

    for h in range(DA_HEADS):
        cols = slice(h * LANES, (h + 1) * LANES)
        q_ref[0, :, cols] = rot(zq_ref[0, :, cols].astype(F32) * QK_SCALE)
        k_ref[0, :, cols] = rot(zk_ref[0, :, cols].astype(F32))


def rope_tables(n):
    pos = jnp.arange(n)
    row = (pos // GRID_W).astype(F32)
    col = (pos % GRID_W).astype(F32)
    inv = ROPE_BASE ** (-jnp.arange(ROPE_NF, dtype=F32) / ROPE_NF)
    ar = row[:, None] * inv
    ac = col[:, None] * inv
    cos64 = jnp.concatenate([jnp.cos(ar), jnp.cos(ar), jnp.cos(ac), jnp.cos(ac)], axis=1)
    sin64 = jnp.concatenate([-jnp.sin(ar), jnp.sin(ar), -jnp.sin(ac), jnp.sin(ac)], axis=1)
    return jnp.tile(cos64, (1, 2)), jnp.tile(sin64, (1, 2))


def rope_qk(z, cos_t, sin_t):
    bsz, n, _ = z.shape
    tr = 1024
    w = DA_HEADS * LANES
    qb, kb = COL["Q"] // w, COL["K"] // w
    out = jax.ShapeDtypeStruct((bsz, n, w), BF16)
    return pl.pallas_call(
        _rope_kernel,
        grid=(bsz, n // tr),
        in_specs=[pl.BlockSpec((1, tr, w), lambda b, i: (b, i, qb)),
                  pl.BlockSpec((1, tr, w), lambda b, i: (b, i, kb)),
                  pl.BlockSpec((tr, LANES), lambda b, i: (i, 0)),
                  pl.BlockSpec((tr, LANES), lambda b, i: (i, 0))],
        out_specs=[pl.BlockSpec((1, tr, w), lambda b, i: (b, i, 0)),
                   pl.BlockSpec((1, tr, w), lambda b, i: (b, i, 0))],
        out_shape=[out, out],
        compiler_params=_params(("arbitrary", "arbitrary")),
        name="rope_qk",
    )(z, z, cos_t, sin_t)


def _attn_kernel(*refs, lam_init, n_lat, ck, prescaled):
    if n_lat:
        lam_ref, g_ref, q_ref, kc_ref, vc_ref, k_ref, v_ref, o_ref, vt_ref = refs
    else:
        lam_ref, g_ref, q_ref, kc_ref, vc_ref, o_ref, vt_ref = refs
    tq = q_ref.shape[1]
    n_ctx = kc_ref.shape[1]

    @pl.when(pl.program_id(2) == 0)
    def _():
        vt_ref[0:DA_VDIM, 0:n_ctx] = vc_ref[0].astype(F32).T.astype(BF16)
        for c in range(n_lat // ck):
            vt_ref[0:DA_VDIM, n_ctx + c * ck:n_ctx + (c + 1) * ck] = (
                v_ref[0, c * ck:(c + 1) * ck, :].astype(F32).T.astype(BF16))
        pad = vt_ref.shape[0] - DA_VDIM
        row = lax.broadcasted_iota(jnp.int32, (pad, vt_ref.shape[1]), 0)
        vt_ref[DA_VDIM:, :] = (row == 0).astype(BF16)

    lp = lam_ref[...]
    lam = (jnp.exp(jnp.sum(lp[0:1] * lp[1:2], axis=1, keepdims=True))
           - jnp.exp(jnp.sum(lp[2:3] * lp[3:4], axis=1, keepdims=True)) + lam_init)

    q = q_ref[0]
    if not prescaled:
        q = (q.astype(F32) * QK_SCALE).astype(BF16)
    lane = lax.broadcasted_iota(jnp.int32, q.shape, 1)
    zero = jnp.zeros_like(q)
    qq = jnp.concatenate([jnp.where(lane < DA_SUB, q, zero), jnp.where(lane >= DA_SUB, q, zero)], axis=0)

    chunks = [(lambda: kc_ref[0], 0, n_ctx)]
    for c in range(n_lat // ck):
        chunks.append((lambda c=c: k_ref[0, c * ck:(c + 1) * ck, :], n_ctx + c * ck, ck))

    def scores_t(c):
        return lax.dot_general(chunks[c][0](), qq, (((1,), (1,)), ((), ())), preferred_element_type=F32)

    def pv_t(p_t, c):
        _, off, width = chunks[c]
        return _dot(vt_ref[:, off:off + width], p_t)

    m = jnp.full((1, 2 * tq), -jnp.inf, F32)
    acc = jnp.zeros((vt_ref.shape[0], 2 * tq), F32)
    pending = None
    s_next = scores_t(0)
    for c in range(len(chunks)):
        s = s_next
        if c + 1 < len(chunks):
            s_next = scores_t(c + 1)
        part = s[0:64]
        for r in range(64, s.shape[0], 64):
            part = jnp.maximum(part, s[r:r + 64])
        m_new = jnp.maximum(m, jnp.max(part, axis=0, keepdims=True))
        alpha = jnp.exp2(m - m_new)
        p_t = jnp.exp2(s - m_new).astype(BF16)
        m = m_new
        if pending is not None:
            p_prev, alpha_prev, c_prev = pending
            acc = acc * alpha_prev + pv_t(p_prev, c_prev)
        pending = (p_t, alpha, c)
    p_prev, alpha_prev, c_prev = pending
    acc = acc * alpha_prev + pv_t(p_prev, c_prev)

    o_t = acc[0:DA_VDIM, :] / acc[DA_VDIM:DA_VDIM + 1, :]
    d = (o_t[:, 0:tq] - lam * o_t[:, tq:]).T
    y = d * lax.rsqrt(jnp.mean(d * d, axis=-1, keepdims=True) + EPS) * g_ref[...]
    o_ref[0] = (y * (1.0 - lam_init)).astype(BF16)


def diff_attention(q_arr, q_col, zc, lat, lam_p, subln, lam_init, prescaled):
    bsz, nq, _ = q_arr.shape
    n_ctx = zc.shape[1]
    tq = min(nq, 512)
    kcb, vcb = COL["K"] // LANES, COL["V"] // LANES
    in_specs = [pl.BlockSpec((4, DA_SUB), lambda b, h, i: (0, 0)),
                pl.BlockSpec((1, DA_VDIM), lambda b, h, i: (0, 0)),
                pl.BlockSpec((1, tq, LANES), lambda b, h, i: (b, i, q_col + h)),
                pl.BlockSpec((1, n_ctx, LANES), lambda b, h, i: (b, 0, kcb + h)),
                pl.BlockSpec((1, n_ctx, LANES), lambda b, h, i: (b, 0, vcb + h))]
    args = [lam_p, subln.reshape(1, DA_VDIM), q_arr, zc, zc]
    n_lat = 0
    if lat is not None:
        k_arr, k_col, v_arr, v_col = lat
        n_lat = k_arr.shape[1]
        in_specs += [pl.BlockSpec((1, n_lat, LANES), lambda b, h, i: (b, 0, k_col + h)),
                     pl.BlockSpec((1, n_lat, LANES), lambda b, h, i: (b, 0, v_col + h))]
        args += [k_arr, v_arr]
    ones_rows = 16
    scratch = [pltpu.VMEM((DA_VDIM + ones_rows, n_ctx + n_lat), BF16)]
    return pl.pallas_call(
        functools.partial(_attn_kernel, lam_init=lam_init, n_lat=n_lat, ck=512, prescaled=prescaled),
        grid=(bsz, DA_HEADS, nq // tq),
        in_specs=in_specs,
        out_specs=pl.BlockSpec((1, tq, LANES), lambda b, h, i: (b, i, h)),
        out_shape=jax.ShapeDtypeStruct((bsz, nq, BR_WIDTH), BF16),
        scratch_shapes=scratch,
        compiler_params=_params(("arbitrary", "arbitrary", "arbitrary")),
        name="diff_attention",
    )(*args)


def hyena_positions(L):
    t = jnp.linspace(0.0, 1.0, L, dtype=F32)[:, None]
    wpos = ((2.0 * math.pi / L) * jnp.arange(L, dtype=F32))[:, None]
    bands = jnp.linspace(1e-4, HY_BANDS - 1, HY_BANDS, dtype=F32)[None, :]
    fwd = jnp.concatenate([t, jnp.cos(bands * wpos), -jnp.sin(bands * wpos)], axis=-1)
    emb = jnp.concatenate([fwd, fwd[0:1], jnp.flip(fwd[1:], axis=0)], axis=0)
    mask = (jnp.arange(2 * L) != L).astype(F32)[:, None]
    pad = jnp.zeros((2 * L, LANES - HY_EMB - 1), F32)
    return jnp.concatenate([emb, pad, mask], axis=-1)


def _filter_kernel(feat_ref, w1_ref, b1_ref, w2_ref, b2_ref, w3_ref, fr_ref, dl_ref, k_ref, ss_ref):
    feat = feat_ref[...]
    hid = jnp.sin(fr_ref[0:1, :] * (_dot_hi(feat, w1_ref[...]) + b1_ref[...]))
    hid = jnp.sin(fr_ref[1:2, :] * (_dot_hi(hid, w2_ref[...]) + b2_ref[...]))
    h = _dot_hi(hid, w3_ref[...])
    t = feat[:, 0:1]
    mask = feat[:, LANES - 1:LANES]
    k = h * jnp.exp(-t * dl_ref[...]) * mask
    k_ref[...] = k

    @pl.when(pl.program_id(0) == 0)
    def _():
        ss_ref[...] = jnp.zeros_like(ss_ref)

    ss_ref[...] += jnp.sum(k * k, axis=0, keepdims=True)


def hyena_filter_taps(feat, w1, b1, w2, b2, w3, freq):
    two_l = feat.shape[0]
    tr = min(two_l // 2, 1024)
    nb = two_l // tr
    w1p = jnp.zeros((LANES, HY_HIDDEN), F32).at[:HY_EMB].set(w1)
    deltas = jnp.abs(jnp.linspace(HY_MIN_DECAY, HY_MAX_DECAY, HY_WIDTH, dtype=F32))[None, :]
    const = lambda i: (0, 0)
    return pl.pallas_call(
        _filter_kernel,
        grid=(nb,),
        in_specs=[pl.BlockSpec((tr, LANES), lambda i: (i, 0)),
                  pl.BlockSpec((LANES, HY_HIDDEN), const),
                  pl.BlockSpec((1, HY_HIDDEN), const),
                  pl.BlockSpec((HY_HIDDEN, HY_HIDDEN), const),
                  pl.BlockSpec((1, HY_HIDDEN), const),
                  pl.BlockSpec((HY_HIDDEN, HY_WIDTH), lambda i: (0, (2 * i) // nb)),
                  pl.BlockSpec((2, HY_HIDDEN), const),
                  pl.BlockSpec((1, HY_WIDTH), const)],
        out_specs=[pl.BlockSpec((tr, HY_WIDTH), lambda i: (i, 0)),
                   pl.BlockSpec((1, HY_WIDTH), const)],
        out_shape=[jax.ShapeDtypeStruct((two_l, HY_WIDTH), F32),
                   jax.ShapeDtypeStruct((1, HY_WIDTH), F32)],
        compiler_params=_params(("arbitrary",)),
        name="hyena_filter",
    )(feat, w1p, b1.reshape(1, -1), w2, b2.reshape(1, -1), w3, freq, deltas)


def _short_conv_kernel(z_ref, prev_ref, next_ref, w_ref, b_ref, x0_ref, uv_ref):
    i = pl.program_id(1)
    last = pl.num_programs(1) - 1
    z = z_ref[0].astype(F32)
    tr = z.shape[0]
    halo = prev_ref.shape[1]
    before = jnp.where(i == 0, 0.0, prev_ref[0, halo - 1:halo, :].astype(F32))
    after = jnp.where(i == last, 0.0, next_ref[0, 0:1, :].astype(F32))
    row = lax.broadcasted_iota(jnp.int32, z.shape, 0)
    zm = jnp.where(row == 0, before, pltpu.roll(z, 1, axis=0))
    zp = jnp.where(row == tr - 1, after, pltpu.roll(z, tr - 1, axis=0))
    y = zm * w_ref[0:1, :] + z * w_ref[1:2, :] + zp * w_ref[2:3, :] + b_ref[...]
    x0_ref[0] = y[:, 0:HY_WIDTH].astype(BF16)
    uv_ref[0] = (y[:, HY_WIDTH:2 * HY_WIDTH] * y[:, 2 * HY_WIDTH:]).astype(BF16)


def hyena_short_conv(z, sw, sb):
    bsz, n, _ = z.shape
    tr = min(n, 512)
    halo = 16
    w3c = 3 * HY_WIDTH
    cb = COL["HY"] // w3c
    nh = n // halo
    out = jax.ShapeDtypeStruct((bsz, n, HY_WIDTH), BF16)
    return pl.pallas_call(
        _short_conv_kernel,
        grid=(bsz, n // tr),
        in_specs=[pl.BlockSpec((1, tr, w3c), lambda b, i: (b, i, cb)),
                  pl.BlockSpec((1, halo, w3c), lambda b, i: (b, jnp.maximum(i * (tr // halo) - 1, 0), cb)),
                  pl.BlockSpec((1, halo, w3c), lambda b, i: (b, jnp.minimum((i + 1) * (tr // halo), nh - 1), cb)),
                  pl.BlockSpec((3, w3c), lambda b, i: (0, 0)),
                  pl.BlockSpec((1, w3c), lambda b, i: (0, 0))],
        out_specs=[pl.BlockSpec((1, tr, HY_WIDTH), lambda b, i: (b, i, 0)),
                   pl.BlockSpec((1, tr, HY_WIDTH), lambda b, i: (b, i, 0))],
        out_shape=[out, out],
        compiler_params=_params(("arbitrary", "arbitrary")),
        name="hyena_short_conv",
    )(z, z, z, sw, sb.reshape(1, w3c))


def dft_tables():
    r = DFT_R
    n_fft = r * r
    idx = jnp.arange(r, dtype=jnp.int32)
    prod = idx[:, None] * idx[None, :]
    ang = (2.0 * math.pi / r) * (prod % r).astype(F32)
    f_re, f_im = jnp.cos(ang), -jnp.sin(ang)
    ang = (2.0 * math.pi / n_fft) * prod.astype(F32)
    t_re, t_im = jnp.cos(ang), -jnp.sin(ang)

    def cmul(a_re, a_im, b_re, b_im):
        return a_re * b_re - a_im * b_im, a_re * b_im + a_im * b_re

    g_re, g_im = cmul(f_re[None, :, :], f_im[None, :, :], t_re[:, None, :], t_im[:, None, :])
    gh_re, gh_im = cmul(f_re[None, :, :], -f_im[None, :, :], t_re[:, :, None], -t_im[:, :, None])
    return tuple(a.astype(BF16) for a in (f_re, f_im, g_re, g_im, gh_re, gh_im))


def _fft1_kernel(*refs, n_in):
    if n_in == 2:
        x_ref, fre_ref, fim_ref, are_ref, aim_ref = refs
        x0 = x_ref[0]
        x1 = x_ref[1]
        fre, fim = fre_ref[...], fim_ref[...]
        are_ref[...] = (_dot(fre, x0) - _dot(fim, x1)).astype(BF16)
        aim_ref[...] = (_dot(fre, x1) + _dot(fim, x0)).astype(BF16)
    else:
        x_ref, fre_ref, fim_ref, are_ref, aim_ref = refs
        x = x_ref[...].astype(BF16)
        are_ref[...] = _dot(fre_ref[...], x).astype(BF16)
        aim_ref[...] = _dot(fim_ref[...], x).astype(BF16)


def fft_level1(x, f_re, f_im):
    r = DFT_R
    cols = x.shape[-1]
    tc = 8192
    if x.ndim == 3:
        n_in, kdim = 2, x.shape[1]
        x_spec = pl.BlockSpec((2, kdim, tc), lambda j: (0, 0, j))
    else:
        n_in, kdim = 1, x.shape[0]
        x_spec = pl.BlockSpec((kdim, tc), lambda j: (0, j))
    out = jax.ShapeDtypeStruct((r, cols), BF16)
    return pl.pallas_call(
        functools.partial(_fft1_kernel, n_in=n_in),
        grid=(cols // tc,),
        in_specs=[x_spec,
                  pl.BlockSpec((r, kdim), lambda j: (0, 0)),
                  pl.BlockSpec((r, kdim), lambda j: (0, 0))],
        out_specs=[pl.BlockSpec((r, tc), lambda j: (0, j)), pl.BlockSpec((r, tc), lambda j: (0, j))],
        out_shape=[out, out],
        compiler_params=_params(("arbitrary",)),
        name="fft_level1",
    )(x, f_re[:, :kdim], f_im[:, :kdim])


def _fft2_filter_kernel(are_ref, aim_ref, gre_ref, gim_ref, sc_ref, kre_ref, kim_ref):
    sc = sc_ref[...]
    for j in range(are_ref.shape[0]):
        a_re, a_im = are_ref[j], aim_ref[j]
        g_re, g_im = gre_ref[j], gim_ref[j]
        kre_ref[j] = (_dot(g_re, a_re) - _dot(g_im, a_im)) * sc
        kim_ref[j] = (_dot(g_re, a_im) + _dot(g_im, a_re)) * sc


def fft_level2_filter(a_re, a_im, g_re, g_im, scale):
    r = DFT_R
    c = a_re.shape[-1]
    tk = 8
    blk_a = pl.BlockSpec((tk, r, c), lambda i: (i, 0, 0))
    blk_g = pl.BlockSpec((tk, r, r), lambda i: (i, 0, 0))
    out = jax.ShapeDtypeStruct((r, r, c), F32)
    return pl.pallas_call(
        _fft2_filter_kernel,
        grid=(r // tk,),
        in_specs=[blk_a, blk_a, blk_g, blk_g, pl.BlockSpec((1, c), lambda i: (0, 0))],
        out_specs=[blk_a, blk_a],
        out_shape=[out, out],
        compiler_params=_params(("arbitrary",)),
        name="fft_level2_filter",
    )(a_re, a_im, g_re, g_im, scale)


def _fft2_conv_kernel(are_ref, aim_ref, gre_ref, gim_ref, hre_ref, him_ref, kre_ref, kim_ref, bre_ref, bim_ref):
    for j in range(are_ref.shape[0]):
        a_re, a_im = are_ref[j], aim_ref[j]
        g_re, g_im = gre_ref[j], gim_ref[j]
        x_re = _dot(g_re, a_re) - _dot(g_im, a_im)
        x_im = _dot(g_re, a_im) + _dot(g_im, a_re)
        k_re, k_im = kre_ref[j], kim_ref[j]
        y_re = (x_re * k_re - x_im * k_im).astype(BF16)
        y_im = (x_re * k_im + x_im * k_re).astype(BF16)
        h_re, h_im = hre_ref[j], him_ref[j]
        bre_ref[j] = (_dot(h_re, y_re) - _dot(h_im, y_im)).astype(BF16)
        bim_ref[j] = (_dot(h_re, y_im) + _dot(h_im, y_re)).astype(BF16)


def fft_level2_conv(a_re, a_im, g_re, g_im, gh_re, gh_im, kf_re, kf_im):
    r = DFT_R
    c = a_re.shape[-1]
    tk = 8
    blk_a = pl.BlockSpec((tk, r, c), lambda i: (i, 0, 0))
    blk_g = pl.BlockSpec((tk, r, r), lambda i: (i, 0, 0))
    out = jax.ShapeDtypeStruct((r, r, c), BF16)
    return pl.pallas_call(
        _fft2_conv_kernel,
        grid=(r // tk,),
        in_specs=[blk_a, blk_a, blk_g, blk_g, blk_g, blk_g, blk_a, blk_a],
        out_specs=[blk_a, blk_a],
        out_shape=[out, out],
        compiler_params=_params(("arbitrary",)),
        name="fft_level2_conv",
    )(a_re, a_im, g_re, g_im, gh_re, gh_im, kf_re, kf_im)


def _ifft1_kernel(bre_ref, bim_ref, fre_ref, fim_ref, uv_ref, x0_ref, bias_ref, o_ref):
    fre, fim = fre_ref[...], fim_ref[...]
    b_re, b_im = bre_ref[...], bim_ref[...]
    y0 = _dot(fre, b_re) + _dot(fim, b_im)
    y1 = _dot(fre, b_im) - _dot(fim, b_re)
    bias = bias_ref[...]
    o_ref[0] = ((y0 + uv_ref[0].astype(F32) * bias) * x0_ref[0].astype(F32)).astype(BF16)
    o_ref[1] = ((y1 + uv_ref[1].astype(F32) * bias) * x0_ref[1].astype(F32)).astype(BF16)


def ifft_level1(b_re, b_im, f_re, f_im, uv, x0, bias_t):
    r = DFT_R
    half = r // 2
    cols = b_re.shape[-1]
    tc = 8192
    blk_b = pl.BlockSpec((r, tc), lambda j: (0, j))
    blk_f = pl.BlockSpec((half, r), lambda j: (0, 0))
    blk_x = pl.BlockSpec((2, half, tc), lambda j: (0, 0, j))
    return pl.pallas_call(
        _ifft1_kernel,
        grid=(cols // tc,),
        in_specs=[blk_b, blk_b, blk_f, blk_f, blk_x, blk_x, pl.BlockSpec((1, tc), lambda j: (0, 0))],
        out_specs=blk_x,
        out_shape=jax.ShapeDtypeStruct((2, half, cols), BF16),
        compiler_params=_params(("arbitrary",)),
        name="ifft_level1",
    )(b_re, b_im, f_re[:half], f_im[:half], uv, x0, bias_t)


def hyena_long_conv(uv, x0, taps, sumsq, bias, tables):
    f_re, f_im, g_re, g_im, gh_re, gh_im = tables
    r = DFT_R
    bsz, L, c = uv.shape
    assert bsz == 2 and 2 * L == r * r
    scale = lax.rsqrt(sumsq) * (1.0 / (r * r))
    ka_re, ka_im = fft_level1(taps.reshape(r, r * c), f_re, f_im)
    kf_re, kf_im = fft_level2_filter(ka_re.reshape(r, r, c), ka_im.reshape(r, r, c), g_re, g_im, scale)
    uv2 = uv.reshape(2, r // 2, r * c)
    a_re, a_im = fft_level1(uv2, f_re, f_im)
    b_re, b_im = fft_level2_conv(a_re.reshape(r, r, c), a_im.reshape(r, r, c), g_re, g_im, gh_re, gh_im,
                                 kf_re, kf_im)
    bias_t = jnp.tile(bias.reshape(1, c), (1, 8192 // c))
    y = ifft_level1(b_re.reshape(r, r * c), b_im.reshape(r, r * c), f_re, f_im, uv2,
                    x0.reshape(2, r // 2, r * c), bias_t)
    return y.reshape(2, L, c)


def dft_small_tables(n_fft):
    idx = jnp.arange(n_fft, dtype=jnp.int32)
    ang = (2.0 * math.pi / n_fft) * ((idx[:, None] * idx[None, :]) % n_fft).astype(F32)
    return jnp.cos(ang).astype(BF16), (-jnp.sin(ang)).astype(BF16)


def _conv_small_kernel(taps_ref, ss_ref, fre_ref, fim_ref, uv_ref, x0_ref, bias_ref, o_ref):
    two_l = taps_ref.shape[0]
    L = two_l // 2
    fre, fim = fre_ref[...], fim_ref[...]
    scale = lax.rsqrt(ss_ref[...]) * (1.0 / two_l)
    taps = taps_ref[...].astype(BF16)
    k_re = _dot(fre, taps) * scale
    k_im = _dot(fim, taps) * scale
    u0, u1 = uv_ref[0], uv_ref[1]
    fre_l, fim_l = fre[:, :L], fim[:, :L]
    x_re = _dot(fre_l, u0) - _dot(fim_l, u1)
    x_im = _dot(fre_l, u1) + _dot(fim_l, u0)
    y_re = (x_re * k_re - x_im * k_im).astype(BF16)
    y_im = (x_re * k_im + x_im * k_re).astype(BF16)
    fre_t, fim_t = fre[:L, :], fim[:L, :]
    y0 = _dot(fre_t, y_re) + _dot(fim_t, y_im)
    y1 = _dot(fre_t, y_im) - _dot(fim_t, y_re)
    bias = bias_ref[...]
    o_ref[0] = ((y0 + u0.astype(F32) * bias) * x0_ref[0].astype(F32)).astype(BF16)
    o_ref[1] = ((y1 + u1.astype(F32) * bias) * x0_ref[1].astype(F32)).astype(BF16)


def hyena_long_conv_small(uv, x0, taps, sumsq, bias, tables):
    bsz, L, c = uv.shape
    assert bsz == 2
    f_re, f_im = tables
    full = lambda shape: pl.BlockSpec(shape, lambda i: (0,) * len(shape))
    return pl.pallas_call(
        _conv_small_kernel,
        grid=(1,),
        in_specs=[full((2 * L, c)), full((1, c)), full((2 * L, 2 * L)), full((2 * L, 2 * L)),
                  full((2, L, c)), full((2, L, c)), full((1, c))],
        out_specs=full((2, L, c)),
        out_shape=jax.ShapeDtypeStruct((2, L, c), BF16),
        compiler_params=_params(("arbitrary",)),
        name="hyena_conv_small",
    )(taps, sumsq, f_re, f_im, uv, x0, bias.reshape(1, c))


def _gmlp_kernel(z_ref, g_ref, b_ref, ws_ref, bs_ref, o_ref):
    zg = z_ref[0].astype(F32)
    gl = 0.5 * zg * (1.0 + lax.erf(zg * (2.0 ** -0.5)))
    w = BR_WIDTH
    u = gl[:, :w]
    v = gl[:, w:]
    mu = jnp.mean(v, axis=-1, keepdims=True)
    var = jnp.mean(jnp.square(v - mu), axis=-1, keepdims=True)
    v = ((v - mu) * lax.rsqrt(var + EPS) * g_ref[...] + b_ref[...]).astype(BF16)
    gw = w // GM_GROUPS
    lane = lax.broadcasted_iota(jnp.int32, (GM_CHUNK, LANES), 1)
    first = lane < gw
    for ci in range(zg.shape[0] // GM_CHUNK):
        rows = slice(ci * GM_CHUNK, (ci + 1) * GM_CHUNK)
        tiles = []
        for t in range(w // LANES):
            vt = v[rows, t * LANES:(t + 1) * LANES]
            tiles.append(jnp.where(first, _dot(ws_ref[2 * t], vt), _dot(ws_ref[2 * t + 1], vt)))
        vm = jnp.concatenate(tiles, axis=1) + bs_ref[...]
        o_ref[0, rows, :] = (u[rows] * vm).astype(BF16)


def gmlp(z, ln_g, ln_b, ws, bs):
    bsz, n, _ = z.shape
    tr = min(n, 512)
    w = BR_WIDTH
    cb = COL["GM"] // (2 * w)
    bs_full = jnp.repeat(bs.T, w // GM_GROUPS, axis=1)
    return pl.pallas_call(
        _gmlp_kernel,
        grid=(bsz, n // tr),
        in_specs=[pl.BlockSpec((1, tr, 2 * w), lambda b, i: (b, i, cb)),
                  pl.BlockSpec((1, w), lambda b, i: (0, 0)),
                  pl.BlockSpec((1, w), lambda b, i: (0, 0)),
                  pl.BlockSpec((GM_GROUPS, GM_CHUNK, GM_CHUNK), lambda b, i: (0, 0, 0)),
                  pl.BlockSpec((GM_CHUNK, w), lambda b, i: (0, 0))],
        out_specs=pl.BlockSpec((1, tr, w), lambda b, i: (b, i, 0)),
        out_shape=jax.ShapeDtypeStruct((bsz, n, w), BF16),
        compiler_params=_params(("arbitrary", "arbitrary")),
        name="gmlp",
    )(z, ln_g.reshape(1, w), ln_b.reshape(1, w), ws.astype(BF16), bs_full)


def _merge_kernel(ya_ref, yb_ref, yc_ref, ga_ref, gb_ref, gc_ref, mg_ref, x_ref, mod_ref, np_ref,
                  wb_ref, wo_ref, o_ref):
    acc = None
    for i, (y_ref, g_ref) in enumerate(((ya_ref, ga_ref), (yb_ref, gb_ref), (yc_ref, gc_ref))):
        g = g_ref[0].astype(F32)
        gated = (y_ref[0].astype(F32) * (g * jax.nn.sigmoid(g))).astype(BF16)
        sel = jax.nn.sigmoid(mg_ref[0, :, i * D_MODEL:(i + 1) * D_MODEL].astype(F32))
        term = sel * _dot(gated, wb_ref[i])
        acc = term if acc is None else acc + term
    out = _dot(acc.astype(BF16), wo_ref[...])
    r = out * lax.rsqrt(jnp.mean(out * out, axis=-1, keepdims=True) + EPS) * np_ref[...]
    gt = mod_ref[0, :, 2 * D_MODEL:]
    o_ref[0] = x_ref[0] + gt * r


def merge_out(ya, yb, yc, z, x, mod_rows, row_of_batch, npost, wb, wo):
    bsz, n, _ = x.shape
    tm = min(n, 512)
    w = BR_WIDTH
    yspec = pl.BlockSpec((1, tm, w), lambda b, i: (b, i, 0))

    def zspec(name):
        cb = COL[name] // w
        return pl.BlockSpec((1, tm, w), lambda b, i: (b, i, cb))

    return pl.pallas_call(
        _merge_kernel,
        grid=(bsz, n // tm),
        in_specs=[yspec, yspec, yspec, zspec("GA"), zspec("GB"), zspec("GC"),
                  pl.BlockSpec((1, tm, 3 * D_MODEL), lambda b, i: (b, i, 0)),
                  pl.BlockSpec((1, tm, D_MODEL), lambda b, i: (b, i, 0)),
                  pl.BlockSpec((1, 1, 3 * D_MODEL), lambda b, i: (row_of_batch(b), 0, 0)),
                  pl.BlockSpec((1, D_MODEL), lambda b, i: (0, 0)),
                  pl.BlockSpec((3, w, D_MODEL), lambda b, i: (0, 0, 0)),
                  pl.BlockSpec((D_MODEL, D_MODEL), lambda b, i: (0, 0))],
        out_specs=pl.BlockSpec((1, tm, D_MODEL), lambda b, i: (b, i, 0)),
        out_shape=jax.ShapeDtypeStruct((bsz, n, D_MODEL), F32),
        compiler_params=_params(("arbitrary", "arbitrary")),
        name="merge_out",
    )(ya, yb, yc, z, z, z, z, x, mod_rows, npost.reshape(1, D_MODEL), wb, wo)


def _permute_cols(w):
    return jnp.concatenate([w[:, _REF_COLS[nm][0]:_REF_COLS[nm][0] + _REF_COLS[nm][1]] for nm in _NEW_ORDER],
                           axis=1)


def kernel(x, c, ctx, c_ctx, ada_w, ada_b, norm_pre, norm_post, w_in, da_lambda, da_subln, hy_short_w,
           hy_short_b, hy_f_w1, hy_f_b1, hy_f_w2, hy_f_b2, hy_f_w3, hy_f_freq, hy_bias, gm_ln_g, gm_ln_b,
           gm_ws, gm_bs, w_branch, w_out):
    bsz, n, _ = x.shape
    n_ctx = ctx.shape[1]
    assert bsz == 2 and 2 * n == DFT_R * DFT_R

    cond8 = jnp.zeros((8, D_MODEL), F32).at[0:bsz].set(c).at[bsz].set(c_ctx)
    mod = modulation_all(cond8, ada_w, ada_b)
    lat_row = lambda b: b
    ctx_row = lambda b: bsz

    cos_t, sin_t = rope_tables(n)
    feat = hyena_positions(n)
    feat_c = hyena_positions(n_ctx)
    tables = dft_tables()
    tables_c = dft_small_tables(2 * n_ctx)
    kb, vb, qb = COL["K"] // LANES, COL["V"] // LANES, COL["Q"] // LANES

    xc = ctx
    for l in range(DEPTH):
        last = l == DEPTH - 1
        lam_init = 0.8 - 0.6 * math.exp(-0.3 * l)
        mod_rows = mod[l].reshape(8, 1, 3 * D_MODEL)
        w_l = _permute_cols(w_in[l]).astype(BF16)
        wb_l = w_branch[l].astype(BF16)
        wo_l = w_out[l].astype(BF16)
        filt_w = (hy_f_w1[l], hy_f_b1[l], hy_f_w2[l], hy_f_b2[l], hy_f_w3[l], hy_f_freq[l])

        z = in_projection(x, mod_rows, lat_row, norm_pre[l], w_l)
        zc = in_projection(xc, mod_rows, ctx_row, norm_pre[l], w_l)

        qr, kr = rope_qk(z, cos_t, sin_t)
        y_a = diff_attention(qr, 0, zc, (kr, 0, z, vb), da_lambda[l], da_subln[l], lam_init, True)
        taps, sumsq = hyena_filter_taps(feat, *filt_w)
        x0, uv = hyena_short_conv(z, hy_short_w[l], hy_short_b[l])
        y_b = hyena_long_conv(uv, x0, taps, sumsq, hy_bias[l], tables)
        y_c = gmlp(z, gm_ln_g[l], gm_ln_b[l], gm_ws[l], gm_bs[l])
        x_new = merge_out(y_a, y_b, y_c, z, x, mod_rows, lat_row, norm_post[l], wb_l, wo_l)

        if not last:
            yc_a = diff_attention(zc, qb, zc, None, da_lambda[l], da_subln[l], lam_init, False)
            taps_c, sumsq_c = hyena_filter_taps(feat_c, *filt_w)
            x0c, uvc = hyena_short_conv(zc, hy_short_w[l], hy_short_b[l])
            yc_b = hyena_long_conv_small(uvc, x0c, taps_c, sumsq_c, hy_bias[l], tables_c)
            yc_c = gmlp(zc, gm_ln_g[l], gm_ln_b[l], gm_ws[l], gm_bs[l])
            xc = merge_out(yc_a, yc_b, yc_c, zc, xc, mod_rows, ctx_row, norm_post[l], wb_l, wo_l)
        x = x_new
    return x
```

```python
import functools
import math

import jax
import jax.numpy as jnp
import numpy as np
from jax import lax
from jax.experimental import pallas as pl
from jax.experimental.pallas import tpu as pltpu

F32 = jnp.float32
BF16 = jnp.bfloat16

D_MODEL = 1024
DEPTH = 4
GRID_W = 64
EPS = 1e-6
BR_WIDTH = 512
DA_SUB = 64
DA_VDIM = 128
DA_HEADS = 4
ROPE_BASE = 10000.0
ROPE_NF = 16
HY_WIDTH = 512
HY_BANDS = 16
HY_EMB = 33
HY_HIDDEN = 64
HY_MIN_DECAY = math.log(1e-2) / 1.5
HY_MAX_DECAY = math.log(1e-2) / 0.3
GM_GROUPS = 8
GM_CHUNK = 128

LANES = 128
DFT_R = 128
QK_SCALE = DA_SUB ** -0.5 * math.log2(math.e)

_REF_COLS = dict(K=(0, 512), V=(512, 512), Q=(1024, 512), GA=(1536, 512), HY=(2048, 1536),
                 GB=(3584, 512), GM=(4096, 1024), GC=(5120, 512), MG=(5632, 3072))
_NEW_ORDER = ("MG", "HY", "K", "GM", "V", "Q", "GA", "GB", "GC")
COL = {}
_off = 0
for _name in _NEW_ORDER:
    COL[_name] = _off
    _off += _REF_COLS[_name][1]
COL_END = _off
for _name in _NEW_ORDER:
    assert COL[_name] % _REF_COLS[_name][1] == 0

VMEM_LIMIT = 48 * 1024 * 1024
INPROJ_VMEM_LIMIT = 56 * 1024 * 1024


def _params(sem):
    return pltpu.CompilerParams(dimension_semantics=sem, vmem_limit_bytes=VMEM_LIMIT)


def _dot(a, b):
    return jnp.dot(a, b, preferred_element_type=F32)


def _dot_hi(a, b):
    return jnp.dot(a, b, preferred_element_type=F32, precision=lax.Precision.HIGHEST)


def _mod_kernel(c_ref, w_ref, b_ref, o_ref):
    cond = c_ref[...]
    s = cond * jax.nn.sigmoid(cond)
    o_ref[0] = _dot_hi(s, w_ref[0]) + b_ref[0]


def modulation_all(cond8, ada_w, ada_b):
    tn = 1024
    return pl.pallas_call(
        _mod_kernel,
        grid=(DEPTH, 3 * D_MODEL // tn),
        in_specs=[pl.BlockSpec((8, D_MODEL), lambda l, j: (0, 0)),
                  pl.BlockSpec((1, D_MODEL, tn), lambda l, j: (l, 0, j)),
                  pl.BlockSpec((1, 1, tn), lambda l, j: (l, 0, j))],
        out_specs=pl.BlockSpec((1, 8, tn), lambda l, j: (l, 0, j)),
        out_shape=jax.ShapeDtypeStruct((DEPTH, 8, 3 * D_MODEL), F32),
        compiler_params=_params(("arbitrary", "arbitrary")),
        name="modulation",
    )(cond8, ada_w, ada_b.reshape(DEPTH, 1, 3 * D_MODEL))


_QK_WIDTH = DA_HEADS * LANES
_INPROJ_CHUNKS = []
_edges = [0, 2048, 4096, COL["K"], COL["K"] + _QK_WIDTH, COL["Q"], COL["Q"] + _QK_WIDTH, COL_END]
for _a, _b in zip(_edges[:-1], _edges[1:]):
    assert _a % 256 == 0 and _b % 256 == 0
    _INPROJ_CHUNKS.append((_a, _b))


def _inproj_kernel(*refs, rope):
    if rope:
        x_ref, mod_ref, g_ref, w_ref, cos_ref, sin_ref, o_ref, q_ref, k_ref = refs
    else:
        x_ref, mod_ref, g_ref, w_ref, o_ref = refs
    x = x_ref[0]
    y = x * lax.rsqrt(jnp.mean(x * x, axis=-1, keepdims=True) + EPS) * g_ref[...]
    sh = mod_ref[0, :, 0:D_MODEL]
    sc = mod_ref[0, :, D_MODEL:2 * D_MODEL]
    h = (y * (1.0 + sc) + sh).astype(BF16)

    if rope:
        lane = lax.broadcasted_iota(jnp.int32, cos_ref.shape, 1)
        low = (lane % (2 * ROPE_NF)) < ROPE_NF
        cs = cos_ref[...]
        sn = sin_ref[...]

        def rot(v):
            partner = jnp.where(low, pltpu.roll(v, LANES - ROPE_NF, axis=1), pltpu.roll(v, ROPE_NF, axis=1))
            return (v * cs + partner * sn).astype(BF16)

    for a, b in _INPROJ_CHUNKS:
        zc = _dot(h, w_ref[:, a:b])
        o_ref[0, :, a:b] = zc.astype(BF16)
        if rope and a in (COL["K"], COL["Q"]):
            dst, scale = (k_ref, 1.0) if a == COL["K"] else (q_ref, QK_SCALE)
            for hd in range(DA_HEADS):
                cols = slice(hd * LANES, (hd + 1) * LANES)
                dst[0, :, cols] = rot(zc[:, cols] * scale)


def in_projection(x, mod_rows, row_of_batch, g, w, rope_tabs=None):
    bsz, n, _ = x.shape
    tm = min(n, 512)
    rope = rope_tabs is not None
    in_specs = [pl.BlockSpec((1, tm, D_MODEL), lambda b, i: (b, i, 0)),
                pl.BlockSpec((1, 1, 3 * D_MODEL), lambda b, i: (row_of_batch(b), 0, 0)),
                pl.BlockSpec((1, D_MODEL), lambda b, i: (0, 0)),
                pl.BlockSpec((D_MODEL, COL_END), lambda b, i: (0, 0), pipeline_mode=pl.Buffered(1))]
    args = [x, mod_rows, g.reshape(1, D_MODEL), w]
    out_specs = [pl.BlockSpec((1, tm, COL_END), lambda b, i: (b, i, 0))]
    out_shape = [jax.ShapeDtypeStruct((bsz, n, COL_END), BF16)]
    if rope:
        in_specs += [pl.BlockSpec((tm, LANES), lambda b, i: (i, 0))] * 2
        args += list(rope_tabs)
        out_specs += [pl.BlockSpec((1, tm, _QK_WIDTH), lambda b, i: (b, i, 0))] * 2
        out_shape += [jax.ShapeDtypeStruct((bsz, n, _QK_WIDTH), BF16)] * 2
    res = pl.pallas_call(
        functools.partial(_inproj_kernel, rope=rope),
        grid=(bsz, n // tm),
        in_specs=in_specs,
        out_specs=out_specs,
        out_shape=out_shape,
        compiler_params=pltpu.CompilerParams(dimension_semantics=("arbitrary", "arbitrary"),
                                             vmem_limit_bytes=INPROJ_VMEM_LIMIT),
        name="in_projection",
    )(*args)
    return res if rope else res[0]


def rope_tables(n):
    pos = jnp.arange(n)
    row = (pos // GRID_W).astype(F32)
    col = (pos % GRID_W).astype(F32)
    inv = ROPE_BASE ** (-jnp.arange(ROPE_NF, dtype=F32) / ROPE_NF)
    ar = row[:, None] * inv
    ac = col[:, None] * inv
    cos64 = jnp.concatenate([jnp.cos(ar), jnp.cos(ar), jnp.cos(ac), jnp.cos(ac)], axis=1)
    sin64 = jnp.concatenate([-jnp.sin(ar), jnp.sin(ar), -jnp.sin(ac), jnp.sin(ac)], axis=1)
    return jnp.tile(cos64, (1, 2)), jnp.tile(sin64, (1, 2))


def _attn_kernel(*refs, lam_init, n_lat, ck, prescaled):
    if n_lat:
        lam_ref, g_ref, q_ref, kc_ref, vc_ref, k_ref, v_ref, o_ref, vt_ref = refs
    else:
        lam_ref, g_ref, q_ref, kc_ref, vc_ref, o_ref, vt_ref = refs
    tq = q_ref.shape[1]
    n_ctx = kc_ref.shape[1]

    @pl.when(pl.program_id(2) == 0)
    def _():
        vt_ref[0:DA_VDIM, 0:n_ctx] = vc_ref[0].astype(F32).T.astype(BF16)
        for c in range(n_lat // ck):
            vt_ref[0:DA_VDIM, n_ctx + c * ck:n_ctx + (c + 1) * ck] = (
                v_ref[0, c * ck:(c + 1) * ck, :].astype(F32).T.astype(BF16))
        pad = vt_ref.shape[0] - DA_VDIM
        row = lax.broadcasted_iota(jnp.int32, (pad, vt_ref.shape[1]), 0)
        vt_ref[DA_VDIM:, :] = (row == 0).astype(BF16)

    lp = lam_ref[...]
    lam = (jnp.exp(jnp.sum(lp[0:1] * lp[1:2], axis=1, keepdims=True))
           - jnp.exp(jnp.sum(lp[2:3] * lp[3:4], axis=1, keepdims=True)) + lam_init)

    q = q_ref[0]
    if not prescaled:
        q = (q.astype(F32) * QK_SCALE).astype(BF16)
    lane = lax.broadcasted_iota(jnp.int32, q.shape, 1)
    zero = jnp.zeros_like(q)
    qq = jnp.concatenate([jnp.where(lane < DA_SUB, q, zero), jnp.where(lane >= DA_SUB, q, zero)], axis=0)

    chunks = [(lambda: kc_ref[0], 0, n_ctx)]
    for c in range(n_lat // ck):
        chunks.append((lambda c=c: k_ref[0, c * ck:(c + 1) * ck, :], n_ctx + c * ck, ck))

    def scores_t(c):
        return lax.dot_general(chunks[c][0](), qq, (((1,), (1,)), ((), ())), preferred_element_type=F32)

    def pv_t(p_t, c):
        _, off, width = chunks[c]
        return _dot(vt_ref[:, off:off + width], p_t)

    m = jnp.full((1, 2 * tq), -jnp.inf, F32)
    acc = jnp.zeros((vt_ref.shape[0], 2 * tq), F32)
    pending = None
    s_next = scores_t(0)
    for c in range(len(chunks)):
        s = s_next
        if c + 1 < len(chunks):
            s_next = scores_t(c + 1)
        part = s[0:64]
        for r in range(64, s.shape[0], 64):
            part = jnp.maximum(part, s[r:r + 64])
        m_new = jnp.maximum(m, jnp.max(part, axis=0, keepdims=True))
        alpha = jnp.exp2(m - m_new)
        p_t = jnp.exp2(s - m_new).astype(BF16)
        m = m_new
        if pending is not None:
            p_prev, alpha_prev, c_prev = pending
            acc = acc * alpha_prev + pv_t(p_prev, c_prev)
        pending = (p_t, alpha, c)
    p_prev, alpha_prev, c_prev = pending
    acc = acc * alpha_prev + pv_t(p_prev, c_prev)

    o_t = acc[0:DA_VDIM, :] / acc[DA_VDIM:DA_VDIM + 1, :]
    d = (o_t[:, 0:tq] - lam * o_t[:, tq:]).T
    y = d * lax.rsqrt(jnp.mean(d * d, axis=-1, keepdims=True) + EPS) * g_ref[...]
    o_ref[0] = (y * (1.0 - lam_init)).astype(BF16)


def diff_attention(q_arr, q_col, zc, lat, lam_p, subln, lam_init, prescaled):
    bsz, nq, _ = q_arr.shape
    n_ctx = zc.shape[1]
    tq = min(nq, 512)
    kcb, vcb = COL["K"] // LANES, COL["V"] // LANES
    in_specs = [pl.BlockSpec((4, DA_SUB), lambda b, h, i: (0, 0)),
                pl.BlockSpec((1, DA_VDIM), lambda b, h, i: (0, 0)),
                pl.BlockSpec((1, tq, LANES), lambda b, h, i: (b, i, q_col + h)),
                pl.BlockSpec((1, n_ctx, LANES), lambda b, h, i: (b, 0, kcb + h)),
                pl.BlockSpec((1, n_ctx, LANES), lambda b, h, i: (b, 0, vcb + h))]
    args = [lam_p, subln.reshape(1, DA_VDIM), q_arr, zc, zc]
    n_lat = 0
    if lat is not None:
        k_arr, k_col, v_arr, v_col = lat
        n_lat = k_arr.shape[1]
        in_specs += [pl.BlockSpec((1, n_lat, LANES), lambda b, h, i: (b, 0, k_col + h)),
                     pl.BlockSpec((1, n_lat, LANES), lambda b, h, i: (b, 0, v_col + h))]
        args += [k_arr, v_arr]
    ones_rows = 16
    scratch = [pltpu.VMEM((DA_VDIM + ones_rows, n_ctx + n_lat), BF16)]
    return pl.pallas_call(
        functools.partial(_attn_kernel, lam_init=lam_init, n_lat=n_lat, ck=512, prescaled=prescaled),
        grid=(bsz, DA_HEADS, nq // tq),
        in_specs=in_specs,
        out_specs=pl.BlockSpec((1, tq, LANES), lambda b, h, i: (b, i, h)),
        out_shape=jax.ShapeDtypeStruct((bsz, nq, BR_WIDTH), BF16),
        scratch_shapes=scratch,
        compiler_params=_params(("arbitrary", "arbitrary", "arbitrary")),
        name="diff_attention",
    )(*args)


def hyena_positions(L):
    t = jnp.linspace(0.0, 1.0, L, dtype=F32)[:, None]
    wpos = ((2.0 * math.pi / L) * jnp.arange(L, dtype=F32))[:, None]
    bands = jnp.linspace(1e-4, HY_BANDS - 1, HY_BANDS, dtype=F32)[None, :]
    fwd = jnp.concatenate([t, jnp.cos(bands * wpos), -jnp.sin(bands * wpos)], axis=-1)
    emb = jnp.concatenate([fwd, fwd[0:1], jnp.flip(fwd[1:], axis=0)], axis=0)
    mask = (jnp.arange(2 * L) != L).astype(F32)[:, None]
    pad = jnp.zeros((2 * L, LANES - HY_EMB - 1), F32)
    return jnp.concatenate([emb, pad, mask], axis=-1)


def _filter_kernel(feat_ref, w1_ref, b1_ref, w2_ref, b2_ref, w3_ref, fr_ref, dl_ref, k_ref, ss_ref):
    half = feat_ref.shape[0] // 2
    feat = feat_ref[...]
    f2 = jnp.concatenate([feat[0:half], feat[half:]], axis=1)
    hid = jnp.sin(fr_ref[0:1, :] * (_dot_hi(f2, w1_ref[...]) + b1_ref[...]))
    hid = jnp.sin(fr_ref[1:2, :] * (_dot_hi(hid, w2_ref[...]) + b2_ref[...]))
    w3 = w3_ref[...]
    zero = jnp.zeros_like(w3)
    h = jnp.concatenate([_dot_hi(hid, jnp.concatenate([w3, zero], axis=0)),
                         _dot_hi(hid, jnp.concatenate([zero, w3], axis=0))], axis=0)
    t = feat[:, 0:1]
    mask = feat[:, LANES - 1:LANES]
    k = h * jnp.exp(-t * dl_ref[...]) * mask
    k_ref[...] = k

    @pl.when(pl.program_id(0) == 0)
    def _():
        ss_ref[...] = jnp.zeros_like(ss_ref)

    ss_ref[...] += jnp.sum(k * k, axis=0, keepdims=True)


def hyena_filter_taps(feat, w1, b1, w2, b2, w3, freq):
    two_l = feat.shape[0]
    tr = min(two_l // 2, 1024)
    nb = two_l // tr
    hd = HY_HIDDEN
    w1d = jnp.zeros((2 * LANES, 2 * hd), F32).at[:HY_EMB, :hd].set(w1).at[LANES:LANES + HY_EMB, hd:].set(w1)
    w2d = jnp.zeros((2 * hd, 2 * hd), F32).at[:hd, :hd].set(w2).at[hd:, hd:].set(w2)
    twice = lambda v: jnp.tile(v.reshape(-1, hd), (1, 2))
    deltas = jnp.abs(jnp.linspace(HY_MIN_DECAY, HY_MAX_DECAY, HY_WIDTH, dtype=F32))[None, :]
    const = lambda i: (0, 0)
    return pl.pallas_call(
        _filter_kernel,
        grid=(nb,),
        in_specs=[pl.BlockSpec((tr, LANES), lambda i: (i, 0)),
                  pl.BlockSpec((2 * LANES, 2 * hd), const),
                  pl.BlockSpec((1, 2 * hd), const),
                  pl.BlockSpec((2 * hd, 2 * hd), const),
                  pl.BlockSpec((1, 2 * hd), const),
                  pl.BlockSpec((hd, HY_WIDTH), lambda i: (0, (2 * i) // nb)),
                  pl.BlockSpec((2, 2 * hd), const),
                  pl.BlockSpec((1, HY_WIDTH), const)],
        out_specs=[pl.BlockSpec((tr, HY_WIDTH), lambda i: (i, 0)),
                   pl.BlockSpec((1, HY_WIDTH), const)],
        out_shape=[jax.ShapeDtypeStruct((two_l, HY_WIDTH), F32),
                   jax.ShapeDtypeStruct((1, HY_WIDTH), F32)],
        compiler_params=_params(("arbitrary",)),
        name="hyena_filter",
    )(feat, w1d, twice(b1), w2d, twice(b2), w3, twice(freq), deltas)


def _short_conv_kernel(z_ref, prev_ref, next_ref, w_ref, b_ref, x0_ref, uv_ref):
    i = pl.program_id(1)
    last = pl.num_programs(1) - 1
    z = z_ref[0].astype(F32)
    tr = z.shape[0]
    halo = prev_ref.shape[1]
    before = jnp.where(i == 0, 0.0, prev_ref[0, halo - 1:halo, :].astype(F32))
    after = jnp.where(i == last, 0.0, next_ref[0, 0:1, :].astype(F32))
    row = lax.broadcasted_iota(jnp.int32, z.shape, 0)
    zm = jnp.where(row == 0, before, pltpu.roll(z, 1, axis=0))
    zp = jnp.where(row == tr - 1, after, pltpu.roll(z, tr - 1, axis=0))
    y = zm * w_ref[0:1, :] + z * w_ref[1:2, :] + zp * w_ref[2:3, :] + b_ref[...]
    x0_ref[0] = y[:, 0:HY_WIDTH].astype(BF16)
    uv_ref[0] = (y[:, HY_WIDTH:2 * HY_WIDTH] * y[:, 2 * HY_WIDTH:]).astype(BF16)


def hyena_short_conv(z, sw, sb):
    bsz, n, _ = z.shape
    tr = min(n, 512)
    halo = 16
    w3c = 3 * HY_WIDTH
    cb = COL["HY"] // w3c
    nh = n // halo
    out = jax.ShapeDtypeStruct((bsz, n, HY_WIDTH), BF16)
    return pl.pallas_call(
        _short_conv_kernel,
        grid=(bsz, n // tr),
        in_specs=[pl.BlockSpec((1, tr, w3c), lambda b, i: (b, i, cb)),
                  pl.BlockSpec((1, halo, w3c), lambda b, i: (b, jnp.maximum(i * (tr // halo) - 1, 0), cb)),
                  pl.BlockSpec((1, halo, w3c), lambda b, i: (b, jnp.minimum((i + 1) * (tr // halo), nh - 1), cb)),
                  pl.BlockSpec((3, w3c), lambda b, i: (0, 0)),
                  pl.BlockSpec((1, w3c), lambda b, i: (0, 0))],
        out_specs=[pl.BlockSpec((1, tr, HY_WIDTH), lambda b, i: (b, i, 0)),
                   pl.BlockSpec((1, tr, HY_WIDTH), lambda b, i: (b, i, 0))],
        out_shape=[out, out],
        compiler_params=_params(("arbitrary", "arbitrary")),
        name="hyena_short_conv",
    )(z, z, z, sw, sb.reshape(1, w3c))


def dft_tables():
    r = DFT_R
    n_fft = r * r
    idx = jnp.arange(r, dtype=jnp.int32)
    prod = idx[:, None] * idx[None, :]
    ang = (2.0 * math.pi / r) * (prod % r).astype(F32)
    f_re, f_im = jnp.cos(ang), -jnp.sin(ang)
    ang = (2.0 * math.pi / n_fft) * prod.astype(F32)
    t_re, t_im = jnp.cos(ang), -jnp.sin(ang)

    def cmul(a_re, a_im, b_re, b_im):
        return a_re * b_re - a_im * b_im, a_re * b_im + a_im * b_re

    g_re, g_im = cmul(f_re[None, :, :], f_im[None, :, :], t_re[:, None, :], t_im[:, None, :])
    gh_re, gh_im = cmul(f_re[None, :, :], -f_im[None, :, :], t_re[:, :, None], -t_im[:, :, None])
    return tuple(a.astype(BF16) for a in (f_re, f_im, g_re, g_im, gh_re, gh_im))


def _fft1_kernel(*refs, n_in):
    if n_in == 2:
        x_ref, fre_ref, fim_ref, are_ref, aim_ref = refs
        x0 = x_ref[0]
        x1 = x_ref[1]
        fre, fim = fre_ref[...], fim_ref[...]
        are_ref[...] = (_dot(fre, x0) - _dot(fim, x1)).astype(BF16)
        aim_ref[...] = (_dot(fre, x1) + _dot(fim, x0)).astype(BF16)
    else:
        x_ref, fre_ref, fim_ref, are_ref, aim_ref = refs
        x = x_ref[...].astype(BF16)
        are_ref[...] = _dot(fre_ref[...], x).astype(BF16)
        aim_ref[...] = _dot(fim_ref[...], x).astype(BF16)


def fft_level1(x, f_re, f_im):
    r = DFT_R
    cols = x.shape[-1]
    tc = 8192
    if x.ndim == 3:
        n_in, kdim = 2, x.shape[1]
        x_spec = pl.BlockSpec((2, kdim, tc), lambda j: (0, 0, j))
    else:
        n_in, kdim = 1, x.shape[0]
        x_spec = pl.BlockSpec((kdim, tc), lambda j: (0, j))
    out = jax.ShapeDtypeStruct((r, cols), BF16)
    return pl.pallas_call(
        functools.partial(_fft1_kernel, n_in=n_in),
        grid=(cols // tc,),
        in_specs=[x_spec,
                  pl.BlockSpec((r, kdim), lambda j: (0, 0)),
                  pl.BlockSpec((r, kdim), lambda j: (0, 0))],
        out_specs=[pl.BlockSpec((r, tc), lambda j: (0, j)), pl.BlockSpec((r, tc), lambda j: (0, j))],
        out_shape=[out, out],
        compiler_params=_params(("arbitrary",)),
        name="fft_level1",
    )(x, f_re[:, :kdim], f_im[:, :kdim])


def _fft2_conv_kernel(are_ref, aim_ref, kare_ref, kaim_ref, gre_ref, gim_ref, hre_ref, him_ref, sc_ref,
                      bre_ref, bim_ref):
    c = are_ref.shape[-1]
    sc = sc_ref[...]
    for j in range(are_ref.shape[0]):
        rhs = jnp.concatenate([are_ref[j], aim_ref[j], kare_ref[j], kaim_ref[j]], axis=1)
        p = _dot(gre_ref[j], rhs)
        q = _dot(gim_ref[j], rhs)
        x_re = p[:, 0:c] - q[:, c:2 * c]
        x_im = p[:, c:2 * c] + q[:, 0:c]
        k_re = (p[:, 2 * c:3 * c] - q[:, 3 * c:]) * sc
        k_im = (p[:, 3 * c:] + q[:, 2 * c:3 * c]) * sc
        y = jnp.concatenate([(x_re * k_re - x_im * k_im).astype(BF16),
                             (x_re * k_im + x_im * k_re).astype(BF16)], axis=1)
        r = _dot(hre_ref[j], y)
        t = _dot(him_ref[j], y)
        bre_ref[j] = (r[:, 0:c] - t[:, c:]).astype(BF16)
        bim_ref[j] = (r[:, c:] + t[:, 0:c]).astype(BF16)


def fft_level2_conv(a_re, a_im, ka_re, ka_im, g_re, g_im, gh_re, gh_im, scale):
    r = DFT_R
    c = a_re.shape[-1]
    tk = 8
    blk_a = pl.BlockSpec((tk, r, c), lambda i: (i, 0, 0))
    blk_g = pl.BlockSpec((tk, r, r), lambda i: (i, 0, 0))
    out = jax.ShapeDtypeStruct((r, r, c), BF16)
    return pl.pallas_call(
        _fft2_conv_kernel,
        grid=(r // tk,),
        in_specs=[blk_a, blk_a, blk_a, blk_a, blk_g, blk_g, blk_g, blk_g, pl.BlockSpec((1, c), lambda i: (0, 0))],
        out_specs=[blk_a, blk_a],
        out_shape=[out, out],
        compiler_params=_params(("arbitrary",)),
        name="fft_level2_conv",
    )(a_re, a_im, ka_re, ka_im, g_re, g_im, gh_re, gh_im, scale)


def _ifft1_kernel(bre_ref, bim_ref, fre_ref, fim_ref, uv_ref, x0_ref, bias_ref, o_ref):
    fre, fim = fre_ref[...], fim_ref[...]
    b_re, b_im = bre_ref[...], bim_ref[...]
    y0 = _dot(fre, b_re) + _dot(fim, b_im)
    y1 = _dot(fre, b_im) - _dot(fim, b_re)
    bias = bias_ref[...]
    o_ref[0] = ((y0 + uv_ref[0].astype(F32) * bias) * x0_ref[0].astype(F32)).astype(BF16)
    o_ref[1] = ((y1 + uv_ref[1].astype(F32) * bias) * x0_ref[1].astype(F32)).astype(BF16)


def ifft_level1(b_re, b_im, f_re, f_im, uv, x0, bias_t):
    r = DFT_R
    half = r // 2
    cols = b_re.shape[-1]
    tc = 8192
    blk_b = pl.BlockSpec((r, tc), lambda j: (0, j))
    blk_f = pl.BlockSpec((half, r), lambda j: (0, 0))
    blk_x = pl.BlockSpec((2, half, tc), lambda j: (0, 0, j))
    return pl.pallas_call(
        _ifft1_kernel,
        grid=(cols // tc,),
        in_specs=[blk_b, blk_b, blk_f, blk_f, blk_x, blk_x, pl.BlockSpec((1, tc), lambda j: (0, 0))],
        out_specs=blk_x,
        out_shape=jax.ShapeDtypeStruct((2, half, cols), BF16),
        compiler_params=_params(("arbitrary",)),
        name="ifft_level1",
    )(b_re, b_im, f_re[:half], f_im[:half], uv, x0, bias_t)


def hyena_long_conv(uv, x0, taps, sumsq, bias, tables):
    f_re, f_im, g_re, g_im, gh_re, gh_im = tables
    r = DFT_R
    bsz, L, c = uv.shape
    assert bsz == 2 and 2 * L == r * r
    scale = lax.rsqrt(sumsq) * (1.0 / (r * r))
    ka_re, ka_im = fft_level1(taps.reshape(r, r * c), f_re, f_im)
    uv2 = uv.reshape(2, r // 2, r * c)
    a_re, a_im = fft_level1(uv2, f_re, f_im)
    b_re, b_im = fft_level2_conv(a_re.reshape(r, r, c), a_im.reshape(r, r, c), ka_re.reshape(r, r, c),
                                 ka_im.reshape(r, r, c), g_re, g_im, gh_re, gh_im, scale)
    bias_t = jnp.tile(bias.reshape(1, c), (1, 8192 // c))
    y = ifft_level1(b_re.reshape(r, r * c), b_im.reshape(r, r * c), f_re, f_im, uv2,
                    x0.reshape(2, r // 2, r * c), bias_t)
    return y.reshape(2, L, c)


def dft_small_tables(n_fft):
    idx = jnp.arange(n_fft, dtype=jnp.int32)
    ang = (2.0 * math.pi / n_fft) * ((idx[:, None] * idx[None, :]) % n_fft).astype(F32)
    return jnp.cos(ang).astype(BF16), (-jnp.sin(ang)).astype(BF16)


def _conv_small_kernel(taps_ref, ss_ref, fre_ref, fim_ref, uv_ref, x0_ref, bias_ref, o_ref):
    two_l = taps_ref.shape[0]
    L = two_l // 2
    fre, fim = fre_ref[...], fim_ref[...]
    scale = lax.rsqrt(ss_ref[...]) * (1.0 / two_l)
    taps = taps_ref[...].astype(BF16)
    k_re = _dot(fre, taps) * scale
    k_im = _dot(fim, taps) * scale
    u0, u1 = uv_ref[0], uv_ref[1]
    fre_l, fim_l = fre[:, :L], fim[:, :L]
    x_re = _dot(fre_l, u0) - _dot(fim_l, u1)
    x_im = _dot(fre_l, u1) + _dot(fim_l, u0)
    y_re = (x_re * k_re - x_im * k_im).astype(BF16)
    y_im = (x_re * k_im + x_im * k_re).astype(BF16)
    fre_t, fim_t = fre[:L, :], fim[:L, :]
    y0 = _dot(fre_t, y_re) + _dot(fim_t, y_im)
    y1 = _dot(fre_t, y_im) - _dot(fim_t, y_re)
    bias = bias_ref[...]
    o_ref[0] = ((y0 + u0.astype(F32) * bias) * x0_ref[0].astype(F32)).astype(BF16)
    o_ref[1] = ((y1 + u1.astype(F32) * bias) * x0_ref[1].astype(F32)).astype(BF16)


def hyena_long_conv_small(uv, x0, taps, sumsq, bias, tables):
    bsz, L, c = uv.shape
    assert bsz == 2
    f_re, f_im = tables
    full = lambda shape: pl.BlockSpec(shape, lambda i: (0,) * len(shape))
    return pl.pallas_call(
        _conv_small_kernel,
        grid=(1,),
        in_specs=[full((2 * L, c)), full((1, c)), full((2 * L, 2 * L)), full((2 * L, 2 * L)),
                  full((2, L, c)), full((2, L, c)), full((1, c))],
        out_specs=full((2, L, c)),
        out_shape=jax.ShapeDtypeStruct((2, L, c), BF16),
        compiler_params=_params(("arbitrary",)),
        name="hyena_conv_small",
    )(taps, sumsq, f_re, f_im, uv, x0, bias.reshape(1, c))


def _gmlp_kernel(z_ref, g_ref, b_ref, ws_ref, bs_ref, o_ref):
    zg = z_ref[0].astype(F32)
    gl = 0.5 * zg * (1.0 + lax.erf(zg * (2.0 ** -0.5)))
    w = BR_WIDTH
    u = gl[:, :w]
    v = gl[:, w:]
    mu = jnp.mean(v, axis=-1, keepdims=True)
    var = jnp.mean(jnp.square(v - mu), axis=-1, keepdims=True)
    v = ((v - mu) * lax.rsqrt(var + EPS) * g_ref[...] + b_ref[...]).astype(BF16)
    gw = w // GM_GROUPS
    lane = lax.broadcasted_iota(jnp.int32, (GM_CHUNK, LANES), 1)
    first = lane < gw
    for ci in range(zg.shape[0] // GM_CHUNK):
        rows = slice(ci * GM_CHUNK, (ci + 1) * GM_CHUNK)
        tiles = []
        for t in range(w // LANES):
            vt = v[rows, t * LANES:(t + 1) * LANES]
            tiles.append(jnp.where(first, _dot(ws_ref[2 * t], vt), _dot(ws_ref[2 * t + 1], vt)))
        vm = jnp.concatenate(tiles, axis=1) + bs_ref[...]
        o_ref[0, rows, :] = (u[rows] * vm).astype(BF16)


def gmlp(z, ln_g, ln_b, ws, bs):
    bsz, n, _ = z.shape
    tr = min(n, 512)
    w = BR_WIDTH
    cb = COL["GM"] // (2 * w)
    bs_full = jnp.repeat(bs.T, w // GM_GROUPS, axis=1)
    return pl.pallas_call(
        _gmlp_kernel,
        grid=(bsz, n // tr),
        in_specs=[pl.BlockSpec((1, tr, 2 * w), lambda b, i: (b, i, cb)),
                  pl.BlockSpec((1, w), lambda b, i: (0, 0)),
                  pl.BlockSpec((1, w), lambda b, i: (0, 0)),
                  pl.BlockSpec((GM_GROUPS, GM_CHUNK, GM_CHUNK), lambda b, i: (0, 0, 0)),
                  pl.BlockSpec((GM_CHUNK, w), lambda b, i: (0, 0))],
        out_specs=pl.BlockSpec((1, tr, w), lambda b, i: (b, i, 0)),
        out_shape=jax.ShapeDtypeStruct((bsz, n, w), BF16),
        compiler_params=_params(("arbitrary", "arbitrary")),
        name="gmlp",
    )(z, ln_g.reshape(1, w), ln_b.reshape(1, w), ws.astype(BF16), bs_full)


def _merge_kernel(ya_ref, yb_ref, yc_ref, ga_ref, gb_ref, gc_ref, mg_ref, x_ref, mod_ref, np_ref,
                  wb_ref, wo_ref, o_ref):
    gt = mod_ref[0, :, 2 * D_MODEL:]
    tm = x_ref.shape[1]
    sub = min(tm, 256)
    for r0 in range(0, tm, sub):
        rows = slice(r0, r0 + sub)
        acc = None
        for i, (y_ref, g_ref) in enumerate(((ya_ref, ga_ref), (yb_ref, gb_ref), (yc_ref, gc_ref))):
            g = g_ref[0, rows, :]
            gated = y_ref[0, rows, :] * (g * jax.nn.sigmoid(g))
            sel = jax.nn.sigmoid(mg_ref[0, rows, i * D_MODEL:(i + 1) * D_MODEL])
            term = sel * _dot(gated, wb_ref[i]).astype(BF16)
            acc = term if acc is None else acc + term
        out = _dot(acc, wo_ref[...])
        r = out * lax.rsqrt(jnp.mean(out * out, axis=-1, keepdims=True) + EPS) * np_ref[...]
        o_ref[0, rows, :] = x_ref[0, rows, :] + gt * r


def merge_out(ya, yb, yc, z, x, mod_rows, row_of_batch, npost, wb, wo):
    bsz, n, _ = x.shape
    tm = min(n, 512)
    w = BR_WIDTH
    yspec = pl.BlockSpec((1, tm, w), lambda b, i: (b, i, 0))

    def zspec(name):
        cb = COL[name] // w
        return pl.BlockSpec((1, tm, w), lambda b, i: (b, i, cb))

    return pl.pallas_call(
        _merge_kernel,
        grid=(bsz, n // tm),
        in_specs=[yspec, yspec, yspec, zspec("GA"), zspec("GB"), zspec("GC"),
                  pl.BlockSpec((1, tm, 3 * D_MODEL), lambda b, i: (b, i, 0)),
                  pl.BlockSpec((1, tm, D_MODEL), lambda b, i: (b, i, 0)),
                  pl.BlockSpec((1, 1, 3 * D_MODEL), lambda b, i: (row_of_batch(b), 0, 0)),
                  pl.BlockSpec((1, D_MODEL), lambda b, i: (0, 0)),
                  pl.BlockSpec((3, w, D_MODEL), lambda b, i: (0, 0, 0)),
                  pl.BlockSpec((D_MODEL, D_MODEL), lambda b, i: (0, 0))],
        out_specs=pl.BlockSpec((1, tm, D_MODEL), lambda b, i: (b, i, 0)),
        out_shape=jax.ShapeDtypeStruct((bsz, n, D_MODEL), F32),
        compiler_params=_params(("arbitrary", "arbitrary")),
        name="merge_out",
    )(ya, yb, yc, z, z, z, z, x, mod_rows, npost.reshape(1, D_MODEL), wb, wo)


def _permute_cols(w):
    return jnp.concatenate([w[:, _REF_COLS[nm][0]:_REF_COLS[nm][0] + _REF_COLS[nm][1]] for nm in _NEW_ORDER],
                           axis=1)


def kernel(x, c, ctx, c_ctx, ada_w, ada_b, norm_pre, norm_post, w_in, da_lambda, da_subln, hy_short_w,
           hy_short_b, hy_f_w1, hy_f_b1, hy_f_w2, hy_f_b2, hy_f_w3, hy_f_freq, hy_bias, gm_ln_g, gm_ln_b,
           gm_ws, gm_bs, w_branch, w_out):
    bsz, n, _ = x.shape
    n_ctx = ctx.shape[1]
    assert bsz == 2 and 2 * n == DFT_R * DFT_R

    cond8 = jnp.zeros((8, D_MODEL), F32).at[0:bsz].set(c).at[bsz].set(c_ctx)
    mod = modulation_all(cond8, ada_w, ada_b)
    lat_row = lambda b: b
    ctx_row = lambda b: bsz

    cos_t, sin_t = rope_tables(n)
    feat = hyena_positions(n)
    feat_c = hyena_positions(n_ctx)
    tables = dft_tables()
    tables_c = dft_small_tables(2 * n_ctx)
    kb, vb, qb = COL["K"] // LANES, COL["V"] // LANES, COL["Q"] // LANES

    xc = ctx
    for l in range(DEPTH):
        last = l == DEPTH - 1
        lam_init = 0.8 - 0.6 * math.exp(-0.3 * l)
        mod_rows = mod[l].reshape(8, 1, 3 * D_MODEL)
        w_l = _permute_cols(w_in[l]).astype(BF16)
        wb_l = w_branch[l].astype(BF16)
        wo_l = w_out[l].astype(BF16)
        filt_w = (hy_f_w1[l], hy_f_b1[l], hy_f_w2[l], hy_f_b2[l], hy_f_w3[l], hy_f_freq[l])

        z, qr, kr = in_projection(x, mod_rows, lat_row, norm_pre[l], w_l, (cos_t, sin_t))
        zc = in_projection(xc, mod_rows, ctx_row, norm_pre[l], w_l)

        taps, sumsq = hyena_filter_taps(feat, *filt_w)
        x0, uv = hyena_short_conv(z, hy_short_w[l], hy_short_b[l])
        y_a = diff_attention(qr, 0, zc, (kr, 0, z, vb), da_lambda[l], da_subln[l], lam_init, True)
        y_b = hyena_long_conv(uv, x0, taps, sumsq, hy_bias[l], tables)
        y_c = gmlp(z, gm_ln_g[l], gm_ln_b[l], gm_ws[l], gm_bs[l])
        x_new = merge_out(y_a, y_b, y_c, z, x, mod_rows, lat_row, norm_post[l], wb_l, wo_l)

        if not last:
            yc_a = diff_attention(zc, qb, zc, None, da_lambda[l], da_subln[l], lam_init, False)
            taps_c, sumsq_c = hyena_filter_taps(feat_c, *filt_w)
            x0c, uvc = hyena_short_conv(zc, hy_short_w[l], hy_short_b[l])
            yc_b = hyena_long_conv_small(uvc, x0c, taps_c, sumsq_c, hy_bias[l], tables_c)
            yc_c = gmlp(zc, gm_ln_g[l], gm_ln_b[l], gm_ws[l], gm_bs[l])
            xc = merge_out(yc_a, yc_b, yc_c, zc, xc, mod_rows, ctx_row, norm_post[l], wb_l, wo_l)
        x = x_new
    return x
```

```python
import functools
import math

import jax
import jax.numpy as jnp
from jax import lax
from jax.experimental import pallas as pl
from jax.experimental.pallas import tpu as pltpu

F32 = jnp.float32
BF16 = jnp.bfloat16

D_MODEL = 1024
DEPTH = 4
GRID_W = 64
EPS = 1e-6
BR_WIDTH = 512
DA_SUB = 64
DA_VDIM = 128
DA_HEADS = 4
ROPE_BASE = 10000.0
ROPE_NF = 16
HY_WIDTH = 512
HY_BANDS = 16
HY_EMB = 33
HY_HIDDEN = 64
HY_MIN_DECAY = math.log(1e-2) / 1.5
HY_MAX_DECAY = math.log(1e-2) / 0.3
GM_GROUPS = 8
GM_CHUNK = 128

LANES = 128
DFT_R = 128
QK_SCALE = DA_SUB ** -0.5 * math.log2(math.e)

_REF_COLS = dict(K=(0, 512), V=(512, 512), Q=(1024, 512), GA=(1536, 512), HY=(2048, 1536),
                 GB=(3584, 512), GM=(4096, 1024), GC=(5120, 512), MG=(5632, 3072))
_GROUPS = (("MG", ("MG",)), ("HY", ("HY",)), ("GM", ("GM",)), ("AT", ("K", "V", "Q", "GA", "GB", "GC")))
GROUP_WIDTH = {g: sum(_REF_COLS[nm][1] for nm in names) for g, names in _GROUPS}
ATC = {}
_off = 0
for _name in _GROUPS[-1][1]:
    ATC[_name] = _off
    _off += _REF_COLS[_name][1]

VMEM_LIMIT = 48 * 1024 * 1024
INPROJ_VMEM_LIMIT = 56 * 1024 * 1024


def _params(sem):
    return pltpu.CompilerParams(dimension_semantics=sem, vmem_limit_bytes=VMEM_LIMIT)


def _dot(a, b):
    return jnp.dot(a, b, preferred_element_type=F32)


def _dot_hi(a, b):
    return jnp.dot(a, b, preferred_element_type=F32, precision=lax.Precision.HIGHEST)


def _mod_kernel(c_ref, w_ref, b_ref, o_ref):
    cond = c_ref[...]
    s = cond * jax.nn.sigmoid(cond)
    o_ref[0] = _dot_hi(s, w_ref[0]) + b_ref[0]


def modulation_all(cond8, ada_w, ada_b):
    tn = 1024
    return pl.pallas_call(
        _mod_kernel,
        grid=(DEPTH, 3 * D_MODEL // tn),
        in_specs=[pl.BlockSpec((8, D_MODEL), lambda l, j: (0, 0)),
                  pl.BlockSpec((1, D_MODEL, tn), lambda l, j: (l, 0, j)),
                  pl.BlockSpec((1, 1, tn), lambda l, j: (l, 0, j))],
        out_specs=pl.BlockSpec((1, 8, tn), lambda l, j: (l, 0, j)),
        out_shape=jax.ShapeDtypeStruct((DEPTH, 8, 3 * D_MODEL), F32),
        compiler_params=_params(("arbitrary", "arbitrary")),
        name="modulation",
    )(cond8, ada_w, ada_b.reshape(DEPTH, 1, 3 * D_MODEL))


_QK_WIDTH = DA_HEADS * LANES
MXU_TILE = 256
_INPROJ_CHUNKS = dict(MG=((0, 2048), (2048, 3072)), HY=((0, 1536),), GM=((0, 1024),),
                      AT=((ATC["K"], ATC["V"]), (ATC["V"], ATC["Q"]), (ATC["Q"], ATC["GA"]),
                          (ATC["GA"], GROUP_WIDTH["AT"])))
for _g, _chunks in _INPROJ_CHUNKS.items():
    assert _chunks[0][0] == 0 and _chunks[-1][1] == GROUP_WIDTH[_g]
    assert all(a % MXU_TILE == 0 and b % MXU_TILE == 0 for a, b in _chunks)


def _inproj_kernel(*refs, rope):
    ng = len(_GROUPS)
    x_ref, mod_ref, g_ref = refs[0:3]
    w_refs = refs[3:3 + ng]
    if rope:
        cos_ref, sin_ref = refs[3 + ng:5 + ng]
        o_refs = refs[5 + ng:5 + 2 * ng]
        q_ref, k_ref = refs[5 + 2 * ng:]
    else:
        o_refs = refs[3 + ng:3 + 2 * ng]
    x = x_ref[0]
    y = x * lax.rsqrt(jnp.mean(x * x, axis=-1, keepdims=True) + EPS) * g_ref[...]
    sh = mod_ref[0, :, 0:D_MODEL]
    sc = mod_ref[0, :, D_MODEL:2 * D_MODEL]
    h = (y * (1.0 + sc) + sh).astype(BF16)

    if rope:
        lane = lax.broadcasted_iota(jnp.int32, cos_ref.shape, 1)
        low = (lane % (2 * ROPE_NF)) < ROPE_NF
        cs = cos_ref[...]
        sn = sin_ref[...]

        def rot(v):
            partner = jnp.where(low, pltpu.roll(v, LANES - ROPE_NF, axis=1), pltpu.roll(v, ROPE_NF, axis=1))
            return (v * cs + partner * sn).astype(BF16)

    for (gname, _), w_ref, o_ref in zip(_GROUPS, w_refs, o_refs):
        for a, b in _INPROJ_CHUNKS[gname]:
            zc = _dot(h, w_ref[:, a:b])
            o_ref[0, :, a:b] = zc.astype(BF16)
            if rope and gname == "AT" and a in (ATC["K"], ATC["Q"]):
                dst, scale = (k_ref, 1.0) if a == ATC["K"] else (q_ref, QK_SCALE)
                for hd in range(DA_HEADS):
                    cols = slice(hd * LANES, (hd + 1) * LANES)
                    dst[0, :, cols] = rot(zc[:, cols] * scale)


def in_projection(x, mod_rows, row_of_batch, g, ws, rope_tabs=None):
    bsz, n, _ = x.shape
    tm = min(n, 512)
    rope = rope_tabs is not None
    in_specs = [pl.BlockSpec((1, tm, D_MODEL), lambda b, i: (b, i, 0)),
                pl.BlockSpec((1, 1, 3 * D_MODEL), lambda b, i: (row_of_batch(b), 0, 0)),
                pl.BlockSpec((1, D_MODEL), lambda b, i: (0, 0))]
    in_specs += [pl.BlockSpec((D_MODEL, GROUP_WIDTH[gn]), lambda b, i: (0, 0), pipeline_mode=pl.Buffered(1))
                 for gn, _ in _GROUPS]
    args = [x, mod_rows, g.reshape(1, D_MODEL)] + list(ws)
    if rope:
        in_specs += [pl.BlockSpec((tm, LANES), lambda b, i: (i, 0))] * 2
        args += list(rope_tabs)
    out_specs = [pl.BlockSpec((1, tm, GROUP_WIDTH[gn]), lambda b, i: (b, i, 0)) for gn, _ in _GROUPS]
    out_shape = [jax.ShapeDtypeStruct((bsz, n, GROUP_WIDTH[gn]), BF16) for gn, _ in _GROUPS]
    if rope:
        out_specs += [pl.BlockSpec((1, tm, _QK_WIDTH), lambda b, i: (b, i, 0))] * 2
        out_shape += [jax.ShapeDtypeStruct((bsz, n, _QK_WIDTH), BF16)] * 2
    res = pl.pallas_call(
        functools.partial(_inproj_kernel, rope=rope),
        grid=(bsz, n // tm),
        in_specs=in_specs,
        out_specs=out_specs,
        out_shape=out_shape,
        compiler_params=pltpu.CompilerParams(dimension_semantics=("arbitrary", "arbitrary"),
                                             vmem_limit_bytes=INPROJ_VMEM_LIMIT),
        name="in_projection",
    )(*args)
    z = {gn: r for (gn, _), r in zip(_GROUPS, res)}
    return (z, res[-2], res[-1]) if rope else z


def rope_tables(n):
    pos = jnp.arange(n)
    row = (pos // GRID_W).astype(F32)
    col = (pos % GRID_W).astype(F32)
    inv = ROPE_BASE ** (-jnp.arange(ROPE_NF, dtype=F32) / ROPE_NF)
    ar = row[:, None] * inv
    ac = col[:, None] * inv
    cos64 = jnp.concatenate([jnp.cos(ar), jnp.cos(ar), jnp.cos(ac), jnp.cos(ac)], axis=1)
    sin64 = jnp.concatenate([-jnp.sin(ar), jnp.sin(ar), -jnp.sin(ac), jnp.sin(ac)], axis=1)
    return jnp.tile(cos64, (1, 2)), jnp.tile(sin64, (1, 2))


def _attn_kernel(*refs, lam_init, n_lat, ck, prescaled):
    if n_lat:
        lam_ref, g_ref, q_ref, kc_ref, vc_ref, k_ref, v_ref, o_ref, vt_ref = refs
    else:
        lam_ref, g_ref, q_ref, kc_ref, vc_ref, o_ref, vt_ref = refs
    tq = q_ref.shape[1]
    n_ctx = kc_ref.shape[1]

    @pl.when(pl.program_id(2) == 0)
    def _():
        vt_ref[0:DA_VDIM, 0:n_ctx] = vc_ref[0].astype(F32).T.astype(BF16)
        for c in range(n_lat // ck):
            vt_ref[0:DA_VDIM, n_ctx + c * ck:n_ctx + (c + 1) * ck] = (
                v_ref[0, c * ck:(c + 1) * ck, :].astype(F32).T.astype(BF16))
        pad = vt_ref.shape[0] - DA_VDIM
        row = lax.broadcasted_iota(jnp.int32, (pad, vt_ref.shape[1]), 0)
        vt_ref[DA_VDIM:, :] = (row == 0).astype(BF16)

    lp = lam_ref[...]
    lam = (jnp.exp(jnp.sum(lp[0:1] * lp[1:2], axis=1, keepdims=True))
           - jnp.exp(jnp.sum(lp[2:3] * lp[3:4], axis=1, keepdims=True)) + lam_init)

    q = q_ref[0]
    if not prescaled:
        q = (q.astype(F32) * QK_SCALE).astype(BF16)
    lane = lax.broadcasted_iota(jnp.int32, q.shape, 1)
    zero = jnp.zeros_like(q)
    qq = jnp.concatenate([jnp.where(lane < DA_SUB, q, zero), jnp.where(lane >= DA_SUB, q, zero)], axis=0)

    chunks = [(lambda: kc_ref[0], 0, n_ctx)]
    for c in range(n_lat // ck):
        chunks.append((lambda c=c: k_ref[0, c * ck:(c + 1) * ck, :], n_ctx + c * ck, ck))

    def scores_t(c):
        return lax.dot_general(chunks[c][0](), qq, (((1,), (1,)), ((), ())), preferred_element_type=F32)

    def pv_t(p_t, c):
        _, off, width = chunks[c]
        return _dot(vt_ref[:, off:off + width], p_t)

    m = jnp.full((1, 2 * tq), -jnp.inf, F32)
    acc = jnp.zeros((vt_ref.shape[0], 2 * tq), F32)
    pending = None
    s_next = scores_t(0)
    for c in range(len(chunks)):
        s = s_next
        if c + 1 < len(chunks):
            s_next = scores_t(c + 1)
        part = s[0:64]
        for r in range(64, s.shape[0], 64):
            part = jnp.maximum(part, s[r:r + 64])
        m_new = jnp.maximum(m, jnp.max(part, axis=0, keepdims=True))
        alpha = jnp.exp2(m - m_new)
        p_t = jnp.exp2(s - m_new).astype(BF16)
        m = m_new
        if pending is not None:
            p_prev, alpha_prev, c_prev = pending
            acc = acc * alpha_prev + pv_t(p_prev, c_prev)
        pending = (p_t, alpha, c)
    p_prev, alpha_prev, c_prev = pending
    acc = acc * alpha_prev + pv_t(p_prev, c_prev)

    o_t = acc[0:DA_VDIM, :] / acc[DA_VDIM:DA_VDIM + 1, :]
    d = (o_t[:, 0:tq] - lam * o_t[:, tq:]).T
    y = d * lax.rsqrt(jnp.mean(d * d, axis=-1, keepdims=True) + EPS) * g_ref[...]
    o_ref[0] = (y * (1.0 - lam_init)).astype(BF16)


def diff_attention(q_arr, q_col, zc, lat, lam_p, subln, lam_init, prescaled):
    bsz, nq, _ = q_arr.shape
    n_ctx = zc.shape[1]
    tq = min(nq, 512)
    kcb, vcb = ATC["K"] // LANES, ATC["V"] // LANES
    in_specs = [pl.BlockSpec((4, DA_SUB), lambda b, h, i: (0, 0)),
                pl.BlockSpec((1, DA_VDIM), lambda b, h, i: (0, 0)),
                pl.BlockSpec((1, tq, LANES), lambda b, h, i: (b, i, q_col + h)),
                pl.BlockSpec((1, n_ctx, LANES), lambda b, h, i: (b, 0, kcb + h)),
                pl.BlockSpec((1, n_ctx, LANES), lambda b, h, i: (b, 0, vcb + h))]
    args = [lam_p, subln.reshape(1, DA_VDIM), q_arr, zc, zc]
    n_lat = 0
    if lat is not None:
        k_arr, k_col, v_arr, v_col = lat
        n_lat = k_arr.shape[1]
        in_specs += [pl.BlockSpec((1, n_lat, LANES), lambda b, h, i: (b, 0, k_col + h)),
                     pl.BlockSpec((1, n_lat, LANES), lambda b, h, i: (b, 0, v_col + h))]
        args += [k_arr, v_arr]
    ones_rows = 16
    scratch = [pltpu.VMEM((DA_VDIM + ones_rows, n_ctx + n_lat), BF16)]
    return pl.pallas_call(
        functools.partial(_attn_kernel, lam_init=lam_init, n_lat=n_lat, ck=512, prescaled=prescaled),
        grid=(bsz, DA_HEADS, nq // tq),
        in_specs=in_specs,
        out_specs=pl.BlockSpec((1, tq, LANES), lambda b, h, i: (b, i, h)),
        out_shape=jax.ShapeDtypeStruct((bsz, nq, BR_WIDTH), BF16),
        scratch_shapes=scratch,
        compiler_params=_params(("arbitrary", "arbitrary", "arbitrary")),
        name="diff_attention",
    )(*args)


def hyena_positions(L):
    t = jnp.linspace(0.0, 1.0, L, dtype=F32)[:, None]
    wpos = ((2.0 * math.pi / L) * jnp.arange(L, dtype=F32))[:, None]
    bands = jnp.linspace(1e-4, HY_BANDS - 1, HY_BANDS, dtype=F32)[None, :]
    fwd = jnp.concatenate([t, jnp.cos(bands * wpos), -jnp.sin(bands * wpos)], axis=-1)
    emb = jnp.concatenate([fwd, fwd[0:1], jnp.flip(fwd[1:], axis=0)], axis=0)
    mask = (jnp.arange(2 * L) != L).astype(F32)[:, None]
    pad = jnp.zeros((2 * L, LANES - HY_EMB - 1), F32)
    return jnp.concatenate([emb, pad, mask], axis=-1)


def _filter_kernel(feat_ref, w1_ref, b1_ref, w2_ref, b2_ref, w3_ref, fr_ref, dl_ref, k_ref, ss_ref):
    half = feat_ref.shape[0] // 2
    feat = feat_ref[...]
    f2 = jnp.concatenate([feat[0:half], feat[half:]], axis=1)
    hid = jnp.sin(fr_ref[0:1, :] * (_dot_hi(f2, w1_ref[...]) + b1_ref[...]))
    hid = jnp.sin(fr_ref[1:2, :] * (_dot_hi(hid, w2_ref[...]) + b2_ref[...]))
    w3 = w3_ref[...]
    zero = jnp.zeros_like(w3)
    h = jnp.concatenate([_dot_hi(hid, jnp.concatenate([w3, zero], axis=0)),
                         _dot_hi(hid, jnp.concatenate([zero, w3], axis=0))], axis=0)
    t = feat[:, 0:1]
    mask = feat[:, LANES - 1:LANES]
    k = h * jnp.exp(-t * dl_ref[...]) * mask
    k_ref[...] = k

    @pl.when(pl.program_id(0) == 0)
    def _():
        ss_ref[...] = jnp.zeros_like(ss_ref)

    ss_ref[...] += jnp.sum(k * k, axis=0, keepdims=True)


def hyena_filter_taps(feat, w1, b1, w2, b2, w3, freq):
    two_l = feat.shape[0]
    tr = min(two_l // 2, 1024)
    nb = two_l // tr
    hd = HY_HIDDEN
    w1d = jnp.zeros((2 * LANES, 2 * hd), F32).at[:HY_EMB, :hd].set(w1).at[LANES:LANES + HY_EMB, hd:].set(w1)
    w2d = jnp.zeros((2 * hd, 2 * hd), F32).at[:hd, :hd].set(w2).at[hd:, hd:].set(w2)
    twice = lambda v: jnp.tile(v.reshape(-1, hd), (1, 2))
    deltas = jnp.abs(jnp.linspace(HY_MIN_DECAY, HY_MAX_DECAY, HY_WIDTH, dtype=F32))[None, :]
    const = lambda i: (0, 0)
    return pl.pallas_call(
        _filter_kernel,
        grid=(nb,),
        in_specs=[pl.BlockSpec((tr, LANES), lambda i: (i, 0)),
                  pl.BlockSpec((2 * LANES, 2 * hd), const),
                  pl.BlockSpec((1, 2 * hd), const),
                  pl.BlockSpec((2 * hd, 2 * hd), const),
                  pl.BlockSpec((1, 2 * hd), const),
                  pl.BlockSpec((hd, HY_WIDTH), lambda i: (0, (2 * i) // nb)),
                  pl.BlockSpec((2, 2 * hd), const),
                  pl.BlockSpec((1, HY_WIDTH), const)],
        out_specs=[pl.BlockSpec((tr, HY_WIDTH), lambda i: (i, 0)),
                   pl.BlockSpec((1, HY_WIDTH), const)],
        out_shape=[jax.ShapeDtypeStruct((two_l, HY_WIDTH), F32),
                   jax.ShapeDtypeStruct((1, HY_WIDTH), F32)],
        compiler_params=_params(("arbitrary",)),
        name="hyena_filter",
    )(feat, w1d, twice(b1), w2d, twice(b2), w3, twice(freq), deltas)


def _short_conv_kernel(z_ref, prev_ref, next_ref, w_ref, b_ref, x0_ref, uv_ref):
    i = pl.program_id(1)
    last = pl.num_programs(1) - 1
    z = z_ref[0].astype(F32)
    tr = z.shape[0]
    halo = prev_ref.shape[1]
    before = jnp.where(i == 0, 0.0, prev_ref[0, halo - 1:halo, :].astype(F32))
    after = jnp.where(i == last, 0.0, next_ref[0, 0:1, :].astype(F32))
    row = lax.broadcasted_iota(jnp.int32, z.shape, 0)
    zm = jnp.where(row == 0, before, pltpu.roll(z, 1, axis=0))
    zp = jnp.where(row == tr - 1, after, pltpu.roll(z, tr - 1, axis=0))
    y = zm * w_ref[0:1, :] + z * w_ref[1:2, :] + zp * w_ref[2:3, :] + b_ref[...]
    x0_ref[0] = y[:, 0:HY_WIDTH].astype(BF16)
    uv_ref[0] = (y[:, HY_WIDTH:2 * HY_WIDTH] * y[:, 2 * HY_WIDTH:]).astype(BF16)


def hyena_short_conv(z, sw, sb):
    bsz, n, _ = z.shape
    tr = min(n, 512)
    halo = 16
    w3c = 3 * HY_WIDTH
    nh = n // halo
    out = jax.ShapeDtypeStruct((bsz, n, HY_WIDTH), BF16)
    return pl.pallas_call(
        _short_conv_kernel,
        grid=(bsz, n // tr),
        in_specs=[pl.BlockSpec((1, tr, w3c), lambda b, i: (b, i, 0)),
                  pl.BlockSpec((1, halo, w3c), lambda b, i: (b, jnp.maximum(i * (tr // halo) - 1, 0), 0)),
                  pl.BlockSpec((1, halo, w3c), lambda b, i: (b, jnp.minimum((i + 1) * (tr // halo), nh - 1), 0)),
                  pl.BlockSpec((3, w3c), lambda b, i: (0, 0)),
                  pl.BlockSpec((1, w3c), lambda b, i: (0, 0))],
        out_specs=[pl.BlockSpec((1, tr, HY_WIDTH), lambda b, i: (b, i, 0)),
                   pl.BlockSpec((1, tr, HY_WIDTH), lambda b, i: (b, i, 0))],
        out_shape=[out, out],
        compiler_params=_params(("arbitrary", "arbitrary")),
        name="hyena_short_conv",
    )(z, z, z, sw, sb.reshape(1, w3c))


def dft_tables():
    r = DFT_R
    n_fft = r * r
    idx = jnp.arange(r, dtype=jnp.int32)
    prod = idx[:, None] * idx[None, :]
    ang = (2.0 * math.pi / r) * (prod % r).astype(F32)
    f_re, f_im = jnp.cos(ang), -jnp.sin(ang)
    ang = (2.0 * math.pi / n_fft) * prod.astype(F32)
    t_re, t_im = jnp.cos(ang), -jnp.sin(ang)

    def cmul(a_re, a_im, b_re, b_im):
        return a_re * b_re - a_im * b_im, a_re * b_im + a_im * b_re

    g_re, g_im = cmul(f_re[None, :, :], f_im[None, :, :], t_re[:, None, :], t_im[:, None, :])
    gh_re, gh_im = cmul(f_re[None, :, :], -f_im[None, :, :], t_re[:, :, None], -t_im[:, :, None])
    return tuple(a.astype(BF16) for a in (f_re, f_im, g_re, g_im, gh_re, gh_im))


def _fft1_kernel(*refs, n_in):
    if n_in == 2:
        x_ref, fre_ref, fim_ref, are_ref, aim_ref = refs
        x0 = x_ref[0]
        x1 = x_ref[1]
        fre, fim = fre_ref[...], fim_ref[...]
        are_ref[...] = (_dot(fre, x0) - _dot(fim, x1)).astype(BF16)
        aim_ref[...] = (_dot(fre, x1) + _dot(fim, x0)).astype(BF16)
    else:
        x_ref, fre_ref, fim_ref, are_ref, aim_ref = refs
        x = x_ref[...].astype(BF16)
        are_ref[...] = _dot(fre_ref[...], x).astype(BF16)
        aim_ref[...] = _dot(fim_ref[...], x).astype(BF16)


def fft_level1(x, f_re, f_im):
    r = DFT_R
    cols = x.shape[-1]
    tc = 8192
    if x.ndim == 3:
        n_in, kdim = 2, x.shape[1]
        x_spec = pl.BlockSpec((2, kdim, tc), lambda j: (0, 0, j))
    else:
        n_in, kdim = 1, x.shape[0]
        x_spec = pl.BlockSpec((kdim, tc), lambda j: (0, j))
    out = jax.ShapeDtypeStruct((r, cols), BF16)
    return pl.pallas_call(
        functools.partial(_fft1_kernel, n_in=n_in),
        grid=(cols // tc,),
        in_specs=[x_spec,
                  pl.BlockSpec((r, kdim), lambda j: (0, 0)),
                  pl.BlockSpec((r, kdim), lambda j: (0, 0))],
        out_specs=[pl.BlockSpec((r, tc), lambda j: (0, j)), pl.BlockSpec((r, tc), lambda j: (0, j))],
        out_shape=[out, out],
        compiler_params=_params(("arbitrary",)),
        name="fft_level1",
    )(x, f_re[:, :kdim], f_im[:, :kdim])


def _fft2_conv_kernel(are_ref, aim_ref, kare_ref, kaim_ref, gre_ref, gim_ref, hre_ref, him_ref, sc_ref,
                      bre_ref, bim_ref):
    c = are_ref.shape[-1]
    sc = sc_ref[...]
    for j in range(are_ref.shape[0]):
        rhs = jnp.concatenate([are_ref[j], aim_ref[j], kare_ref[j], kaim_ref[j]], axis=1)
        p = _dot(gre_ref[j], rhs)
        q = _dot(gim_ref[j], rhs)
        x_re = p[:, 0:c] - q[:, c:2 * c]
        x_im = p[:, c:2 * c] + q[:, 0:c]
        k_re = (p[:, 2 * c:3 * c] - q[:, 3 * c:]) * sc
        k_im = (p[:, 3 * c:] + q[:, 2 * c:3 * c]) * sc
        y = jnp.concatenate([(x_re * k_re - x_im * k_im).astype(BF16),
                             (x_re * k_im + x_im * k_re).astype(BF16)], axis=1)
        r = _dot(hre_ref[j], y)
        t = _dot(him_ref[j], y)
        bre_ref[j] = (r[:, 0:c] - t[:, c:]).astype(BF16)
        bim_ref[j] = (r[:, c:] + t[:, 0:c]).astype(BF16)


def fft_level2_conv(a_re, a_im, ka_re, ka_im, g_re, g_im, gh_re, gh_im, scale):
    r = DFT_R
    c = a_re.shape[-1]
    tk = 8
    blk_a = pl.BlockSpec((tk, r, c), lambda i: (i, 0, 0))
    blk_g = pl.BlockSpec((tk, r, r), lambda i: (i, 0, 0))
    out = jax.ShapeDtypeStruct((r, r, c), BF16)
    return pl.pallas_call(
        _fft2_conv_kernel,
        grid=(r // tk,),
        in_specs=[blk_a, blk_a, blk_a, blk_a, blk_g, blk_g, blk_g, blk_g, pl.BlockSpec((1, c), lambda i: (0, 0))],
        out_specs=[blk_a, blk_a],
        out_shape=[out, out],
        compiler_params=_params(("arbitrary",)),
        name="fft_level2_conv",
    )(a_re, a_im, ka_re, ka_im, g_re, g_im, gh_re, gh_im, scale)


def _ifft1_kernel(bre_ref, bim_ref, fre_ref, fim_ref, uv_ref, x0_ref, bias_ref, o_ref):
    fre, fim = fre_ref[...], fim_ref[...]
    b_re, b_im = bre_ref[...], bim_ref[...]
    y0 = _dot(fre, b_re) + _dot(fim, b_im)
    y1 = _dot(fre, b_im) - _dot(fim, b_re)
    bias = bias_ref[...]
    o_ref[0] = ((y0 + uv_ref[0].astype(F32) * bias) * x0_ref[0].astype(F32)).astype(BF16)
    o_ref[1] = ((y1 + uv_ref[1].astype(F32) * bias) * x0_ref[1].astype(F32)).astype(BF16)


def ifft_level1(b_re, b_im, f_re, f_im, uv, x0, bias_t):
    r = DFT_R
    half = r // 2
    cols = b_re.shape[-1]
    tc = 8192
    blk_b = pl.BlockSpec((r, tc), lambda j: (0, j))
    blk_f = pl.BlockSpec((half, r), lambda j: (0, 0))
    blk_x = pl.BlockSpec((2, half, tc), lambda j: (0, 0, j))
    return pl.pallas_call(
        _ifft1_kernel,
        grid=(cols // tc,),
        in_specs=[blk_b, blk_b, blk_f, blk_f, blk_x, blk_x, pl.BlockSpec((1, tc), lambda j: (0, 0))],
        out_specs=blk_x,
        out_shape=jax.ShapeDtypeStruct((2, half, cols), BF16),
        compiler_params=_params(("arbitrary",)),
        name="ifft_level1",
    )(b_re, b_im, f_re[:half], f_im[:half], uv, x0, bias_t)


def hyena_long_conv(uv, x0, taps, sumsq, bias, tables):
    f_re, f_im, g_re, g_im, gh_re, gh_im = tables
    r = DFT_R
    bsz, L, c = uv.shape
    assert bsz == 2 and 2 * L == r * r
    scale = lax.rsqrt(sumsq) * (1.0 / (r * r))
    ka_re, ka_im = fft_level1(taps.reshape(r, r * c), f_re, f_im)
    uv2 = uv.reshape(2, r // 2, r * c)
    a_re, a_im = fft_level1(uv2, f_re, f_im)
    b_re, b_im = fft_level2_conv(a_re.reshape(r, r, c), a_im.reshape(r, r, c), ka_re.reshape(r, r, c),
                                 ka_im.reshape(r, r, c), g_re, g_im, gh_re, gh_im, scale)
    bias_t = jnp.tile(bias.reshape(1, c), (1, 8192 // c))
    y = ifft_level1(b_re.reshape(r, r * c), b_im.reshape(r, r * c), f_re, f_im, uv2,
                    x0.reshape(2, r // 2, r * c), bias_t)
    return y.reshape(2, L, c)


def dft_small_tables(n_fft):
    idx = jnp.arange(n_fft, dtype=jnp.int32)
    ang = (2.0 * math.pi / n_fft) * ((idx[:, None] * idx[None, :]) % n_fft).astype(F32)
    return jnp.cos(ang).astype(BF16), (-jnp.sin(ang)).astype(BF16)


def _conv_small_kernel(taps_ref, ss_ref, fre_ref, fim_ref, uv_ref, x0_ref, bias_ref, o_ref):
    two_l = taps_ref.shape[0]
    L = two_l // 2
    fre, fim = fre_ref[...], fim_ref[...]
    scale = lax.rsqrt(ss_ref[...]) * (1.0 / two_l)
    taps = taps_ref[...].astype(BF16)
    k_re = _dot(fre, taps) * scale
    k_im = _dot(fim, taps) * scale
    u0, u1 = uv_ref[0], uv_ref[1]
    fre_l, fim_l = fre[:, :L], fim[:, :L]
    x_re = _dot(fre_l, u0) - _dot(fim_l, u1)
    x_im = _dot(fre_l, u1) + _dot(fim_l, u0)
    y_re = (x_re * k_re - x_im * k_im).astype(BF16)
    y_im = (x_re * k_im + x_im * k_re).astype(BF16)
    fre_t, fim_t = fre[:L, :], fim[:L, :]
    y0 = _dot(fre_t, y_re) + _dot(fim_t, y_im)
    y1 = _dot(fre_t, y_im) - _dot(fim_t, y_re)
    bias = bias_ref[...]
    o_ref[0] = ((y0 + u0.astype(F32) * bias) * x0_ref[0].astype(F32)).astype(BF16)
    o_ref[1] = ((y1 + u1.astype(F32) * bias) * x0_ref[1].astype(F32)).astype(BF16)


def hyena_long_conv_small(uv, x0, taps, sumsq, bias, tables):
    bsz, L, c = uv.shape
    assert bsz == 2
    f_re, f_im = tables
    full = lambda shape: pl.BlockSpec(shape, lambda i: (0,) * len(shape))
    return pl.pallas_call(
        _conv_small_kernel,
        grid=(1,),
        in_specs=[full((2 * L, c)), full((1, c)), full((2 * L, 2 * L)), full((2 * L, 2 * L)),
                  full((2, L, c)), full((2, L, c)), full((1, c))],
        out_specs=full((2, L, c)),
        out_shape=jax.ShapeDtypeStruct((2, L, c), BF16),
        compiler_params=_params(("arbitrary",)),
        name="hyena_conv_small",
    )(taps, sumsq, f_re, f_im, uv, x0, bias.reshape(1, c))


def _gmlp_kernel(z_ref, g_ref, b_ref, ws_ref, bs_ref, o_ref):
    zg = z_ref[0].astype(F32)
    gl = 0.5 * zg * (1.0 + lax.erf(zg * (2.0 ** -0.5)))
    w = BR_WIDTH
    u = gl[:, :w]
    v = gl[:, w:]
    mu = jnp.mean(v, axis=-1, keepdims=True)
    var = jnp.mean(jnp.square(v - mu), axis=-1, keepdims=True)
    v = ((v - mu) * lax.rsqrt(var + EPS) * g_ref[...] + b_ref[...]).astype(BF16)
    gw = w // GM_GROUPS
    lane = lax.broadcasted_iota(jnp.int32, (GM_CHUNK, LANES), 1)
    first = lane < gw
    for ci in range(zg.shape[0] // GM_CHUNK):
        rows = slice(ci * GM_CHUNK, (ci + 1) * GM_CHUNK)
        tiles = []
        for t in range(w // LANES):
            vt = v[rows, t * LANES:(t + 1) * LANES]
            tiles.append(jnp.where(first, _dot(ws_ref[2 * t], vt), _dot(ws_ref[2 * t + 1], vt)))
        vm = jnp.concatenate(tiles, axis=1) + bs_ref[...]
        o_ref[0, rows, :] = (u[rows] * vm).astype(BF16)


def gmlp(z, ln_g, ln_b, ws, bs):
    bsz, n, _ = z.shape
    tr = min(n, 512)
    w = BR_WIDTH
    bs_full = jnp.repeat(bs.T, w // GM_GROUPS, axis=1)
    return pl.pallas_call(
        _gmlp_kernel,
        grid=(bsz, n // tr),
        in_specs=[pl.BlockSpec((1, tr, 2 * w), lambda b, i: (b, i, 0)),
                  pl.BlockSpec((1, w), lambda b, i: (0, 0)),
                  pl.BlockSpec((1, w), lambda b, i: (0, 0)),
                  pl.BlockSpec((GM_GROUPS, GM_CHUNK, GM_CHUNK), lambda b, i: (0, 0, 0)),
                  pl.BlockSpec((GM_CHUNK, w), lambda b, i: (0, 0))],
        out_specs=pl.BlockSpec((1, tr, w), lambda b, i: (b, i, 0)),
        out_shape=jax.ShapeDtypeStruct((bsz, n, w), BF16),
        compiler_params=_params(("arbitrary", "arbitrary")),
        name="gmlp",
    )(z, ln_g.reshape(1, w), ln_b.reshape(1, w), ws.astype(BF16), bs_full)


def _merge_kernel(ya_ref, yb_ref, yc_ref, ga_ref, gb_ref, gc_ref, mg_ref, x_ref, mod_ref, np_ref,
                  wb_ref, wo_ref, o_ref):
    acc = None
    for i, (y_ref, g_ref) in enumerate(((ya_ref, ga_ref), (yb_ref, gb_ref), (yc_ref, gc_ref))):
        g = g_ref[0]
        gated = y_ref[0] * (g * jax.nn.sigmoid(g))
        sel = jax.nn.sigmoid(mg_ref[0, :, i * D_MODEL:(i + 1) * D_MODEL])
        term = sel * _dot(gated, wb_ref[i]).astype(BF16)
        acc = term if acc is None else acc + term
    out = _dot(acc, wo_ref[...])
    r = out * lax.rsqrt(jnp.mean(out * out, axis=-1, keepdims=True) + EPS) * np_ref[...]
    gt = mod_ref[0, :, 2 * D_MODEL:]
    o_ref[0] = x_ref[0] + gt * r


def merge_out(ya, yb, yc, zat, zmg, x, mod_rows, row_of_batch, npost, wb, wo):
    bsz, n, _ = x.shape
    tm = min(n, 512)
    w = BR_WIDTH
    yspec = pl.BlockSpec((1, tm, w), lambda b, i: (b, i, 0))

    def zspec(name):
        cb = ATC[name] // w
        return pl.BlockSpec((1, tm, w), lambda b, i: (b, i, cb))

    return pl.pallas_call(
        _merge_kernel,
        grid=(bsz, n // tm),
        in_specs=[yspec, yspec, yspec, zspec("GA"), zspec("GB"), zspec("GC"),
                  pl.BlockSpec((1, tm, 3 * D_MODEL), lambda b, i: (b, i, 0)),
                  pl.BlockSpec((1, tm, D_MODEL), lambda b, i: (b, i, 0)),
                  pl.BlockSpec((1, 1, 3 * D_MODEL), lambda b, i: (row_of_batch(b), 0, 0)),
                  pl.BlockSpec((1, D_MODEL), lambda b, i: (0, 0)),
                  pl.BlockSpec((3, w, D_MODEL), lambda b, i: (0, 0, 0)),
                  pl.BlockSpec((D_MODEL, D_MODEL), lambda b, i: (0, 0))],
        out_specs=pl.BlockSpec((1, tm, D_MODEL), lambda b, i: (b, i, 0)),
        out_shape=jax.ShapeDtypeStruct((bsz, n, D_MODEL), F32),
        compiler_params=_params(("arbitrary", "arbitrary")),
        name="merge_out",
    )(ya, yb, yc, zat, zat, zat, zmg, x, mod_rows, npost.reshape(1, D_MODEL), wb, wo)


def _group_weights(w):
    def cols(nm):
        start, width = _REF_COLS[nm]
        return w[:, start:start + width]
    return [jnp.concatenate([cols(nm) for nm in names], axis=1).astype(BF16) for _, names in _GROUPS]


def kernel(x, c, ctx, c_ctx, ada_w, ada_b, norm_pre, norm_post, w_in, da_lambda, da_subln, hy_short_w,
           hy_short_b, hy_f_w1, hy_f_b1, hy_f_w2, hy_f_b2, hy_f_w3, hy_f_freq, hy_bias, gm_ln_g, gm_ln_b,
           gm_ws, gm_bs, w_branch, w_out):
    bsz, n, _ = x.shape
    n_ctx = ctx.shape[1]
    assert bsz == 2 and 2 * n == DFT_R * DFT_R

    cond8 = jnp.zeros((8, D_MODEL), F32).at[0:bsz].set(c).at[bsz].set(c_ctx)
    mod = modulation_all(cond8, ada_w, ada_b)
    lat_row = lambda b: b
    ctx_row = lambda b: bsz

    cos_t, sin_t = rope_tables(n)
    feat = hyena_positions(n)
    feat_c = hyena_positions(n_ctx)
    tables = dft_tables()
    tables_c = dft_small_tables(2 * n_ctx)
    vb, qb = ATC["V"] // LANES, ATC["Q"] // LANES

    xc = ctx
    for l in range(DEPTH):
        last = l == DEPTH - 1
        lam_init = 0.8 - 0.6 * math.exp(-0.3 * l)
        mod_rows = mod[l].reshape(8, 1, 3 * D_MODEL)
        w_l = _group_weights(w_in[l])
        wb_l = w_branch[l].astype(BF16)
        wo_l = w_out[l].astype(BF16)
        filt_w = (hy_f_w1[l], hy_f_b1[l], hy_f_w2[l], hy_f_b2[l], hy_f_w3[l], hy_f_freq[l])

        z, qr, kr = in_projection(x, mod_rows, lat_row, norm_pre[l], w_l, (cos_t, sin_t))
        zc = in_projection(xc, mod_rows, ctx_row, norm_pre[l], w_l)

        taps, sumsq = hyena_filter_taps(feat, *filt_w)
        x0, uv = hyena_short_conv(z["HY"], hy_short_w[l], hy_short_b[l])
        y_a = diff_attention(qr, 0, zc["AT"], (kr, 0, z["AT"], vb), da_lambda[l], da_subln[l], lam_init, True)
        y_b = hyena_long_conv(uv, x0, taps, sumsq, hy_bias[l], tables)
        y_c = gmlp(z["GM"], gm_ln_g[l], gm_ln_b[l], gm_ws[l], gm_bs[l])
        x_new = merge_out(y_a, y_b, y_c, z["AT"], z["MG"], x, mod_rows, lat_row, norm_post[l], wb_l, wo_l)

        if not last:
            yc_a = diff_attention(zc["AT"], qb, zc["AT"], None, da_lambda[l], da_subln[l], lam_init, False)
            taps_c, sumsq_c = hyena_filter_taps(feat_c, *filt_w)
            x0c, uvc = hyena_short_conv(zc["HY"], hy_short_w[l], hy_short_b[l])
            yc_b = hyena_long_conv_small(uvc, x0c, taps_c, sumsq_c, hy_bias[l], tables_c)
            yc_c = gmlp(zc["GM"], gm_ln_g[l], gm_ln_b[l], gm_ws[l], gm_bs[l])
            xc = merge_out(yc_a, yc_b, yc_c, zc["AT"], zc["MG"], xc, mod_rows, ctx_row, norm_post[l], wb_l, wo_l)
        x = x_new
    return x
```

```python
import functools
import math

import jax
import jax.numpy as jnp
from jax import lax
from jax.experimental import pallas as pl
from jax.experimental.pallas import tpu as pltpu

F32 = jnp.float32
BF16 = jnp.bfloat16

D_MODEL = 1024
DEPTH = 4
GRID_W = 64
EPS = 1e-6
BR_WIDTH = 512
DA_SUB = 64
DA_VDIM = 128
DA_HEADS = 4
ROPE_BASE = 10000.0
ROPE_NF = 16
HY_WIDTH = 512
HY_BANDS = 16
HY_EMB = 33
HY_HIDDEN = 64
HY_MIN_DECAY = math.log(1e-2) / 1.5
HY_MAX_DECAY = math.log(1e-2) / 0.3
GM_GROUPS = 8
GM_CHUNK = 128

LANES = 128
DFT_R = 128
QK_SCALE = DA_SUB ** -0.5 * math.log2(math.e)

_REF_COLS = dict(K=(0, 512), V=(512, 512), Q=(1024, 512), GA=(1536, 512), HY=(2048, 1536),
                 GB=(3584, 512), GM=(4096, 1024), GC=(5120, 512), MG=(5632, 3072))
_GROUPS = (("MG", ("MG",)), ("HY", ("HY",)), ("GM", ("GM",)), ("AT", ("K", "V", "Q", "GA", "GB", "GC")))
GROUP_WIDTH = {g: sum(_REF_COLS[nm][1] for nm in names) for g, names in _GROUPS}
ATC = {}
_off = 0
for _name in _GROUPS[-1][1]:
    ATC[_name] = _off
    _off += _REF_COLS[_name][1]

VMEM_LIMIT = 48 * 1024 * 1024
INPROJ_VMEM_LIMIT = 56 * 1024 * 1024


def _params(sem):
    return pltpu.CompilerParams(dimension_semantics=sem, vmem_limit_bytes=VMEM_LIMIT)


def _dot(a, b):
    return jnp.dot(a, b, preferred_element_type=F32)


def _dot_hi(a, b):
    return jnp.dot(a, b, preferred_element_type=F32, precision=lax.Precision.HIGHEST)


def _mod_kernel(c_ref, w_ref, b_ref, o_ref):
    cond = c_ref[...]
    s = cond * jax.nn.sigmoid(cond)
    o_ref[0] = _dot_hi(s, w_ref[0]) + b_ref[0]


def modulation_all(cond8, ada_w, ada_b):
    tn = 1024
    return pl.pallas_call(
        _mod_kernel,
        grid=(DEPTH, 3 * D_MODEL // tn),
        in_specs=[pl.BlockSpec((8, D_MODEL), lambda l, j: (0, 0)),
                  pl.BlockSpec((1, D_MODEL, tn), lambda l, j: (l, 0, j)),
                  pl.BlockSpec((1, 1, tn), lambda l, j: (l, 0, j))],
        out_specs=pl.BlockSpec((1, 8, tn), lambda l, j: (l, 0, j)),
        out_shape=jax.ShapeDtypeStruct((DEPTH, 8, 3 * D_MODEL), F32),
        compiler_params=_params(("arbitrary", "arbitrary")),
        name="modulation",
    )(cond8, ada_w, ada_b.reshape(DEPTH, 1, 3 * D_MODEL))


_QK_WIDTH = DA_HEADS * LANES
MXU_TILE = 256
_INPROJ_CHUNKS = dict(MG=((0, 2048), (2048, 3072)), HY=((0, 1536),), GM=((0, 1024),),
                      AT=((ATC["K"], ATC["V"]), (ATC["V"], ATC["Q"]), (ATC["Q"], ATC["GA"]),
                          (ATC["GA"], GROUP_WIDTH["AT"])))
for _g, _chunks in _INPROJ_CHUNKS.items():
    assert _chunks[0][0] == 0 and _chunks[-1][1] == GROUP_WIDTH[_g]
    assert all(a % MXU_TILE == 0 and b % MXU_TILE == 0 for a, b in _chunks)


def _inproj_kernel(*refs, rope):
    ng = len(_GROUPS)
    x_ref, mod_ref, g_ref = refs[0:3]
    w_refs = refs[3:3 + ng]
    if rope:
        cos_ref, sin_ref = refs[3 + ng:5 + ng]
        o_refs = refs[5 + ng:5 + 2 * ng]
        q_ref, k_ref = refs[5 + 2 * ng:]
    else:
        o_refs = refs[3 + ng:3 + 2 * ng]
    x = x_ref[0]
    y = x * lax.rsqrt(jnp.mean(x * x, axis=-1, keepdims=True) + EPS) * g_ref[...]
    sh = mod_ref[0, :, 0:D_MODEL]
    sc = mod_ref[0, :, D_MODEL:2 * D_MODEL]
    h = (y * (1.0 + sc) + sh).astype(BF16)

    if rope:
        lane = lax.broadcasted_iota(jnp.int32, cos_ref.shape, 1)
        low = (lane % (2 * ROPE_NF)) < ROPE_NF
        cs = cos_ref[...]
        sn = sin_ref[...]

        def rot(v):
            partner = jnp.where(low, pltpu.roll(v, LANES - ROPE_NF, axis=1), pltpu.roll(v, ROPE_NF, axis=1))
            return (v * cs + partner * sn).astype(BF16)

    for (gname, _), w_ref, o_ref in zip(_GROUPS, w_refs, o_refs):
        for a, b in _INPROJ_CHUNKS[gname]:
            zc = _dot(h, w_ref[:, a:b])
            o_ref[0, :, a:b] = zc.astype(BF16)
            if rope and gname == "AT" and a in (ATC["K"], ATC["Q"]):
                dst, scale = (k_ref, 1.0) if a == ATC["K"] else (q_ref, QK_SCALE)
                for hd in range(DA_HEADS):
                    cols = slice(hd * LANES, (hd + 1) * LANES)
                    dst[0, :, cols] = rot(zc[:, cols] * scale)


def in_projection(x, mod_rows, row_of_batch, g, ws, rope_tabs=None):
    bsz, n, _ = x.shape
    tm = min(n, 512)
    rope = rope_tabs is not None
    in_specs = [pl.BlockSpec((1, tm, D_MODEL), lambda b, i: (b, i, 0)),
                pl.BlockSpec((1, 1, 3 * D_MODEL), lambda b, i: (row_of_batch(b), 0, 0)),
                pl.BlockSpec((1, D_MODEL), lambda b, i: (0, 0))]
    in_specs += [pl.BlockSpec((D_MODEL, GROUP_WIDTH[gn]), lambda b, i: (0, 0), pipeline_mode=pl.Buffered(1))
                 for gn, _ in _GROUPS]
    args = [x, mod_rows, g.reshape(1, D_MODEL)] + list(ws)
    if rope:
        in_specs += [pl.BlockSpec((tm, LANES), lambda b, i: (i, 0))] * 2
        args += list(rope_tabs)
    out_specs = [pl.BlockSpec((1, tm, GROUP_WIDTH[gn]), lambda b, i: (b, i, 0)) for gn, _ in _GROUPS]
    out_shape = [jax.ShapeDtypeStruct((bsz, n, GROUP_WIDTH[gn]), BF16) for gn, _ in _GROUPS]
    if rope:
        out_specs += [pl.BlockSpec((1, tm, _QK_WIDTH), lambda b, i: (b, i, 0))] * 2
        out_shape += [jax.ShapeDtypeStruct((bsz, n, _QK_WIDTH), BF16)] * 2
    res = pl.pallas_call(
        functools.partial(_inproj_kernel, rope=rope),
        grid=(bsz, n // tm),
        in_specs=in_specs,
        out_specs=out_specs,
        out_shape=out_shape,
        compiler_params=pltpu.CompilerParams(dimension_semantics=("arbitrary", "arbitrary"),
                                             vmem_limit_bytes=INPROJ_VMEM_LIMIT),
        name="in_projection",
    )(*args)
    z = {gn: r for (gn, _), r in zip(_GROUPS, res)}
    return (z, res[-2], res[-1]) if rope else z


def rope_tables(n):
    pos = jnp.arange(n)
    row = (pos // GRID_W).astype(F32)
    col = (pos % GRID_W).astype(F32)
    inv = ROPE_BASE ** (-jnp.arange(ROPE_NF, dtype=F32) / ROPE_NF)
    ar = row[:, None] * inv
    ac = col[:, None] * inv
    cos64 = jnp.concatenate([jnp.cos(ar), jnp.cos(ar), jnp.cos(ac), jnp.cos(ac)], axis=1)
    sin64 = jnp.concatenate([-jnp.sin(ar), jnp.sin(ar), -jnp.sin(ac), jnp.sin(ac)], axis=1)
    return jnp.tile(cos64, (1, 2)), jnp.tile(sin64, (1, 2))


def _attn_kernel(*refs, lam_init, n_lat, ck, prescaled):
    if n_lat:
        lam_ref, g_ref, q_ref, kc_ref, vc_ref, k_ref, v_ref, o_ref, vt_ref = refs
    else:
        lam_ref, g_ref, q_ref, kc_ref, vc_ref, o_ref, vt_ref = refs
    tq = q_ref.shape[1]
    n_ctx = kc_ref.shape[1]

    @pl.when(pl.program_id(2) == 0)
    def _():
        vt_ref[0:DA_VDIM, 0:n_ctx] = vc_ref[0].astype(F32).T.astype(BF16)
        for c in range(n_lat // ck):
            vt_ref[0:DA_VDIM, n_ctx + c * ck:n_ctx + (c + 1) * ck] = (
                v_ref[0, c * ck:(c + 1) * ck, :].astype(F32).T.astype(BF16))
        pad = vt_ref.shape[0] - DA_VDIM
        row = lax.broadcasted_iota(jnp.int32, (pad, vt_ref.shape[1]), 0)
        vt_ref[DA_VDIM:, :] = (row == 0).astype(BF16)

    lp = lam_ref[...]
    lam = (jnp.exp(jnp.sum(lp[0:1] * lp[1:2], axis=1, keepdims=True))
           - jnp.exp(jnp.sum(lp[2:3] * lp[3:4], axis=1, keepdims=True)) + lam_init)

    q = q_ref[0]
    if not prescaled:
        q = (q.astype(F32) * QK_SCALE).astype(BF16)
    lane = lax.broadcasted_iota(jnp.int32, q.shape, 1)
    zero = jnp.zeros_like(q)
    qq = jnp.concatenate([jnp.where(lane < DA_SUB, q, zero), jnp.where(lane >= DA_SUB, q, zero)], axis=0)

    chunks = [(lambda: kc_ref[0], 0, n_ctx)]
    for c in range(n_lat // ck):
        chunks.append((lambda c=c: k_ref[0, c * ck:(c + 1) * ck, :], n_ctx + c * ck, ck))

    def scores_t(c):
        return lax.dot_general(chunks[c][0](), qq, (((1,), (1,)), ((), ())), preferred_element_type=F32)

    def pv_t(p_t, c):
        _, off, width = chunks[c]
        return _dot(vt_ref[:, off:off + width], p_t)

    m = jnp.full((1, 2 * tq), -jnp.inf, F32)
    acc = jnp.zeros((vt_ref.shape[0], 2 * tq), F32)
    pending = None
    s_next = scores_t(0)
    for c in range(len(chunks)):
        s = s_next
        if c + 1 < len(chunks):
            s_next = scores_t(c + 1)
        part = s[0:64]
        for r in range(64, s.shape[0], 64):
            part = jnp.maximum(part, s[r:r + 64])
        m_new = jnp.maximum(m, jnp.max(part, axis=0, keepdims=True))
        alpha = jnp.exp2(m - m_new)
        p_t = jnp.exp2(s - m_new).astype(BF16)
        m = m_new
        if pending is not None:
            p_prev, alpha_prev, c_prev = pending
            acc = acc * alpha_prev + pv_t(p_prev, c_prev)
        pending = (p_t, alpha, c)
    p_prev, alpha_prev, c_prev = pending
    acc = acc * alpha_prev + pv_t(p_prev, c_prev)

    o_t = acc[0:DA_VDIM, :] / acc[DA_VDIM:DA_VDIM + 1, :]
    d = (o_t[:, 0:tq] - lam * o_t[:, tq:]).T
    y = d * lax.rsqrt(jnp.mean(d * d, axis=-1, keepdims=True) + EPS) * g_ref[...]
    o_ref[0] = (y * (1.0 - lam_init)).astype(BF16)


def diff_attention(q_arr, q_col, zc, lat, lam_p, subln, lam_init, prescaled):
    bsz, nq, _ = q_arr.shape
    n_ctx = zc.shape[1]
    tq = min(nq, 512)
    kcb, vcb = ATC["K"] // LANES, ATC["V"] // LANES
    in_specs = [pl.BlockSpec((4, DA_SUB), lambda b, h, i: (0, 0)),
                pl.BlockSpec((1, DA_VDIM), lambda b, h, i: (0, 0)),
                pl.BlockSpec((1, tq, LANES), lambda b, h, i: (b, i, q_col + h)),
                pl.BlockSpec((1, n_ctx, LANES), lambda b, h, i: (b, 0, kcb + h)),
                pl.BlockSpec((1, n_ctx, LANES), lambda b, h, i: (b, 0, vcb + h))]
    args = [lam_p, subln.reshape(1, DA_VDIM), q_arr, zc, zc]
    n_lat = 0
    if lat is not None:
        k_arr, k_col, v_arr, v_col = lat
        n_lat = k_arr.shape[1]
        in_specs += [pl.BlockSpec((1, n_lat, LANES), lambda b, h, i: (b, 0, k_col + h)),
                     pl.BlockSpec((1, n_lat, LANES), lambda b, h, i: (b, 0, v_col + h))]
        args += [k_arr, v_arr]
    ones_rows = 16
    scratch = [pltpu.VMEM((DA_VDIM + ones_rows, n_ctx + n_lat), BF16)]
    return pl.pallas_call(
        functools.partial(_attn_kernel, lam_init=lam_init, n_lat=n_lat, ck=512, prescaled=prescaled),
        grid=(bsz, DA_HEADS, nq // tq),
        in_specs=in_specs,
        out_specs=pl.BlockSpec((1, tq, LANES), lambda b, h, i: (b, i, h)),
        out_shape=jax.ShapeDtypeStruct((bsz, nq, BR_WIDTH), BF16),
        scratch_shapes=scratch,
        compiler_params=_params(("arbitrary", "arbitrary", "arbitrary")),
        name="diff_attention",
    )(*args)


def hyena_positions(L):
    t = jnp.linspace(0.0, 1.0, L, dtype=F32)[:, None]
    wpos = ((2.0 * math.pi / L) * jnp.arange(L, dtype=F32))[:, None]
    bands = jnp.linspace(1e-4, HY_BANDS - 1, HY_BANDS, dtype=F32)[None, :]
    fwd = jnp.concatenate([t, jnp.cos(bands * wpos), -jnp.sin(bands * wpos)], axis=-1)
    emb = jnp.concatenate([fwd, fwd[0:1], jnp.flip(fwd[1:], axis=0)], axis=0)
    mask = (jnp.arange(2 * L) != L).astype(F32)[:, None]
    pad = jnp.zeros((2 * L, LANES - HY_EMB - 1), F32)
    return jnp.concatenate([emb, pad, mask], axis=-1)


def _filter_kernel(feat_ref, w1_ref, b1_ref, w2_ref, b2_ref, w3_ref, fr_ref, dl_ref, k_ref, ss_ref):
    half = feat_ref.shape[0] // 2
    feat = feat_ref[...]
    f2 = jnp.concatenate([feat[0:half], feat[half:]], axis=1)
    hid = jnp.sin(fr_ref[0:1, :] * (_dot_hi(f2, w1_ref[...]) + b1_ref[...]))
    hid = jnp.sin(fr_ref[1:2, :] * (_dot_hi(hid, w2_ref[...]) + b2_ref[...]))
    w3 = w3_ref[...]
    zero = jnp.zeros_like(w3)
    h = jnp.concatenate([_dot_hi(hid, jnp.concatenate([w3, zero], axis=0)),
                         _dot_hi(hid, jnp.concatenate([zero, w3], axis=0))], axis=0)
    t = feat[:, 0:1]
    mask = feat[:, LANES - 1:LANES]
    k = h * jnp.exp(-t * dl_ref[...]) * mask
    k_ref[...] = k.astype(k_ref.dtype)

    @pl.when(pl.program_id(0) == 0)
    def _():
        ss_ref[...] = jnp.zeros_like(ss_ref)

    ss_ref[...] += jnp.sum(k * k, axis=0, keepdims=True)


def hyena_filter_taps(feat, w1, b1, w2, b2, w3, freq):
    two_l = feat.shape[0]
    tr = min(two_l // 2, 1024)
    nb = two_l // tr
    hd = HY_HIDDEN
    w1d = jnp.zeros((2 * LANES, 2 * hd), F32).at[:HY_EMB, :hd].set(w1).at[LANES:LANES + HY_EMB, hd:].set(w1)
    w2d = jnp.zeros((2 * hd, 2 * hd), F32).at[:hd, :hd].set(w2).at[hd:, hd:].set(w2)
    twice = lambda v: jnp.tile(v.reshape(-1, hd), (1, 2))
    deltas = jnp.abs(jnp.linspace(HY_MIN_DECAY, HY_MAX_DECAY, HY_WIDTH, dtype=F32))[None, :]
    const = lambda i: (0, 0)
    return pl.pallas_call(
        _filter_kernel,
        grid=(nb,),
        in_specs=[pl.BlockSpec((tr, LANES), lambda i: (i, 0)),
                  pl.BlockSpec((2 * LANES, 2 * hd), const),
                  pl.BlockSpec((1, 2 * hd), const),
                  pl.BlockSpec((2 * hd, 2 * hd), const),
                  pl.BlockSpec((1, 2 * hd), const),
                  pl.BlockSpec((hd, HY_WIDTH), lambda i: (0, (2 * i) // nb)),
                  pl.BlockSpec((2, 2 * hd), const),
                  pl.BlockSpec((1, HY_WIDTH), const)],
        out_specs=[pl.BlockSpec((tr, HY_WIDTH), lambda i: (i, 0)),
                   pl.BlockSpec((1, HY_WIDTH), const)],
        out_shape=[jax.ShapeDtypeStruct((two_l, HY_WIDTH), BF16),
                   jax.ShapeDtypeStruct((1, HY_WIDTH), F32)],
        compiler_params=_params(("arbitrary",)),
        name="hyena_filter",
    )(feat, w1d, twice(b1), w2d, twice(b2), w3, twice(freq), deltas)


def _short_conv_kernel(z_ref, prev_ref, next_ref, w_ref, b_ref, x0_ref, uv_ref):
    i = pl.program_id(1)
    last = pl.num_programs(1) - 1
    z = z_ref[0].astype(F32)
    tr = z.shape[0]
    halo = prev_ref.shape[1]
    before = jnp.where(i == 0, 0.0, prev_ref[0, halo - 1:halo, :].astype(F32))
    after = jnp.where(i == last, 0.0, next_ref[0, 0:1, :].astype(F32))
    row = lax.broadcasted_iota(jnp.int32, z.shape, 0)
    zm = jnp.where(row == 0, before, pltpu.roll(z, 1, axis=0))
    zp = jnp.where(row == tr - 1, after, pltpu.roll(z, tr - 1, axis=0))
    y = zm * w_ref[0:1, :] + z * w_ref[1:2, :] + zp * w_ref[2:3, :] + b_ref[...]
    x0_ref[0] = y[:, 0:HY_WIDTH].astype(BF16)
    uv_ref[0] = (y[:, HY_WIDTH:2 * HY_WIDTH] * y[:, 2 * HY_WIDTH:]).astype(BF16)


def hyena_short_conv(z, sw, sb):
    bsz, n, _ = z.shape
    tr = min(n, 512)
    halo = 16
    w3c = 3 * HY_WIDTH
    nh = n // halo
    out = jax.ShapeDtypeStruct((bsz, n, HY_WIDTH), BF16)
    return pl.pallas_call(
        _short_conv_kernel,
        grid=(bsz, n // tr),
        in_specs=[pl.BlockSpec((1, tr, w3c), lambda b, i: (b, i, 0)),
                  pl.BlockSpec((1, halo, w3c), lambda b, i: (b, jnp.maximum(i * (tr // halo) - 1, 0), 0)),
                  pl.BlockSpec((1, halo, w3c), lambda b, i: (b, jnp.minimum((i + 1) * (tr // halo), nh - 1), 0)),
                  pl.BlockSpec((3, w3c), lambda b, i: (0, 0)),
                  pl.BlockSpec((1, w3c), lambda b, i: (0, 0))],
        out_specs=[pl.BlockSpec((1, tr, HY_WIDTH), lambda b, i: (b, i, 0)),
                   pl.BlockSpec((1, tr, HY_WIDTH), lambda b, i: (b, i, 0))],
        out_shape=[out, out],
        compiler_params=_params(("arbitrary", "arbitrary")),
        name="hyena_short_conv",
    )(z, z, z, sw, sb.reshape(1, w3c))


def dft_tables():
    r = DFT_R
    n_fft = r * r
    idx = jnp.arange(r, dtype=jnp.int32)
    prod = idx[:, None] * idx[None, :]
    ang = (2.0 * math.pi / r) * (prod % r).astype(F32)
    f_re, f_im = jnp.cos(ang), -jnp.sin(ang)
    ang = (2.0 * math.pi / n_fft) * prod.astype(F32)
    t_re, t_im = jnp.cos(ang), -jnp.sin(ang)

    def cmul(a_re, a_im, b_re, b_im):
        return a_re * b_re - a_im * b_im, a_re * b_im + a_im * b_re

    g_re, g_im = cmul(f_re[None, :, :], f_im[None, :, :], t_re[:, None, :], t_im[:, None, :])
    gh_re, gh_im = cmul(f_re[None, :, :], -f_im[None, :, :], t_re[:, :, None], -t_im[:, :, None])
    def cplx(re, im):
        return jnp.concatenate([jnp.concatenate([re, -im], axis=-1),
                                jnp.concatenate([im, re], axis=-1)], axis=-2).astype(BF16)

    half = r // 2
    return dict(f_pad=cplx(f_re[:, :half], f_im[:, :half]),
                f_real=jnp.concatenate([f_re, f_im], axis=0).astype(BF16),
                g=cplx(g_re, g_im), gh=cplx(gh_re, gh_im),
                f_inv=cplx(f_re[:half], -f_im[:half]))


def _fft1_kernel(x_ref, f_ref, are_ref, aim_ref):
    r = are_ref.shape[0]
    x = x_ref[...]
    p = _dot(f_ref[...], x.reshape(-1, x.shape[-1]))
    are_ref[...] = p[0:r].astype(BF16)
    aim_ref[...] = p[r:].astype(BF16)


def fft_level1(x, f):
    r = DFT_R
    cols = x.shape[-1]
    tc = 8192
    if x.ndim == 3:
        x_spec = pl.BlockSpec((2, x.shape[1], tc), lambda j: (0, 0, j))
    else:
        x_spec = pl.BlockSpec((x.shape[0], tc), lambda j: (0, j))
    out = jax.ShapeDtypeStruct((r, cols), BF16)
    return pl.pallas_call(
        _fft1_kernel,
        grid=(cols // tc,),
        in_specs=[x_spec, pl.BlockSpec(f.shape, lambda j: (0, 0))],
        out_specs=[pl.BlockSpec((r, tc), lambda j: (0, j)), pl.BlockSpec((r, tc), lambda j: (0, j))],
        out_shape=[out, out],
        compiler_params=_params(("arbitrary",)),
        name="fft_level1",
    )(x, f)


def _fft2_conv_kernel(are_ref, aim_ref, kare_ref, kaim_ref, g_ref, h_ref, sc_ref, bre_ref, bim_ref):
    r = are_ref.shape[1]
    c = are_ref.shape[-1]
    sc = sc_ref[...]
    for j in range(are_ref.shape[0]):
        rhs = jnp.concatenate([jnp.concatenate([are_ref[j], aim_ref[j]], axis=0),
                               jnp.concatenate([kare_ref[j], kaim_ref[j]], axis=0)], axis=1)
        xk = _dot(g_ref[j], rhs)
        x_re, x_im = xk[0:r, 0:c], xk[r:, 0:c]
        k_re, k_im = xk[0:r, c:] * sc, xk[r:, c:] * sc
        y = jnp.concatenate([(x_re * k_re - x_im * k_im).astype(BF16),
                             (x_re * k_im + x_im * k_re).astype(BF16)], axis=0)
        bm = _dot(h_ref[j], y)
        bre_ref[j] = bm[0:r].astype(BF16)
        bim_ref[j] = bm[r:].astype(BF16)


def fft_level2_conv(a_re, a_im, ka_re, ka_im, g, gh, scale):
    r = DFT_R
    c = a_re.shape[-1]
    tk = 8
    blk_a = pl.BlockSpec((tk, r, c), lambda i: (i, 0, 0))
    blk_g = pl.BlockSpec((tk, 2 * r, 2 * r), lambda i: (i, 0, 0))
    out = jax.ShapeDtypeStruct((r, r, c), BF16)
    return pl.pallas_call(
        _fft2_conv_kernel,
        grid=(r // tk,),
        in_specs=[blk_a, blk_a, blk_a, blk_a, blk_g, blk_g, pl.BlockSpec((1, c), lambda i: (0, 0))],
        out_specs=[blk_a, blk_a],
        out_shape=[out, out],
        compiler_params=_params(("arbitrary",)),
        name="fft_level2_conv",
    )(a_re, a_im, ka_re, ka_im, g, gh, scale)


def _ifft1_kernel(bre_ref, bim_ref, f_ref, uv_ref, x0_ref, bias_ref, o_ref):
    half = f_ref.shape[0] // 2
    y = _dot(f_ref[...], jnp.concatenate([bre_ref[...], bim_ref[...]], axis=0))
    bias = bias_ref[...]
    o_ref[0] = ((y[0:half] + uv_ref[0].astype(F32) * bias) * x0_ref[0].astype(F32)).astype(BF16)
    o_ref[1] = ((y[half:] + uv_ref[1].astype(F32) * bias) * x0_ref[1].astype(F32)).astype(BF16)


def ifft_level1(b_re, b_im, f_inv, uv, x0, bias_t):
    r = DFT_R
    half = r // 2
    cols = b_re.shape[-1]
    tc = 8192
    blk_b = pl.BlockSpec((r, tc), lambda j: (0, j))
    blk_x = pl.BlockSpec((2, half, tc), lambda j: (0, 0, j))
    return pl.pallas_call(
        _ifft1_kernel,
        grid=(cols // tc,),
        in_specs=[blk_b, blk_b, pl.BlockSpec(f_inv.shape, lambda j: (0, 0)), blk_x, blk_x,
                  pl.BlockSpec((1, tc), lambda j: (0, 0))],
        out_specs=blk_x,
        out_shape=jax.ShapeDtypeStruct((2, half, cols), BF16),
        compiler_params=_params(("arbitrary",)),
        name="ifft_level1",
    )(b_re, b_im, f_inv, uv, x0, bias_t)


def hyena_long_conv(uv, x0, taps, sumsq, bias, tables):
    r = DFT_R
    bsz, L, c = uv.shape
    assert bsz == 2 and 2 * L == r * r
    scale = lax.rsqrt(sumsq) * (1.0 / (r * r))
    ka_re, ka_im = fft_level1(taps.reshape(r, r * c), tables["f_real"])
    uv2 = uv.reshape(2, r // 2, r * c)
    a_re, a_im = fft_level1(uv2, tables["f_pad"])
    b_re, b_im = fft_level2_conv(a_re.reshape(r, r, c), a_im.reshape(r, r, c), ka_re.reshape(r, r, c),
                                 ka_im.reshape(r, r, c), tables["g"], tables["gh"], scale)
    bias_t = jnp.tile(bias.reshape(1, c), (1, 8192 // c))
    y = ifft_level1(b_re.reshape(r, r * c), b_im.reshape(r, r * c), tables["f_inv"], uv2,
                    x0.reshape(2, r // 2, r * c), bias_t)
    return y.reshape(2, L, c)


def dft_small_tables(n_fft):
    idx = jnp.arange(n_fft, dtype=jnp.int32)
    ang = (2.0 * math.pi / n_fft) * ((idx[:, None] * idx[None, :]) % n_fft).astype(F32)
    return jnp.cos(ang).astype(BF16), (-jnp.sin(ang)).astype(BF16)


def _conv_small_kernel(taps_ref, ss_ref, fre_ref, fim_ref, uv_ref, x0_ref, bias_ref, o_ref):
    two_l = taps_ref.shape[0]
    L = two_l // 2
    fre, fim = fre_ref[...], fim_ref[...]
    scale = lax.rsqrt(ss_ref[...]) * (1.0 / two_l)
    taps = taps_ref[...]
    k_re = _dot(fre, taps) * scale
    k_im = _dot(fim, taps) * scale
    u0, u1 = uv_ref[0], uv_ref[1]
    fre_l, fim_l = fre[:, :L], fim[:, :L]
    x_re = _dot(fre_l, u0) - _dot(fim_l, u1)
    x_im = _dot(fre_l, u1) + _dot(fim_l, u0)
    y_re = (x_re * k_re - x_im * k_im).astype(BF16)
    y_im = (x_re * k_im + x_im * k_re).astype(BF16)
    fre_t, fim_t = fre[:L, :], fim[:L, :]
    y0 = _dot(fre_t, y_re) + _dot(fim_t, y_im)
    y1 = _dot(fre_t, y_im) - _dot(fim_t, y_re)
    bias = bias_ref[...]
    o_ref[0] = ((y0 + u0.astype(F32) * bias) * x0_ref[0].astype(F32)).astype(BF16)
    o_ref[1] = ((y1 + u1.astype(F32) * bias) * x0_ref[1].astype(F32)).astype(BF16)


def hyena_long_conv_small(uv, x0, taps, sumsq, bias, tables):
    bsz, L, c = uv.shape
    assert bsz == 2
    f_re, f_im = tables
    full = lambda shape: pl.BlockSpec(shape, lambda i: (0,) * len(shape))
    return pl.pallas_call(
        _conv_small_kernel,
        grid=(1,),
        in_specs=[full((2 * L, c)), full((1, c)), full((2 * L, 2 * L)), full((2 * L, 2 * L)),
                  full((2, L, c)), full((2, L, c)), full((1, c))],
        out_specs=full((2, L, c)),
        out_shape=jax.ShapeDtypeStruct((2, L, c), BF16),
        compiler_params=_params(("arbitrary",)),
        name="hyena_conv_small",
    )(taps, sumsq, f_re, f_im, uv, x0, bias.reshape(1, c))


def _gmlp_kernel(z_ref, g_ref, b_ref, ws_ref, bs_ref, o_ref):
    zg = z_ref[0].astype(F32)
    gl = 0.5 * zg * (1.0 + lax.erf(zg * (2.0 ** -0.5)))
    w = BR_WIDTH
    u = gl[:, :w]
    v = gl[:, w:]
    mu = jnp.mean(v, axis=-1, keepdims=True)
    var = jnp.mean(jnp.square(v - mu), axis=-1, keepdims=True)
    v = ((v - mu) * lax.rsqrt(var + EPS) * g_ref[...] + b_ref[...]).astype(BF16)
    gw = w // GM_GROUPS
    lane = lax.broadcasted_iota(jnp.int32, (GM_CHUNK, LANES), 1)
    first = lane < gw
    for ci in range(zg.shape[0] // GM_CHUNK):
        rows = slice(ci * GM_CHUNK, (ci + 1) * GM_CHUNK)
        tiles = []
        for t in range(w // LANES):
            vt = v[rows, t * LANES:(t + 1) * LANES]
            tiles.append(jnp.where(first, _dot(ws_ref[2 * t], vt), _dot(ws_ref[2 * t + 1], vt)))
        vm = jnp.concatenate(tiles, axis=1) + bs_ref[...]
        o_ref[0, rows, :] = (u[rows] * vm).astype(BF16)


def gmlp(z, ln_g, ln_b, ws, bs):
    bsz, n, _ = z.shape
    tr = min(n, 512)
    w = BR_WIDTH
    bs_full = jnp.repeat(bs.T, w // GM_GROUPS, axis=1)
    return pl.pallas_call(
        _gmlp_kernel,
        grid=(bsz, n // tr),
        in_specs=[pl.BlockSpec((1, tr, 2 * w), lambda b, i: (b, i, 0)),
                  pl.BlockSpec((1, w), lambda b, i: (0, 0)),
                  pl.BlockSpec((1, w), lambda b, i: (0, 0)),
                  pl.BlockSpec((GM_GROUPS, GM_CHUNK, GM_CHUNK), lambda b, i: (0, 0, 0)),
                  pl.BlockSpec((GM_CHUNK, w), lambda b, i: (0, 0))],
        out_specs=pl.BlockSpec((1, tr, w), lambda b, i: (b, i, 0)),
        out_shape=jax.ShapeDtypeStruct((bsz, n, w), BF16),
        compiler_params=_params(("arbitrary", "arbitrary")),
        name="gmlp",
    )(z, ln_g.reshape(1, w), ln_b.reshape(1, w), ws.astype(BF16), bs_full)


def _merge_kernel(ya_ref, yb_ref, yc_ref, ga_ref, gb_ref, gc_ref, mg_ref, x_ref, mod_ref, np_ref,
                  wb_ref, wo_ref, o_ref):
    acc = None
    for i, (y_ref, g_ref) in enumerate(((ya_ref, ga_ref), (yb_ref, gb_ref), (yc_ref, gc_ref))):
        g = g_ref[0]
        gated = y_ref[0] * (g * jax.nn.sigmoid(g))
        sel = jax.nn.sigmoid(mg_ref[0, :, i * D_MODEL:(i + 1) * D_MODEL])
        term = sel * _dot(gated, wb_ref[i]).astype(BF16)
        acc = term if acc is None else acc + term
    out = _dot(acc, wo_ref[...])
    r = out * lax.rsqrt(jnp.mean(out * out, axis=-1, keepdims=True) + EPS) * np_ref[...]
    gt = mod_ref[0, :, 2 * D_MODEL:]
    o_ref[0] = x_ref[0] + gt * r


def merge_out(ya, yb, yc, zat, zmg, x, mod_rows, row_of_batch, npost, wb, wo):
    bsz, n, _ = x.shape
    tm = min(n, 512)
    w = BR_WIDTH
    yspec = pl.BlockSpec((1, tm, w), lambda b, i: (b, i, 0))

    def zspec(name):
        cb = ATC[name] // w
        return pl.BlockSpec((1, tm, w), lambda b, i: (b, i, cb))

    return pl.pallas_call(
        _merge_kernel,
        grid=(bsz, n // tm),
        in_specs=[yspec, yspec, yspec, zspec("GA"), zspec("GB"), zspec("GC"),
                  pl.BlockSpec((1, tm, 3 * D_MODEL), lambda b, i: (b, i, 0)),
                  pl.BlockSpec((1, tm, D_MODEL), lambda b, i: (b, i, 0)),
                  pl.BlockSpec((1, 1, 3 * D_MODEL), lambda b, i: (row_of_batch(b), 0, 0)),
                  pl.BlockSpec((1, D_MODEL), lambda b, i: (0, 0)),
                  pl.BlockSpec((3, w, D_MODEL), lambda b, i: (0, 0, 0)),
                  pl.BlockSpec((D_MODEL, D_MODEL), lambda b, i: (0, 0))],
        out_specs=pl.BlockSpec((1, tm, D_MODEL), lambda b, i: (b, i, 0)),
        out_shape=jax.ShapeDtypeStruct((bsz, n, D_MODEL), F32),
        compiler_params=_params(("arbitrary", "arbitrary")),
        name="merge_out",
    )(ya, yb, yc, zat, zat, zat, zmg, x, mod_rows, npost.reshape(1, D_MODEL), wb, wo)


def _group_weights(w):
    def cols(nm):
        start, width = _REF_COLS[nm]
        return w[:, start:start + width]
    return [jnp.concatenate([cols(nm) for nm in names], axis=1).astype(BF16) for _, names in _GROUPS]


def kernel(x, c, ctx, c_ctx, ada_w, ada_b, norm_pre, norm_post, w_in, da_lambda, da_subln, hy_short_w,
           hy_short_b, hy_f_w1, hy_f_b1, hy_f_w2, hy_f_b2, hy_f_w3, hy_f_freq, hy_bias, gm_ln_g, gm_ln_b,
           gm_ws, gm_bs, w_branch, w_out):
    bsz, n, _ = x.shape
    n_ctx = ctx.shape[1]
    assert bsz == 2 and 2 * n == DFT_R * DFT_R

    cond8 = jnp.zeros((8, D_MODEL), F32).at[0:bsz].set(c).at[bsz].set(c_ctx)
    mod = modulation_all(cond8, ada_w, ada_b)
    lat_row = lambda b: b
    ctx_row = lambda b: bsz

    cos_t, sin_t = rope_tables(n)
    feat = hyena_positions(n)
    feat_c = hyena_positions(n_ctx)
    tables = dft_tables()
    tables_c = dft_small_tables(2 * n_ctx)
    vb, qb = ATC["V"] // LANES, ATC["Q"] // LANES

    xc = ctx
    for l in range(DEPTH):
        last = l == DEPTH - 1
        lam_init = 0.8 - 0.6 * math.exp(-0.3 * l)
        mod_rows = mod[l].reshape(8, 1, 3 * D_MODEL)
        w_l = _group_weights(w_in[l])
        wb_l = w_branch[l].astype(BF16)
        wo_l = w_out[l].astype(BF16)
        filt_w = (hy_f_w1[l], hy_f_b1[l], hy_f_w2[l], hy_f_b2[l], hy_f_w3[l], hy_f_freq[l])

        z, qr, kr = in_projection(x, mod_rows, lat_row, norm_pre[l], w_l, (cos_t, sin_t))
        zc = in_projection(xc, mod_rows, ctx_row, norm_pre[l], w_l)

        taps, sumsq = hyena_filter_taps(feat, *filt_w)
        x0, uv = hyena_short_conv(z["HY"], hy_short_w[l], hy_short_b[l])
        y_a = diff_attention(qr, 0, zc["AT"], (kr, 0, z["AT"], vb), da_lambda[l], da_subln[l], lam_init, True)
        y_b = hyena_long_conv(uv, x0, taps, sumsq, hy_bias[l], tables)
        y_c = gmlp(z["GM"], gm_ln_g[l], gm_ln_b[l], gm_ws[l], gm_bs[l])
        x_new = merge_out(y_a, y_b, y_c, z["AT"], z["MG"], x, mod_rows, lat_row, norm_post[l], wb_l, wo_l)

        if not last:
            yc_a = diff_attention(zc["AT"], qb, zc["AT"], None, da_lambda[l], da_subln[l], lam_init, False)
            taps_c, sumsq_c = hyena_filter_taps(feat_c, *filt_w)
            x0c, uvc = hyena_short_conv(zc["HY"], hy_short_w[l], hy_short_b[l])
            yc_b = hyena_long_conv_small(uvc, x0c, taps_c, sumsq_c, hy_bias[l], tables_c)
            yc_c = gmlp(zc["GM"], gm_ln_g[l], gm_ln_b[l], gm_ws[l], gm_bs[l])
            xc = merge_out(yc_a, yc_b, yc_c, zc["AT"], zc["MG"], xc, mod_rows, ctx_row, norm_post[l], wb_l, wo_l)
        x = x_new
    return x
```

```python
import functools
import math

import jax
import jax.numpy as jnp
from jax import lax
from jax.experimental import pallas as pl
from jax.experimental.pallas import tpu as pltpu

F32 = jnp.float32
BF16 = jnp.bfloat16

D_MODEL = 1024
DEPTH = 4
GRID_W = 64
EPS = 1e-6
BR_WIDTH = 512
DA_SUB = 64
DA_VDIM = 128
DA_HEADS = 4
ROPE_BASE = 10000.0
ROPE_NF = 16
HY_WIDTH = 512
HY_BANDS = 16
HY_EMB = 33
HY_HIDDEN = 64
HY_MIN_DECAY = math.log(1e-2) / 1.5
HY_MAX_DECAY = math.log(1e-2) / 0.3
GM_GROUPS = 8
GM_CHUNK = 128

LANES = 128
DFT_R = 128
QK_SCALE = DA_SUB ** -0.5 * math.log2(math.e)

_REF_COLS = dict(K=(0, 512), V=(512, 512), Q=(1024, 512), GA=(1536, 512), HY=(2048, 1536),
                 GB=(3584, 512), GM=(4096, 1024), GC=(5120, 512), MG=(5632, 3072))
_GROUPS = (("MG", ("MG",)), ("HY", ("HY",)), ("GM", ("GM",)), ("AT", ("K", "V", "Q", "GA", "GB", "GC")))
GROUP_WIDTH = {g: sum(_REF_COLS[nm][1] for nm in names) for g, names in _GROUPS}
ATC = {}
_off = 0
for _name in _GROUPS[-1][1]:
    ATC[_name] = _off
    _off += _REF_COLS[_name][1]

VMEM_LIMIT = 48 * 1024 * 1024
INPROJ_VMEM_LIMIT = 56 * 1024 * 1024


def _params(sem):
    return pltpu.CompilerParams(dimension_semantics=sem, vmem_limit_bytes=VMEM_LIMIT)


def _dot(a, b):
    return jnp.dot(a, b, preferred_element_type=F32)


def _dot_hi(a, b):
    return jnp.dot(a, b, preferred_element_type=F32, precision=lax.Precision.HIGHEST)


def _mod_kernel(c_ref, w_ref, b_ref, o_ref):
    cond = c_ref[...]
    s = cond * jax.nn.sigmoid(cond)
    o_ref[0] = _dot_hi(s, w_ref[0]) + b_ref[0]


def modulation_all(cond8, ada_w, ada_b):
    tn = 1024
    return pl.pallas_call(
        _mod_kernel,
        grid=(DEPTH, 3 * D_MODEL // tn),
        in_specs=[pl.BlockSpec((8, D_MODEL), lambda l, j: (0, 0)),
                  pl.BlockSpec((1, D_MODEL, tn), lambda l, j: (l, 0, j)),
                  pl.BlockSpec((1, 1, tn), lambda l, j: (l, 0, j))],
        out_specs=pl.BlockSpec((1, 8, tn), lambda l, j: (l, 0, j)),
        out_shape=jax.ShapeDtypeStruct((DEPTH, 8, 3 * D_MODEL), F32),
        compiler_params=_params(("arbitrary", "arbitrary")),
        name="modulation",
    )(cond8, ada_w, ada_b.reshape(DEPTH, 1, 3 * D_MODEL))


_QK_WIDTH = DA_HEADS * LANES
MXU_TILE = 256
MAX_CHUNK = 2048
_INPROJ_CHUNKS = []
for _g, _names in _GROUPS:
    _dst = 0
    for _name in _names:
        _start, _width = _REF_COLS[_name]
        for _o in range(0, _width, MAX_CHUNK):
            _w = min(MAX_CHUNK, _width - _o)
            assert (_start + _o) % MXU_TILE == 0 and _w % MXU_TILE == 0
            _INPROJ_CHUNKS.append((_start + _o, _g, _dst + _o, _w))
        _dst += _width
W_IN_COLS = sum(w for _, w in _REF_COLS.values())


def _inproj_kernel(*refs, rope):
    ng = len(_GROUPS)
    x_ref, mod_ref, g_ref, w_ref = refs[0:4]
    if rope:
        cos_ref, sin_ref = refs[4:6]
        o_refs = refs[6:6 + ng]
        q_ref, k_ref = refs[6 + ng:]
    else:
        o_refs = refs[4:4 + ng]
    o_of = {gn: o_ref for (gn, _), o_ref in zip(_GROUPS, o_refs)}
    x = x_ref[0]
    y = x * lax.rsqrt(jnp.mean(x * x, axis=-1, keepdims=True) + EPS) * g_ref[...]
    sh = mod_ref[0, :, 0:D_MODEL]
    sc = mod_ref[0, :, D_MODEL:2 * D_MODEL]
    h = (y * (1.0 + sc) + sh).astype(BF16)

    if rope:
        lane = lax.broadcasted_iota(jnp.int32, cos_ref.shape, 1)
        low = (lane % (2 * ROPE_NF)) < ROPE_NF
        cs = cos_ref[...]
        sn = sin_ref[...]

        def rot(v):
            partner = jnp.where(low, pltpu.roll(v, LANES - ROPE_NF, axis=1), pltpu.roll(v, ROPE_NF, axis=1))
            return (v * cs + partner * sn).astype(BF16)

    for src, gname, dst, width in _INPROJ_CHUNKS:
        zc = _dot(h, w_ref[:, src:src + width])
        o_of[gname][0, :, dst:dst + width] = zc.astype(BF16)
        if rope and gname == "AT" and dst in (ATC["K"], ATC["Q"]):
            out, scale = (k_ref, 1.0) if dst == ATC["K"] else (q_ref, QK_SCALE)
            for hd in range(DA_HEADS):
                cols = slice(hd * LANES, (hd + 1) * LANES)
                out[0, :, cols] = rot(zc[:, cols] * scale)


def in_projection(x, mod_rows, row_of_batch, g, w, rope_tabs=None):
    bsz, n, _ = x.shape
    tm = min(n, 512)
    rope = rope_tabs is not None
    in_specs = [pl.BlockSpec((1, tm, D_MODEL), lambda b, i: (b, i, 0)),
                pl.BlockSpec((1, 1, 3 * D_MODEL), lambda b, i: (row_of_batch(b), 0, 0)),
                pl.BlockSpec((1, D_MODEL), lambda b, i: (0, 0)),
                pl.BlockSpec((D_MODEL, W_IN_COLS), lambda b, i: (0, 0), pipeline_mode=pl.Buffered(1))]
    args = [x, mod_rows, g.reshape(1, D_MODEL), w]
    if rope:
        in_specs += [pl.BlockSpec((tm, LANES), lambda b, i: (i, 0))] * 2
        args += list(rope_tabs)
    out_specs = [pl.BlockSpec((1, tm, GROUP_WIDTH[gn]), lambda b, i: (b, i, 0)) for gn, _ in _GROUPS]
    out_shape = [jax.ShapeDtypeStruct((bsz, n, GROUP_WIDTH[gn]), BF16) for gn, _ in _GROUPS]
    if rope:
        out_specs += [pl.BlockSpec((1, tm, _QK_WIDTH), lambda b, i: (b, i, 0))] * 2
        out_shape += [jax.ShapeDtypeStruct((bsz, n, _QK_WIDTH), BF16)] * 2
    res = pl.pallas_call(
        functools.partial(_inproj_kernel, rope=rope),
        grid=(bsz, n // tm),
        in_specs=in_specs,
        out_specs=out_specs,
        out_shape=out_shape,
        compiler_params=pltpu.CompilerParams(dimension_semantics=("arbitrary", "arbitrary"),
                                             vmem_limit_bytes=INPROJ_VMEM_LIMIT),
        name="in_projection",
    )(*args)
    z = {gn: r for (gn, _), r in zip(_GROUPS, res)}
    return (z, res[-2], res[-1]) if rope else z


def rope_tables(n):
    pos = jnp.arange(n)
    row = (pos // GRID_W).astype(F32)
    col = (pos % GRID_W).astype(F32)
    inv = ROPE_BASE ** (-jnp.arange(ROPE_NF, dtype=F32) / ROPE_NF)
    ar = row[:, None] * inv
    ac = col[:, None] * inv
    cos64 = jnp.concatenate([jnp.cos(ar), jnp.cos(ar), jnp.cos(ac), jnp.cos(ac)], axis=1)
    sin64 = jnp.concatenate([-jnp.sin(ar), jnp.sin(ar), -jnp.sin(ac), jnp.sin(ac)], axis=1)
    return jnp.tile(cos64, (1, 2)), jnp.tile(sin64, (1, 2))


def _attn_kernel(*refs, lam_init, n_lat, ck, prescaled):
    if n_lat:
        lam_ref, g_ref, q_ref, kc_ref, vc_ref, k_ref, v_ref, o_ref, vt_ref = refs
    else:
        lam_ref, g_ref, q_ref, kc_ref, vc_ref, o_ref, vt_ref = refs
    tq = q_ref.shape[1]
    n_ctx = kc_ref.shape[1]

    @pl.when(pl.program_id(2) == 0)
    def _():
        vt_ref[0:DA_VDIM, 0:n_ctx] = vc_ref[0].astype(F32).T.astype(BF16)
        for c in range(n_lat // ck):
            vt_ref[0:DA_VDIM, n_ctx + c * ck:n_ctx + (c + 1) * ck] = (
                v_ref[0, c * ck:(c + 1) * ck, :].astype(F32).T.astype(BF16))
        pad = vt_ref.shape[0] - DA_VDIM
        row = lax.broadcasted_iota(jnp.int32, (pad, vt_ref.shape[1]), 0)
        vt_ref[DA_VDIM:, :] = (row == 0).astype(BF16)

    lp = lam_ref[...]
    lam = (jnp.exp(jnp.sum(lp[0:1] * lp[1:2], axis=1, keepdims=True))
           - jnp.exp(jnp.sum(lp[2:3] * lp[3:4], axis=1, keepdims=True)) + lam_init)

    q = q_ref[0]
    if not prescaled:
        q = (q.astype(F32) * QK_SCALE).astype(BF16)
    lane = lax.broadcasted_iota(jnp.int32, q.shape, 1)
    zero = jnp.zeros_like(q)
    qq = jnp.concatenate([jnp.where(lane < DA_SUB, q, zero), jnp.where(lane >= DA_SUB, q, zero)], axis=0)

    chunks = [(lambda: kc_ref[0], 0, n_ctx)]
    for c in range(n_lat // ck):
        chunks.append((lambda c=c: k_ref[0, c * ck:(c + 1) * ck, :], n_ctx + c * ck, ck))

    def scores_t(c):
        return lax.dot_general(chunks[c][0](), qq, (((1,), (1,)), ((), ())), preferred_element_type=F32)

    def pv_t(p_t, c):
        _, off, width = chunks[c]
        return _dot(vt_ref[:, off:off + width], p_t)

    m = jnp.full((1, 2 * tq), -jnp.inf, F32)
    acc = jnp.zeros((vt_ref.shape[0], 2 * tq), F32)
    pending = None
    s_next = scores_t(0)
    for c in range(len(chunks)):
        s = s_next
        if c + 1 < len(chunks):
            s_next = scores_t(c + 1)
        part = s[0:64]
        for r in range(64, s.shape[0], 64):
            part = jnp.maximum(part, s[r:r + 64])
        m_new = jnp.maximum(m, jnp.max(part, axis=0, keepdims=True))
        alpha = jnp.exp2(m - m_new)
        p_t = jnp.exp2(s - m_new).astype(BF16)
        m = m_new
        if pending is not None:
            p_prev, alpha_prev, c_prev = pending
            acc = acc * alpha_prev + pv_t(p_prev, c_prev)
        pending = (p_t, alpha, c)
    p_prev, alpha_prev, c_prev = pending
    acc = acc * alpha_prev + pv_t(p_prev, c_prev)

    o_t = acc[0:DA_VDIM, :] / acc[DA_VDIM:DA_VDIM + 1, :]
    d = (o_t[:, 0:tq] - lam * o_t[:, tq:]).T
    y = d * lax.rsqrt(jnp.mean(d * d, axis=-1, keepdims=True) + EPS) * g_ref[...]
    o_ref[0] = (y * (1.0 - lam_init)).astype(BF16)


def diff_attention(q_arr, q_col, zc, lat, lam_p, subln, lam_init, prescaled):
    bsz, nq, _ = q_arr.shape
    n_ctx = zc.shape[1]
    tq = min(nq, 512)
    kcb, vcb = ATC["K"] // LANES, ATC["V"] // LANES
    in_specs = [pl.BlockSpec((4, DA_SUB), lambda b, h, i: (0, 0)),
                pl.BlockSpec((1, DA_VDIM), lambda b, h, i: (0, 0)),
                pl.BlockSpec((1, tq, LANES), lambda b, h, i: (b, i, q_col + h)),
                pl.BlockSpec((1, n_ctx, LANES), lambda b, h, i: (b, 0, kcb + h)),
                pl.BlockSpec((1, n_ctx, LANES), lambda b, h, i: (b, 0, vcb + h))]
    args = [lam_p, subln.reshape(1, DA_VDIM), q_arr, zc, zc]
    n_lat = 0
    if lat is not None:
        k_arr, k_col, v_arr, v_col = lat
        n_lat = k_arr.shape[1]
        in_specs += [pl.BlockSpec((1, n_lat, LANES), lambda b, h, i: (b, 0, k_col + h)),
                     pl.BlockSpec((1, n_lat, LANES), lambda b, h, i: (b, 0, v_col + h))]
        args += [k_arr, v_arr]
    ones_rows = 16
    scratch = [pltpu.VMEM((DA_VDIM + ones_rows, n_ctx + n_lat), BF16)]
    return pl.pallas_call(
        functools.partial(_attn_kernel, lam_init=lam_init, n_lat=n_lat, ck=512, prescaled=prescaled),
        grid=(bsz, DA_HEADS, nq // tq),
        in_specs=in_specs,
        out_specs=pl.BlockSpec((1, tq, LANES), lambda b, h, i: (b, i, h)),
        out_shape=jax.ShapeDtypeStruct((bsz, nq, BR_WIDTH), BF16),
        scratch_shapes=scratch,
        compiler_params=_params(("arbitrary", "arbitrary", "arbitrary")),
        name="diff_attention",
    )(*args)


def hyena_positions(L):
    t = jnp.linspace(0.0, 1.0, L, dtype=F32)[:, None]
    wpos = ((2.0 * math.pi / L) * jnp.arange(L, dtype=F32))[:, None]
    bands = jnp.linspace(1e-4, HY_BANDS - 1, HY_BANDS, dtype=F32)[None, :]
    fwd = jnp.concatenate([t, jnp.cos(bands * wpos), -jnp.sin(bands * wpos)], axis=-1)
    emb = jnp.concatenate([fwd, fwd[0:1], jnp.flip(fwd[1:], axis=0)], axis=0)
    mask = (jnp.arange(2 * L) != L).astype(F32)[:, None]
    pad = jnp.zeros((2 * L, LANES - HY_EMB - 1), F32)
    return jnp.concatenate([emb, pad, mask], axis=-1)


def _filter_kernel(feat_ref, w1_ref, b1_ref, w2_ref, b2_ref, w3_ref, fr_ref, dl_ref, k_ref, ss_ref):
    half = feat_ref.shape[0] // 2
    feat = feat_ref[...]
    f2 = jnp.concatenate([feat[0:half], feat[half:]], axis=1)
    hid = jnp.sin(fr_ref[0:1, :] * (_dot_hi(f2, w1_ref[...]) + b1_ref[...]))
    hid = jnp.sin(fr_ref[1:2, :] * (_dot_hi(hid, w2_ref[...]) + b2_ref[...]))
    w3 = w3_ref[...]
    zero = jnp.zeros_like(w3)
    h = jnp.concatenate([_dot_hi(hid, jnp.concatenate([w3, zero], axis=0)),
                         _dot_hi(hid, jnp.concatenate([zero, w3], axis=0))], axis=0)
    t = feat[:, 0:1]
    mask = feat[:, LANES - 1:LANES]
    k = h * jnp.exp(-t * dl_ref[...]) * mask
    k_ref[...] = k.astype(k_ref.dtype)

    @pl.when(pl.program_id(0) == 0)
    def _():
        ss_ref[...] = jnp.zeros_like(ss_ref)

    ss_ref[...] += jnp.sum(k * k, axis=0, keepdims=True)


def hyena_filter_taps(feat, w1, b1, w2, b2, w3, freq):
    two_l = feat.shape[0]
    tr = min(two_l // 2, 1024)
    nb = two_l // tr
    hd = HY_HIDDEN
    w1d = jnp.zeros((2 * LANES, 2 * hd), F32).at[:HY_EMB, :hd].set(w1).at[LANES:LANES + HY_EMB, hd:].set(w1)
    w2d = jnp.zeros((2 * hd, 2 * hd), F32).at[:hd, :hd].set(w2).at[hd:, hd:].set(w2)
    twice = lambda v: jnp.tile(v.reshape(-1, hd), (1, 2))
    deltas = jnp.abs(jnp.linspace(HY_MIN_DECAY, HY_MAX_DECAY, HY_WIDTH, dtype=F32))[None, :]
    const = lambda i: (0, 0)
    return pl.pallas_call(
        _filter_kernel,
        grid=(nb,),
        in_specs=[pl.BlockSpec((tr, LANES), lambda i: (i, 0)),
                  pl.BlockSpec((2 * LANES, 2 * hd), const),
                  pl.BlockSpec((1, 2 * hd), const),
                  pl.BlockSpec((2 * hd, 2 * hd), const),
                  pl.BlockSpec((1, 2 * hd), const),
                  pl.BlockSpec((hd, HY_WIDTH), lambda i: (0, (2 * i) // nb)),
                  pl.BlockSpec((2, 2 * hd), const),
                  pl.BlockSpec((1, HY_WIDTH), const)],
        out_specs=[pl.BlockSpec((tr, HY_WIDTH), lambda i: (i, 0)),
                   pl.BlockSpec((1, HY_WIDTH), const)],
        out_shape=[jax.ShapeDtypeStruct((two_l, HY_WIDTH), BF16),
                   jax.ShapeDtypeStruct((1, HY_WIDTH), F32)],
        compiler_params=_params(("arbitrary",)),
        name="hyena_filter",
    )(feat, w1d, twice(b1), w2d, twice(b2), w3, twice(freq), deltas)


def _short_conv_kernel(z_ref, prev_ref, next_ref, w_ref, b_ref, x0_ref, uv_ref):
    i = pl.program_id(1)
    last = pl.num_programs(1) - 1
    z = z_ref[0].astype(F32)
    tr = z.shape[0]
    halo = prev_ref.shape[1]
    before = jnp.where(i == 0, 0.0, prev_ref[0, halo - 1:halo, :].astype(F32))
    after = jnp.where(i == last, 0.0, next_ref[0, 0:1, :].astype(F32))
    row = lax.broadcasted_iota(jnp.int32, z.shape, 0)
    zm = jnp.where(row == 0, before, pltpu.roll(z, 1, axis=0))
    zp = jnp.where(row == tr - 1, after, pltpu.roll(z, tr - 1, axis=0))
    y = zm * w_ref[0:1, :] + z * w_ref[1:2, :] + zp * w_ref[2:3, :] + b_ref[...]
    x0_ref[0] = y[:, 0:HY_WIDTH].astype(BF16)
    uv_ref[0] = (y[:, HY_WIDTH:2 * HY_WIDTH] * y[:, 2 * HY_WIDTH:]).astype(BF16)


def hyena_short_conv(z, sw, sb):
    bsz, n, _ = z.shape
    tr = min(n, 512)
    halo = 16
    w3c = 3 * HY_WIDTH
    nh = n // halo
    out = jax.ShapeDtypeStruct((bsz, n, HY_WIDTH), BF16)
    return pl.pallas_call(
        _short_conv_kernel,
        grid=(bsz, n // tr),
        in_specs=[pl.BlockSpec((1, tr, w3c), lambda b, i: (b, i, 0)),
                  pl.BlockSpec((1, halo, w3c), lambda b, i: (b, jnp.maximum(i * (tr // halo) - 1, 0), 0)),
                  pl.BlockSpec((1, halo, w3c), lambda b, i: (b, jnp.minimum((i + 1) * (tr // halo), nh - 1), 0)),
                  pl.BlockSpec((3, w3c), lambda b, i: (0, 0)),
                  pl.BlockSpec((1, w3c), lambda b, i: (0, 0))],
        out_specs=[pl.BlockSpec((1, tr, HY_WIDTH), lambda b, i: (b, i, 0)),
                   pl.BlockSpec((1, tr, HY_WIDTH), lambda b, i: (b, i, 0))],
        out_shape=[out, out],
        compiler_params=_params(("arbitrary", "arbitrary")),
        name="hyena_short_conv",
    )(z, z, z, sw, sb.reshape(1, w3c))


def dft_tables():
    r = DFT_R
    n_fft = r * r
    idx = jnp.arange(r, dtype=jnp.int32)
    prod = idx[:, None] * idx[None, :]
    ang = (2.0 * math.pi / r) * (prod % r).astype(F32)
    f_re, f_im = jnp.cos(ang), -jnp.sin(ang)
    ang = (2.0 * math.pi / n_fft) * prod.astype(F32)
    t_re, t_im = jnp.cos(ang), -jnp.sin(ang)

    def cmul(a_re, a_im, b_re, b_im):
        return a_re * b_re - a_im * b_im, a_re * b_im + a_im * b_re

    g_re, g_im = cmul(f_re[None, :, :], f_im[None, :, :], t_re[:, None, :], t_im[:, None, :])
    def cplx(re, im):
        return jnp.concatenate([jnp.concatenate([re, -im], axis=-1),
                                jnp.concatenate([im, re], axis=-1)], axis=-2).astype(BF16)

    half = r // 2
    return dict(f_pad=cplx(f_re[:, :half], f_im[:, :half]),
                f_real=jnp.concatenate([f_re, f_im], axis=0).astype(BF16),
                g=cplx(g_re, g_im),
                f_inv=cplx(f_re[:half], -f_im[:half]))


def _fft1_kernel(x_ref, f_ref, are_ref, aim_ref):
    r = are_ref.shape[0]
    x = x_ref[...]
    p = _dot(f_ref[...], x.reshape(-1, x.shape[-1]))
    are_ref[...] = p[0:r].astype(BF16)
    aim_ref[...] = p[r:].astype(BF16)


def fft_level1(x, f):
    r = DFT_R
    cols = x.shape[-1]
    tc = 8192
    if x.ndim == 3:
        x_spec = pl.BlockSpec((2, x.shape[1], tc), lambda j: (0, 0, j))
    else:
        x_spec = pl.BlockSpec((x.shape[0], tc), lambda j: (0, j))
    out = jax.ShapeDtypeStruct((r, cols), BF16)
    return pl.pallas_call(
        _fft1_kernel,
        grid=(cols // tc,),
        in_specs=[x_spec, pl.BlockSpec(f.shape, lambda j: (0, 0))],
        out_specs=[pl.BlockSpec((r, tc), lambda j: (0, j)), pl.BlockSpec((r, tc), lambda j: (0, j))],
        out_shape=[out, out],
        compiler_params=_params(("arbitrary",)),
        name="fft_level1",
    )(x, f)


def _fft2_conv_kernel(are_ref, aim_ref, kare_ref, kaim_ref, g_ref, sc_ref, bre_ref, bim_ref):
    r = are_ref.shape[1]
    c = are_ref.shape[-1]
    sc = sc_ref[...]
    for j in range(are_ref.shape[0]):
        rhs = jnp.concatenate([jnp.concatenate([are_ref[j], aim_ref[j]], axis=0),
                               jnp.concatenate([kare_ref[j], kaim_ref[j]], axis=0)], axis=1)
        xk = _dot(g_ref[j], rhs)
        x_re, x_im = xk[0:r, 0:c], xk[r:, 0:c]
        k_re, k_im = xk[0:r, c:] * sc, xk[r:, c:] * sc
        y = jnp.concatenate([(x_re * k_re - x_im * k_im).astype(BF16),
                             (x_re * k_im + x_im * k_re).astype(BF16)], axis=0)
        bm = lax.dot_general(g_ref[j], y, (((0,), (0,)), ((), ())), preferred_element_type=F32)
        bre_ref[j] = bm[0:r].astype(BF16)
        bim_ref[j] = bm[r:].astype(BF16)


def fft_level2_conv(a_re, a_im, ka_re, ka_im, g, scale):
    r = DFT_R
    c = a_re.shape[-1]
    tk = 8
    blk_a = pl.BlockSpec((tk, r, c), lambda i: (i, 0, 0))
    blk_g = pl.BlockSpec((tk, 2 * r, 2 * r), lambda i: (i, 0, 0))
    out = jax.ShapeDtypeStruct((r, r, c), BF16)
    return pl.pallas_call(
        _fft2_conv_kernel,
        grid=(r // tk,),
        in_specs=[blk_a, blk_a, blk_a, blk_a, blk_g, pl.BlockSpec((1, c), lambda i: (0, 0))],
        out_specs=[blk_a, blk_a],
        out_shape=[out, out],
        compiler_params=_params(("arbitrary",)),
        name="fft_level2_conv",
    )(a_re, a_im, ka_re, ka_im, g, scale)


def _ifft1_kernel(bre_ref, bim_ref, f_ref, uv_ref, x0_ref, bias_ref, o_ref):
    half = f_ref.shape[0] // 2
    y = _dot(f_ref[...], jnp.concatenate([bre_ref[...], bim_ref[...]], axis=0))
    bias = bias_ref[...]
    o_ref[0] = ((y[0:half] + uv_ref[0].astype(F32) * bias) * x0_ref[0].astype(F32)).astype(BF16)
    o_ref[1] = ((y[half:] + uv_ref[1].astype(F32) * bias) * x0_ref[1].astype(F32)).astype(BF16)


def ifft_level1(b_re, b_im, f_inv, uv, x0, bias_t):
    r = DFT_R
    half = r // 2
    cols = b_re.shape[-1]
    tc = 8192
    blk_b = pl.BlockSpec((r, tc), lambda j: (0, j))
    blk_x = pl.BlockSpec((2, half, tc), lambda j: (0, 0, j))
    return pl.pallas_call(
        _ifft1_kernel,
        grid=(cols // tc,),
        in_specs=[blk_b, blk_b, pl.BlockSpec(f_inv.shape, lambda j: (0, 0)), blk_x, blk_x,
                  pl.BlockSpec((1, tc), lambda j: (0, 0))],
        out_specs=blk_x,
        out_shape=jax.ShapeDtypeStruct((2, half, cols), BF16),
        compiler_params=_params(("arbitrary",)),
        name="ifft_level1",
    )(b_re, b_im, f_inv, uv, x0, bias_t)


def hyena_long_conv(uv, x0, taps, sumsq, bias, tables):
    r = DFT_R
    bsz, L, c = uv.shape
    assert bsz == 2 and 2 * L == r * r
    scale = lax.rsqrt(sumsq) * (1.0 / (r * r))
    ka_re, ka_im = fft_level1(taps.reshape(r, r * c), tables["f_real"])
    uv2 = uv.reshape(2, r // 2, r * c)
    a_re, a_im = fft_level1(uv2, tables["f_pad"])
    b_re, b_im = fft_level2_conv(a_re.reshape(r, r, c), a_im.reshape(r, r, c), ka_re.reshape(r, r, c),
                                 ka_im.reshape(r, r, c), tables["g"], scale)
    bias_t = jnp.tile(bias.reshape(1, c), (1, 8192 // c))
    y = ifft_level1(b_re.reshape(r, r * c), b_im.reshape(r, r * c), tables["f_inv"], uv2,
                    x0.reshape(2, r // 2, r * c), bias_t)
    return y.reshape(2, L, c)


def dft_small_tables(n_fft):
    idx = jnp.arange(n_fft, dtype=jnp.int32)
    ang = (2.0 * math.pi / n_fft) * ((idx[:, None] * idx[None, :]) % n_fft).astype(F32)
    return jnp.cos(ang).astype(BF16), (-jnp.sin(ang)).astype(BF16)


def _conv_small_kernel(taps_ref, ss_ref, fre_ref, fim_ref, uv_ref, x0_ref, bias_ref, o_ref):
    two_l = taps_ref.shape[0]
    L = two_l // 2
    fre, fim = fre_ref[...], fim_ref[...]
    scale = lax.rsqrt(ss_ref[...]) * (1.0 / two_l)
    taps = taps_ref[...]
    k_re = _dot(fre, taps) * scale
    k_im = _dot(fim, taps) * scale
    u0, u1 = uv_ref[0], uv_ref[1]
    fre_l, fim_l = fre[:, :L], fim[:, :L]
    x_re = _dot(fre_l, u0) - _dot(fim_l, u1)
    x_im = _dot(fre_l, u1) + _dot(fim_l, u0)
    y_re = (x_re * k_re - x_im * k_im).astype(BF16)
    y_im = (x_re * k_im + x_im * k_re).astype(BF16)
    fre_t, fim_t = fre[:L, :], fim[:L, :]
    y0 = _dot(fre_t, y_re) + _dot(fim_t, y_im)
    y1 = _dot(fre_t, y_im) - _dot(fim_t, y_re)
    bias = bias_ref[...]
    o_ref[0] = ((y0 + u0.astype(F32) * bias) * x0_ref[0].astype(F32)).astype(BF16)
    o_ref[1] = ((y1 + u1.astype(F32) * bias) * x0_ref[1].astype(F32)).astype(BF16)


def hyena_long_conv_small(uv, x0, taps, sumsq, bias, tables):
    bsz, L, c = uv.shape
    assert bsz == 2
    f_re, f_im = tables
    full = lambda shape: pl.BlockSpec(shape, lambda i: (0,) * len(shape))
    return pl.pallas_call(
        _conv_small_kernel,
        grid=(1,),
        in_specs=[full((2 * L, c)), full((1, c)), full((2 * L, 2 * L)), full((2 * L, 2 * L)),
                  full((2, L, c)), full((2, L, c)), full((1, c))],
        out_specs=full((2, L, c)),
        out_shape=jax.ShapeDtypeStruct((2, L, c), BF16),
        compiler_params=_params(("arbitrary",)),
        name="hyena_conv_small",
    )(taps, sumsq, f_re, f_im, uv, x0, bias.reshape(1, c))


def _gmlp_kernel(z_ref, g_ref, b_ref, ws_ref, bs_ref, o_ref):
    zg = z_ref[0].astype(F32)
    gl = 0.5 * zg * (1.0 + lax.erf(zg * (2.0 ** -0.5)))
    w = BR_WIDTH
    u = gl[:, :w]
    v = gl[:, w:]
    mu = jnp.mean(v, axis=-1, keepdims=True)
    var = jnp.mean(jnp.square(v - mu), axis=-1, keepdims=True)
    v = ((v - mu) * lax.rsqrt(var + EPS) * g_ref[...] + b_ref[...]).astype(BF16)
    gw = w // GM_GROUPS
    lane = lax.broadcasted_iota(jnp.int32, (GM_CHUNK, LANES), 1)
    first = lane < gw
    for ci in range(zg.shape[0] // GM_CHUNK):
        rows = slice(ci * GM_CHUNK, (ci + 1) * GM_CHUNK)
        tiles = []
        for t in range(w // LANES):
            vt = v[rows, t * LANES:(t + 1) * LANES]
            tiles.append(jnp.where(first, _dot(ws_ref[2 * t], vt), _dot(ws_ref[2 * t + 1], vt)))
        vm = jnp.concatenate(tiles, axis=1) + bs_ref[...]
        o_ref[0, rows, :] = (u[rows] * vm).astype(BF16)


def gmlp(z, ln_g, ln_b, ws, bs):
    bsz, n, _ = z.shape
    tr = min(n, 512)
    w = BR_WIDTH
    bs_full = jnp.repeat(bs.T, w // GM_GROUPS, axis=1)
    return pl.pallas_call(
        _gmlp_kernel,
        grid=(bsz, n // tr),
        in_specs=[pl.BlockSpec((1, tr, 2 * w), lambda b, i: (b, i, 0)),
                  pl.BlockSpec((1, w), lambda b, i: (0, 0)),
                  pl.BlockSpec((1, w), lambda b, i: (0, 0)),
                  pl.BlockSpec((GM_GROUPS, GM_CHUNK, GM_CHUNK), lambda b, i: (0, 0, 0)),
                  pl.BlockSpec((GM_CHUNK, w), lambda b, i: (0, 0))],
        out_specs=pl.BlockSpec((1, tr, w), lambda b, i: (b, i, 0)),
        out_shape=jax.ShapeDtypeStruct((bsz, n, w), BF16),
        compiler_params=_params(("arbitrary", "arbitrary")),
        name="gmlp",
    )(z, ln_g.reshape(1, w), ln_b.reshape(1, w), ws.astype(BF16), bs_full)


def _merge_kernel(ya_ref, yb_ref, yc_ref, ga_ref, gb_ref, gc_ref, mg_ref, x_ref, mod_ref, np_ref,
                  wb_ref, wo_ref, o_ref):
    acc = None
    for i, (y_ref, g_ref) in enumerate(((ya_ref, ga_ref), (yb_ref, gb_ref), (yc_ref, gc_ref))):
        g = g_ref[0]
        gated = y_ref[0] * (g * jax.nn.sigmoid(g))
        sel = jax.nn.sigmoid(mg_ref[0, :, i * D_MODEL:(i + 1) * D_MODEL])
        term = sel * _dot(gated, wb_ref[i]).astype(BF16)
        acc = term if acc is None else acc + term
    out = _dot(acc, wo_ref[...])
    r = out * lax.rsqrt(jnp.mean(out * out, axis=-1, keepdims=True) + EPS) * np_ref[...]
    gt = mod_ref[0, :, 2 * D_MODEL:]
    o_ref[0] = x_ref[0] + gt * r


def merge_out(ya, yb, yc, zat, zmg, x, mod_rows, row_of_batch, npost, wb, wo):
    bsz, n, _ = x.shape
    tm = min(n, 512)
    w = BR_WIDTH
    yspec = pl.BlockSpec((1, tm, w), lambda b, i: (b, i, 0))

    def zspec(name):
        cb = ATC[name] // w
        return pl.BlockSpec((1, tm, w), lambda b, i: (b, i, cb))

    return pl.pallas_call(
        _merge_kernel,
        grid=(bsz, n // tm),
        in_specs=[yspec, yspec, yspec, zspec("GA"), zspec("GB"), zspec("GC"),
                  pl.BlockSpec((1, tm, 3 * D_MODEL), lambda b, i: (b, i, 0)),
                  pl.BlockSpec((1, tm, D_MODEL), lambda b, i: (b, i, 0)),
                  pl.BlockSpec((1, 1, 3 * D_MODEL), lambda b, i: (row_of_batch(b), 0, 0)),
                  pl.BlockSpec((1, D_MODEL), lambda b, i: (0, 0)),
                  pl.BlockSpec((3, w, D_MODEL), lambda b, i: (0, 0, 0)),
                  pl.BlockSpec((D_MODEL, D_MODEL), lambda b, i: (0, 0))],
        out_specs=pl.BlockSpec((1, tm, D_MODEL), lambda b, i: (b, i, 0)),
        out_shape=jax.ShapeDtypeStruct((bsz, n, D_MODEL), F32),
        compiler_params=_params(("arbitrary", "arbitrary")),
        name="merge_out",
    )(ya, yb, yc, zat, zat, zat, zmg, x, mod_rows, npost.reshape(1, D_MODEL), wb, wo)


def kernel(x, c, ctx, c_ctx, ada_w, ada_b, norm_pre, norm_post, w_in, da_lambda, da_subln, hy_short_w,
           hy_short_b, hy_f_w1, hy_f_b1, hy_f_w2, hy_f_b2, hy_f_w3, hy_f_freq, hy_bias, gm_ln_g, gm_ln_b,
           gm_ws, gm_bs, w_branch, w_out):
    bsz, n, _ = x.shape
    n_ctx = ctx.shape[1]
    assert bsz == 2 and 2 * n == DFT_R * DFT_R

    cond8 = jnp.zeros((8, D_MODEL), F32).at[0:bsz].set(c).at[bsz].set(c_ctx)
    mod = modulation_all(cond8, ada_w, ada_b)
    lat_row = lambda b: b
    ctx_row = lambda b: bsz

    cos_t, sin_t = rope_tables(n)
    feat = hyena_positions(n)
    feat_c = hyena_positions(n_ctx)
    tables = dft_tables()
    tables_c = dft_small_tables(2 * n_ctx)
    vb, qb = ATC["V"] // LANES, ATC["Q"] // LANES

    xc = ctx
    for l in range(DEPTH):
        last = l == DEPTH - 1
        lam_init = 0.8 - 0.6 * math.exp(-0.3 * l)
        mod_rows = mod[l].reshape(8, 1, 3 * D_MODEL)
        w_l = w_in[l].astype(BF16)
        wb_l = w_branch[l].astype(BF16)
        wo_l = w_out[l].astype(BF16)
        filt_w = (hy_f_w1[l], hy_f_b1[l], hy_f_w2[l], hy_f_b2[l], hy_f_w3[l], hy_f_freq[l])

        z, qr, kr = in_projection(x, mod_rows, lat_row, norm_pre[l], w_l, (cos_t, sin_t))
        zc = in_projection(xc, mod_rows, ctx_row, norm_pre[l], w_l)

        taps, sumsq = hyena_filter_taps(feat, *filt_w)
        x0, uv = hyena_short_conv(z["HY"], hy_short_w[l], hy_short_b[l])
        y_a = diff_attention(qr, 0, zc["AT"], (kr, 0, z["AT"], vb), da_lambda[l], da_subln[l], lam_init, True)
        y_b = hyena_long_conv(uv, x0, taps, sumsq, hy_bias[l], tables)
        y_c = gmlp(z["GM"], gm_ln_g[l], gm_ln_b[l], gm_ws[l], gm_bs[l])
        x_new = merge_out(y_a, y_b, y_c, z["AT"], z["MG"], x, mod_rows, lat_row, norm_post[l], wb_l, wo_l)

        if not last:
            yc_a = diff_attention(zc["AT"], qb, zc["AT"], None, da_lambda[l], da_subln[l], lam_init, False)
            taps_c, sumsq_c = hyena_filter_taps(feat_c, *filt_w)
            x0c, uvc = hyena_short_conv(zc["HY"], hy_short_w[l], hy_short_b[l])
            yc_b = hyena_long_conv_small(uvc, x0c, taps_c, sumsq_c, hy_bias[l], tables_c)
            yc_c = gmlp(zc["GM"], gm_ln_g[l], gm_ln_b[l], gm_ws[l], gm_bs[l])
            xc = merge_out(yc_a, yc_b, yc_c, zc["AT"], zc["MG"], xc, mod_rows, ctx_row, norm_post[l], wb_l, wo_l)
        x = x_new
    return x
```

```python
import functools
import math

import jax
import jax.numpy as jnp
from jax import lax
from jax.experimental import pallas as pl
from jax.experimental.pallas import tpu as pltpu

F32 = jnp.float32
BF16 = jnp.bfloat16

D_MODEL = 1024
DEPTH = 4
GRID_W = 64
EPS = 1e-6
BR_WIDTH = 512
DA_SUB = 64
DA_VDIM = 128
DA_HEADS = 4
ROPE_BASE = 10000.0
ROPE_NF = 16
HY_WIDTH = 512
HY_BANDS = 16
HY_EMB = 33
HY_HIDDEN = 64
HY_MIN_DECAY = math.log(1e-2) / 1.5
HY_MAX_DECAY = math.log(1e-2) / 0.3
GM_GROUPS = 8
GM_CHUNK = 128

LANES = 128
DFT_R = 128
QK_SCALE = DA_SUB ** -0.5 * math.log2(math.e)

_REF_COLS = dict(K=(0, 512), V=(512, 512), Q=(1024, 512), GA=(1536, 512), HY=(2048, 1536),
                 GB=(3584, 512), GM=(4096, 1024), GC=(5120, 512), MG=(5632, 3072))
_GROUPS = (("MG", ("MG",)), ("HY", ("HY",)), ("GM", ("GM",)), ("AT", ("K", "V", "Q", "GA", "GB", "GC")))
GROUP_WIDTH = {g: sum(_REF_COLS[nm][1] for nm in names) for g, names in _GROUPS}
ATC = {}
_off = 0
for _name in _GROUPS[-1][1]:
    ATC[_name] = _off
    _off += _REF_COLS[_name][1]

VMEM_LIMIT = 48 * 1024 * 1024
INPROJ_VMEM_LIMIT = 56 * 1024 * 1024


def _params(sem):
    return pltpu.CompilerParams(dimension_semantics=sem, vmem_limit_bytes=VMEM_LIMIT)


def _dot(a, b):
    return jnp.dot(a, b, preferred_element_type=F32)


def _dot_hi(a, b):
    return jnp.dot(a, b, preferred_element_type=F32, precision=lax.Precision.HIGHEST)


def _mod_kernel(c_ref, w_ref, b_ref, o_ref):
    cond = c_ref[...]
    s = cond * jax.nn.sigmoid(cond)
    o_ref[0] = _dot_hi(s, w_ref[0]) + b_ref[0]


def modulation_all(cond8, ada_w, ada_b):
    tn = 1024
    return pl.pallas_call(
        _mod_kernel,
        grid=(DEPTH, 3 * D_MODEL // tn),
        in_specs=[pl.BlockSpec((8, D_MODEL), lambda l, j: (0, 0)),
                  pl.BlockSpec((1, D_MODEL, tn), lambda l, j: (l, 0, j)),
                  pl.BlockSpec((1, 1, tn), lambda l, j: (l, 0, j))],
        out_specs=pl.BlockSpec((1, 8, tn), lambda l, j: (l, 0, j)),
        out_shape=jax.ShapeDtypeStruct((DEPTH, 8, 3 * D_MODEL), F32),
        compiler_params=_params(("arbitrary", "arbitrary")),
        name="modulation",
    )(cond8, ada_w, ada_b.reshape(DEPTH, 1, 3 * D_MODEL))


_QK_WIDTH = DA_HEADS * LANES
MXU_TILE = 256
MAX_CHUNK = 2048
_INPROJ_CHUNKS = []
for _g, _names in _GROUPS:
    _dst = 0
    for _name in _names:
        _start, _width = _REF_COLS[_name]
        for _o in range(0, _width, MAX_CHUNK):
            _w = min(MAX_CHUNK, _width - _o)
            assert (_start + _o) % MXU_TILE == 0 and _w % MXU_TILE == 0
            _INPROJ_CHUNKS.append((_start + _o, _g, _dst + _o, _w))
        _dst += _width
W_IN_COLS = sum(w for _, w in _REF_COLS.values())


def _inproj_kernel(*refs, rope):
    ng = len(_GROUPS)
    x_ref, mod_ref, g_ref, w_ref = refs[0:4]
    if rope:
        cos_ref, sin_ref = refs[4:6]
        o_refs = refs[6:6 + ng]
        q_ref, k_ref = refs[6 + ng:]
    else:
        o_refs = refs[4:4 + ng]
    o_of = {gn: o_ref for (gn, _), o_ref in zip(_GROUPS, o_refs)}
    x = x_ref[0]
    y = x * lax.rsqrt(jnp.mean(x * x, axis=-1, keepdims=True) + EPS) * g_ref[...]
    sh = mod_ref[0, :, 0:D_MODEL]
    sc = mod_ref[0, :, D_MODEL:2 * D_MODEL]
    h = (y * (1.0 + sc) + sh).astype(BF16)

    if rope:
        lane = lax.broadcasted_iota(jnp.int32, cos_ref.shape, 1)
        low = (lane % (2 * ROPE_NF)) < ROPE_NF
        cs = cos_ref[...]
        sn = sin_ref[...]

        def rot(v):
            partner = jnp.where(low, pltpu.roll(v, LANES - ROPE_NF, axis=1), pltpu.roll(v, ROPE_NF, axis=1))
            return (v * cs + partner * sn).astype(BF16)

    for src, gname, dst, width in _INPROJ_CHUNKS:
        zc = _dot(h, w_ref[:, src:src + width])
        o_of[gname][0, :, dst:dst + width] = zc.astype(BF16)
        if rope and gname == "AT" and dst in (ATC["K"], ATC["Q"]):
            out, scale = (k_ref, 1.0) if dst == ATC["K"] else (q_ref, QK_SCALE)
            for hd in range(DA_HEADS):
                cols = slice(hd * LANES, (hd + 1) * LANES)
                out[0, :, cols] = rot(zc[:, cols] * scale)


def in_projection(x, mod_rows, row_of_batch, g, w, rope_tabs=None):
    bsz, n, _ = x.shape
    tm = min(n, 512)
    rope = rope_tabs is not None
    in_specs = [pl.BlockSpec((1, tm, D_MODEL), lambda b, i: (b, i, 0)),
                pl.BlockSpec((1, 1, 3 * D_MODEL), lambda b, i: (row_of_batch(b), 0, 0)),
                pl.BlockSpec((1, D_MODEL), lambda b, i: (0, 0)),
                pl.BlockSpec((D_MODEL, W_IN_COLS), lambda b, i: (0, 0), pipeline_mode=pl.Buffered(1))]
    args = [x, mod_rows, g.reshape(1, D_MODEL), w]
    if rope:
        in_specs += [pl.BlockSpec((tm, LANES), lambda b, i: (i, 0))] * 2
        args += list(rope_tabs)
    out_specs = [pl.BlockSpec((1, tm, GROUP_WIDTH[gn]), lambda b, i: (b, i, 0)) for gn, _ in _GROUPS]
    out_shape = [jax.ShapeDtypeStruct((bsz, n, GROUP_WIDTH[gn]), BF16) for gn, _ in _GROUPS]
    if rope:
        out_specs += [pl.BlockSpec((1, tm, _QK_WIDTH), lambda b, i: (b, i, 0))] * 2
        out_shape += [jax.ShapeDtypeStruct((bsz, n, _QK_WIDTH), BF16)] * 2
    res = pl.pallas_call(
        functools.partial(_inproj_kernel, rope=rope),
        grid=(bsz, n // tm),
        in_specs=in_specs,
        out_specs=out_specs,
        out_shape=out_shape,
        compiler_params=pltpu.CompilerParams(dimension_semantics=("arbitrary", "arbitrary"),
                                             vmem_limit_bytes=INPROJ_VMEM_LIMIT),
        name="in_projection",
    )(*args)
    z = {gn: r for (gn, _), r in zip(_GROUPS, res)}
    return (z, res[-2], res[-1]) if rope else z


def rope_tables(n):
    pos = jnp.arange(n)
    row = (pos // GRID_W).astype(F32)
    col = (pos % GRID_W).astype(F32)
    inv = ROPE_BASE ** (-jnp.arange(ROPE_NF, dtype=F32) / ROPE_NF)
    ar = row[:, None] * inv
    ac = col[:, None] * inv
    cos64 = jnp.concatenate([jnp.cos(ar), jnp.cos(ar), jnp.cos(ac), jnp.cos(ac)], axis=1)
    sin64 = jnp.concatenate([-jnp.sin(ar), jnp.sin(ar), -jnp.sin(ac), jnp.sin(ac)], axis=1)
    return jnp.tile(cos64, (1, 2)), jnp.tile(sin64, (1, 2))


def _attn_kernel(*refs, lam_init, n_lat, ck, prescaled):
    if n_lat:
        lam_ref, g_ref, q_ref, kc_ref, vc_ref, k_ref, v_ref, o_ref, vt_ref = refs
    else:
        lam_ref, g_ref, q_ref, kc_ref, vc_ref, o_ref, vt_ref = refs
    tq = q_ref.shape[1]
    n_ctx = kc_ref.shape[1]

    @pl.when(pl.program_id(2) == 0)
    def _():
        vt_ref[0:DA_VDIM, 0:n_ctx] = vc_ref[0].astype(F32).T.astype(BF16)
        for c in range(n_lat // ck):
            vt_ref[0:DA_VDIM, n_ctx + c * ck:n_ctx + (c + 1) * ck] = (
                v_ref[0, c * ck:(c + 1) * ck, :].astype(F32).T.astype(BF16))
        pad = vt_ref.shape[0] - DA_VDIM
        row = lax.broadcasted_iota(jnp.int32, (pad, vt_ref.shape[1]), 0)
        vt_ref[DA_VDIM:, :] = (row == 0).astype(BF16)

    lp = lam_ref[...]
    lam = (jnp.exp(jnp.sum(lp[0:1] * lp[1:2], axis=1, keepdims=True))
           - jnp.exp(jnp.sum(lp[2:3] * lp[3:4], axis=1, keepdims=True)) + lam_init)

    q = q_ref[0]
    if not prescaled:
        q = (q.astype(F32) * QK_SCALE).astype(BF16)
    lane = lax.broadcasted_iota(jnp.int32, q.shape, 1)
    zero = jnp.zeros_like(q)
    qq = jnp.concatenate([jnp.where(lane < DA_SUB, q, zero), jnp.where(lane >= DA_SUB, q, zero)], axis=0)

    chunks = [(lambda: kc_ref[0], 0, n_ctx)]
    for c in range(n_lat // ck):
        chunks.append((lambda c=c: k_ref[0, c * ck:(c + 1) * ck, :], n_ctx + c * ck, ck))

    def scores_t(c):
        return lax.dot_general(chunks[c][0](), qq, (((1,), (1,)), ((), ())), preferred_element_type=F32)

    def pv_t(p_t, c):
        _, off, width = chunks[c]
        return _dot(vt_ref[:, off:off + width], p_t)

    m = jnp.full((1, 2 * tq), -jnp.inf, F32)
    acc = jnp.zeros((vt_ref.shape[0], 2 * tq), F32)
    pending = None
    s_next = scores_t(0)
    for c in range(len(chunks)):
        s = s_next
        if c + 1 < len(chunks):
            s_next = scores_t(c + 1)
        part = s[0:64]
        for r in range(64, s.shape[0], 64):
            part = jnp.maximum(part, s[r:r + 64])
        m_new = jnp.maximum(m, jnp.max(part, axis=0, keepdims=True))
        alpha = jnp.exp2(m - m_new)
        p_t = jnp.exp2(s - m_new).astype(BF16)
        m = m_new
        if pending is not None:
            p_prev, alpha_prev, c_prev = pending
            acc = acc * alpha_prev + pv_t(p_prev, c_prev)
        pending = (p_t, alpha, c)
    p_prev, alpha_prev, c_prev = pending
    acc = acc * alpha_prev + pv_t(p_prev, c_prev)

    o_t = acc[0:DA_VDIM, :] / acc[DA_VDIM:DA_VDIM + 1, :]
    d = (o_t[:, 0:tq] - lam * o_t[:, tq:]).T
    y = d * lax.rsqrt(jnp.mean(d * d, axis=-1, keepdims=True) + EPS) * g_ref[...]
    o_ref[0] = (y * (1.0 - lam_init)).astype(BF16)


def diff_attention(q_arr, q_col, zc, lat, lam_p, subln, lam_init, prescaled):
    bsz, nq, _ = q_arr.shape
    n_ctx = zc.shape[1]
    tq = min(nq, 512)
    kcb, vcb = ATC["K"] // LANES, ATC["V"] // LANES
    in_specs = [pl.BlockSpec((4, DA_SUB), lambda b, h, i: (0, 0)),
                pl.BlockSpec((1, DA_VDIM), lambda b, h, i: (0, 0)),
                pl.BlockSpec((1, tq, LANES), lambda b, h, i: (b, i, q_col + h)),
                pl.BlockSpec((1, n_ctx, LANES), lambda b, h, i: (b, 0, kcb + h)),
                pl.BlockSpec((1, n_ctx, LANES), lambda b, h, i: (b, 0, vcb + h))]
    args = [lam_p, subln.reshape(1, DA_VDIM), q_arr, zc, zc]
    n_lat = 0
    if lat is not None:
        k_arr, k_col, v_arr, v_col = lat
        n_lat = k_arr.shape[1]
        in_specs += [pl.BlockSpec((1, n_lat, LANES), lambda b, h, i: (b, 0, k_col + h)),
                     pl.BlockSpec((1, n_lat, LANES), lambda b, h, i: (b, 0, v_col + h))]
        args += [k_arr, v_arr]
    ones_rows = 16
    scratch = [pltpu.VMEM((DA_VDIM + ones_rows, n_ctx + n_lat), BF16)]
    return pl.pallas_call(
        functools.partial(_attn_kernel, lam_init=lam_init, n_lat=n_lat, ck=512, prescaled=prescaled),
        grid=(bsz, DA_HEADS, nq // tq),
        in_specs=in_specs,
        out_specs=pl.BlockSpec((1, tq, LANES), lambda b, h, i: (b, i, h)),
        out_shape=jax.ShapeDtypeStruct((bsz, nq, BR_WIDTH), BF16),
        scratch_shapes=scratch,
        compiler_params=_params(("arbitrary", "arbitrary", "arbitrary")),
        name="diff_attention",
    )(*args)


def hyena_positions(L):
    t = jnp.linspace(0.0, 1.0, L, dtype=F32)[:, None]
    wpos = ((2.0 * math.pi / L) * jnp.arange(L, dtype=F32))[:, None]
    bands = jnp.linspace(1e-4, HY_BANDS - 1, HY_BANDS, dtype=F32)[None, :]
    fwd = jnp.concatenate([t, jnp.cos(bands * wpos), -jnp.sin(bands * wpos)], axis=-1)
    emb = jnp.concatenate([fwd, fwd[0:1], jnp.flip(fwd[1:], axis=0)], axis=0)
    mask = (jnp.arange(2 * L) != L).astype(F32)[:, None]
    pad = jnp.zeros((2 * L, LANES - HY_EMB - 1), F32)
    return jnp.concatenate([emb, pad, mask], axis=-1)


def _filter_kernel(feat_ref, w1_ref, b1_ref, w2_ref, b2_ref, w3_ref, fr_ref, dl_ref, k_ref, ss_ref):
    half = feat_ref.shape[0] // 2
    feat = feat_ref[...]
    f2 = jnp.concatenate([feat[0:half], feat[half:]], axis=1)
    hid = jnp.sin(fr_ref[0:1, :] * (_dot_hi(f2, w1_ref[...]) + b1_ref[...]))
    hid = jnp.sin(fr_ref[1:2, :] * (_dot_hi(hid, w2_ref[...]) + b2_ref[...]))
    w3 = w3_ref[...].astype(BF16)
    zero = jnp.zeros_like(w3)
    hid = hid.astype(BF16)
    h = jnp.concatenate([_dot(hid, jnp.concatenate([w3, zero], axis=0)),
                         _dot(hid, jnp.concatenate([zero, w3], axis=0))], axis=0)
    t = feat[:, 0:1]
    mask = feat[:, LANES - 1:LANES]
    k = h * jnp.exp(-t * dl_ref[...]) * mask
    k_ref[...] = k.astype(k_ref.dtype)

    @pl.when(pl.program_id(0) == 0)
    def _():
        ss_ref[...] = jnp.zeros_like(ss_ref)

    ss_ref[...] += jnp.sum(k * k, axis=0, keepdims=True)


def hyena_filter_taps(feat, w1, b1, w2, b2, w3, freq):
    two_l = feat.shape[0]
    tr = min(two_l // 2, 1024)
    nb = two_l // tr
    hd = HY_HIDDEN
    w1d = jnp.zeros((2 * LANES, 2 * hd), F32).at[:HY_EMB, :hd].set(w1).at[LANES:LANES + HY_EMB, hd:].set(w1)
    w2d = jnp.zeros((2 * hd, 2 * hd), F32).at[:hd, :hd].set(w2).at[hd:, hd:].set(w2)
    twice = lambda v: jnp.tile(v.reshape(-1, hd), (1, 2))
    deltas = jnp.abs(jnp.linspace(HY_MIN_DECAY, HY_MAX_DECAY, HY_WIDTH, dtype=F32))[None, :]
    const = lambda i: (0, 0)
    return pl.pallas_call(
        _filter_kernel,
        grid=(nb,),
        in_specs=[pl.BlockSpec((tr, LANES), lambda i: (i, 0)),
                  pl.BlockSpec((2 * LANES, 2 * hd), const),
                  pl.BlockSpec((1, 2 * hd), const),
                  pl.BlockSpec((2 * hd, 2 * hd), const),
                  pl.BlockSpec((1, 2 * hd), const),
                  pl.BlockSpec((hd, HY_WIDTH), lambda i: (0, (2 * i) // nb)),
                  pl.BlockSpec((2, 2 * hd), const),
                  pl.BlockSpec((1, HY_WIDTH), const)],
        out_specs=[pl.BlockSpec((tr, HY_WIDTH), lambda i: (i, 0)),
                   pl.BlockSpec((1, HY_WIDTH), const)],
        out_shape=[jax.ShapeDtypeStruct((two_l, HY_WIDTH), BF16),
                   jax.ShapeDtypeStruct((1, HY_WIDTH), F32)],
        compiler_params=_params(("arbitrary",)),
        name="hyena_filter",
    )(feat, w1d, twice(b1), w2d, twice(b2), w3, twice(freq), deltas)


def _short_conv_kernel(z_ref, prev_ref, next_ref, w_ref, b_ref, x0_ref, uv_ref):
    i = pl.program_id(1)
    last = pl.num_programs(1) - 1
    z = z_ref[0].astype(F32)
    tr = z.shape[0]
    halo = prev_ref.shape[1]
    before = jnp.where(i == 0, 0.0, prev_ref[0, halo - 1:halo, :].astype(F32))
    after = jnp.where(i == last, 0.0, next_ref[0, 0:1, :].astype(F32))
    row = lax.broadcasted_iota(jnp.int32, z.shape, 0)
    zm = jnp.where(row == 0, before, pltpu.roll(z, 1, axis=0))
    zp = jnp.where(row == tr - 1, after, pltpu.roll(z, tr - 1, axis=0))
    y = zm * w_ref[0:1, :] + z * w_ref[1:2, :] + zp * w_ref[2:3, :] + b_ref[...]
    x0_ref[0] = y[:, 0:HY_WIDTH].astype(BF16)
    uv_ref[0] = (y[:, HY_WIDTH:2 * HY_WIDTH] * y[:, 2 * HY_WIDTH:]).astype(BF16)


def hyena_short_conv(z, sw, sb):
    bsz, n, _ = z.shape
    tr = min(n, 512)
    halo = 16
    w3c = 3 * HY_WIDTH
    nh = n // halo
    out = jax.ShapeDtypeStruct((bsz, n, HY_WIDTH), BF16)
    return pl.pallas_call(
        _short_conv_kernel,
        grid=(bsz, n // tr),
        in_specs=[pl.BlockSpec((1, tr, w3c), lambda b, i: (b, i, 0)),
                  pl.BlockSpec((1, halo, w3c), lambda b, i: (b, jnp.maximum(i * (tr // halo) - 1, 0), 0)),
                  pl.BlockSpec((1, halo, w3c), lambda b, i: (b, jnp.minimum((i + 1) * (tr // halo), nh - 1), 0)),
                  pl.BlockSpec((3, w3c), lambda b, i: (0, 0)),
                  pl.BlockSpec((1, w3c), lambda b, i: (0, 0))],
        out_specs=[pl.BlockSpec((1, tr, HY_WIDTH), lambda b, i: (b, i, 0)),
                   pl.BlockSpec((1, tr, HY_WIDTH), lambda b, i: (b, i, 0))],
        out_shape=[out, out],
        compiler_params=_params(("arbitrary", "arbitrary")),
        name="hyena_short_conv",
    )(z, z, z, sw, sb.reshape(1, w3c))


def dft_tables():
    r = DFT_R
    n_fft = r * r
    idx = jnp.arange(r, dtype=jnp.int32)
    prod = idx[:, None] * idx[None, :]
    ang = (2.0 * math.pi / r) * (prod % r).astype(F32)
    f_re, f_im = jnp.cos(ang), -jnp.sin(ang)
    ang = (2.0 * math.pi / n_fft) * prod.astype(F32)
    t_re, t_im = jnp.cos(ang), -jnp.sin(ang)

    def cmul(a_re, a_im, b_re, b_im):
        return a_re * b_re - a_im * b_im, a_re * b_im + a_im * b_re

    g_re, g_im = cmul(f_re[None, :, :], f_im[None, :, :], t_re[:, None, :], t_im[:, None, :])
    def cplx(re, im):
        return jnp.concatenate([jnp.concatenate([re, -im], axis=-1),
                                jnp.concatenate([im, re], axis=-1)], axis=-2).astype(BF16)

    half = r // 2
    return dict(f_pad=cplx(f_re[:, :half], f_im[:, :half]),
                f_real=jnp.concatenate([f_re, f_im], axis=0).astype(BF16),
                g=cplx(g_re, g_im),
                f_inv=cplx(f_re[:half], -f_im[:half]))


def _fft1_kernel(x_ref, f_ref, are_ref, aim_ref):
    r = are_ref.shape[0]
    x = x_ref[...]
    p = _dot(f_ref[...], x.reshape(-1, x.shape[-1]))
    are_ref[...] = p[0:r].astype(BF16)
    aim_ref[...] = p[r:].astype(BF16)


def fft_level1(x, f):
    r = DFT_R
    cols = x.shape[-1]
    tc = 8192
    if x.ndim == 3:
        x_spec = pl.BlockSpec((2, x.shape[1], tc), lambda j: (0, 0, j))
    else:
        x_spec = pl.BlockSpec((x.shape[0], tc), lambda j: (0, j))
    out = jax.ShapeDtypeStruct((r, cols), BF16)
    return pl.pallas_call(
        _fft1_kernel,
        grid=(cols // tc,),
        in_specs=[x_spec, pl.BlockSpec(f.shape, lambda j: (0, 0))],
        out_specs=[pl.BlockSpec((r, tc), lambda j: (0, j)), pl.BlockSpec((r, tc), lambda j: (0, j))],
        out_shape=[out, out],
        compiler_params=_params(("arbitrary",)),
        name="fft_level1",
    )(x, f)


def _fft2_conv_kernel(are_ref, aim_ref, kare_ref, kaim_ref, g_ref, sc_ref, bre_ref, bim_ref):
    r = are_ref.shape[1]
    c = are_ref.shape[-1]
    sc = sc_ref[...]
    for j in range(are_ref.shape[0]):
        rhs = jnp.concatenate([jnp.concatenate([are_ref[j], aim_ref[j]], axis=0),
                               jnp.concatenate([kare_ref[j], kaim_ref[j]], axis=0)], axis=1)
        xk = _dot(g_ref[j], rhs)
        x_re, x_im = xk[0:r, 0:c], xk[r:, 0:c]
        k_re, k_im = xk[0:r, c:] * sc, xk[r:, c:] * sc
        y = jnp.concatenate([(x_re * k_re - x_im * k_im).astype(BF16),
                             (x_re * k_im + x_im * k_re).astype(BF16)], axis=0)
        bm = lax.dot_general(g_ref[j], y, (((0,), (0,)), ((), ())), preferred_element_type=F32)
        bre_ref[j] = bm[0:r].astype(BF16)
        bim_ref[j] = bm[r:].astype(BF16)


def fft_level2_conv(a_re, a_im, ka_re, ka_im, g, scale):
    r = DFT_R
    c = a_re.shape[-1]
    tk = 8
    blk_a = pl.BlockSpec((tk, r, c), lambda i: (i, 0, 0))
    blk_g = pl.BlockSpec((tk, 2 * r, 2 * r), lambda i: (i, 0, 0))
    out = jax.ShapeDtypeStruct((r, r, c), BF16)
    return pl.pallas_call(
        _fft2_conv_kernel,
        grid=(r // tk,),
        in_specs=[blk_a, blk_a, blk_a, blk_a, blk_g, pl.BlockSpec((1, c), lambda i: (0, 0))],
        out_specs=[blk_a, blk_a],
        out_shape=[out, out],
        compiler_params=_params(("arbitrary",)),
        name="fft_level2_conv",
    )(a_re, a_im, ka_re, ka_im, g, scale)


def _ifft1_kernel(bre_ref, bim_ref, f_ref, uv_ref, x0_ref, bias_ref, o_ref):
    half = f_ref.shape[0] // 2
    y = _dot(f_ref[...], jnp.concatenate([bre_ref[...], bim_ref[...]], axis=0))
    bias = bias_ref[...]
    o_ref[0] = ((y[0:half] + uv_ref[0].astype(F32) * bias) * x0_ref[0].astype(F32)).astype(BF16)
    o_ref[1] = ((y[half:] + uv_ref[1].astype(F32) * bias) * x0_ref[1].astype(F32)).astype(BF16)


def ifft_level1(b_re, b_im, f_inv, uv, x0, bias_t):
    r = DFT_R
    half = r // 2
    cols = b_re.shape[-1]
    tc = 8192
    blk_b = pl.BlockSpec((r, tc), lambda j: (0, j))
    blk_x = pl.BlockSpec((2, half, tc), lambda j: (0, 0, j))
    return pl.pallas_call(
        _ifft1_kernel,
        grid=(cols // tc,),
        in_specs=[blk_b, blk_b, pl.BlockSpec(f_inv.shape, lambda j: (0, 0)), blk_x, blk_x,
                  pl.BlockSpec((1, tc), lambda j: (0, 0))],
        out_specs=blk_x,
        out_shape=jax.ShapeDtypeStruct((2, half, cols), BF16),
        compiler_params=_params(("arbitrary",)),
        name="ifft_level1",
    )(b_re, b_im, f_inv, uv, x0, bias_t)


def hyena_long_conv(uv, x0, taps, sumsq, bias, tables):
    r = DFT_R
    bsz, L, c = uv.shape
    assert bsz == 2 and 2 * L == r * r
    scale = lax.rsqrt(sumsq) * (1.0 / (r * r))
    ka_re, ka_im = fft_level1(taps.reshape(r, r * c), tables["f_real"])
    uv2 = uv.reshape(2, r // 2, r * c)
    a_re, a_im = fft_level1(uv2, tables["f_pad"])
    b_re, b_im = fft_level2_conv(a_re.reshape(r, r, c), a_im.reshape(r, r, c), ka_re.reshape(r, r, c),
                                 ka_im.reshape(r, r, c), tables["g"], scale)
    bias_t = jnp.tile(bias.reshape(1, c), (1, 8192 // c))
    y = ifft_level1(b_re.reshape(r, r * c), b_im.reshape(r, r * c), tables["f_inv"], uv2,
                    x0.reshape(2, r // 2, r * c), bias_t)
    return y.reshape(2, L, c)


def dft_small_tables(n_fft):
    idx = jnp.arange(n_fft, dtype=jnp.int32)
    ang = (2.0 * math.pi / n_fft) * ((idx[:, None] * idx[None, :]) % n_fft).astype(F32)
    return jnp.cos(ang).astype(BF16), (-jnp.sin(ang)).astype(BF16)


def _conv_small_kernel(taps_ref, ss_ref, fre_ref, fim_ref, uv_ref, x0_ref, bias_ref, o_ref):
    two_l = taps_ref.shape[0]
    L = two_l // 2
    fre, fim = fre_ref[...], fim_ref[...]
    scale = lax.rsqrt(ss_ref[...]) * (1.0 / two_l)
    taps = taps_ref[...]
    k_re = _dot(fre, taps) * scale
    k_im = _dot(fim, taps) * scale
    u0, u1 = uv_ref[0], uv_ref[1]
    fre_l, fim_l = fre[:, :L], fim[:, :L]
    x_re = _dot(fre_l, u0) - _dot(fim_l, u1)
    x_im = _dot(fre_l, u1) + _dot(fim_l, u0)
    y_re = (x_re * k_re - x_im * k_im).astype(BF16)
    y_im = (x_re * k_im + x_im * k_re).astype(BF16)
    fre_t, fim_t = fre[:L, :], fim[:L, :]
    y0 = _dot(fre_t, y_re) + _dot(fim_t, y_im)
    y1 = _dot(fre_t, y_im) - _dot(fim_t, y_re)
    bias = bias_ref[...]
    o_ref[0] = ((y0 + u0.astype(F32) * bias) * x0_ref[0].astype(F32)).astype(BF16)
    o_ref[1] = ((y1 + u1.astype(F32) * bias) * x0_ref[1].astype(F32)).astype(BF16)


def hyena_long_conv_small(uv, x0, taps, sumsq, bias, tables):
    bsz, L, c = uv.shape
    assert bsz == 2
    f_re, f_im = tables
    full = lambda shape: pl.BlockSpec(shape, lambda i: (0,) * len(shape))
    return pl.pallas_call(
        _conv_small_kernel,
        grid=(1,),
        in_specs=[full((2 * L, c)), full((1, c)), full((2 * L, 2 * L)), full((2 * L, 2 * L)),
                  full((2, L, c)), full((2, L, c)), full((1, c))],
        out_specs=full((2, L, c)),
        out_shape=jax.ShapeDtypeStruct((2, L, c), BF16),
        compiler_params=_params(("arbitrary",)),
        name="hyena_conv_small",
    )(taps, sumsq, f_re, f_im, uv, x0, bias.reshape(1, c))


def _gmlp_kernel(z_ref, g_ref, b_ref, ws_ref, bs_ref, o_ref):
    zg = z_ref[0].astype(F32)
    gl = 0.5 * zg * (1.0 + lax.erf(zg * (2.0 ** -0.5)))
    w = BR_WIDTH
    u = gl[:, :w]
    v = gl[:, w:]
    mu = jnp.mean(v, axis=-1, keepdims=True)
    var = jnp.mean(jnp.square(v - mu), axis=-1, keepdims=True)
    v = ((v - mu) * lax.rsqrt(var + EPS) * g_ref[...] + b_ref[...]).astype(BF16)
    gw = w // GM_GROUPS
    lane = lax.broadcasted_iota(jnp.int32, (GM_CHUNK, LANES), 1)
    first = lane < gw
    for ci in range(zg.shape[0] // GM_CHUNK):
        rows = slice(ci * GM_CHUNK, (ci + 1) * GM_CHUNK)
        tiles = []
        for t in range(w // LANES):
            vt = v[rows, t * LANES:(t + 1) * LANES]
            tiles.append(jnp.where(first, _dot(ws_ref[2 * t], vt), _dot(ws_ref[2 * t + 1], vt)))
        vm = jnp.concatenate(tiles, axis=1) + bs_ref[...]
        o_ref[0, rows, :] = (u[rows] * vm).astype(BF16)


def gmlp(z, ln_g, ln_b, ws, bs):
    bsz, n, _ = z.shape
    tr = min(n, 512)
    w = BR_WIDTH
    bs_full = jnp.repeat(bs.T, w // GM_GROUPS, axis=1)
    return pl.pallas_call(
        _gmlp_kernel,
        grid=(bsz, n // tr),
        in_specs=[pl.BlockSpec((1, tr, 2 * w), lambda b, i: (b, i, 0)),
                  pl.BlockSpec((1, w), lambda b, i: (0, 0)),
                  pl.BlockSpec((1, w), lambda b, i: (0, 0)),
                  pl.BlockSpec((GM_GROUPS, GM_CHUNK, GM_CHUNK), lambda b, i: (0, 0, 0)),
                  pl.BlockSpec((GM_CHUNK, w), lambda b, i: (0, 0))],
        out_specs=pl.BlockSpec((1, tr, w), lambda b, i: (b, i, 0)),
        out_shape=jax.ShapeDtypeStruct((bsz, n, w), BF16),
        compiler_params=_params(("arbitrary", "arbitrary")),
        name="gmlp",
    )(z, ln_g.reshape(1, w), ln_b.reshape(1, w), ws.astype(BF16), bs_full)


def _merge_kernel(ya_ref, yb_ref, yc_ref, ga_ref, gb_ref, gc_ref, mg_ref, x_ref, mod_ref, np_ref,
                  wb_ref, wo_ref, o_ref):
    acc = None
    for i, (y_ref, g_ref) in enumerate(((ya_ref, ga_ref), (yb_ref, gb_ref), (yc_ref, gc_ref))):
        g = g_ref[0]
        gated = y_ref[0] * (g * jax.nn.sigmoid(g))
        sel = jax.nn.sigmoid(mg_ref[0, :, i * D_MODEL:(i + 1) * D_MODEL])
        term = sel * _dot(gated, wb_ref[i]).astype(BF16)
        acc = term if acc is None else acc + term
    out = _dot(acc, wo_ref[...])
    r = out * lax.rsqrt(jnp.mean(out * out, axis=-1, keepdims=True) + EPS) * np_ref[...]
    gt = mod_ref[0, :, 2 * D_MODEL:]
    o_ref[0] = x_ref[0] + gt * r


def merge_out(ya, yb, yc, zat, zmg, x, mod_rows, row_of_batch, npost, wb, wo):
    bsz, n, _ = x.shape
    tm = min(n, 512)
    w = BR_WIDTH
    yspec = pl.BlockSpec((1, tm, w), lambda b, i: (b, i, 0))

    def zspec(name):
        cb = ATC[name] // w
        return pl.BlockSpec((1, tm, w), lambda b, i: (b, i, cb))

    return pl.pallas_call(
        _merge_kernel,
        grid=(bsz, n // tm),
        in_specs=[yspec, yspec, yspec, zspec("GA"), zspec("GB"), zspec("GC"),
                  pl.BlockSpec((1, tm, 3 * D_MODEL), lambda b, i: (b, i, 0)),
                  pl.BlockSpec((1, tm, D_MODEL), lambda b, i: (b, i, 0)),
                  pl.BlockSpec((1, 1, 3 * D_MODEL), lambda b, i: (row_of_batch(b), 0, 0)),
                  pl.BlockSpec((1, D_MODEL), lambda b, i: (0, 0)),
                  pl.BlockSpec((3, w, D_MODEL), lambda b, i: (0, 0, 0)),
                  pl.BlockSpec((D_MODEL, D_MODEL), lambda b, i: (0, 0))],
        out_specs=pl.BlockSpec((1, tm, D_MODEL), lambda b, i: (b, i, 0)),
        out_shape=jax.ShapeDtypeStruct((bsz, n, D_MODEL), F32),
        compiler_params=_params(("arbitrary", "arbitrary")),
        name="merge_out",
    )(ya, yb, yc, zat, zat, zat, zmg, x, mod_rows, npost.reshape(1, D_MODEL), wb, wo)


def kernel(x, c, ctx, c_ctx, ada_w, ada_b, norm_pre, norm_post, w_in, da_lambda, da_subln, hy_short_w,
           hy_short_b, hy_f_w1, hy_f_b1, hy_f_w2, hy_f_b2, hy_f_w3, hy_f_freq, hy_bias, gm_ln_g, gm_ln_b,
           gm_ws, gm_bs, w_branch, w_out):
    bsz, n, _ = x.shape
    n_ctx = ctx.shape[1]
    assert bsz == 2 and 2 * n == DFT_R * DFT_R

    cond8 = jnp.zeros((8, D_MODEL), F32).at[0:bsz].set(c).at[bsz].set(c_ctx)
    mod = modulation_all(cond8, ada_w, ada_b)
    lat_row = lambda b: b
    ctx_row = lambda b: bsz

    cos_t, sin_t = rope_tables(n)
    feat = hyena_positions(n)
    feat_c = hyena_positions(n_ctx)
    tables = dft_tables()
    tables_c = dft_small_tables(2 * n_ctx)
    vb, qb = ATC["V"] // LANES, ATC["Q"] // LANES

    xc = ctx
    for l in range(DEPTH):
        last = l == DEPTH - 1
        lam_init = 0.8 - 0.6 * math.exp(-0.3 * l)
        mod_rows = mod[l].reshape(8, 1, 3 * D_MODEL)
        w_l = w_in[l].astype(BF16)
        wb_l = w_branch[l].astype(BF16)
        wo_l = w_out[l].astype(BF16)
        filt_w = (hy_f_w1[l], hy_f_b1[l], hy_f_w2[l], hy_f_b2[l], hy_f_w3[l], hy_f_freq[l])

        z, qr, kr = in_projection(x, mod_rows, lat_row, norm_pre[l], w_l, (cos_t, sin_t))
        zc = in_projection(xc, mod_rows, ctx_row, norm_pre[l], w_l)

        taps, sumsq = hyena_filter_taps(feat, *filt_w)
        x0, uv = hyena_short_conv(z["HY"], hy_short_w[l], hy_short_b[l])
        y_a = diff_attention(qr, 0, zc["AT"], (kr, 0, z["AT"], vb), da_lambda[l], da_subln[l], lam_init, True)
        y_b = hyena_long_conv(uv, x0, taps, sumsq, hy_bias[l], tables)
        y_c = gmlp(z["GM"], gm_ln_g[l], gm_ln_b[l], gm_ws[l], gm_bs[l])
        x_new = merge_out(y_a, y_b, y_c, z["AT"], z["MG"], x, mod_rows, lat_row, norm_post[l], wb_l, wo_l)

        if not last:
            yc_a = diff_attention(zc["AT"], qb, zc["AT"], None, da_lambda[l], da_subln[l], lam_init, False)
            taps_c, sumsq_c = hyena_filter_taps(feat_c, *filt_w)
            x0c, uvc = hyena_short_conv(zc["HY"], hy_short_w[l], hy_short_b[l])
            yc_b = hyena_long_conv_small(uvc, x0c, taps_c, sumsq_c, hy_bias[l], tables_c)
            yc_c = gmlp(zc["GM"], gm_ln_g[l], gm_ln_b[l], gm_ws[l], gm_bs[l])
            xc = merge_out(yc_a, yc_b, yc_c, zc["AT"], zc["MG"], xc, mod_rows, ctx_row, norm_post[l], wb_l, wo_l)
        x = x_new
    return x
```

```python
import functools
import math

import jax
import jax.numpy as jnp
from jax import lax
from jax.experimental import pallas as pl
from jax.experimental.pallas import tpu as pltpu

F32 = jnp.float32
BF16 = jnp.bfloat16

D_MODEL = 1024
DEPTH = 4
GRID_W = 64
EPS = 1e-6
BR_WIDTH = 512
DA_SUB = 64
DA_VDIM = 128
DA_HEADS = 4
ROPE_BASE = 10000.0
ROPE_NF = 16
HY_WIDTH = 512
HY_BANDS = 16
HY_EMB = 33
HY_HIDDEN = 64
HY_MIN_DECAY = math.log(1e-2) / 1.5
HY_MAX_DECAY = math.log(1e-2) / 0.3
GM_GROUPS = 8
GM_CHUNK = 128

LANES = 128
DFT_R = 128
QK_SCALE = DA_SUB ** -0.5 * math.log2(math.e)

_REF_COLS = dict(K=(0, 512), V=(512, 512), Q=(1024, 512), GA=(1536, 512), HY=(2048, 1536),
                 GB=(3584, 512), GM=(4096, 1024), GC=(5120, 512), MG=(5632, 3072))
_GROUPS = (("MG", ("MG",)), ("HY", ("HY",)), ("GM", ("GM",)), ("AT", ("K", "V", "Q", "GA", "GB", "GC")))
GROUP_WIDTH = {g: sum(_REF_COLS[nm][1] for nm in names) for g, names in _GROUPS}
GROUP_WIDTH["GM"] = BR_WIDTH
ATC = {}
_off = 0
for _name in _GROUPS[-1][1]:
    ATC[_name] = _off
    _off += _REF_COLS[_name][1]

VMEM_LIMIT = 48 * 1024 * 1024
INPROJ_VMEM_LIMIT = 56 * 1024 * 1024


def _params(sem):
    return pltpu.CompilerParams(dimension_semantics=sem, vmem_limit_bytes=VMEM_LIMIT)


def _dot(a, b):
    return jnp.dot(a, b, preferred_element_type=F32)


def _dot_hi(a, b):
    return jnp.dot(a, b, preferred_element_type=F32, precision=lax.Precision.HIGHEST)


def _mod_kernel(c_ref, w_ref, b_ref, o_ref):
    cond = c_ref[...]
    s = cond * jax.nn.sigmoid(cond)
    o_ref[0] = _dot_hi(s, w_ref[0]) + b_ref[0]


def modulation_all(cond8, ada_w, ada_b):
    tn = 1024
    return pl.pallas_call(
        _mod_kernel,
        grid=(DEPTH, 3 * D_MODEL // tn),
        in_specs=[pl.BlockSpec((8, D_MODEL), lambda l, j: (0, 0)),
                  pl.BlockSpec((1, D_MODEL, tn), lambda l, j: (l, 0, j)),
                  pl.BlockSpec((1, 1, tn), lambda l, j: (l, 0, j))],
        out_specs=pl.BlockSpec((1, 8, tn), lambda l, j: (l, 0, j)),
        out_shape=jax.ShapeDtypeStruct((DEPTH, 8, 3 * D_MODEL), F32),
        compiler_params=_params(("arbitrary", "arbitrary")),
        name="modulation",
    )(cond8, ada_w, ada_b.reshape(DEPTH, 1, 3 * D_MODEL))


def _gmlp_math(zg, g_ref, b_ref, ws_ref, bs_ref):
    gl = 0.5 * zg * (1.0 + lax.erf(zg * (2.0 ** -0.5)))
    w = BR_WIDTH
    u = gl[:, :w]
    v = gl[:, w:]
    mu = jnp.mean(v, axis=-1, keepdims=True)
    var = jnp.mean(jnp.square(v - mu), axis=-1, keepdims=True)
    v = ((v - mu) * lax.rsqrt(var + EPS) * g_ref[...] + b_ref[...]).astype(BF16)
    gw = w // GM_GROUPS
    lane = lax.broadcasted_iota(jnp.int32, (GM_CHUNK, LANES), 1)
    first = lane < gw
    out = []
    for ci in range(zg.shape[0] // GM_CHUNK):
        rows = slice(ci * GM_CHUNK, (ci + 1) * GM_CHUNK)
        tiles = []
        for t in range(w // LANES):
            vt = v[rows, t * LANES:(t + 1) * LANES]
            tiles.append(jnp.where(first, _dot(ws_ref[2 * t], vt), _dot(ws_ref[2 * t + 1], vt)))
        vm = jnp.concatenate(tiles, axis=1) + bs_ref[...]
        out.append((u[rows] * vm).astype(BF16))
    return jnp.concatenate(out, axis=0)


def gmlp_params(ln_g, ln_b, ws, bs):
    w = BR_WIDTH
    bs_full = jnp.repeat(bs.T, w // GM_GROUPS, axis=1)
    return ln_g.reshape(1, w), ln_b.reshape(1, w), ws.astype(BF16), bs_full


_QK_WIDTH = DA_HEADS * LANES
MXU_TILE = 256
MAX_CHUNK = 2048
_INPROJ_CHUNKS = []
for _g, _names in _GROUPS:
    _dst = 0
    for _name in _names:
        _start, _width = _REF_COLS[_name]
        for _o in range(0, _width, MAX_CHUNK):
            _w = min(MAX_CHUNK, _width - _o)
            assert (_start + _o) % MXU_TILE == 0 and _w % MXU_TILE == 0
            _INPROJ_CHUNKS.append((_start + _o, _g, _dst + _o, _w))
        _dst += _width
W_IN_COLS = sum(w for _, w in _REF_COLS.values())


def _inproj_kernel(*refs, rope):
    ng = len(_GROUPS)
    x_ref, mod_ref, g_ref, w_ref = refs[0:4]
    gm_refs = refs[4:8]
    if rope:
        cos_ref, sin_ref = refs[8:10]
        o_refs = refs[10:10 + ng]
        q_ref, k_ref = refs[10 + ng:]
    else:
        o_refs = refs[8:8 + ng]
    o_of = {gn: o_ref for (gn, _), o_ref in zip(_GROUPS, o_refs)}
    x = x_ref[0]
    y = x * lax.rsqrt(jnp.mean(x * x, axis=-1, keepdims=True) + EPS) * g_ref[...]
    sh = mod_ref[0, :, 0:D_MODEL]
    sc = mod_ref[0, :, D_MODEL:2 * D_MODEL]
    h = (y * (1.0 + sc) + sh).astype(BF16)

    if rope:
        lane = lax.broadcasted_iota(jnp.int32, cos_ref.shape, 1)
        low = (lane % (2 * ROPE_NF)) < ROPE_NF
        cs = cos_ref[...]
        sn = sin_ref[...]

        def rot(v):
            partner = jnp.where(low, pltpu.roll(v, LANES - ROPE_NF, axis=1), pltpu.roll(v, ROPE_NF, axis=1))
            return (v * cs + partner * sn).astype(BF16)

    for src, gname, dst, width in _INPROJ_CHUNKS:
        zc = _dot(h, w_ref[:, src:src + width])
        if gname == "GM":
            o_of[gname][0] = _gmlp_math(zc, *gm_refs)
            continue
        o_of[gname][0, :, dst:dst + width] = zc.astype(BF16)
        if rope and gname == "AT" and dst in (ATC["K"], ATC["Q"]):
            out, scale = (k_ref, 1.0) if dst == ATC["K"] else (q_ref, QK_SCALE)
            for hd in range(DA_HEADS):
                cols = slice(hd * LANES, (hd + 1) * LANES)
                out[0, :, cols] = rot(zc[:, cols] * scale)


def in_projection(x, mod_rows, row_of_batch, g, w, gm, rope_tabs=None):
    bsz, n, _ = x.shape
    tm = min(n, 512)
    rope = rope_tabs is not None
    in_specs = [pl.BlockSpec((1, tm, D_MODEL), lambda b, i: (b, i, 0)),
                pl.BlockSpec((1, 1, 3 * D_MODEL), lambda b, i: (row_of_batch(b), 0, 0)),
                pl.BlockSpec((1, D_MODEL), lambda b, i: (0, 0)),
                pl.BlockSpec((D_MODEL, W_IN_COLS), lambda b, i: (0, 0), pipeline_mode=pl.Buffered(1))]
    w_half = BR_WIDTH
    in_specs += [pl.BlockSpec((1, w_half), lambda b, i: (0, 0)),
                 pl.BlockSpec((1, w_half), lambda b, i: (0, 0)),
                 pl.BlockSpec((GM_GROUPS, GM_CHUNK, GM_CHUNK), lambda b, i: (0, 0, 0)),
                 pl.BlockSpec((GM_CHUNK, w_half), lambda b, i: (0, 0))]
    args = [x, mod_rows, g.reshape(1, D_MODEL), w] + list(gm)
    if rope:
        in_specs += [pl.BlockSpec((tm, LANES), lambda b, i: (i, 0))] * 2
        args += list(rope_tabs)
    out_specs = [pl.BlockSpec((1, tm, GROUP_WIDTH[gn]), lambda b, i: (b, i, 0)) for gn, _ in _GROUPS]
    out_shape = [jax.ShapeDtypeStruct((bsz, n, GROUP_WIDTH[gn]), BF16) for gn, _ in _GROUPS]
    if rope:
        out_specs += [pl.BlockSpec((1, tm, _QK_WIDTH), lambda b, i: (b, i, 0))] * 2
        out_shape += [jax.ShapeDtypeStruct((bsz, n, _QK_WIDTH), BF16)] * 2
    res = pl.pallas_call(
        functools.partial(_inproj_kernel, rope=rope),
        grid=(bsz, n // tm),
        in_specs=in_specs,
        out_specs=out_specs,
        out_shape=out_shape,
        compiler_params=pltpu.CompilerParams(dimension_semantics=("arbitrary", "arbitrary"),
                                             vmem_limit_bytes=INPROJ_VMEM_LIMIT),
        name="in_projection",
    )(*args)
    z = {gn: r for (gn, _), r in zip(_GROUPS, res)}
    return (z, res[-2], res[-1]) if rope else z


def rope_tables(n):
    pos = jnp.arange(n)
    row = (pos // GRID_W).astype(F32)
    col = (pos % GRID_W).astype(F32)
    inv = ROPE_BASE ** (-jnp.arange(ROPE_NF, dtype=F32) / ROPE_NF)
    ar = row[:, None] * inv
    ac = col[:, None] * inv
    cos64 = jnp.concatenate([jnp.cos(ar), jnp.cos(ar), jnp.cos(ac), jnp.cos(ac)], axis=1)
    sin64 = jnp.concatenate([-jnp.sin(ar), jnp.sin(ar), -jnp.sin(ac), jnp.sin(ac)], axis=1)
    return jnp.tile(cos64, (1, 2)), jnp.tile(sin64, (1, 2))


def _attn_kernel(*refs, lam_init, n_lat, ck, prescaled):
    if n_lat:
        lam_ref, g_ref, q_ref, kc_ref, vc_ref, k_ref, v_ref, o_ref, vt_ref = refs
    else:
        lam_ref, g_ref, q_ref, kc_ref, vc_ref, o_ref, vt_ref = refs
    tq = q_ref.shape[1]
    n_ctx = kc_ref.shape[1]

    @pl.when(pl.program_id(2) == 0)
    def _():
        vt_ref[0:DA_VDIM, 0:n_ctx] = vc_ref[0].astype(F32).T.astype(BF16)
        for c in range(n_lat // ck):
            vt_ref[0:DA_VDIM, n_ctx + c * ck:n_ctx + (c + 1) * ck] = (
                v_ref[0, c * ck:(c + 1) * ck, :].astype(F32).T.astype(BF16))
        pad = vt_ref.shape[0] - DA_VDIM
        row = lax.broadcasted_iota(jnp.int32, (pad, vt_ref.shape[1]), 0)
        vt_ref[DA_VDIM:, :] = (row == 0).astype(BF16)

    lp = lam_ref[...]
    lam = (jnp.exp(jnp.sum(lp[0:1] * lp[1:2], axis=1, keepdims=True))
           - jnp.exp(jnp.sum(lp[2:3] * lp[3:4], axis=1, keepdims=True)) + lam_init)

    q = q_ref[0]
    if not prescaled:
        q = (q.astype(F32) * QK_SCALE).astype(BF16)
    lane = lax.broadcasted_iota(jnp.int32, q.shape, 1)
    zero = jnp.zeros_like(q)
    qq = jnp.concatenate([jnp.where(lane < DA_SUB, q, zero), jnp.where(lane >= DA_SUB, q, zero)], axis=0)

    chunks = [(lambda: kc_ref[0], 0, n_ctx)]
    for c in range(n_lat // ck):
        chunks.append((lambda c=c: k_ref[0, c * ck:(c + 1) * ck, :], n_ctx + c * ck, ck))

    def scores_t(c):
        return lax.dot_general(chunks[c][0](), qq, (((1,), (1,)), ((), ())), preferred_element_type=F32)

    def pv_t(p_t, c):
        _, off, width = chunks[c]
        return _dot(vt_ref[:, off:off + width], p_t)

    m = jnp.full((1, 2 * tq), -jnp.inf, F32)
    acc = jnp.zeros((vt_ref.shape[0], 2 * tq), F32)
    pending = None
    s_next = scores_t(0)
    for c in range(len(chunks)):
        s = s_next
        if c + 1 < len(chunks):
            s_next = scores_t(c + 1)
        part = s[0:64]
        for r in range(64, s.shape[0], 64):
            part = jnp.maximum(part, s[r:r + 64])
        m_new = jnp.maximum(m, jnp.max(part, axis=0, keepdims=True))
        alpha = jnp.exp2(m - m_new)
        p_t = jnp.exp2(s - m_new).astype(BF16)
        m = m_new
        if pending is not None:
            p_prev, alpha_prev, c_prev = pending
            acc = acc * alpha_prev + pv_t(p_prev, c_prev)
        pending = (p_t, alpha, c)
    p_prev, alpha_prev, c_prev = pending
    acc = acc * alpha_prev + pv_t(p_prev, c_prev)

    o_t = acc[0:DA_VDIM, :] / acc[DA_VDIM:DA_VDIM + 1, :]
    d = (o_t[:, 0:tq] - lam * o_t[:, tq:]).T
    y = d * lax.rsqrt(jnp.mean(d * d, axis=-1, keepdims=True) + EPS) * g_ref[...]
    o_ref[0] = (y * (1.0 - lam_init)).astype(BF16)


def diff_attention(q_arr, q_col, zc, lat, lam_p, subln, lam_init, prescaled):
    bsz, nq, _ = q_arr.shape
    n_ctx = zc.shape[1]
    tq = min(nq, 512)
    kcb, vcb = ATC["K"] // LANES, ATC["V"] // LANES
    in_specs = [pl.BlockSpec((4, DA_SUB), lambda b, h, i: (0, 0)),
                pl.BlockSpec((1, DA_VDIM), lambda b, h, i: (0, 0)),
                pl.BlockSpec((1, tq, LANES), lambda b, h, i: (b, i, q_col + h)),
                pl.BlockSpec((1, n_ctx, LANES), lambda b, h, i: (b, 0, kcb + h)),
                pl.BlockSpec((1, n_ctx, LANES), lambda b, h, i: (b, 0, vcb + h))]
    args = [lam_p, subln.reshape(1, DA_VDIM), q_arr, zc, zc]
    n_lat = 0
    if lat is not None:
        k_arr, k_col, v_arr, v_col = lat
        n_lat = k_arr.shape[1]
        in_specs += [pl.BlockSpec((1, n_lat, LANES), lambda b, h, i: (b, 0, k_col + h)),
                     pl.BlockSpec((1, n_lat, LANES), lambda b, h, i: (b, 0, v_col + h))]
        args += [k_arr, v_arr]
    ones_rows = 16
    scratch = [pltpu.VMEM((DA_VDIM + ones_rows, n_ctx + n_lat), BF16)]
    return pl.pallas_call(
        functools.partial(_attn_kernel, lam_init=lam_init, n_lat=n_lat, ck=512, prescaled=prescaled),
        grid=(bsz, DA_HEADS, nq // tq),
        in_specs=in_specs,
        out_specs=pl.BlockSpec((1, tq, LANES), lambda b, h, i: (b, i, h)),
        out_shape=jax.ShapeDtypeStruct((bsz, nq, BR_WIDTH), BF16),
        scratch_shapes=scratch,
        compiler_params=_params(("arbitrary", "arbitrary", "arbitrary")),
        name="diff_attention",
    )(*args)


def hyena_positions(L):
    t = jnp.linspace(0.0, 1.0, L, dtype=F32)[:, None]
    wpos = ((2.0 * math.pi / L) * jnp.arange(L, dtype=F32))[:, None]
    bands = jnp.linspace(1e-4, HY_BANDS - 1, HY_BANDS, dtype=F32)[None, :]
    fwd = jnp.concatenate([t, jnp.cos(bands * wpos), -jnp.sin(bands * wpos)], axis=-1)
    emb = jnp.concatenate([fwd, fwd[0:1], jnp.flip(fwd[1:], axis=0)], axis=0)
    mask = (jnp.arange(2 * L) != L).astype(F32)[:, None]
    pad = jnp.zeros((2 * L, LANES - HY_EMB - 1), F32)
    return jnp.concatenate([emb, pad, mask], axis=-1)


def _filter_kernel(feat_ref, w1_ref, b1_ref, w2_ref, b2_ref, w3_ref, fr_ref, dl_ref, k_ref, ss_ref):
    half = feat_ref.shape[0] // 2
    feat = feat_ref[...]
    f2 = jnp.concatenate([feat[0:half], feat[half:]], axis=1)
    hid = jnp.sin(fr_ref[0:1, :] * (_dot_hi(f2, w1_ref[...]) + b1_ref[...]))
    hid = jnp.sin(fr_ref[1:2, :] * (_dot_hi(hid, w2_ref[...]) + b2_ref[...]))
    w3 = w3_ref[...].astype(BF16)
    zero = jnp.zeros_like(w3)
    hid = hid.astype(BF16)
    h = jnp.concatenate([_dot(hid, jnp.concatenate([w3, zero], axis=0)),
                         _dot(hid, jnp.concatenate([zero, w3], axis=0))], axis=0)
    t = feat[:, 0:1]
    mask = feat[:, LANES - 1:LANES]
    k = h * jnp.exp(-t * dl_ref[...]) * mask
    k_ref[...] = k.astype(k_ref.dtype)

    @pl.when(pl.program_id(0) == 0)
    def _():
        ss_ref[...] = jnp.zeros_like(ss_ref)

    ss_ref[...] += jnp.sum(k * k, axis=0, keepdims=True)


def hyena_filter_taps(feat, w1, b1, w2, b2, w3, freq):
    two_l = feat.shape[0]
    tr = min(two_l // 2, 1024)
    nb = two_l // tr
    hd = HY_HIDDEN
    w1d = jnp.zeros((2 * LANES, 2 * hd), F32).at[:HY_EMB, :hd].set(w1).at[LANES:LANES + HY_EMB, hd:].set(w1)
    w2d = jnp.zeros((2 * hd, 2 * hd), F32).at[:hd, :hd].set(w2).at[hd:, hd:].set(w2)
    twice = lambda v: jnp.tile(v.reshape(-1, hd), (1, 2))
    deltas = jnp.abs(jnp.linspace(HY_MIN_DECAY, HY_MAX_DECAY, HY_WIDTH, dtype=F32))[None, :]
    const = lambda i: (0, 0)
    return pl.pallas_call(
        _filter_kernel,
        grid=(nb,),
        in_specs=[pl.BlockSpec((tr, LANES), lambda i: (i, 0)),
                  pl.BlockSpec((2 * LANES, 2 * hd), const),
                  pl.BlockSpec((1, 2 * hd), const),
                  pl.BlockSpec((2 * hd, 2 * hd), const),
                  pl.BlockSpec((1, 2 * hd), const),
                  pl.BlockSpec((hd, HY_WIDTH), lambda i: (0, (2 * i) // nb)),
                  pl.BlockSpec((2, 2 * hd), const),
                  pl.BlockSpec((1, HY_WIDTH), const)],
        out_specs=[pl.BlockSpec((tr, HY_WIDTH), lambda i: (i, 0)),
                   pl.BlockSpec((1, HY_WIDTH), const)],
        out_shape=[jax.ShapeDtypeStruct((two_l, HY_WIDTH), BF16),
                   jax.ShapeDtypeStruct((1, HY_WIDTH), F32)],
        compiler_params=_params(("arbitrary",)),
        name="hyena_filter",
    )(feat, w1d, twice(b1), w2d, twice(b2), w3, twice(freq), deltas)


def _short_conv_kernel(z_ref, prev_ref, next_ref, w_ref, b_ref, x0_ref, uv_ref):
    i = pl.program_id(1)
    last = pl.num_programs(1) - 1
    z = z_ref[0].astype(F32)
    tr = z.shape[0]
    halo = prev_ref.shape[1]
    before = jnp.where(i == 0, 0.0, prev_ref[0, halo - 1:halo, :].astype(F32))
    after = jnp.where(i == last, 0.0, next_ref[0, 0:1, :].astype(F32))
    row = lax.broadcasted_iota(jnp.int32, z.shape, 0)
    zm = jnp.where(row == 0, before, pltpu.roll(z, 1, axis=0))
    zp = jnp.where(row == tr - 1, after, pltpu.roll(z, tr - 1, axis=0))
    y = zm * w_ref[0:1, :] + z * w_ref[1:2, :] + zp * w_ref[2:3, :] + b_ref[...]
    x0_ref[0] = y[:, 0:HY_WIDTH].astype(BF16)
    uv_ref[0] = (y[:, HY_WIDTH:2 * HY_WIDTH] * y[:, 2 * HY_WIDTH:]).astype(BF16)


def hyena_short_conv(z, sw, sb):
    bsz, n, _ = z.shape
    tr = min(n, 512)
    halo = 16
    w3c = 3 * HY_WIDTH
    nh = n // halo
    out = jax.ShapeDtypeStruct((bsz, n, HY_WIDTH), BF16)
    return pl.pallas_call(
        _short_conv_kernel,
        grid=(bsz, n // tr),
        in_specs=[pl.BlockSpec((1, tr, w3c), lambda b, i: (b, i, 0)),
                  pl.BlockSpec((1, halo, w3c), lambda b, i: (b, jnp.maximum(i * (tr // halo) - 1, 0), 0)),
                  pl.BlockSpec((1, halo, w3c), lambda b, i: (b, jnp.minimum((i + 1) * (tr // halo), nh - 1), 0)),
                  pl.BlockSpec((3, w3c), lambda b, i: (0, 0)),
                  pl.BlockSpec((1, w3c), lambda b, i: (0, 0))],
        out_specs=[pl.BlockSpec((1, tr, HY_WIDTH), lambda b, i: (b, i, 0)),
                   pl.BlockSpec((1, tr, HY_WIDTH), lambda b, i: (b, i, 0))],
        out_shape=[out, out],
        compiler_params=_params(("arbitrary", "arbitrary")),
        name="hyena_short_conv",
    )(z, z, z, sw, sb.reshape(1, w3c))


def dft_tables():
    r = DFT_R
    n_fft = r * r
    idx = jnp.arange(r, dtype=jnp.int32)
    prod = idx[:, None] * idx[None, :]
    ang = (2.0 * math.pi / r) * (prod % r).astype(F32)
    f_re, f_im = jnp.cos(ang), -jnp.sin(ang)
    ang = (2.0 * math.pi / n_fft) * prod.astype(F32)
    t_re, t_im = jnp.cos(ang), -jnp.sin(ang)

    def cmul(a_re, a_im, b_re, b_im):
        return a_re * b_re - a_im * b_im, a_re * b_im + a_im * b_re

    g_re, g_im = cmul(f_re[None, :, :], f_im[None, :, :], t_re[:, None, :], t_im[:, None, :])
    def cplx(re, im):
        return jnp.concatenate([jnp.concatenate([re, -im], axis=-1),
                                jnp.concatenate([im, re], axis=-1)], axis=-2).astype(BF16)

    half = r // 2
    return dict(f_pad=cplx(f_re[:, :half], f_im[:, :half]),
                f_real=jnp.concatenate([f_re, f_im], axis=0).astype(BF16),
                g=cplx(g_re, g_im),
                f_inv=cplx(f_re[:half], -f_im[:half]))


def _fft1_kernel(x_ref, f_ref, are_ref, aim_ref):
    r = are_ref.shape[0]
    x = x_ref[...]
    p = _dot(f_ref[...], x.reshape(-1, x.shape[-1]))
    are_ref[...] = p[0:r].astype(BF16)
    aim_ref[...] = p[r:].astype(BF16)


def fft_level1(x, f):
    r = DFT_R
    cols = x.shape[-1]
    tc = 8192
    if x.ndim == 3:
        x_spec = pl.BlockSpec((2, x.shape[1], tc), lambda j: (0, 0, j))
    else:
        x_spec = pl.BlockSpec((x.shape[0], tc), lambda j: (0, j))
    out = jax.ShapeDtypeStruct((r, cols), BF16)
    return pl.pallas_call(
        _fft1_kernel,
        grid=(cols // tc,),
        in_specs=[x_spec, pl.BlockSpec(f.shape, lambda j: (0, 0))],
        out_specs=[pl.BlockSpec((r, tc), lambda j: (0, j)), pl.BlockSpec((r, tc), lambda j: (0, j))],
        out_shape=[out, out],
        compiler_params=_params(("arbitrary",)),
        name="fft_level1",
    )(x, f)


def _fft2_conv_kernel(are_ref, aim_ref, kare_ref, kaim_ref, g_ref, sc_ref, bre_ref, bim_ref):
    r = are_ref.shape[1]
    c = are_ref.shape[-1]
    sc = sc_ref[...]
    for j in range(are_ref.shape[0]):
        rhs = jnp.concatenate([jnp.concatenate([are_ref[j], aim_ref[j]], axis=0),
                               jnp.concatenate([kare_ref[j], kaim_ref[j]], axis=0)], axis=1)
        xk = _dot(g_ref[j], rhs)
        x_re, x_im = xk[0:r, 0:c], xk[r:, 0:c]
        k_re, k_im = xk[0:r, c:] * sc, xk[r:, c:] * sc
        y = jnp.concatenate([(x_re * k_re - x_im * k_im).astype(BF16),
                             (x_re * k_im + x_im * k_re).astype(BF16)], axis=0)
        bm = lax.dot_general(g_ref[j], y, (((0,), (0,)), ((), ())), preferred_element_type=F32)
        bre_ref[j] = bm[0:r].astype(BF16)
        bim_ref[j] = bm[r:].astype(BF16)


def fft_level2_conv(a_re, a_im, ka_re, ka_im, g, scale):
    r = DFT_R
    c = a_re.shape[-1]
    tk = 8
    blk_a = pl.BlockSpec((tk, r, c), lambda i: (i, 0, 0))
    blk_g = pl.BlockSpec((tk, 2 * r, 2 * r), lambda i: (i, 0, 0))
    out = jax.ShapeDtypeStruct((r, r, c), BF16)
    return pl.pallas_call(
        _fft2_conv_kernel,
        grid=(r // tk,),
        in_specs=[blk_a, blk_a, blk_a, blk_a, blk_g, pl.BlockSpec((1, c), lambda i: (0, 0))],
        out_specs=[blk_a, blk_a],
        out_shape=[out, out],
        compiler_params=_params(("arbitrary",)),
        name="fft_level2_conv",
    )(a_re, a_im, ka_re, ka_im, g, scale)


def _ifft1_kernel(bre_ref, bim_ref, f_ref, uv_ref, x0_ref, bias_ref, o_ref):
    half = f_ref.shape[0] // 2
    y = _dot(f_ref[...], jnp.concatenate([bre_ref[...], bim_ref[...]], axis=0))
    bias = bias_ref[...]
    o_ref[0] = ((y[0:half] + uv_ref[0].astype(F32) * bias) * x0_ref[0].astype(F32)).astype(BF16)
    o_ref[1] = ((y[half:] + uv_ref[1].astype(F32) * bias) * x0_ref[1].astype(F32)).astype(BF16)


def ifft_level1(b_re, b_im, f_inv, uv, x0, bias_t):
    r = DFT_R
    half = r // 2
    cols = b_re.shape[-1]
    tc = 8192
    blk_b = pl.BlockSpec((r, tc), lambda j: (0, j))
    blk_x = pl.BlockSpec((2, half, tc), lambda j: (0, 0, j))
    return pl.pallas_call(
        _ifft1_kernel,
        grid=(cols // tc,),
        in_specs=[blk_b, blk_b, pl.BlockSpec(f_inv.shape, lambda j: (0, 0)), blk_x, blk_x,
                  pl.BlockSpec((1, tc), lambda j: (0, 0))],
        out_specs=blk_x,
        out_shape=jax.ShapeDtypeStruct((2, half, cols), BF16),
        compiler_params=_params(("arbitrary",)),
        name="ifft_level1",
    )(b_re, b_im, f_inv, uv, x0, bias_t)


def hyena_long_conv(uv, x0, taps, sumsq, bias, tables):
    r = DFT_R
    bsz, L, c = uv.shape
    assert bsz == 2 and 2 * L == r * r
    scale = lax.rsqrt(sumsq) * (1.0 / (r * r))
    ka_re, ka_im = fft_level1(taps.reshape(r, r * c), tables["f_real"])
    uv2 = uv.reshape(2, r // 2, r * c)
    a_re, a_im = fft_level1(uv2, tables["f_pad"])
    b_re, b_im = fft_level2_conv(a_re.reshape(r, r, c), a_im.reshape(r, r, c), ka_re.reshape(r, r, c),
                                 ka_im.reshape(r, r, c), tables["g"], scale)
    bias_t = jnp.tile(bias.reshape(1, c), (1, 8192 // c))
    y = ifft_level1(b_re.reshape(r, r * c), b_im.reshape(r, r * c), tables["f_inv"], uv2,
                    x0.reshape(2, r // 2, r * c), bias_t)
    return y.reshape(2, L, c)


def dft_small_tables(n_fft):
    idx = jnp.arange(n_fft, dtype=jnp.int32)
    ang = (2.0 * math.pi / n_fft) * ((idx[:, None] * idx[None, :]) % n_fft).astype(F32)
    return jnp.cos(ang).astype(BF16), (-jnp.sin(ang)).astype(BF16)


def _conv_small_kernel(taps_ref, ss_ref, fre_ref, fim_ref, uv_ref, x0_ref, bias_ref, o_ref):
    two_l = taps_ref.shape[0]
    L = two_l // 2
    fre, fim = fre_ref[...], fim_ref[...]
    scale = lax.rsqrt(ss_ref[...]) * (1.0 / two_l)
    taps = taps_ref[...]
    k_re = _dot(fre, taps) * scale
    k_im = _dot(fim, taps) * scale
    u0, u1 = uv_ref[0], uv_ref[1]
    fre_l, fim_l = fre[:, :L], fim[:, :L]
    x_re = _dot(fre_l, u0) - _dot(fim_l, u1)
    x_im = _dot(fre_l, u1) + _dot(fim_l, u0)
    y_re = (x_re * k_re - x_im * k_im).astype(BF16)
    y_im = (x_re * k_im + x_im * k_re).astype(BF16)
    fre_t, fim_t = fre[:L, :], fim[:L, :]
    y0 = _dot(fre_t, y_re) + _dot(fim_t, y_im)
    y1 = _dot(fre_t, y_im) - _dot(fim_t, y_re)
    bias = bias_ref[...]
    o_ref[0] = ((y0 + u0.astype(F32) * bias) * x0_ref[0].astype(F32)).astype(BF16)
    o_ref[1] = ((y1 + u1.astype(F32) * bias) * x0_ref[1].astype(F32)).astype(BF16)


def hyena_long_conv_small(uv, x0, taps, sumsq, bias, tables):
    bsz, L, c = uv.shape
    assert bsz == 2
    f_re, f_im = tables
    full = lambda shape: pl.BlockSpec(shape, lambda i: (0,) * len(shape))
    return pl.pallas_call(
        _conv_small_kernel,
        grid=(1,),
        in_specs=[full((2 * L, c)), full((1, c)), full((2 * L, 2 * L)), full((2 * L, 2 * L)),
                  full((2, L, c)), full((2, L, c)), full((1, c))],
        out_specs=full((2, L, c)),
        out_shape=jax.ShapeDtypeStruct((2, L, c), BF16),
        compiler_params=_params(("arbitrary",)),
        name="hyena_conv_small",
    )(taps, sumsq, f_re, f_im, uv, x0, bias.reshape(1, c))


def _merge_kernel(ya_ref, yb_ref, yc_ref, ga_ref, gb_ref, gc_ref, mg_ref, x_ref, mod_ref, np_ref,
                  wb_ref, wo_ref, o_ref):
    acc = None
    for i, (y_ref, g_ref) in enumerate(((ya_ref, ga_ref), (yb_ref, gb_ref), (yc_ref, gc_ref))):
        g = g_ref[0]
        gated = y_ref[0] * (g * jax.nn.sigmoid(g))
        sel = jax.nn.sigmoid(mg_ref[0, :, i * D_MODEL:(i + 1) * D_MODEL])
        term = sel * _dot(gated, wb_ref[i]).astype(BF16)
        acc = term if acc is None else acc + term
    out = _dot(acc, wo_ref[...])
    r = out * lax.rsqrt(jnp.mean(out * out, axis=-1, keepdims=True) + EPS) * np_ref[...]
    gt = mod_ref[0, :, 2 * D_MODEL:]
    o_ref[0] = x_ref[0] + gt * r


def merge_out(ya, yb, yc, zat, zmg, x, mod_rows, row_of_batch, npost, wb, wo):
    bsz, n, _ = x.shape
    tm = min(n, 512)
    w = BR_WIDTH
    yspec = pl.BlockSpec((1, tm, w), lambda b, i: (b, i, 0))

    def zspec(name):
        cb = ATC[name] // w
        return pl.BlockSpec((1, tm, w), lambda b, i: (b, i, cb))

    return pl.pallas_call(
        _merge_kernel,
        grid=(bsz, n // tm),
        in_specs=[yspec, yspec, yspec, zspec("GA"), zspec("GB"), zspec("GC"),
                  pl.BlockSpec((1, tm, 3 * D_MODEL), lambda b, i: (b, i, 0)),
                  pl.BlockSpec((1, tm, D_MODEL), lambda b, i: (b, i, 0)),
                  pl.BlockSpec((1, 1, 3 * D_MODEL), lambda b, i: (row_of_batch(b), 0, 0)),
                  pl.BlockSpec((1, D_MODEL), lambda b, i: (0, 0)),
                  pl.BlockSpec((3, w, D_MODEL), lambda b, i: (0, 0, 0)),
                  pl.BlockSpec((D_MODEL, D_MODEL), lambda b, i: (0, 0))],
        out_specs=pl.BlockSpec((1, tm, D_MODEL), lambda b, i: (b, i, 0)),
        out_shape=jax.ShapeDtypeStruct((bsz, n, D_MODEL), F32),
        compiler_params=_params(("arbitrary", "arbitrary")),
        name="merge_out",
    )(ya, yb, yc, zat, zat, zat, zmg, x, mod_rows, npost.reshape(1, D_MODEL), wb, wo)


def kernel(x, c, ctx, c_ctx, ada_w, ada_b, norm_pre, norm_post, w_in, da_lambda, da_subln, hy_short_w,
           hy_short_b, hy_f_w1, hy_f_b1, hy_f_w2, hy_f_b2, hy_f_w3, hy_f_freq, hy_bias, gm_ln_g, gm_ln_b,
           gm_ws, gm_bs, w_branch, w_out):
    bsz, n, _ = x.shape
    n_ctx = ctx.shape[1]
    assert bsz == 2 and 2 * n == DFT_R * DFT_R

    cond8 = jnp.zeros((8, D_MODEL), F32).at[0:bsz].set(c).at[bsz].set(c_ctx)
    mod = modulation_all(cond8, ada_w, ada_b)
    lat_row = lambda b: b
    ctx_row = lambda b: bsz

    cos_t, sin_t = rope_tables(n)
    feat = hyena_positions(n)
    feat_c = hyena_positions(n_ctx)
    tables = dft_tables()
    tables_c = dft_small_tables(2 * n_ctx)
    vb, qb = ATC["V"] // LANES, ATC["Q"] // LANES

    xc = ctx
    for l in range(DEPTH):
        last = l == DEPTH - 1
        lam_init = 0.8 - 0.6 * math.exp(-0.3 * l)
        mod_rows = mod[l].reshape(8, 1, 3 * D_MODEL)
        w_l = w_in[l].astype(BF16)
        wb_l = w_branch[l].astype(BF16)
        wo_l = w_out[l].astype(BF16)
        filt_w = (hy_f_w1[l], hy_f_b1[l], hy_f_w2[l], hy_f_b2[l], hy_f_w3[l], hy_f_freq[l])

        gm_l = gmlp_params(gm_ln_g[l], gm_ln_b[l], gm_ws[l], gm_bs[l])
        z, qr, kr = in_projection(x, mod_rows, lat_row, norm_pre[l], w_l, gm_l, (cos_t, sin_t))
        zc = in_projection(xc, mod_rows, ctx_row, norm_pre[l], w_l, gm_l)

        taps, sumsq = hyena_filter_taps(feat, *filt_w)
        x0, uv = hyena_short_conv(z["HY"], hy_short_w[l], hy_short_b[l])
        y_a = diff_attention(qr, 0, zc["AT"], (kr, 0, z["AT"], vb), da_lambda[l], da_subln[l], lam_init, True)
        y_b = hyena_long_conv(uv, x0, taps, sumsq, hy_bias[l], tables)
        y_c = z["GM"]
        x_new = merge_out(y_a, y_b, y_c, z["AT"], z["MG"], x, mod_rows, lat_row, norm_post[l], wb_l, wo_l)

        if not last:
            yc_a = diff_attention(zc["AT"], qb, zc["AT"], None, da_lambda[l], da_subln[l], lam_init, False)
            taps_c, sumsq_c = hyena_filter_taps(feat_c, *filt_w)
            x0c, uvc = hyena_short_conv(zc["HY"], hy_short_w[l], hy_short_b[l])
            yc_b = hyena_long_conv_small(uvc, x0c, taps_c, sumsq_c, hy_bias[l], tables_c)
            yc_c = zc["GM"]
            xc = merge_out(yc_a, yc_b, yc_c, zc["AT"], zc["MG"], xc, mod_rows, ctx_row, norm_post[l], wb_l, wo_l)
        x = x_new
    return x
```

```python
import functools
import math

import jax
import jax.numpy as jnp
from jax import lax
from jax.experimental import pallas as pl
from jax.experimental.pallas import tpu as pltpu

F32 = jnp.float32
BF16 = jnp.bfloat16

D_MODEL = 1024
DEPTH = 4
GRID_W = 64
EPS = 1e-6
BR_WIDTH = 512
DA_SUB = 64
DA_VDIM = 128
DA_HEADS = 4
ROPE_BASE = 10000.0
ROPE_NF = 16
HY_WIDTH = 512
HY_BANDS = 16
HY_EMB = 33
HY_HIDDEN = 64
HY_MIN_DECAY = math.log(1e-2) / 1.5
HY_MAX_DECAY = math.log(1e-2) / 0.3
GM_GROUPS = 8
GM_CHUNK = 128

LANES = 128
DFT_R = 128
QK_SCALE = DA_SUB ** -0.5 * math.log2(math.e)

_REF_COLS = dict(K=(0, 512), V=(512, 512), Q=(1024, 512), GA=(1536, 512), HY=(2048, 1536),
                 GB=(3584, 512), GM=(4096, 1024), GC=(5120, 512), MG=(5632, 3072))
_GROUPS = (("MG", ("MG",)), ("HY", ("HY",)), ("GM", ("GM",)), ("AT", ("K", "V", "Q", "GA", "GB", "GC")))
GROUP_WIDTH = {g: sum(_REF_COLS[nm][1] for nm in names) for g, names in _GROUPS}
ATC = {}
_off = 0
for _name in _GROUPS[-1][1]:
    ATC[_name] = _off
    _off += _REF_COLS[_name][1]

VMEM_LIMIT = 48 * 1024 * 1024
INPROJ_VMEM_LIMIT = 56 * 1024 * 1024


def _params(sem):
    return pltpu.CompilerParams(dimension_semantics=sem, vmem_limit_bytes=VMEM_LIMIT)


def _dot(a, b):
    return jnp.dot(a, b, preferred_element_type=F32)


def _dot_hi(a, b):
    return jnp.dot(a, b, preferred_element_type=F32, precision=lax.Precision.HIGHEST)


def _mod_kernel(c_ref, w_ref, b_ref, o_ref):
    cond = c_ref[...]
    s = cond * jax.nn.sigmoid(cond)
    o_ref[0] = _dot_hi(s, w_ref[0]) + b_ref[0]


def modulation_all(cond8, ada_w, ada_b):
    tn = 1024
    return pl.pallas_call(
        _mod_kernel,
        grid=(DEPTH, 3 * D_MODEL // tn),
        in_specs=[pl.BlockSpec((8, D_MODEL), lambda l, j: (0, 0)),
                  pl.BlockSpec((1, D_MODEL, tn), lambda l, j: (l, 0, j)),
                  pl.BlockSpec((1, 1, tn), lambda l, j: (l, 0, j))],
        out_specs=pl.BlockSpec((1, 8, tn), lambda l, j: (l, 0, j)),
        out_shape=jax.ShapeDtypeStruct((DEPTH, 8, 3 * D_MODEL), F32),
        compiler_params=_params(("arbitrary", "arbitrary")),
        name="modulation",
    )(cond8, ada_w, ada_b.reshape(DEPTH, 1, 3 * D_MODEL))


def _gmlp_math(zg, g_ref, b_ref, ws_ref, bs_ref):
    gl = 0.5 * zg * (1.0 + lax.erf(zg * (2.0 ** -0.5)))
    w = BR_WIDTH
    u = gl[:, :w]
    v = gl[:, w:]
    mu = jnp.mean(v, axis=-1, keepdims=True)
    var = jnp.mean(jnp.square(v - mu), axis=-1, keepdims=True)
    v = ((v - mu) * lax.rsqrt(var + EPS) * g_ref[...] + b_ref[...]).astype(BF16)
    gw = w // GM_GROUPS
    lane = lax.broadcasted_iota(jnp.int32, (GM_CHUNK, LANES), 1)
    first = lane < gw
    out = []
    for ci in range(zg.shape[0] // GM_CHUNK):
        rows = slice(ci * GM_CHUNK, (ci + 1) * GM_CHUNK)
        tiles = []
        for t in range(w // LANES):
            vt = v[rows, t * LANES:(t + 1) * LANES]
            tiles.append(jnp.where(first, _dot(ws_ref[2 * t], vt), _dot(ws_ref[2 * t + 1], vt)))
        vm = jnp.concatenate(tiles, axis=1) + bs_ref[...]
        out.append((u[rows] * vm).astype(BF16))
    return jnp.concatenate(out, axis=0)


def gmlp_params(ln_g, ln_b, ws, bs):
    w = BR_WIDTH
    bs_full = jnp.repeat(bs.T, w // GM_GROUPS, axis=1)
    return ln_g.reshape(1, w), ln_b.reshape(1, w), ws.astype(BF16), bs_full


_QK_WIDTH = DA_HEADS * LANES
MXU_TILE = 256
MAX_CHUNK = 2048
SUBLANES = 8
_INPROJ_CHUNKS = []
for _g, _names in _GROUPS:
    _dst = 0
    for _name in _names:
        _start, _width = _REF_COLS[_name]
        for _o in range(0, _width, MAX_CHUNK):
            _w = min(MAX_CHUNK, _width - _o)
            assert (_start + _o) % MXU_TILE == 0 and _w % MXU_TILE == 0
            _INPROJ_CHUNKS.append((_start + _o, _g, _dst + _o, _w))
        _dst += _width
W_IN_COLS = sum(w for _, w in _REF_COLS.values())
_PROJ_OUT = (("MG", GROUP_WIDTH["MG"]), ("AT", GROUP_WIDTH["AT"]), ("GM", BR_WIDTH), ("X0", HY_WIDTH),
             ("UV", HY_WIDTH))


def _inproj_kernel(*refs, rope):
    x_ref, xp_ref, xn_ref, mod_ref, g_ref, w_ref = refs[0:6]
    gm_refs = refs[6:10]
    sw_ref, sb_ref = refs[10:12]
    pos = 12
    if rope:
        cos_ref, sin_ref = refs[12:14]
        pos = 14
    o_of = {name: ref for (name, _), ref in zip(_PROJ_OUT, refs[pos:])}
    if rope:
        q_ref, k_ref = refs[pos + len(_PROJ_OUT):]
    tm = x_ref.shape[1]
    sh = mod_ref[0, :, 0:D_MODEL]
    sc = mod_ref[0, :, D_MODEL:2 * D_MODEL]

    def norm_mod(xv):
        y = xv * lax.rsqrt(jnp.mean(xv * xv, axis=-1, keepdims=True) + EPS) * g_ref[...]
        return (y * (1.0 + sc) + sh).astype(BF16)

    h = norm_mod(x_ref[0])
    h_halo = norm_mod(jnp.concatenate([xp_ref[0], xn_ref[0]], axis=0))

    if rope:
        lane = lax.broadcasted_iota(jnp.int32, cos_ref.shape, 1)
        low = (lane % (2 * ROPE_NF)) < ROPE_NF
        cs = cos_ref[...]
        sn = sin_ref[...]

        def rot(v):
            partner = jnp.where(low, pltpu.roll(v, LANES - ROPE_NF, axis=1), pltpu.roll(v, ROPE_NF, axis=1))
            return (v * cs + partner * sn).astype(BF16)

    for src, gname, dst, width in _INPROJ_CHUNKS:
        if gname == "HY":
            zc = _dot(jnp.concatenate([h, h_halo], axis=0), w_ref[:, src:src + width])
            i = pl.program_id(1)
            z = zc[0:tm]
            before = jnp.where(i == 0, 0.0, zc[tm + SUBLANES - 1:tm + SUBLANES])
            after = jnp.where(i == pl.num_programs(1) - 1, 0.0, zc[tm + SUBLANES:tm + SUBLANES + 1])
            row = lax.broadcasted_iota(jnp.int32, z.shape, 0)
            zm = jnp.where(row == 0, before, pltpu.roll(z, 1, axis=0))
            zp = jnp.where(row == tm - 1, after, pltpu.roll(z, tm - 1, axis=0))
            y = zm * sw_ref[0:1, :] + z * sw_ref[1:2, :] + zp * sw_ref[2:3, :] + sb_ref[...]
            o_of["X0"][0] = y[:, 0:HY_WIDTH].astype(BF16)
            o_of["UV"][0] = (y[:, HY_WIDTH:2 * HY_WIDTH] * y[:, 2 * HY_WIDTH:]).astype(BF16)
            continue
        zc = _dot(h, w_ref[:, src:src + width])
        if gname == "GM":
            o_of["GM"][0] = _gmlp_math(zc, *gm_refs)
            continue
        o_of[gname][0, :, dst:dst + width] = zc.astype(BF16)
        if rope and gname == "AT" and dst in (ATC["K"], ATC["Q"]):
            out, scale = (k_ref, 1.0) if dst == ATC["K"] else (q_ref, QK_SCALE)
            for hd in range(DA_HEADS):
                cols = slice(hd * LANES, (hd + 1) * LANES)
                out[0, :, cols] = rot(zc[:, cols] * scale)


def in_projection(x, mod_rows, row_of_batch, g, w, gm, sconv, rope_tabs=None):
    bsz, n, _ = x.shape
    tm = min(n, 512)
    rope = rope_tabs is not None
    hb = tm // SUBLANES
    last_hb = n // SUBLANES - 1
    w_half = BR_WIDTH
    w3c = 3 * HY_WIDTH
    const2 = lambda b, i: (0, 0)
    in_specs = [pl.BlockSpec((1, tm, D_MODEL), lambda b, i: (b, i, 0)),
                pl.BlockSpec((1, SUBLANES, D_MODEL), lambda b, i: (b, jnp.maximum(i * hb - 1, 0), 0)),
                pl.BlockSpec((1, SUBLANES, D_MODEL), lambda b, i: (b, jnp.minimum((i + 1) * hb, last_hb), 0)),
                pl.BlockSpec((1, 1, 3 * D_MODEL), lambda b, i: (row_of_batch(b), 0, 0)),
                pl.BlockSpec((1, D_MODEL), const2),
                pl.BlockSpec((D_MODEL, W_IN_COLS), const2, pipeline_mode=pl.Buffered(1)),
                pl.BlockSpec((1, w_half), const2),
                pl.BlockSpec((1, w_half), const2),
                pl.BlockSpec((GM_GROUPS, GM_CHUNK, GM_CHUNK), lambda b, i: (0, 0, 0)),
                pl.BlockSpec((GM_CHUNK, w_half), const2),
                pl.BlockSpec((3, w3c), const2),
                pl.BlockSpec((1, w3c), const2)]
    args = [x, x, x, mod_rows, g.reshape(1, D_MODEL), w] + list(gm) + list(sconv)
    if rope:
        in_specs += [pl.BlockSpec((tm, LANES), lambda b, i: (i, 0))] * 2
        args += list(rope_tabs)
    out_specs = [pl.BlockSpec((1, tm, width), lambda b, i: (b, i, 0)) for _, width in _PROJ_OUT]
    out_shape = [jax.ShapeDtypeStruct((bsz, n, width), BF16) for _, width in _PROJ_OUT]
    if rope:
        out_specs += [pl.BlockSpec((1, tm, _QK_WIDTH), lambda b, i: (b, i, 0))] * 2
        out_shape += [jax.ShapeDtypeStruct((bsz, n, _QK_WIDTH), BF16)] * 2
    res = pl.pallas_call(
        functools.partial(_inproj_kernel, rope=rope),
        grid=(bsz, n // tm),
        in_specs=in_specs,
        out_specs=out_specs,
        out_shape=out_shape,
        compiler_params=pltpu.CompilerParams(dimension_semantics=("arbitrary", "arbitrary"),
                                             vmem_limit_bytes=INPROJ_VMEM_LIMIT),
        name="in_projection",
    )(*args)
    z = {name: r for (name, _), r in zip(_PROJ_OUT, res)}
    return (z, res[-2], res[-1]) if rope else z


def rope_tables(n):
    pos = jnp.arange(n)
    row = (pos // GRID_W).astype(F32)
    col = (pos % GRID_W).astype(F32)
    inv = ROPE_BASE ** (-jnp.arange(ROPE_NF, dtype=F32) / ROPE_NF)
    ar = row[:, None] * inv
    ac = col[:, None] * inv
    cos64 = jnp.concatenate([jnp.cos(ar), jnp.cos(ar), jnp.cos(ac), jnp.cos(ac)], axis=1)
    sin64 = jnp.concatenate([-jnp.sin(ar), jnp.sin(ar), -jnp.sin(ac), jnp.sin(ac)], axis=1)
    return jnp.tile(cos64, (1, 2)), jnp.tile(sin64, (1, 2))


def _attn_kernel(*refs, lam_init, n_lat, ck, prescaled):
    if n_lat:
        lam_ref, g_ref, q_ref, kc_ref, vc_ref, k_ref, v_ref, o_ref, vt_ref = refs
    else:
        lam_ref, g_ref, q_ref, kc_ref, vc_ref, o_ref, vt_ref = refs
    tq = q_ref.shape[1]
    n_ctx = kc_ref.shape[1]

    @pl.when(pl.program_id(2) == 0)
    def _():
        vt_ref[0:DA_VDIM, 0:n_ctx] = vc_ref[0].astype(F32).T.astype(BF16)
        for c in range(n_lat // ck):
            vt_ref[0:DA_VDIM, n_ctx + c * ck:n_ctx + (c + 1) * ck] = (
                v_ref[0, c * ck:(c + 1) * ck, :].astype(F32).T.astype(BF16))
        pad = vt_ref.shape[0] - DA_VDIM
        row = lax.broadcasted_iota(jnp.int32, (pad, vt_ref.shape[1]), 0)
        vt_ref[DA_VDIM:, :] = (row == 0).astype(BF16)

    lp = lam_ref[...]
    lam = (jnp.exp(jnp.sum(lp[0:1] * lp[1:2], axis=1, keepdims=True))
           - jnp.exp(jnp.sum(lp[2:3] * lp[3:4], axis=1, keepdims=True)) + lam_init)

    q = q_ref[0]
    if not prescaled:
        q = (q.astype(F32) * QK_SCALE).astype(BF16)
    lane = lax.broadcasted_iota(jnp.int32, q.shape, 1)
    zero = jnp.zeros_like(q)
    qq = jnp.concatenate([jnp.where(lane < DA_SUB, q, zero), jnp.where(lane >= DA_SUB, q, zero)], axis=0)

    chunks = [(lambda: kc_ref[0], 0, n_ctx)]
    for c in range(n_lat // ck):
        chunks.append((lambda c=c: k_ref[0, c * ck:(c + 1) * ck, :], n_ctx + c * ck, ck))

    def scores_t(c):
        return lax.dot_general(chunks[c][0](), qq, (((1,), (1,)), ((), ())), preferred_element_type=F32)

    def pv_t(p_t, c):
        _, off, width = chunks[c]
        return _dot(vt_ref[:, off:off + width], p_t)

    m = jnp.full((1, 2 * tq), -jnp.inf, F32)
    acc = jnp.zeros((vt_ref.shape[0], 2 * tq), F32)
    pending = None
    s_next = scores_t(0)
    for c in range(len(chunks)):
        s = s_next
        if c + 1 < len(chunks):
            s_next = scores_t(c + 1)
        part = s[0:64]
        for r in range(64, s.shape[0], 64):
            part = jnp.maximum(part, s[r:r + 64])
        m_new = jnp.maximum(m, jnp.max(part, axis=0, keepdims=True))
        alpha = jnp.exp2(m - m_new)
        p_t = jnp.exp2(s - m_new).astype(BF16)
        m = m_new
        if pending is not None:
            p_prev, alpha_prev, c_prev = pending
            acc = acc * alpha_prev + pv_t(p_prev, c_prev)
        pending = (p_t, alpha, c)
    p_prev, alpha_prev, c_prev = pending
    acc = acc * alpha_prev + pv_t(p_prev, c_prev)

    o_t = acc[0:DA_VDIM, :] / acc[DA_VDIM:DA_VDIM + 1, :]
    d = (o_t[:, 0:tq] - lam * o_t[:, tq:]).T
    y = d * lax.rsqrt(jnp.mean(d * d, axis=-1, keepdims=True) + EPS) * g_ref[...]
    o_ref[0] = (y * (1.0 - lam_init)).astype(BF16)


def diff_attention(q_arr, q_col, zc, lat, lam_p, subln, lam_init, prescaled):
    bsz, nq, _ = q_arr.shape
    n_ctx = zc.shape[1]
    tq = min(nq, 512)
    kcb, vcb = ATC["K"] // LANES, ATC["V"] // LANES
    in_specs = [pl.BlockSpec((4, DA_SUB), lambda b, h, i: (0, 0)),
                pl.BlockSpec((1, DA_VDIM), lambda b, h, i: (0, 0)),
                pl.BlockSpec((1, tq, LANES), lambda b, h, i: (b, i, q_col + h)),
                pl.BlockSpec((1, n_ctx, LANES), lambda b, h, i: (b, 0, kcb + h)),
                pl.BlockSpec((1, n_ctx, LANES), lambda b, h, i: (b, 0, vcb + h))]
    args = [lam_p, subln.reshape(1, DA_VDIM), q_arr, zc, zc]
    n_lat = 0
    if lat is not None:
        k_arr, k_col, v_arr, v_col = lat
        n_lat = k_arr.shape[1]
        in_specs += [pl.BlockSpec((1, n_lat, LANES), lambda b, h, i: (b, 0, k_col + h)),
                     pl.BlockSpec((1, n_lat, LANES), lambda b, h, i: (b, 0, v_col + h))]
        args += [k_arr, v_arr]
    ones_rows = 16
    scratch = [pltpu.VMEM((DA_VDIM + ones_rows, n_ctx + n_lat), BF16)]
    return pl.pallas_call(
        functools.partial(_attn_kernel, lam_init=lam_init, n_lat=n_lat, ck=512, prescaled=prescaled),
        grid=(bsz, DA_HEADS, nq // tq),
        in_specs=in_specs,
        out_specs=pl.BlockSpec((1, tq, LANES), lambda b, h, i: (b, i, h)),
        out_shape=jax.ShapeDtypeStruct((bsz, nq, BR_WIDTH), BF16),
        scratch_shapes=scratch,
        compiler_params=_params(("arbitrary", "arbitrary", "arbitrary")),
        name="diff_attention",
    )(*args)


def hyena_positions(L):
    t = jnp.linspace(0.0, 1.0, L, dtype=F32)[:, None]
    wpos = ((2.0 * math.pi / L) * jnp.arange(L, dtype=F32))[:, None]
    bands = jnp.linspace(1e-4, HY_BANDS - 1, HY_BANDS, dtype=F32)[None, :]
    fwd = jnp.concatenate([t, jnp.cos(bands * wpos), -jnp.sin(bands * wpos)], axis=-1)
    emb = jnp.concatenate([fwd, fwd[0:1], jnp.flip(fwd[1:], axis=0)], axis=0)
    mask = (jnp.arange(2 * L) != L).astype(F32)[:, None]
    pad = jnp.zeros((2 * L, LANES - HY_EMB - 1), F32)
    return jnp.concatenate([emb, pad, mask], axis=-1)


def _filter_kernel(feat_ref, w1_ref, b1_ref, w2_ref, b2_ref, w3_ref, fr_ref, dl_ref, k_ref, ss_ref):
    half = feat_ref.shape[0] // 2
    feat = feat_ref[...]
    f2 = jnp.concatenate([feat[0:half], feat[half:]], axis=1)
    hid = jnp.sin(fr_ref[0:1, :] * (_dot_hi(f2, w1_ref[...]) + b1_ref[...]))
    hid = jnp.sin(fr_ref[1:2, :] * (_dot_hi(hid, w2_ref[...]) + b2_ref[...]))
    w3 = w3_ref[...].astype(BF16)
    zero = jnp.zeros_like(w3)
    hid = hid.astype(BF16)
    h = jnp.concatenate([_dot(hid, jnp.concatenate([w3, zero], axis=0)),
                         _dot(hid, jnp.concatenate([zero, w3], axis=0))], axis=0)
    t = feat[:, 0:1]
    mask = feat[:, LANES - 1:LANES]
    k = h * jnp.exp(-t * dl_ref[...]) * mask
    k_ref[...] = k.astype(k_ref.dtype)

    @pl.when(pl.program_id(0) == 0)
    def _():
        ss_ref[...] = jnp.zeros_like(ss_ref)

    ss_ref[...] += jnp.sum(k * k, axis=0, keepdims=True)


def hyena_filter_taps(feat, w1, b1, w2, b2, w3, freq):
    two_l = feat.shape[0]
    tr = min(two_l // 2, 1024)
    nb = two_l // tr
    hd = HY_HIDDEN
    w1d = jnp.zeros((2 * LANES, 2 * hd), F32).at[:HY_EMB, :hd].set(w1).at[LANES:LANES + HY_EMB, hd:].set(w1)
    w2d = jnp.zeros((2 * hd, 2 * hd), F32).at[:hd, :hd].set(w2).at[hd:, hd:].set(w2)
    twice = lambda v: jnp.tile(v.reshape(-1, hd), (1, 2))
    deltas = jnp.abs(jnp.linspace(HY_MIN_DECAY, HY_MAX_DECAY, HY_WIDTH, dtype=F32))[None, :]
    const = lambda i: (0, 0)
    return pl.pallas_call(
        _filter_kernel,
        grid=(nb,),
        in_specs=[pl.BlockSpec((tr, LANES), lambda i: (i, 0)),
                  pl.BlockSpec((2 * LANES, 2 * hd), const),
                  pl.BlockSpec((1, 2 * hd), const),
                  pl.BlockSpec((2 * hd, 2 * hd), const),
                  pl.BlockSpec((1, 2 * hd), const),
                  pl.BlockSpec((hd, HY_WIDTH), lambda i: (0, (2 * i) // nb)),
                  pl.BlockSpec((2, 2 * hd), const),
                  pl.BlockSpec((1, HY_WIDTH), const)],
        out_specs=[pl.BlockSpec((tr, HY_WIDTH), lambda i: (i, 0)),
                   pl.BlockSpec((1, HY_WIDTH), const)],
        out_shape=[jax.ShapeDtypeStruct((two_l, HY_WIDTH), BF16),
                   jax.ShapeDtypeStruct((1, HY_WIDTH), F32)],
        compiler_params=_params(("arbitrary",)),
        name="hyena_filter",
    )(feat, w1d, twice(b1), w2d, twice(b2), w3, twice(freq), deltas)


def dft_tables():
    r = DFT_R
    n_fft = r * r
    idx = jnp.arange(r, dtype=jnp.int32)
    prod = idx[:, None] * idx[None, :]
    ang = (2.0 * math.pi / r) * (prod % r).astype(F32)
    f_re, f_im = jnp.cos(ang), -jnp.sin(ang)
    ang = (2.0 * math.pi / n_fft) * prod.astype(F32)
    t_re, t_im = jnp.cos(ang), -jnp.sin(ang)

    def cmul(a_re, a_im, b_re, b_im):
        return a_re * b_re - a_im * b_im, a_re * b_im + a_im * b_re

    g_re, g_im = cmul(f_re[None, :, :], f_im[None, :, :], t_re[:, None, :], t_im[:, None, :])
    def cplx(re, im):
        return jnp.concatenate([jnp.concatenate([re, -im], axis=-1),
                                jnp.concatenate([im, re], axis=-1)], axis=-2).astype(BF16)

    half = r // 2
    return dict(f_pad=cplx(f_re[:, :half], f_im[:, :half]),
                f_real=jnp.concatenate([f_re, f_im], axis=0).astype(BF16),
                g=cplx(g_re, g_im),
                f_inv=cplx(f_re[:half], -f_im[:half]))


def _fft1_kernel(x_ref, f_ref, are_ref, aim_ref):
    r = are_ref.shape[0]
    x = x_ref[...]
    p = _dot(f_ref[...], x.reshape(-1, x.shape[-1]))
    are_ref[...] = p[0:r].astype(BF16)
    aim_ref[...] = p[r:].astype(BF16)


def fft_level1(x, f):
    r = DFT_R
    cols = x.shape[-1]
    tc = 8192
    if x.ndim == 3:
        x_spec = pl.BlockSpec((2, x.shape[1], tc), lambda j: (0, 0, j))
    else:
        x_spec = pl.BlockSpec((x.shape[0], tc), lambda j: (0, j))
    out = jax.ShapeDtypeStruct((r, cols), BF16)
    return pl.pallas_call(
        _fft1_kernel,
        grid=(cols // tc,),
        in_specs=[x_spec, pl.BlockSpec(f.shape, lambda j: (0, 0))],
        out_specs=[pl.BlockSpec((r, tc), lambda j: (0, j)), pl.BlockSpec((r, tc), lambda j: (0, j))],
        out_shape=[out, out],
        compiler_params=_params(("arbitrary",)),
        name="fft_level1",
    )(x, f)


def _fft2_conv_kernel(are_ref, aim_ref, kare_ref, kaim_ref, g_ref, sc_ref, bre_ref, bim_ref):
    r = are_ref.shape[1]
    c = are_ref.shape[-1]
    sc = sc_ref[...]
    for j in range(are_ref.shape[0]):
        rhs = jnp.concatenate([jnp.concatenate([are_ref[j], aim_ref[j]], axis=0),
                               jnp.concatenate([kare_ref[j], kaim_ref[j]], axis=0)], axis=1)
        xk = _dot(g_ref[j], rhs)
        x_re, x_im = xk[0:r, 0:c], xk[r:, 0:c]
        k_re, k_im = xk[0:r, c:] * sc, xk[r:, c:] * sc
        y = jnp.concatenate([(x_re * k_re - x_im * k_im).astype(BF16),
                             (x_re * k_im + x_im * k_re).astype(BF16)], axis=0)
        bm = lax.dot_general(g_ref[j], y, (((0,), (0,)), ((), ())), preferred_element_type=F32)
        bre_ref[j] = bm[0:r].astype(BF16)
        bim_ref[j] = bm[r:].astype(BF16)


def fft_level2_conv(a_re, a_im, ka_re, ka_im, g, scale):
    r = DFT_R
    c = a_re.shape[-1]
    tk = 8
    blk_a = pl.BlockSpec((tk, r, c), lambda i: (i, 0, 0))
    blk_g = pl.BlockSpec((tk, 2 * r, 2 * r), lambda i: (i, 0, 0))
    out = jax.ShapeDtypeStruct((r, r, c), BF16)
    return pl.pallas_call(
        _fft2_conv_kernel,
        grid=(r // tk,),
        in_specs=[blk_a, blk_a, blk_a, blk_a, blk_g, pl.BlockSpec((1, c), lambda i: (0, 0))],
        out_specs=[blk_a, blk_a],
        out_shape=[out, out],
        compiler_params=_params(("arbitrary",)),
        name="fft_level2_conv",
    )(a_re, a_im, ka_re, ka_im, g, scale)


def _ifft1_kernel(bre_ref, bim_ref, f_ref, uv_ref, x0_ref, bias_ref, o_ref):
    half = f_ref.shape[0] // 2
    y = _dot(f_ref[...], jnp.concatenate([bre_ref[...], bim_ref[...]], axis=0))
    bias = bias_ref[...]
    o_ref[0] = ((y[0:half] + uv_ref[0].astype(F32) * bias) * x0_ref[0].astype(F32)).astype(BF16)
    o_ref[1] = ((y[half:] + uv_ref[1].astype(F32) * bias) * x0_ref[1].astype(F32)).astype(BF16)


def ifft_level1(b_re, b_im, f_inv, uv, x0, bias_t):
    r = DFT_R
    half = r // 2
    cols = b_re.shape[-1]
    tc = 8192
    blk_b = pl.BlockSpec((r, tc), lambda j: (0, j))
    blk_x = pl.BlockSpec((2, half, tc), lambda j: (0, 0, j))
    return pl.pallas_call(
        _ifft1_kernel,
        grid=(cols // tc,),
        in_specs=[blk_b, blk_b, pl.BlockSpec(f_inv.shape, lambda j: (0, 0)), blk_x, blk_x,
                  pl.BlockSpec((1, tc), lambda j: (0, 0))],
        out_specs=blk_x,
        out_shape=jax.ShapeDtypeStruct((2, half, cols), BF16),
        compiler_params=_params(("arbitrary",)),
        name="ifft_level1",
    )(b_re, b_im, f_inv, uv, x0, bias_t)


def hyena_long_conv(uv, x0, taps, sumsq, bias, tables):
    r = DFT_R
    bsz, L, c = uv.shape
    assert bsz == 2 and 2 * L == r * r
    scale = lax.rsqrt(sumsq) * (1.0 / (r * r))
    ka_re, ka_im = fft_level1(taps.reshape(r, r * c), tables["f_real"])
    uv2 = uv.reshape(2, r // 2, r * c)
    a_re, a_im = fft_level1(uv2, tables["f_pad"])
    b_re, b_im = fft_level2_conv(a_re.reshape(r, r, c), a_im.reshape(r, r, c), ka_re.reshape(r, r, c),
                                 ka_im.reshape(r, r, c), tables["g"], scale)
    bias_t = jnp.tile(bias.reshape(1, c), (1, 8192 // c))
    y = ifft_level1(b_re.reshape(r, r * c), b_im.reshape(r, r * c), tables["f_inv"], uv2,
                    x0.reshape(2, r // 2, r * c), bias_t)
    return y.reshape(2, L, c)


def dft_small_tables(n_fft):
    idx = jnp.arange(n_fft, dtype=jnp.int32)
    ang = (2.0 * math.pi / n_fft) * ((idx[:, None] * idx[None, :]) % n_fft).astype(F32)
    return jnp.cos(ang).astype(BF16), (-jnp.sin(ang)).astype(BF16)


def _conv_small_kernel(taps_ref, ss_ref, fre_ref, fim_ref, uv_ref, x0_ref, bias_ref, o_ref):
    two_l = taps_ref.shape[0]
    L = two_l // 2
    fre, fim = fre_ref[...], fim_ref[...]
    scale = lax.rsqrt(ss_ref[...]) * (1.0 / two_l)
    taps = taps_ref[...]
    k_re = _dot(fre, taps) * scale
    k_im = _dot(fim, taps) * scale
    u0, u1 = uv_ref[0], uv_ref[1]
    fre_l, fim_l = fre[:, :L], fim[:, :L]
    x_re = _dot(fre_l, u0) - _dot(fim_l, u1)
    x_im = _dot(fre_l, u1) + _dot(fim_l, u0)
    y_re = (x_re * k_re - x_im * k_im).astype(BF16)
    y_im = (x_re * k_im + x_im * k_re).astype(BF16)
    fre_t, fim_t = fre[:L, :], fim[:L, :]
    y0 = _dot(fre_t, y_re) + _dot(fim_t, y_im)
    y1 = _dot(fre_t, y_im) - _dot(fim_t, y_re)
    bias = bias_ref[...]
    o_ref[0] = ((y0 + u0.astype(F32) * bias) * x0_ref[0].astype(F32)).astype(BF16)
    o_ref[1] = ((y1 + u1.astype(F32) * bias) * x0_ref[1].astype(F32)).astype(BF16)


def hyena_long_conv_small(uv, x0, taps, sumsq, bias, tables):
    bsz, L, c = uv.shape
    assert bsz == 2
    f_re, f_im = tables
    full = lambda shape: pl.BlockSpec(shape, lambda i: (0,) * len(shape))
    return pl.pallas_call(
        _conv_small_kernel,
        grid=(1,),
        in_specs=[full((2 * L, c)), full((1, c)), full((2 * L, 2 * L)), full((2 * L, 2 * L)),
                  full((2, L, c)), full((2, L, c)), full((1, c))],
        out_specs=full((2, L, c)),
        out_shape=jax.ShapeDtypeStruct((2, L, c), BF16),
        compiler_params=_params(("arbitrary",)),
        name="hyena_conv_small",
    )(taps, sumsq, f_re, f_im, uv, x0, bias.reshape(1, c))


def _merge_kernel(ya_ref, yb_ref, yc_ref, ga_ref, gb_ref, gc_ref, mg_ref, x_ref, mod_ref, np_ref,
                  wb_ref, wo_ref, o_ref):
    acc = None
    for i, (y_ref, g_ref) in enumerate(((ya_ref, ga_ref), (yb_ref, gb_ref), (yc_ref, gc_ref))):
        g = g_ref[0]
        gated = y_ref[0] * (g * jax.nn.sigmoid(g))
        sel = jax.nn.sigmoid(mg_ref[0, :, i * D_MODEL:(i + 1) * D_MODEL])
        term = sel * _dot(gated, wb_ref[i]).astype(BF16)
        acc = term if acc is None else acc + term
    out = _dot(acc, wo_ref[...])
    r = out * lax.rsqrt(jnp.mean(out * out, axis=-1, keepdims=True) + EPS) * np_ref[...]
    gt = mod_ref[0, :, 2 * D_MODEL:]
    o_ref[0] = x_ref[0] + gt * r


def merge_out(ya, yb, yc, zat, zmg, x, mod_rows, row_of_batch, npost, wb, wo):
    bsz, n, _ = x.shape
    tm = min(n, 512)
    w = BR_WIDTH
    yspec = pl.BlockSpec((1, tm, w), lambda b, i: (b, i, 0))

    def zspec(name):
        cb = ATC[name] // w
        return pl.BlockSpec((1, tm, w), lambda b, i: (b, i, cb))

    return pl.pallas_call(
        _merge_kernel,
        grid=(bsz, n // tm),
        in_specs=[yspec, yspec, yspec, zspec("GA"), zspec("GB"), zspec("GC"),
                  pl.BlockSpec((1, tm, 3 * D_MODEL), lambda b, i: (b, i, 0)),
                  pl.BlockSpec((1, tm, D_MODEL), lambda b, i: (b, i, 0)),
                  pl.BlockSpec((1, 1, 3 * D_MODEL), lambda b, i: (row_of_batch(b), 0, 0)),
                  pl.BlockSpec((1, D_MODEL), lambda b, i: (0, 0)),
                  pl.BlockSpec((3, w, D_MODEL), lambda b, i: (0, 0, 0)),
                  pl.BlockSpec((D_MODEL, D_MODEL), lambda b, i: (0, 0))],
        out_specs=pl.BlockSpec((1, tm, D_MODEL), lambda b, i: (b, i, 0)),
        out_shape=jax.ShapeDtypeStruct((bsz, n, D_MODEL), F32),
        compiler_params=_params(("arbitrary", "arbitrary")),
        name="merge_out",
    )(ya, yb, yc, zat, zat, zat, zmg, x, mod_rows, npost.reshape(1, D_MODEL), wb, wo)


def kernel(x, c, ctx, c_ctx, ada_w, ada_b, norm_pre, norm_post, w_in, da_lambda, da_subln, hy_short_w,
           hy_short_b, hy_f_w1, hy_f_b1, hy_f_w2, hy_f_b2, hy_f_w3, hy_f_freq, hy_bias, gm_ln_g, gm_ln_b,
           gm_ws, gm_bs, w_branch, w_out):
    bsz, n, _ = x.shape
    n_ctx = ctx.shape[1]
    assert bsz == 2 and 2 * n == DFT_R * DFT_R

    cond8 = jnp.zeros((8, D_MODEL), F32).at[0:bsz].set(c).at[bsz].set(c_ctx)
    mod = modulation_all(cond8, ada_w, ada_b)
    lat_row = lambda b: b
    ctx_row = lambda b: bsz

    cos_t, sin_t = rope_tables(n)
    feat = hyena_positions(n)
    feat_c = hyena_positions(n_ctx)
    tables = dft_tables()
    tables_c = dft_small_tables(2 * n_ctx)
    vb, qb = ATC["V"] // LANES, ATC["Q"] // LANES

    xc = ctx
    for l in range(DEPTH):
        last = l == DEPTH - 1
        lam_init = 0.8 - 0.6 * math.exp(-0.3 * l)
        mod_rows = mod[l].reshape(8, 1, 3 * D_MODEL)
        w_l = w_in[l].astype(BF16)
        wb_l = w_branch[l].astype(BF16)
        wo_l = w_out[l].astype(BF16)
        filt_w = (hy_f_w1[l], hy_f_b1[l], hy_f_w2[l], hy_f_b2[l], hy_f_w3[l], hy_f_freq[l])

        gm_l = gmlp_params(gm_ln_g[l], gm_ln_b[l], gm_ws[l], gm_bs[l])
        sconv_l = (hy_short_w[l], hy_short_b[l].reshape(1, -1))
        z, qr, kr = in_projection(x, mod_rows, lat_row, norm_pre[l], w_l, gm_l, sconv_l, (cos_t, sin_t))
        zc = in_projection(xc, mod_rows, ctx_row, norm_pre[l], w_l, gm_l, sconv_l)

        taps, sumsq = hyena_filter_taps(feat, *filt_w)
        y_a = diff_attention(qr, 0, zc["AT"], (kr, 0, z["AT"], vb), da_lambda[l], da_subln[l], lam_init, True)
        y_b = hyena_long_conv(z["UV"], z["X0"], taps, sumsq, hy_bias[l], tables)
        y_c = z["GM"]
        x_new = merge_out(y_a, y_b, y_c, z["AT"], z["MG"], x, mod_rows, lat_row, norm_post[l], wb_l, wo_l)

        if not last:
            yc_a = diff_attention(zc["AT"], qb, zc["AT"], None, da_lambda[l], da_subln[l], lam_init, False)
            taps_c, sumsq_c = hyena_filter_taps(feat_c, *filt_w)
            yc_b = hyena_long_conv_small(zc["UV"], zc["X0"], taps_c, sumsq_c, hy_bias[l], tables_c)
            yc_c = zc["GM"]
            xc = merge_out(yc_a, yc_b, yc_c, zc["AT"], zc["MG"], xc, mod_rows, ctx_row, norm_post[l], wb_l, wo_l)
        x = x_new
    return x
```

```python
import functools
import math

import jax
import jax.numpy as jnp
from jax import lax
from jax.experimental import pallas as pl
from jax.experimental.pallas import tpu as pltpu

F32 = jnp.float32
BF16 = jnp.bfloat16

D_MODEL = 1024
DEPTH = 4
GRID_W = 64
EPS = 1e-6
BR_WIDTH = 512
DA_SUB = 64
DA_VDIM = 128
DA_HEADS = 4
ROPE_BASE = 10000.0
ROPE_NF = 16
HY_WIDTH = 512
HY_BANDS = 16
HY_EMB = 33
HY_HIDDEN = 64
HY_MIN_DECAY = math.log(1e-2) / 1.5
HY_MAX_DECAY = math.log(1e-2) / 0.3
GM_GROUPS = 8
GM_CHUNK = 128

LANES = 128
DFT_R = 128
QK_SCALE = DA_SUB ** -0.5 * math.log2(math.e)

_REF_COLS = dict(K=(0, 512), V=(512, 512), Q=(1024, 512), GA=(1536, 512), HY=(2048, 1536),
                 GB=(3584, 512), GM=(4096, 1024), GC=(5120, 512), MG=(5632, 3072))
_GROUPS = (("MG", ("MG",)), ("HY", ("HY",)), ("GM", ("GM",)), ("AT", ("K", "V", "Q", "GA", "GB", "GC")))
GROUP_WIDTH = {g: sum(_REF_COLS[nm][1] for nm in names) for g, names in _GROUPS}
ATC = {}
_off = 0
for _name in _GROUPS[-1][1]:
    ATC[_name] = _off
    _off += _REF_COLS[_name][1]

VMEM_LIMIT = 48 * 1024 * 1024
INPROJ_VMEM_LIMIT = 56 * 1024 * 1024


def _params(sem):
    return pltpu.CompilerParams(dimension_semantics=sem, vmem_limit_bytes=VMEM_LIMIT)


def _dot(a, b):
    return jnp.dot(a, b, preferred_element_type=F32)


def _dot_hi(a, b):
    return jnp.dot(a, b, preferred_element_type=F32, precision=lax.Precision.HIGHEST)


def _mod_kernel(c_ref, w_ref, b_ref, o_ref):
    cond = c_ref[...]
    s = cond * jax.nn.sigmoid(cond)
    o_ref[0] = _dot_hi(s, w_ref[0]) + b_ref[0]


def modulation_all(cond8, ada_w, ada_b):
    tn = 1024
    return pl.pallas_call(
        _mod_kernel,
        grid=(DEPTH, 3 * D_MODEL // tn),
        in_specs=[pl.BlockSpec((8, D_MODEL), lambda l, j: (0, 0)),
                  pl.BlockSpec((1, D_MODEL, tn), lambda l, j: (l, 0, j)),
                  pl.BlockSpec((1, 1, tn), lambda l, j: (l, 0, j))],
        out_specs=pl.BlockSpec((1, 8, tn), lambda l, j: (l, 0, j)),
        out_shape=jax.ShapeDtypeStruct((DEPTH, 8, 3 * D_MODEL), F32),
        compiler_params=_params(("arbitrary", "arbitrary")),
        name="modulation",
    )(cond8, ada_w, ada_b.reshape(DEPTH, 1, 3 * D_MODEL))


def _gmlp_math(zg, g_ref, b_ref, ws_ref, bs_ref):
    gl = 0.5 * zg * (1.0 + lax.erf(zg * (2.0 ** -0.5)))
    w = BR_WIDTH
    u = gl[:, :w]
    v = gl[:, w:]
    mu = jnp.mean(v, axis=-1, keepdims=True)
    var = jnp.mean(jnp.square(v - mu), axis=-1, keepdims=True)
    v = ((v - mu) * lax.rsqrt(var + EPS) * g_ref[...] + b_ref[...]).astype(BF16)
    gw = w // GM_GROUPS
    lane = lax.broadcasted_iota(jnp.int32, (GM_CHUNK, LANES), 1)
    first = lane < gw
    out = []
    for ci in range(zg.shape[0] // GM_CHUNK):
        rows = slice(ci * GM_CHUNK, (ci + 1) * GM_CHUNK)
        tiles = []
        for t in range(w // LANES):
            vt = v[rows, t * LANES:(t + 1) * LANES]
            tiles.append(jnp.where(first, _dot(ws_ref[2 * t], vt), _dot(ws_ref[2 * t + 1], vt)))
        vm = jnp.concatenate(tiles, axis=1) + bs_ref[...]
        out.append((u[rows] * vm).astype(BF16))
    return jnp.concatenate(out, axis=0)


def gmlp_params(ln_g, ln_b, ws, bs):
    w = BR_WIDTH
    bs_full = jnp.repeat(bs.T, w // GM_GROUPS, axis=1)
    return ln_g.reshape(1, w), ln_b.reshape(1, w), ws.astype(BF16), bs_full


_QK_WIDTH = DA_HEADS * LANES
MXU_TILE = 256
MAX_CHUNK = 2048
SUBLANES = 8
_INPROJ_CHUNKS = []
for _g, _names in _GROUPS:
    _dst = 0
    for _name in _names:
        _start, _width = _REF_COLS[_name]
        for _o in range(0, _width, MAX_CHUNK):
            _w = min(MAX_CHUNK, _width - _o)
            assert (_start + _o) % MXU_TILE == 0 and _w % MXU_TILE == 0
            _INPROJ_CHUNKS.append((_start + _o, _g, _dst + _o, _w))
        _dst += _width
W_IN_COLS = sum(w for _, w in _REF_COLS.values())
_PROJ_OUT = (("MG", GROUP_WIDTH["MG"]), ("AT", GROUP_WIDTH["AT"]), ("GM", BR_WIDTH), ("X0", HY_WIDTH),
             ("UV", HY_WIDTH))


def _inproj_kernel(*refs, rope):
    x_ref, xp_ref, xn_ref, mod_ref, g_ref, w_ref = refs[0:6]
    gm_refs = refs[6:10]
    sw_ref, sb_ref = refs[10:12]
    pos = 12
    if rope:
        cos_ref, sin_ref = refs[12:14]
        pos = 14
    o_of = {name: ref for (name, _), ref in zip(_PROJ_OUT, refs[pos:])}
    if rope:
        q_ref, k_ref = refs[pos + len(_PROJ_OUT):]
    tm = x_ref.shape[1]
    sh = mod_ref[0, :, 0:D_MODEL]
    sc = mod_ref[0, :, D_MODEL:2 * D_MODEL]

    def norm_mod(xv):
        y = xv * lax.rsqrt(jnp.mean(xv * xv, axis=-1, keepdims=True) + EPS) * g_ref[...]
        return (y * (1.0 + sc) + sh).astype(BF16)

    h = norm_mod(x_ref[0])
    h_halo = norm_mod(jnp.concatenate([xp_ref[0], xn_ref[0]], axis=0))

    if rope:
        lane = lax.broadcasted_iota(jnp.int32, cos_ref.shape, 1)
        low = (lane % (2 * ROPE_NF)) < ROPE_NF
        cs = cos_ref[...]
        sn = sin_ref[...]

        def rot(v):
            partner = jnp.where(low, pltpu.roll(v, LANES - ROPE_NF, axis=1), pltpu.roll(v, ROPE_NF, axis=1))
            return (v * cs + partner * sn).astype(BF16)

    for src, gname, dst, width in _INPROJ_CHUNKS:
        if gname == "HY":
            zc = _dot(jnp.concatenate([h, h_halo], axis=0), w_ref[:, src:src + width])
            i = pl.program_id(1)
            z = zc[0:tm]
            before = jnp.where(i == 0, 0.0, zc[tm + SUBLANES - 1:tm + SUBLANES])
            after = jnp.where(i == pl.num_programs(1) - 1, 0.0, zc[tm + SUBLANES:tm + SUBLANES + 1])
            row = lax.broadcasted_iota(jnp.int32, z.shape, 0)
            zm = jnp.where(row == 0, before, pltpu.roll(z, 1, axis=0))
            zp = jnp.where(row == tm - 1, after, pltpu.roll(z, tm - 1, axis=0))
            y = zm * sw_ref[0:1, :] + z * sw_ref[1:2, :] + zp * sw_ref[2:3, :] + sb_ref[...]
            o_of["X0"][0] = y[:, 0:HY_WIDTH].astype(BF16)
            o_of["UV"][0] = (y[:, HY_WIDTH:2 * HY_WIDTH] * y[:, 2 * HY_WIDTH:]).astype(BF16)
            continue
        zc = _dot(h, w_ref[:, src:src + width])
        if gname == "GM":
            o_of["GM"][0] = _gmlp_math(zc, *gm_refs)
            continue
        o_of[gname][0, :, dst:dst + width] = zc.astype(BF16)
        if rope and gname == "AT" and dst in (ATC["K"], ATC["Q"]):
            out, scale = (k_ref, 1.0) if dst == ATC["K"] else (q_ref, QK_SCALE)
            for hd in range(DA_HEADS):
                cols = slice(hd * LANES, (hd + 1) * LANES)
                out[0, :, cols] = rot(zc[:, cols] * scale)


def in_projection(x, mod_rows, row_of_batch, g, w, gm, sconv, rope_tabs=None):
    bsz, n, _ = x.shape
    tm = min(n, 512)
    rope = rope_tabs is not None
    hb = tm // SUBLANES
    last_hb = n // SUBLANES - 1
    w_half = BR_WIDTH
    w3c = 3 * HY_WIDTH
    const2 = lambda b, i: (0, 0)
    in_specs = [pl.BlockSpec((1, tm, D_MODEL), lambda b, i: (b, i, 0)),
                pl.BlockSpec((1, SUBLANES, D_MODEL), lambda b, i: (b, jnp.maximum(i * hb - 1, 0), 0)),
                pl.BlockSpec((1, SUBLANES, D_MODEL), lambda b, i: (b, jnp.minimum((i + 1) * hb, last_hb), 0)),
                pl.BlockSpec((1, 1, 3 * D_MODEL), lambda b, i: (row_of_batch(b), 0, 0)),
                pl.BlockSpec((1, D_MODEL), const2),
                pl.BlockSpec((D_MODEL, W_IN_COLS), const2, pipeline_mode=pl.Buffered(1)),
                pl.BlockSpec((1, w_half), const2),
                pl.BlockSpec((1, w_half), const2),
                pl.BlockSpec((GM_GROUPS, GM_CHUNK, GM_CHUNK), lambda b, i: (0, 0, 0)),
                pl.BlockSpec((GM_CHUNK, w_half), const2),
                pl.BlockSpec((3, w3c), const2),
                pl.BlockSpec((1, w3c), const2)]
    args = [x, x, x, mod_rows, g.reshape(1, D_MODEL), w] + list(gm) + list(sconv)
    if rope:
        in_specs += [pl.BlockSpec((tm, LANES), lambda b, i: (i, 0))] * 2
        args += list(rope_tabs)
    out_specs = [pl.BlockSpec((1, tm, width), lambda b, i: (b, i, 0)) for _, width in _PROJ_OUT]
    out_shape = [jax.ShapeDtypeStruct((bsz, n, width), BF16) for _, width in _PROJ_OUT]
    if rope:
        out_specs += [pl.BlockSpec((1, tm, _QK_WIDTH), lambda b, i: (b, i, 0))] * 2
        out_shape += [jax.ShapeDtypeStruct((bsz, n, _QK_WIDTH), BF16)] * 2
    res = pl.pallas_call(
        functools.partial(_inproj_kernel, rope=rope),
        grid=(bsz, n // tm),
        in_specs=in_specs,
        out_specs=out_specs,
        out_shape=out_shape,
        compiler_params=pltpu.CompilerParams(dimension_semantics=("arbitrary", "arbitrary"),
                                             vmem_limit_bytes=INPROJ_VMEM_LIMIT),
        name="in_projection",
    )(*args)
    z = {name: r for (name, _), r in zip(_PROJ_OUT, res)}
    return (z, res[-2], res[-1]) if rope else z


def rope_tables(n):
    pos = jnp.arange(n)
    row = (pos // GRID_W).astype(F32)
    col = (pos % GRID_W).astype(F32)
    inv = ROPE_BASE ** (-jnp.arange(ROPE_NF, dtype=F32) / ROPE_NF)
    ar = row[:, None] * inv
    ac = col[:, None] * inv
    cos64 = jnp.concatenate([jnp.cos(ar), jnp.cos(ar), jnp.cos(ac), jnp.cos(ac)], axis=1)
    sin64 = jnp.concatenate([-jnp.sin(ar), jnp.sin(ar), -jnp.sin(ac), jnp.sin(ac)], axis=1)
    return jnp.tile(cos64, (1, 2)), jnp.tile(sin64, (1, 2))


def _attn_kernel(*refs, lam_init, n_lat, ck, prescaled):
    if n_lat:
        lam_ref, g_ref, q_ref, kc_ref, vc_ref, k_ref, v_ref, o_ref, vt_ref = refs
    else:
        lam_ref, g_ref, q_ref, kc_ref, vc_ref, o_ref, vt_ref = refs
    tq = q_ref.shape[1]
    n_ctx = kc_ref.shape[1]

    @pl.when(pl.program_id(2) == 0)
    def _():
        vt_ref[0:DA_VDIM, 0:n_ctx] = vc_ref[0].astype(F32).T.astype(BF16)
        for c in range(n_lat // ck):
            vt_ref[0:DA_VDIM, n_ctx + c * ck:n_ctx + (c + 1) * ck] = (
                v_ref[0, c * ck:(c + 1) * ck, :].astype(F32).T.astype(BF16))
        pad = vt_ref.shape[0] - DA_VDIM
        row = lax.broadcasted_iota(jnp.int32, (pad, vt_ref.shape[1]), 0)
        vt_ref[DA_VDIM:, :] = (row == 0).astype(BF16)

    lp = lam_ref[...]
    lam = (jnp.exp(jnp.sum(lp[0:1] * lp[1:2], axis=1, keepdims=True))
           - jnp.exp(jnp.sum(lp[2:3] * lp[3:4], axis=1, keepdims=True)) + lam_init)

    q = q_ref[0]
    if not prescaled:
        q = (q.astype(F32) * QK_SCALE).astype(BF16)
    lane = lax.broadcasted_iota(jnp.int32, q.shape, 1)
    zero = jnp.zeros_like(q)
    qq = jnp.concatenate([jnp.where(lane < DA_SUB, q, zero), jnp.where(lane >= DA_SUB, q, zero)], axis=0)

    chunks = [(lambda: kc_ref[0], 0, n_ctx)]
    for c in range(n_lat // ck):
        chunks.append((lambda c=c: k_ref[0, c * ck:(c + 1) * ck, :], n_ctx + c * ck, ck))

    def scores_t(c):
        return lax.dot_general(chunks[c][0](), qq, (((1,), (1,)), ((), ())), preferred_element_type=F32)

    def pv_t(p_t, c):
        _, off, width = chunks[c]
        return _dot(vt_ref[:, off:off + width], p_t)

    m = jnp.full((1, 2 * tq), -jnp.inf, F32)
    acc = jnp.zeros((vt_ref.shape[0], 2 * tq), F32)
    pending = None
    s_next = scores_t(0)
    for c in range(len(chunks)):
        s = s_next
        if c + 1 < len(chunks):
            s_next = scores_t(c + 1)
        part = s[0:64]
        for r in range(64, s.shape[0], 64):
            part = jnp.maximum(part, s[r:r + 64])
        m_new = jnp.maximum(m, jnp.max(part, axis=0, keepdims=True))
        alpha = jnp.exp2(m - m_new)
        p_t = jnp.exp2(s - m_new).astype(BF16)
        m = m_new
        if pending is not None:
            p_prev, alpha_prev, c_prev = pending
            acc = acc * alpha_prev + pv_t(p_prev, c_prev)
        pending = (p_t, alpha, c)
    p_prev, alpha_prev, c_prev = pending
    acc = acc * alpha_prev + pv_t(p_prev, c_prev)

    o_t = acc[0:DA_VDIM, :] / acc[DA_VDIM:DA_VDIM + 1, :]
    d = (o_t[:, 0:tq] - lam * o_t[:, tq:]).T
    y = d * lax.rsqrt(jnp.mean(d * d, axis=-1, keepdims=True) + EPS) * g_ref[...]
    o_ref[0] = (y * (1.0 - lam_init)).astype(BF16)


def diff_attention(q_arr, q_col, zc, lat, lam_p, subln, lam_init, prescaled):
    bsz, nq, _ = q_arr.shape
    n_ctx = zc.shape[1]
    tq = min(nq, 512)
    kcb, vcb = ATC["K"] // LANES, ATC["V"] // LANES
    in_specs = [pl.BlockSpec((4, DA_SUB), lambda b, h, i: (0, 0)),
                pl.BlockSpec((1, DA_VDIM), lambda b, h, i: (0, 0)),
                pl.BlockSpec((1, tq, LANES), lambda b, h, i: (b, i, q_col + h)),
                pl.BlockSpec((1, n_ctx, LANES), lambda b, h, i: (b, 0, kcb + h)),
                pl.BlockSpec((1, n_ctx, LANES), lambda b, h, i: (b, 0, vcb + h))]
    args = [lam_p, subln.reshape(1, DA_VDIM), q_arr, zc, zc]
    n_lat = 0
    if lat is not None:
        k_arr, k_col, v_arr, v_col = lat
        n_lat = k_arr.shape[1]
        in_specs += [pl.BlockSpec((1, n_lat, LANES), lambda b, h, i: (b, 0, k_col + h)),
                     pl.BlockSpec((1, n_lat, LANES), lambda b, h, i: (b, 0, v_col + h))]
        args += [k_arr, v_arr]
    ones_rows = 16
    scratch = [pltpu.VMEM((DA_VDIM + ones_rows, n_ctx + n_lat), BF16)]
    return pl.pallas_call(
        functools.partial(_attn_kernel, lam_init=lam_init, n_lat=n_lat, ck=512, prescaled=prescaled),
        grid=(bsz, DA_HEADS, nq // tq),
        in_specs=in_specs,
        out_specs=pl.BlockSpec((1, tq, LANES), lambda b, h, i: (b, i, h)),
        out_shape=jax.ShapeDtypeStruct((bsz, nq, BR_WIDTH), BF16),
        scratch_shapes=scratch,
        compiler_params=_params(("arbitrary", "arbitrary", "arbitrary")),
        name="diff_attention",
    )(*args)


def hyena_positions(L):
    t = jnp.linspace(0.0, 1.0, L, dtype=F32)[:, None]
    wpos = ((2.0 * math.pi / L) * jnp.arange(L, dtype=F32))[:, None]
    bands = jnp.linspace(1e-4, HY_BANDS - 1, HY_BANDS, dtype=F32)[None, :]
    fwd = jnp.concatenate([t, jnp.cos(bands * wpos), -jnp.sin(bands * wpos)], axis=-1)
    emb = jnp.concatenate([fwd, fwd[0:1], jnp.flip(fwd[1:], axis=0)], axis=0)
    mask = (jnp.arange(2 * L) != L).astype(F32)[:, None]
    pad = jnp.zeros((2 * L, LANES - HY_EMB - 1), F32)
    return jnp.concatenate([emb, pad, mask], axis=-1)


def _filter_kernel(feat_ref, w1_ref, b1_ref, w2_ref, b2_ref, w3_ref, fr_ref, dl_ref, k_ref, ss_ref):
    half = feat_ref.shape[0] // 2
    feat = feat_ref[...]
    f2 = jnp.concatenate([feat[0:half], feat[half:]], axis=1)
    hid = jnp.sin(fr_ref[0:1, :] * (_dot_hi(f2, w1_ref[...]) + b1_ref[...]))
    hid = jnp.sin(fr_ref[1:2, :] * (_dot_hi(hid, w2_ref[...]) + b2_ref[...]))
    w3 = w3_ref[...].astype(BF16)
    zero = jnp.zeros_like(w3)
    hid = hid.astype(BF16)
    h = jnp.concatenate([_dot(hid, jnp.concatenate([w3, zero], axis=0)),
                         _dot(hid, jnp.concatenate([zero, w3], axis=0))], axis=0)
    t = feat[:, 0:1]
    mask = feat[:, LANES - 1:LANES]
    k = h * jnp.exp(-t * dl_ref[...]) * mask
    k_ref[...] = k.astype(k_ref.dtype)

    @pl.when(pl.program_id(0) == 0)
    def _():
        ss_ref[...] = jnp.zeros_like(ss_ref)

    ss_ref[...] += jnp.sum(k * k, axis=0, keepdims=True)


def hyena_filter_taps(feat, w1, b1, w2, b2, w3, freq):
    two_l = feat.shape[0]
    tr = min(two_l // 2, 1024)
    nb = two_l // tr
    hd = HY_HIDDEN
    w1d = jnp.zeros((2 * LANES, 2 * hd), F32).at[:HY_EMB, :hd].set(w1).at[LANES:LANES + HY_EMB, hd:].set(w1)
    w2d = jnp.zeros((2 * hd, 2 * hd), F32).at[:hd, :hd].set(w2).at[hd:, hd:].set(w2)
    twice = lambda v: jnp.tile(v.reshape(-1, hd), (1, 2))
    deltas = jnp.abs(jnp.linspace(HY_MIN_DECAY, HY_MAX_DECAY, HY_WIDTH, dtype=F32))[None, :]
    const = lambda i: (0, 0)
    return pl.pallas_call(
        _filter_kernel,
        grid=(nb,),
        in_specs=[pl.BlockSpec((tr, LANES), lambda i: (i, 0)),
                  pl.BlockSpec((2 * LANES, 2 * hd), const),
                  pl.BlockSpec((1, 2 * hd), const),
                  pl.BlockSpec((2 * hd, 2 * hd), const),
                  pl.BlockSpec((1, 2 * hd), const),
                  pl.BlockSpec((hd, HY_WIDTH), lambda i: (0, (2 * i) // nb)),
                  pl.BlockSpec((2, 2 * hd), const),
                  pl.BlockSpec((1, HY_WIDTH), const)],
        out_specs=[pl.BlockSpec((tr, HY_WIDTH), lambda i: (i, 0)),
                   pl.BlockSpec((1, HY_WIDTH), const)],
        out_shape=[jax.ShapeDtypeStruct((two_l, HY_WIDTH), BF16),
                   jax.ShapeDtypeStruct((1, HY_WIDTH), F32)],
        compiler_params=_params(("arbitrary",)),
        name="hyena_filter",
    )(feat, w1d, twice(b1), w2d, twice(b2), w3, twice(freq), deltas)


def dft_tables():
    r = DFT_R
    n_fft = r * r
    idx = jnp.arange(r, dtype=jnp.int32)
    prod = idx[:, None] * idx[None, :]
    ang = (2.0 * math.pi / r) * (prod % r).astype(F32)
    f_re, f_im = jnp.cos(ang), -jnp.sin(ang)
    ang = (2.0 * math.pi / n_fft) * prod.astype(F32)
    t_re, t_im = jnp.cos(ang), -jnp.sin(ang)

    def cmul(a_re, a_im, b_re, b_im):
        return a_re * b_re - a_im * b_im, a_re * b_im + a_im * b_re

    g_re, g_im = cmul(f_re[None, :, :], f_im[None, :, :], t_re[:, None, :], t_im[:, None, :])
    def cplx(re, im):
        return jnp.concatenate([jnp.concatenate([re, -im], axis=-1),
                                jnp.concatenate([im, re], axis=-1)], axis=-2).astype(BF16)

    half = r // 2
    return dict(f_pad=cplx(f_re[:, :half], f_im[:, :half]),
                f_real=jnp.concatenate([f_re, f_im], axis=0).astype(BF16),
                g=cplx(g_re, g_im),
                f_inv=cplx(f_re[:half], -f_im[:half]))


def _fft1_kernel(x_ref, f_ref, are_ref, aim_ref):
    r = are_ref.shape[0]
    c = are_ref.shape[-1]
    x = x_ref[...]
    p = _dot(f_ref[...], x.reshape(-1, x.shape[-1]))
    for t in range(are_ref.shape[1]):
        are_ref[:, t, :] = p[0:r, t * c:(t + 1) * c].astype(BF16)
        aim_ref[:, t, :] = p[r:, t * c:(t + 1) * c].astype(BF16)


def fft_level1(x, f):
    r = DFT_R
    cols = x.shape[-1]
    tc = 8192
    if x.ndim == 3:
        x_spec = pl.BlockSpec((2, x.shape[1], tc), lambda j: (0, 0, j))
    else:
        x_spec = pl.BlockSpec((x.shape[0], tc), lambda j: (0, j))
    c = cols // r
    out = jax.ShapeDtypeStruct((r, r, c), BF16)
    o_spec = pl.BlockSpec((r, tc // c, c), lambda j: (0, j, 0))
    return pl.pallas_call(
        _fft1_kernel,
        grid=(cols // tc,),
        in_specs=[x_spec, pl.BlockSpec(f.shape, lambda j: (0, 0))],
        out_specs=[o_spec, o_spec],
        out_shape=[out, out],
        compiler_params=_params(("arbitrary",)),
        name="fft_level1",
    )(x, f)


def _fft2_conv_kernel(are_ref, aim_ref, kare_ref, kaim_ref, g_ref, sc_ref, bre_ref, bim_ref):
    r = are_ref.shape[1]
    c = are_ref.shape[-1]
    sc = sc_ref[...]
    for j in range(are_ref.shape[0]):
        rhs = jnp.concatenate([jnp.concatenate([are_ref[j], aim_ref[j]], axis=0),
                               jnp.concatenate([kare_ref[j], kaim_ref[j]], axis=0)], axis=1).astype(BF16)
        xk = _dot(g_ref[j], rhs)
        x_re, x_im = xk[0:r, 0:c], xk[r:, 0:c]
        k_re, k_im = xk[0:r, c:] * sc, xk[r:, c:] * sc
        y = jnp.concatenate([(x_re * k_re - x_im * k_im).astype(BF16),
                             (x_re * k_im + x_im * k_re).astype(BF16)], axis=0)
        bm = lax.dot_general(g_ref[j], y, (((0,), (0,)), ((), ())), preferred_element_type=F32)
        bre_ref[j] = bm[0:r].astype(BF16)
        bim_ref[j] = bm[r:].astype(BF16)


def fft_level2_conv(a_re, a_im, ka_re, ka_im, g, scale):
    r = DFT_R
    c = a_re.shape[-1]
    tk = 8
    blk_a = pl.BlockSpec((tk, r, c), lambda i: (i, 0, 0))
    blk_g = pl.BlockSpec((tk, 2 * r, 2 * r), lambda i: (i, 0, 0))
    out = jax.ShapeDtypeStruct((r, r, c), BF16)
    return pl.pallas_call(
        _fft2_conv_kernel,
        grid=(r // tk,),
        in_specs=[blk_a, blk_a, blk_a, blk_a, blk_g, pl.BlockSpec((1, c), lambda i: (0, 0))],
        out_specs=[blk_a, blk_a],
        out_shape=[out, out],
        compiler_params=_params(("arbitrary",)),
        name="fft_level2_conv",
    )(a_re, a_im, ka_re, ka_im, g, scale)


def _ifft1_kernel(bre_ref, bim_ref, f_ref, uv_ref, x0_ref, bias_ref, o_ref):
    half = f_ref.shape[0] // 2
    y = _dot(f_ref[...], jnp.concatenate([bre_ref[...], bim_ref[...]], axis=0))
    bias = bias_ref[...]
    o_ref[0] = ((y[0:half] + uv_ref[0].astype(F32) * bias) * x0_ref[0].astype(F32)).astype(BF16)
    o_ref[1] = ((y[half:] + uv_ref[1].astype(F32) * bias) * x0_ref[1].astype(F32)).astype(BF16)


def ifft_level1(b_re, b_im, f_inv, uv, x0, bias_t):
    r = DFT_R
    half = r // 2
    cols = b_re.shape[-1]
    tc = 8192
    blk_b = pl.BlockSpec((r, tc), lambda j: (0, j))
    blk_x = pl.BlockSpec((2, half, tc), lambda j: (0, 0, j))
    return pl.pallas_call(
        _ifft1_kernel,
        grid=(cols // tc,),
        in_specs=[blk_b, blk_b, pl.BlockSpec(f_inv.shape, lambda j: (0, 0)), blk_x, blk_x,
                  pl.BlockSpec((1, tc), lambda j: (0, 0))],
        out_specs=blk_x,
        out_shape=jax.ShapeDtypeStruct((2, half, cols), BF16),
        compiler_params=_params(("arbitrary",)),
        name="ifft_level1",
    )(b_re, b_im, f_inv, uv, x0, bias_t)


def hyena_long_conv(uv, x0, taps, sumsq, bias, tables):
    r = DFT_R
    bsz, L, c = uv.shape
    assert bsz == 2 and 2 * L == r * r
    scale = lax.rsqrt(sumsq) * (1.0 / (r * r))
    ka_re, ka_im = fft_level1(taps.reshape(r, r * c), tables["f_real"])
    uv2 = uv.reshape(2, r // 2, r * c)
    a_re, a_im = fft_level1(uv2, tables["f_pad"])
    b_re, b_im = fft_level2_conv(a_re, a_im, ka_re, ka_im, tables["g"], scale)
    bias_t = jnp.tile(bias.reshape(1, c), (1, 8192 // c))
    y = ifft_level1(b_re.reshape(r, r * c), b_im.reshape(r, r * c), tables["f_inv"], uv2,
                    x0.reshape(2, r // 2, r * c), bias_t)
    return y.reshape(2, L, c)


def dft_small_tables(n_fft):
    idx = jnp.arange(n_fft, dtype=jnp.int32)
    ang = (2.0 * math.pi / n_fft) * ((idx[:, None] * idx[None, :]) % n_fft).astype(F32)
    return jnp.cos(ang).astype(BF16), (-jnp.sin(ang)).astype(BF16)


def _conv_small_kernel(taps_ref, ss_ref, fre_ref, fim_ref, uv_ref, x0_ref, bias_ref, o_ref):
    two_l = taps_ref.shape[0]
    L = two_l // 2
    fre, fim = fre_ref[...], fim_ref[...]
    scale = lax.rsqrt(ss_ref[...]) * (1.0 / two_l)
    taps = taps_ref[...]
    k_re = _dot(fre, taps) * scale
    k_im = _dot(fim, taps) * scale
    u0, u1 = uv_ref[0], uv_ref[1]
    fre_l, fim_l = fre[:, :L], fim[:, :L]
    x_re = _dot(fre_l, u0) - _dot(fim_l, u1)
    x_im = _dot(fre_l, u1) + _dot(fim_l, u0)
    y_re = (x_re * k_re - x_im * k_im).astype(BF16)
    y_im = (x_re * k_im + x_im * k_re).astype(BF16)
    fre_t, fim_t = fre[:L, :], fim[:L, :]
    y0 = _dot(fre_t, y_re) + _dot(fim_t, y_im)
    y1 = _dot(fre_t, y_im) - _dot(fim_t, y_re)
    bias = bias_ref[...]
    o_ref[0] = ((y0 + u0.astype(F32) * bias) * x0_ref[0].astype(F32)).astype(BF16)
    o_ref[1] = ((y1 + u1.astype(F32) * bias) * x0_ref[1].astype(F32)).astype(BF16)


def hyena_long_conv_small(uv, x0, taps, sumsq, bias, tables):
    bsz, L, c = uv.shape
    assert bsz == 2
    f_re, f_im = tables
    full = lambda shape: pl.BlockSpec(shape, lambda i: (0,) * len(shape))
    return pl.pallas_call(
        _conv_small_kernel,
        grid=(1,),
        in_specs=[full((2 * L, c)), full((1, c)), full((2 * L, 2 * L)), full((2 * L, 2 * L)),
                  full((2, L, c)), full((2, L, c)), full((1, c))],
        out_specs=full((2, L, c)),
        out_shape=jax.ShapeDtypeStruct((2, L, c), BF16),
        compiler_params=_params(("arbitrary",)),
        name="hyena_conv_small",
    )(taps, sumsq, f_re, f_im, uv, x0, bias.reshape(1, c))


def _merge_kernel(ya_ref, yb_ref, yc_ref, ga_ref, gb_ref, gc_ref, mg_ref, x_ref, mod_ref, np_ref,
                  wb_ref, wo_ref, o_ref):
    acc = None
    for i, (y_ref, g_ref) in enumerate(((ya_ref, ga_ref), (yb_ref, gb_ref), (yc_ref, gc_ref))):
        g = g_ref[0]
        gated = y_ref[0] * (g * jax.nn.sigmoid(g))
        sel = jax.nn.sigmoid(mg_ref[0, :, i * D_MODEL:(i + 1) * D_MODEL])
        term = sel * _dot(gated, wb_ref[i]).astype(BF16)
        acc = term if acc is None else acc + term
    out = _dot(acc, wo_ref[...])
    r = out * lax.rsqrt(jnp.mean(out * out, axis=-1, keepdims=True) + EPS) * np_ref[...]
    gt = mod_ref[0, :, 2 * D_MODEL:]
    o_ref[0] = x_ref[0] + gt * r


def merge_out(ya, yb, yc, zat, zmg, x, mod_rows, row_of_batch, npost, wb, wo):
    bsz, n, _ = x.shape
    tm = min(n, 512)
    w = BR_WIDTH
    yspec = pl.BlockSpec((1, tm, w), lambda b, i: (b, i, 0))

    def zspec(name):
        cb = ATC[name] // w
        return pl.BlockSpec((1, tm, w), lambda b, i: (b, i, cb))

    return pl.pallas_call(
        _merge_kernel,
        grid=(bsz, n // tm),
        in_specs=[yspec, yspec, yspec, zspec("GA"), zspec("GB"), zspec("GC"),
                  pl.BlockSpec((1, tm, 3 * D_MODEL), lambda b, i: (b, i, 0)),
                  pl.BlockSpec((1, tm, D_MODEL), lambda b, i: (b, i, 0)),
                  pl.BlockSpec((1, 1, 3 * D_MODEL), lambda b, i: (row_of_batch(b), 0, 0)),
                  pl.BlockSpec((1, D_MODEL), lambda b, i: (0, 0)),
                  pl.BlockSpec((3, w, D_MODEL), lambda b, i: (0, 0, 0)),
                  pl.BlockSpec((D_MODEL, D_MODEL), lambda b, i: (0, 0))],
        out_specs=pl.BlockSpec((1, tm, D_MODEL), lambda b, i: (b, i, 0)),
        out_shape=jax.ShapeDtypeStruct((bsz, n, D_MODEL), F32),
        compiler_params=_params(("arbitrary", "arbitrary")),
        name="merge_out",
    )(ya, yb, yc, zat, zat, zat, zmg, x, mod_rows, npost.reshape(1, D_MODEL), wb, wo)


def kernel(x, c, ctx, c_ctx, ada_w, ada_b, norm_pre, norm_post, w_in, da_lambda, da_subln, hy_short_w,
           hy_short_b, hy_f_w1, hy_f_b1, hy_f_w2, hy_f_b2, hy_f_w3, hy_f_freq, hy_bias, gm_ln_g, gm_ln_b,
           gm_ws, gm_bs, w_branch, w_out):
    bsz, n, _ = x.shape
    n_ctx = ctx.shape[1]
    assert bsz == 2 and 2 * n == DFT_R * DFT_R

    cond8 = jnp.zeros((8, D_MODEL), F32).at[0:bsz].set(c).at[bsz].set(c_ctx)
    mod = modulation_all(cond8, ada_w, ada_b)
    lat_row = lambda b: b
    ctx_row = lambda b: bsz

    cos_t, sin_t = rope_tables(n)
    feat = hyena_positions(n)
    feat_c = hyena_positions(n_ctx)
    tables = dft_tables()
    tables_c = dft_small_tables(2 * n_ctx)
    vb, qb = ATC["V"] // LANES, ATC["Q"] // LANES

    xc = ctx
    for l in range(DEPTH):
        last = l == DEPTH - 1
        lam_init = 0.8 - 0.6 * math.exp(-0.3 * l)
        mod_rows = mod[l].reshape(8, 1, 3 * D_MODEL)
        w_l = w_in[l].astype(BF16)
        wb_l = w_branch[l].astype(BF16)
        wo_l = w_out[l].astype(BF16)
        filt_w = (hy_f_w1[l], hy_f_b1[l], hy_f_w2[l], hy_f_b2[l], hy_f_w3[l], hy_f_freq[l])

        gm_l = gmlp_params(gm_ln_g[l], gm_ln_b[l], gm_ws[l], gm_bs[l])
        sconv_l = (hy_short_w[l], hy_short_b[l].reshape(1, -1))
        z, qr, kr = in_projection(x, mod_rows, lat_row, norm_pre[l], w_l, gm_l, sconv_l, (cos_t, sin_t))
        zc = in_projection(xc, mod_rows, ctx_row, norm_pre[l], w_l, gm_l, sconv_l)

        taps, sumsq = hyena_filter_taps(feat, *filt_w)
        y_a = diff_attention(qr, 0, zc["AT"], (kr, 0, z["AT"], vb), da_lambda[l], da_subln[l], lam_init, True)
        y_b = hyena_long_conv(z["UV"], z["X0"], taps, sumsq, hy_bias[l], tables)
        y_c = z["GM"]
        x_new = merge_out(y_a, y_b, y_c, z["AT"], z["MG"], x, mod_rows, lat_row, norm_post[l], wb_l, wo_l)

        if not last:
            yc_a = diff_attention(zc["AT"], qb, zc["AT"], None, da_lambda[l], da_subln[l], lam_init, False)
            taps_c, sumsq_c = hyena_filter_taps(feat_c, *filt_w)
            yc_b = hyena_long_conv_small(zc["UV"], zc["X0"], taps_c, sumsq_c, hy_bias[l], tables_c)
            yc_c = zc["GM"]
            xc = merge_out(yc_a, yc_b, yc_c, zc["AT"], zc["MG"], xc, mod_rows, ctx_row, norm_post[l], wb_l, wo_l)
        x = x_new
    return x
```

```python
import functools
import math

import jax
import jax.numpy as jnp
from jax import lax
from jax.experimental import pallas as pl
from jax.experimental.pallas import tpu as pltpu

F32 = jnp.float32
BF16 = jnp.bfloat16

D_MODEL = 1024
DEPTH = 4
GRID_W = 64
EPS = 1e-6
BR_WIDTH = 512
DA_SUB = 64
DA_VDIM = 128
DA_HEADS = 4
ROPE_BASE = 10000.0
ROPE_NF = 16
HY_WIDTH = 512
HY_BANDS = 16
HY_EMB = 33
HY_HIDDEN = 64
HY_MIN_DECAY = math.log(1e-2) / 1.5
HY_MAX_DECAY = math.log(1e-2) / 0.3
GM_GROUPS = 8
GM_CHUNK = 128

LANES = 128
DFT_R = 128
QK_SCALE = DA_SUB ** -0.5 * math.log2(math.e)

_REF_COLS = dict(K=(0, 512), V=(512, 512), Q=(1024, 512), GA=(1536, 512), HY=(2048, 1536),
                 GB=(3584, 512), GM=(4096, 1024), GC=(5120, 512), MG=(5632, 3072))
_GROUPS = (("MG", ("MG",)), ("HY", ("HY",)), ("GM", ("GM",)), ("AT", ("K", "V", "Q", "GA", "GB", "GC")))
GROUP_WIDTH = {g: sum(_REF_COLS[nm][1] for nm in names) for g, names in _GROUPS}
ATC = {}
_off = 0
for _name in _GROUPS[-1][1]:
    ATC[_name] = _off
    _off += _REF_COLS[_name][1]

VMEM_LIMIT = 48 * 1024 * 1024
INPROJ_VMEM_LIMIT = 56 * 1024 * 1024


def _params(sem):
    return pltpu.CompilerParams(dimension_semantics=sem, vmem_limit_bytes=VMEM_LIMIT)


def _dot(a, b):
    return jnp.dot(a, b, preferred_element_type=F32)


def _dot_hi(a, b):
    return jnp.dot(a, b, preferred_element_type=F32, precision=lax.Precision.HIGHEST)


def _mod_kernel(c_ref, w_ref, b_ref, o_ref):
    cond = c_ref[...]
    s = cond * jax.nn.sigmoid(cond)
    o_ref[0] = _dot_hi(s, w_ref[0]) + b_ref[0]


def modulation_all(cond8, ada_w, ada_b):
    tn = 1024
    return pl.pallas_call(
        _mod_kernel,
        grid=(DEPTH, 3 * D_MODEL // tn),
        in_specs=[pl.BlockSpec((8, D_MODEL), lambda l, j: (0, 0)),
                  pl.BlockSpec((1, D_MODEL, tn), lambda l, j: (l, 0, j)),
                  pl.BlockSpec((1, 1, tn), lambda l, j: (l, 0, j))],
        out_specs=pl.BlockSpec((1, 8, tn), lambda l, j: (l, 0, j)),
        out_shape=jax.ShapeDtypeStruct((DEPTH, 8, 3 * D_MODEL), F32),
        compiler_params=_params(("arbitrary", "arbitrary")),
        name="modulation",
    )(cond8, ada_w, ada_b.reshape(DEPTH, 1, 3 * D_MODEL))


def _gmlp_math(zg, g_ref, b_ref, ws_ref, bs_ref):
    gl = 0.5 * zg * (1.0 + lax.erf(zg * (2.0 ** -0.5)))
    w = BR_WIDTH
    u = gl[:, :w]
    v = gl[:, w:]
    mu = jnp.mean(v, axis=-1, keepdims=True)
    var = jnp.mean(jnp.square(v - mu), axis=-1, keepdims=True)
    v = ((v - mu) * lax.rsqrt(var + EPS) * g_ref[...] + b_ref[...]).astype(BF16)
    gw = w // GM_GROUPS
    lane = lax.broadcasted_iota(jnp.int32, (GM_CHUNK, LANES), 1)
    first = lane < gw
    out = []
    for ci in range(zg.shape[0] // GM_CHUNK):
        rows = slice(ci * GM_CHUNK, (ci + 1) * GM_CHUNK)
        tiles = []
        for t in range(w // LANES):
            vt = v[rows, t * LANES:(t + 1) * LANES]
            tiles.append(jnp.where(first, _dot(ws_ref[2 * t], vt), _dot(ws_ref[2 * t + 1], vt)))
        vm = jnp.concatenate(tiles, axis=1) + bs_ref[...]
        out.append((u[rows] * vm).astype(BF16))
    return jnp.concatenate(out, axis=0)


def gmlp_params(ln_g, ln_b, ws, bs):
    w = BR_WIDTH
    bs_full = jnp.repeat(bs.T, w // GM_GROUPS, axis=1)
    return ln_g.reshape(1, w), ln_b.reshape(1, w), ws.astype(BF16), bs_full


_QK_WIDTH = DA_HEADS * LANES
MXU_TILE = 256
MAX_CHUNK = 2048
SUBLANES = 8
_INPROJ_CHUNKS = []
for _g, _names in _GROUPS:
    _dst = 0
    for _name in _names:
        _start, _width = _REF_COLS[_name]
        for _o in range(0, _width, MAX_CHUNK):
            _w = min(MAX_CHUNK, _width - _o)
            assert (_start + _o) % MXU_TILE == 0 and _w % MXU_TILE == 0
            _INPROJ_CHUNKS.append((_start + _o, _g, _dst + _o, _w))
        _dst += _width
W_IN_COLS = sum(w for _, w in _REF_COLS.values())
_PROJ_OUT = (("MG", GROUP_WIDTH["MG"]), ("AT", GROUP_WIDTH["AT"]), ("GM", BR_WIDTH), ("X0", HY_WIDTH),
             ("UV", HY_WIDTH))


def _inproj_kernel(*refs, rope):
    x_ref, xp_ref, xn_ref, mod_ref, g_ref, w_ref = refs[0:6]
    gm_refs = refs[6:10]
    sw_ref, sb_ref = refs[10:12]
    pos = 12
    if rope:
        cos_ref, sin_ref = refs[12:14]
        pos = 14
    o_of = {name: ref for (name, _), ref in zip(_PROJ_OUT, refs[pos:])}
    if rope:
        q_ref, k_ref = refs[pos + len(_PROJ_OUT):]
    tm = x_ref.shape[1]
    sh = mod_ref[0, :, 0:D_MODEL]
    sc = mod_ref[0, :, D_MODEL:2 * D_MODEL]

    def norm_mod(xv):
        y = xv * lax.rsqrt(jnp.mean(xv * xv, axis=-1, keepdims=True) + EPS) * g_ref[...]
        return (y * (1.0 + sc) + sh).astype(BF16)

    h = norm_mod(x_ref[0])
    h_halo = norm_mod(jnp.concatenate([xp_ref[0], xn_ref[0]], axis=0))

    if rope:
        lane = lax.broadcasted_iota(jnp.int32, cos_ref.shape, 1)
        low = (lane % (2 * ROPE_NF)) < ROPE_NF
        cs = cos_ref[...]
        sn = sin_ref[...]

        def rot(v):
            partner = jnp.where(low, pltpu.roll(v, LANES - ROPE_NF, axis=1), pltpu.roll(v, ROPE_NF, axis=1))
            return (v * cs + partner * sn).astype(BF16)

    for src, gname, dst, width in _INPROJ_CHUNKS:
        if gname == "HY":
            zc = _dot(jnp.concatenate([h, h_halo], axis=0), w_ref[:, src:src + width])
            i = pl.program_id(1)
            z = zc[0:tm]
            before = jnp.where(i == 0, 0.0, zc[tm + SUBLANES - 1:tm + SUBLANES])
            after = jnp.where(i == pl.num_programs(1) - 1, 0.0, zc[tm + SUBLANES:tm + SUBLANES + 1])
            row = lax.broadcasted_iota(jnp.int32, z.shape, 0)
            zm = jnp.where(row == 0, before, pltpu.roll(z, 1, axis=0))
            zp = jnp.where(row == tm - 1, after, pltpu.roll(z, tm - 1, axis=0))
            y = zm * sw_ref[0:1, :] + z * sw_ref[1:2, :] + zp * sw_ref[2:3, :] + sb_ref[...]
            o_of["X0"][0] = y[:, 0:HY_WIDTH].astype(BF16)
            o_of["UV"][0] = (y[:, HY_WIDTH:2 * HY_WIDTH] * y[:, 2 * HY_WIDTH:]).astype(BF16)
            continue
        zc = _dot(h, w_ref[:, src:src + width])
        if gname == "GM":
            o_of["GM"][0] = _gmlp_math(zc, *gm_refs)
            continue
        o_of[gname][0, :, dst:dst + width] = zc.astype(BF16)
        if rope and gname == "AT" and dst in (ATC["K"], ATC["Q"]):
            out, scale = (k_ref, 1.0) if dst == ATC["K"] else (q_ref, QK_SCALE)
            for hd in range(DA_HEADS):
                cols = slice(hd * LANES, (hd + 1) * LANES)
                out[0, :, cols] = rot(zc[:, cols] * scale)


def in_projection(x, mod_rows, row_of_batch, g, w, gm, sconv, rope_tabs=None):
    bsz, n, _ = x.shape
    tm = min(n, 512)
    rope = rope_tabs is not None
    hb = tm // SUBLANES
    last_hb = n // SUBLANES - 1
    w_half = BR_WIDTH
    w3c = 3 * HY_WIDTH
    const2 = lambda b, i: (0, 0)
    in_specs = [pl.BlockSpec((1, tm, D_MODEL), lambda b, i: (b, i, 0)),
                pl.BlockSpec((1, SUBLANES, D_MODEL), lambda b, i: (b, jnp.maximum(i * hb - 1, 0), 0)),
                pl.BlockSpec((1, SUBLANES, D_MODEL), lambda b, i: (b, jnp.minimum((i + 1) * hb, last_hb), 0)),
                pl.BlockSpec((1, 1, 3 * D_MODEL), lambda b, i: (row_of_batch(b), 0, 0)),
                pl.BlockSpec((1, D_MODEL), const2),
                pl.BlockSpec((D_MODEL, W_IN_COLS), const2, pipeline_mode=pl.Buffered(1)),
                pl.BlockSpec((1, w_half), const2),
                pl.BlockSpec((1, w_half), const2),
                pl.BlockSpec((GM_GROUPS, GM_CHUNK, GM_CHUNK), lambda b, i: (0, 0, 0)),
                pl.BlockSpec((GM_CHUNK, w_half), const2),
                pl.BlockSpec((3, w3c), const2),
                pl.BlockSpec((1, w3c), const2)]
    args = [x, x, x, mod_rows, g.reshape(1, D_MODEL), w] + list(gm) + list(sconv)
    if rope:
        in_specs += [pl.BlockSpec((tm, LANES), lambda b, i: (i, 0))] * 2
        args += list(rope_tabs)
    out_specs = [pl.BlockSpec((1, tm, width), lambda b, i: (b, i, 0)) for _, width in _PROJ_OUT]
    out_shape = [jax.ShapeDtypeStruct((bsz, n, width), BF16) for _, width in _PROJ_OUT]
    if rope:
        out_specs += [pl.BlockSpec((1, tm, _QK_WIDTH), lambda b, i: (b, i, 0))] * 2
        out_shape += [jax.ShapeDtypeStruct((bsz, n, _QK_WIDTH), BF16)] * 2
    res = pl.pallas_call(
        functools.partial(_inproj_kernel, rope=rope),
        grid=(bsz, n // tm),
        in_specs=in_specs,
        out_specs=out_specs,
        out_shape=out_shape,
        compiler_params=pltpu.CompilerParams(dimension_semantics=("arbitrary", "arbitrary"),
                                             vmem_limit_bytes=INPROJ_VMEM_LIMIT),
        name="in_projection",
    )(*args)
    z = {name: r for (name, _), r in zip(_PROJ_OUT, res)}
    return (z, res[-2], res[-1]) if rope else z


def rope_tables(n):
    pos = jnp.arange(n)
    row = (pos // GRID_W).astype(F32)
    col = (pos % GRID_W).astype(F32)
    inv = ROPE_BASE ** (-jnp.arange(ROPE_NF, dtype=F32) / ROPE_NF)
    ar = row[:, None] * inv
    ac = col[:, None] * inv
    cos64 = jnp.concatenate([jnp.cos(ar), jnp.cos(ar), jnp.cos(ac), jnp.cos(ac)], axis=1)
    sin64 = jnp.concatenate([-jnp.sin(ar), jnp.sin(ar), -jnp.sin(ac), jnp.sin(ac)], axis=1)
    return jnp.tile(cos64, (1, 2)), jnp.tile(sin64, (1, 2))


def _attn_kernel(*refs, lam_init, n_lat, ck, prescaled):
    if n_lat:
        lam_ref, g_ref, q_ref, kc_ref, vc_ref, k_ref, v_ref, o_ref, vt_ref = refs
    else:
        lam_ref, g_ref, q_ref, kc_ref, vc_ref, o_ref, vt_ref = refs
    tq = q_ref.shape[1]
    n_ctx = kc_ref.shape[1]

    @pl.when(pl.program_id(2) == 0)
    def _():
        vt_ref[0:DA_VDIM, 0:n_ctx] = vc_ref[0].astype(F32).T.astype(BF16)
        for c in range(n_lat // ck):
            vt_ref[0:DA_VDIM, n_ctx + c * ck:n_ctx + (c + 1) * ck] = (
                v_ref[0, c * ck:(c + 1) * ck, :].astype(F32).T.astype(BF16))
        pad = vt_ref.shape[0] - DA_VDIM
        row = lax.broadcasted_iota(jnp.int32, (pad, vt_ref.shape[1]), 0)
        vt_ref[DA_VDIM:, :] = (row == 0).astype(BF16)

    lp = lam_ref[...]
    lam = (jnp.exp(jnp.sum(lp[0:1] * lp[1:2], axis=1, keepdims=True))
           - jnp.exp(jnp.sum(lp[2:3] * lp[3:4], axis=1, keepdims=True)) + lam_init)

    q = q_ref[0]
    if not prescaled:
        q = (q.astype(F32) * QK_SCALE).astype(BF16)
    lane = lax.broadcasted_iota(jnp.int32, q.shape, 1)
    zero = jnp.zeros_like(q)
    qq = jnp.concatenate([jnp.where(lane < DA_SUB, q, zero), jnp.where(lane >= DA_SUB, q, zero)], axis=0)

    chunks = [(lambda: kc_ref[0], 0, n_ctx)]
    for c in range(n_lat // ck):
        chunks.append((lambda c=c: k_ref[0, c * ck:(c + 1) * ck, :], n_ctx + c * ck, ck))

    def scores_t(c):
        return lax.dot_general(chunks[c][0](), qq, (((1,), (1,)), ((), ())), preferred_element_type=F32)

    def pv_t(p_t, c):
        _, off, width = chunks[c]
        return _dot(vt_ref[:, off:off + width], p_t)

    m = jnp.full((1, 2 * tq), -jnp.inf, F32)
    acc = jnp.zeros((vt_ref.shape[0], 2 * tq), F32)
    pending = None
    s_next = scores_t(0)
    for c in range(len(chunks)):
        s = s_next
        if c + 1 < len(chunks):
            s_next = scores_t(c + 1)
        part = s[0:64]
        for r in range(64, s.shape[0], 64):
            part = jnp.maximum(part, s[r:r + 64])
        m_new = jnp.maximum(m, jnp.max(part, axis=0, keepdims=True))
        alpha = jnp.exp2(m - m_new)
        p_t = jnp.exp2(s - m_new).astype(BF16)
        m = m_new
        if pending is not None:
            p_prev, alpha_prev, c_prev = pending
            acc = acc * alpha_prev + pv_t(p_prev, c_prev)
        pending = (p_t, alpha, c)
    p_prev, alpha_prev, c_prev = pending
    acc = acc * alpha_prev + pv_t(p_prev, c_prev)

    o_t = acc[0:DA_VDIM, :] / acc[DA_VDIM:DA_VDIM + 1, :]
    d = (o_t[:, 0:tq] - lam * o_t[:, tq:]).T
    y = d * lax.rsqrt(jnp.mean(d * d, axis=-1, keepdims=True) + EPS) * g_ref[...]
    o_ref[0] = (y * (1.0 - lam_init)).astype(BF16)


def diff_attention(q_arr, q_col, zc, lat, lam_p, subln, lam_init, prescaled):
    bsz, nq, _ = q_arr.shape
    n_ctx = zc.shape[1]
    tq = min(nq, 512)
    kcb, vcb = ATC["K"] // LANES, ATC["V"] // LANES
    in_specs = [pl.BlockSpec((4, DA_SUB), lambda b, h, i: (0, 0)),
                pl.BlockSpec((1, DA_VDIM), lambda b, h, i: (0, 0)),
                pl.BlockSpec((1, tq, LANES), lambda b, h, i: (b, i, q_col + h)),
                pl.BlockSpec((1, n_ctx, LANES), lambda b, h, i: (b, 0, kcb + h)),
                pl.BlockSpec((1, n_ctx, LANES), lambda b, h, i: (b, 0, vcb + h))]
    args = [lam_p, subln.reshape(1, DA_VDIM), q_arr, zc, zc]
    n_lat = 0
    if lat is not None:
        k_arr, k_col, v_arr, v_col = lat
        n_lat = k_arr.shape[1]
        in_specs += [pl.BlockSpec((1, n_lat, LANES), lambda b, h, i: (b, 0, k_col + h)),
                     pl.BlockSpec((1, n_lat, LANES), lambda b, h, i: (b, 0, v_col + h))]
        args += [k_arr, v_arr]
    ones_rows = 16
    scratch = [pltpu.VMEM((DA_VDIM + ones_rows, n_ctx + n_lat), BF16)]
    return pl.pallas_call(
        functools.partial(_attn_kernel, lam_init=lam_init, n_lat=n_lat, ck=512, prescaled=prescaled),
        grid=(bsz, DA_HEADS, nq // tq),
        in_specs=in_specs,
        out_specs=pl.BlockSpec((1, tq, LANES), lambda b, h, i: (b, i, h)),
        out_shape=jax.ShapeDtypeStruct((bsz, nq, BR_WIDTH), BF16),
        scratch_shapes=scratch,
        compiler_params=_params(("arbitrary", "arbitrary", "arbitrary")),
        name="diff_attention",
    )(*args)


def hyena_positions(L):
    t = jnp.linspace(0.0, 1.0, L, dtype=F32)[:, None]
    wpos = ((2.0 * math.pi / L) * jnp.arange(L, dtype=F32))[:, None]
    bands = jnp.linspace(1e-4, HY_BANDS - 1, HY_BANDS, dtype=F32)[None, :]
    fwd = jnp.concatenate([t, jnp.cos(bands * wpos), -jnp.sin(bands * wpos)], axis=-1)
    emb = jnp.concatenate([fwd, fwd[0:1], jnp.flip(fwd[1:], axis=0)], axis=0)
    mask = (jnp.arange(2 * L) != L).astype(F32)[:, None]
    pad = jnp.zeros((2 * L, LANES - HY_EMB - 1), F32)
    return jnp.concatenate([emb, pad, mask], axis=-1)


def _filter_kernel(feat_ref, w1_ref, b1_ref, w2_ref, b2_ref, w3_ref, fr_ref, dl_ref, k_ref, ss_ref):
    half = feat_ref.shape[0] // 2
    feat = feat_ref[...]
    f2 = jnp.concatenate([feat[0:half], feat[half:]], axis=1)
    hid = jnp.sin(fr_ref[0:1, :] * (_dot_hi(f2, w1_ref[...]) + b1_ref[...]))
    hid = jnp.sin(fr_ref[1:2, :] * (_dot_hi(hid, w2_ref[...]) + b2_ref[...]))
    w3 = w3_ref[...].astype(BF16)
    zero = jnp.zeros_like(w3)
    hid = hid.astype(BF16)
    h = jnp.concatenate([_dot(hid, jnp.concatenate([w3, zero], axis=0)),
                         _dot(hid, jnp.concatenate([zero, w3], axis=0))], axis=0)
    t = feat[:, 0:1]
    mask = feat[:, LANES - 1:LANES]
    k = h * jnp.exp(-t * dl_ref[...]) * mask
    k_ref[...] = k.astype(k_ref.dtype)

    @pl.when(pl.program_id(0) == 0)
    def _():
        ss_ref[...] = jnp.zeros_like(ss_ref)

    ss_ref[...] += jnp.sum(k * k, axis=0, keepdims=True)


def hyena_filter_taps(feat, w1, b1, w2, b2, w3, freq):
    two_l = feat.shape[0]
    tr = min(two_l // 2, 1024)
    nb = two_l // tr
    hd = HY_HIDDEN
    w1d = jnp.zeros((2 * LANES, 2 * hd), F32).at[:HY_EMB, :hd].set(w1).at[LANES:LANES + HY_EMB, hd:].set(w1)
    w2d = jnp.zeros((2 * hd, 2 * hd), F32).at[:hd, :hd].set(w2).at[hd:, hd:].set(w2)
    twice = lambda v: jnp.tile(v.reshape(-1, hd), (1, 2))
    deltas = jnp.abs(jnp.linspace(HY_MIN_DECAY, HY_MAX_DECAY, HY_WIDTH, dtype=F32))[None, :]
    const = lambda i: (0, 0)
    return pl.pallas_call(
        _filter_kernel,
        grid=(nb,),
        in_specs=[pl.BlockSpec((tr, LANES), lambda i: (i, 0)),
                  pl.BlockSpec((2 * LANES, 2 * hd), const),
                  pl.BlockSpec((1, 2 * hd), const),
                  pl.BlockSpec((2 * hd, 2 * hd), const),
                  pl.BlockSpec((1, 2 * hd), const),
                  pl.BlockSpec((hd, HY_WIDTH), lambda i: (0, (2 * i) // nb)),
                  pl.BlockSpec((2, 2 * hd), const),
                  pl.BlockSpec((1, HY_WIDTH), const)],
        out_specs=[pl.BlockSpec((tr, HY_WIDTH), lambda i: (i, 0)),
                   pl.BlockSpec((1, HY_WIDTH), const)],
        out_shape=[jax.ShapeDtypeStruct((two_l, HY_WIDTH), BF16),
                   jax.ShapeDtypeStruct((1, HY_WIDTH), F32)],
        compiler_params=_params(("arbitrary",)),
        name="hyena_filter",
    )(feat, w1d, twice(b1), w2d, twice(b2), w3, twice(freq), deltas)


def _filter_dft_kernel(feat_ref, w1_ref, b1_ref, w2_ref, b2_ref, w3_ref, fr_ref, dl_ref, k_ref, ss_ref):
    r = DFT_R
    c = dl_ref.shape[-1]
    rows = feat_ref.shape[0]
    half = rows // 2
    feat = feat_ref[...]
    f2 = jnp.concatenate([feat[0:half], feat[half:]], axis=1)
    hid = jnp.sin(fr_ref[0:1, :] * (_dot_hi(f2, w1_ref[...]) + b1_ref[...]))
    hid = jnp.sin(fr_ref[1:2, :] * (_dot_hi(hid, w2_ref[...]) + b2_ref[...])).astype(BF16)
    w3 = w3_ref[...].astype(BF16)
    zero = jnp.zeros_like(w3)
    w_pad = (jnp.concatenate([w3, zero], axis=0), jnp.concatenate([zero, w3], axis=0))
    ss = jnp.zeros((1, c), F32)
    for j in range(rows // r):
        src = (j * r) % half
        hh = _dot(hid[src:src + r], w_pad[(j * r) // half])
        fj = feat[j * r:(j + 1) * r]
        k = jnp.concatenate([hh[0:r // 2, 0:c], hh[r // 2:, c:]], axis=0)
        k = k * jnp.exp(-fj[:, 0:1] * dl_ref[...]) * fj[:, LANES - 1:LANES]
        k_ref[:, j * c:(j + 1) * c] = k.astype(k_ref.dtype)
        ss = ss + jnp.sum(k * k, axis=0, keepdims=True)

    @pl.when(pl.program_id(0) == 0)
    def _():
        ss_ref[...] = jnp.zeros_like(ss_ref)

    ss_ref[...] += ss


def dft_order(feat):
    r = DFT_R
    return feat.reshape(r, r, LANES).transpose(1, 0, 2).reshape(r * r, LANES)


def hyena_filter_taps_dft(feat_p, w1, b1, w2, b2, w3, freq):
    r = DFT_R
    assert feat_p.shape[0] == r * r
    tb = 16
    hd = HY_HIDDEN
    w1d = jnp.zeros((2 * LANES, 2 * hd), F32).at[:HY_EMB, :hd].set(w1).at[LANES:LANES + HY_EMB, hd:].set(w1)
    w2d = jnp.zeros((2 * hd, 2 * hd), F32).at[:hd, :hd].set(w2).at[hd:, hd:].set(w2)
    twice = lambda v: jnp.tile(v.reshape(-1, hd), (1, 2))
    deltas = jnp.abs(jnp.linspace(HY_MIN_DECAY, HY_MAX_DECAY, HY_WIDTH, dtype=F32))[None, :]
    const = lambda i: (0, 0)
    return pl.pallas_call(
        _filter_dft_kernel,
        grid=(r // tb,),
        in_specs=[pl.BlockSpec((tb * r, LANES), lambda i: (i, 0)),
                  pl.BlockSpec((2 * LANES, 2 * hd), const),
                  pl.BlockSpec((1, 2 * hd), const),
                  pl.BlockSpec((2 * hd, 2 * hd), const),
                  pl.BlockSpec((1, 2 * hd), const),
                  pl.BlockSpec((hd, 2 * HY_WIDTH), const),
                  pl.BlockSpec((2, 2 * hd), const),
                  pl.BlockSpec((1, HY_WIDTH), const)],
        out_specs=[pl.BlockSpec((r, tb * HY_WIDTH), lambda i: (0, i)),
                   pl.BlockSpec((1, HY_WIDTH), const)],
        out_shape=[jax.ShapeDtypeStruct((r, r * HY_WIDTH), BF16),
                   jax.ShapeDtypeStruct((1, HY_WIDTH), F32)],
        compiler_params=_params(("arbitrary",)),
        name="hyena_filter_dft",
    )(feat_p, w1d, twice(b1), w2d, twice(b2), w3, twice(freq), deltas)


def dft_tables():
    r = DFT_R
    n_fft = r * r
    idx = jnp.arange(r, dtype=jnp.int32)
    prod = idx[:, None] * idx[None, :]
    ang = (2.0 * math.pi / r) * (prod % r).astype(F32)
    f_re, f_im = jnp.cos(ang), -jnp.sin(ang)
    ang = (2.0 * math.pi / n_fft) * prod.astype(F32)
    t_re, t_im = jnp.cos(ang), -jnp.sin(ang)

    def cmul(a_re, a_im, b_re, b_im):
        return a_re * b_re - a_im * b_im, a_re * b_im + a_im * b_re

    g_re, g_im = cmul(f_re[None, :, :], f_im[None, :, :], t_re[:, None, :], t_im[:, None, :])
    def cplx(re, im):
        return jnp.concatenate([jnp.concatenate([re, -im], axis=-1),
                                jnp.concatenate([im, re], axis=-1)], axis=-2).astype(BF16)

    half = r // 2
    return dict(f_pad=cplx(f_re[:, :half], f_im[:, :half]),
                f_real=jnp.concatenate([f_re, f_im], axis=0).astype(BF16),
                g=cplx(g_re, g_im),
                f_inv=cplx(f_re[:half], -f_im[:half]))


def _fft1_kernel(x_ref, f_ref, are_ref, aim_ref):
    r = are_ref.shape[0]
    x = x_ref[...]
    p = _dot(f_ref[...], x.reshape(-1, x.shape[-1]))
    are_ref[...] = p[0:r].astype(BF16)
    aim_ref[...] = p[r:].astype(BF16)


def fft_level1(x, f):
    r = DFT_R
    cols = x.shape[-1]
    tc = 8192
    if x.ndim == 3:
        x_spec = pl.BlockSpec((2, x.shape[1], tc), lambda j: (0, 0, j))
    else:
        x_spec = pl.BlockSpec((x.shape[0], tc), lambda j: (0, j))
    out = jax.ShapeDtypeStruct((r, cols), BF16)
    return pl.pallas_call(
        _fft1_kernel,
        grid=(cols // tc,),
        in_specs=[x_spec, pl.BlockSpec(f.shape, lambda j: (0, 0))],
        out_specs=[pl.BlockSpec((r, tc), lambda j: (0, j)), pl.BlockSpec((r, tc), lambda j: (0, j))],
        out_shape=[out, out],
        compiler_params=_params(("arbitrary",)),
        name="fft_level1",
    )(x, f)


def _fft2_conv_kernel(are_ref, aim_ref, kare_ref, kaim_ref, g_ref, sc_ref, bre_ref, bim_ref):
    r = are_ref.shape[1]
    c = are_ref.shape[-1]
    sc = sc_ref[...]
    for j in range(are_ref.shape[0]):
        rhs = jnp.concatenate([jnp.concatenate([are_ref[j], aim_ref[j]], axis=0),
                               jnp.concatenate([kare_ref[j], kaim_ref[j]], axis=0)], axis=1)
        xk = _dot(g_ref[j], rhs)
        x_re, x_im = xk[0:r, 0:c], xk[r:, 0:c]
        k_re, k_im = xk[0:r, c:] * sc, xk[r:, c:] * sc
        y = jnp.concatenate([(x_re * k_re - x_im * k_im).astype(BF16),
                             (x_re * k_im + x_im * k_re).astype(BF16)], axis=0)
        bm = lax.dot_general(g_ref[j], y, (((0,), (0,)), ((), ())), preferred_element_type=F32)
        bre_ref[j] = bm[0:r].astype(BF16)
        bim_ref[j] = bm[r:].astype(BF16)


def fft_level2_conv(a_re, a_im, ka_re, ka_im, g, scale):
    r = DFT_R
    c = a_re.shape[-1]
    tk = 8
    blk_a = pl.BlockSpec((tk, r, c), lambda i: (i, 0, 0))
    blk_g = pl.BlockSpec((tk, 2 * r, 2 * r), lambda i: (i, 0, 0))
    out = jax.ShapeDtypeStruct((r, r, c), BF16)
    return pl.pallas_call(
        _fft2_conv_kernel,
        grid=(r // tk,),
        in_specs=[blk_a, blk_a, blk_a, blk_a, blk_g, pl.BlockSpec((1, c), lambda i: (0, 0))],
        out_specs=[blk_a, blk_a],
        out_shape=[out, out],
        compiler_params=_params(("arbitrary",)),
        name="fft_level2_conv",
    )(a_re, a_im, ka_re, ka_im, g, scale)


def _ifft1_kernel(bre_ref, bim_ref, f_ref, uv_ref, bias_ref, o_ref):
    half = f_ref.shape[0] // 2
    y = _dot(f_ref[...], jnp.concatenate([bre_ref[...], bim_ref[...]], axis=0))
    bias = bias_ref[...]
    o_ref[0] = (y[0:half] + uv_ref[0].astype(F32) * bias).astype(BF16)
    o_ref[1] = (y[half:] + uv_ref[1].astype(F32) * bias).astype(BF16)


def ifft_level1(b_re, b_im, f_inv, uv, bias_t):
    r = DFT_R
    half = r // 2
    cols = b_re.shape[-1]
    tc = 8192
    blk_b = pl.BlockSpec((r, tc), lambda j: (0, j))
    blk_x = pl.BlockSpec((2, half, tc), lambda j: (0, 0, j))
    return pl.pallas_call(
        _ifft1_kernel,
        grid=(cols // tc,),
        in_specs=[blk_b, blk_b, pl.BlockSpec(f_inv.shape, lambda j: (0, 0)), blk_x,
                  pl.BlockSpec((1, tc), lambda j: (0, 0))],
        out_specs=blk_x,
        out_shape=jax.ShapeDtypeStruct((2, half, cols), BF16),
        compiler_params=_params(("arbitrary",)),
        name="ifft_level1",
    )(b_re, b_im, f_inv, uv, bias_t)


def hyena_long_conv(uv, taps, sumsq, bias, tables):
    r = DFT_R
    bsz, L, c = uv.shape
    assert bsz == 2 and 2 * L == r * r
    scale = lax.rsqrt(sumsq) * (1.0 / (r * r))
    ka_re, ka_im = fft_level1(taps, tables["f_real"])
    uv2 = uv.reshape(2, r // 2, r * c)
    a_re, a_im = fft_level1(uv2, tables["f_pad"])
    b_re, b_im = fft_level2_conv(a_re.reshape(r, r, c), a_im.reshape(r, r, c), ka_re.reshape(r, r, c),
                                 ka_im.reshape(r, r, c), tables["g"], scale)
    bias_t = jnp.tile(bias.reshape(1, c), (1, 8192 // c))
    y = ifft_level1(b_re.reshape(r, r * c), b_im.reshape(r, r * c), tables["f_inv"], uv2, bias_t)
    return y.reshape(2, L, c)


def dft_small_tables(n_fft):
    idx = jnp.arange(n_fft, dtype=jnp.int32)
    ang = (2.0 * math.pi / n_fft) * ((idx[:, None] * idx[None, :]) % n_fft).astype(F32)
    return jnp.cos(ang).astype(BF16), (-jnp.sin(ang)).astype(BF16)


def _conv_small_kernel(taps_ref, ss_ref, fre_ref, fim_ref, uv_ref, bias_ref, o_ref):
    two_l = taps_ref.shape[0]
    L = two_l // 2
    fre, fim = fre_ref[...], fim_ref[...]
    scale = lax.rsqrt(ss_ref[...]) * (1.0 / two_l)
    taps = taps_ref[...]
    k_re = _dot(fre, taps) * scale
    k_im = _dot(fim, taps) * scale
    u0, u1 = uv_ref[0], uv_ref[1]
    fre_l, fim_l = fre[:, :L], fim[:, :L]
    x_re = _dot(fre_l, u0) - _dot(fim_l, u1)
    x_im = _dot(fre_l, u1) + _dot(fim_l, u0)
    y_re = (x_re * k_re - x_im * k_im).astype(BF16)
    y_im = (x_re * k_im + x_im * k_re).astype(BF16)
    fre_t, fim_t = fre[:L, :], fim[:L, :]
    y0 = _dot(fre_t, y_re) + _dot(fim_t, y_im)
    y1 = _dot(fre_t, y_im) - _dot(fim_t, y_re)
    bias = bias_ref[...]
    o_ref[0] = (y0 + u0.astype(F32) * bias).astype(BF16)
    o_ref[1] = (y1 + u1.astype(F32) * bias).astype(BF16)


def hyena_long_conv_small(uv, taps, sumsq, bias, tables):
    bsz, L, c = uv.shape
    assert bsz == 2
    f_re, f_im = tables
    full = lambda shape: pl.BlockSpec(shape, lambda i: (0,) * len(shape))
    return pl.pallas_call(
        _conv_small_kernel,
        grid=(1,),
        in_specs=[full((2 * L, c)), full((1, c)), full((2 * L, 2 * L)), full((2 * L, 2 * L)),
                  full((2, L, c)), full((1, c))],
        out_specs=full((2, L, c)),
        out_shape=jax.ShapeDtypeStruct((2, L, c), BF16),
        compiler_params=_params(("arbitrary",)),
        name="hyena_conv_small",
    )(taps, sumsq, f_re, f_im, uv, bias.reshape(1, c))


def _merge_kernel(ya_ref, yb_ref, x0_ref, yc_ref, ga_ref, gb_ref, gc_ref, mg_ref, x_ref, mod_ref, np_ref,
                  wb_ref, wo_ref, o_ref):
    acc = None
    for i, (y_ref, g_ref) in enumerate(((ya_ref, ga_ref), (yb_ref, gb_ref), (yc_ref, gc_ref))):
        g = g_ref[0]
        y = y_ref[0] * x0_ref[0] if i == 1 else y_ref[0]
        gated = y * (g * jax.nn.sigmoid(g))
        sel = jax.nn.sigmoid(mg_ref[0, :, i * D_MODEL:(i + 1) * D_MODEL])
        term = sel * _dot(gated, wb_ref[i]).astype(BF16)
        acc = term if acc is None else acc + term
    out = _dot(acc, wo_ref[...])
    r = out * lax.rsqrt(jnp.mean(out * out, axis=-1, keepdims=True) + EPS) * np_ref[...]
    gt = mod_ref[0, :, 2 * D_MODEL:]
    o_ref[0] = x_ref[0] + gt * r


def merge_out(ya, yb, x0, yc, zat, zmg, x, mod_rows, row_of_batch, npost, wb, wo):
    bsz, n, _ = x.shape
    tm = min(n, 512)
    w = BR_WIDTH
    yspec = pl.BlockSpec((1, tm, w), lambda b, i: (b, i, 0))

    def zspec(name):
        cb = ATC[name] // w
        return pl.BlockSpec((1, tm, w), lambda b, i: (b, i, cb))

    return pl.pallas_call(
        _merge_kernel,
        grid=(bsz, n // tm),
        in_specs=[yspec, yspec, yspec, yspec, zspec("GA"), zspec("GB"), zspec("GC"),
                  pl.BlockSpec((1, tm, 3 * D_MODEL), lambda b, i: (b, i, 0)),
                  pl.BlockSpec((1, tm, D_MODEL), lambda b, i: (b, i, 0)),
                  pl.BlockSpec((1, 1, 3 * D_MODEL), lambda b, i: (row_of_batch(b), 0, 0)),
                  pl.BlockSpec((1, D_MODEL), lambda b, i: (0, 0)),
                  pl.BlockSpec((3, w, D_MODEL), lambda b, i: (0, 0, 0)),
                  pl.BlockSpec((D_MODEL, D_MODEL), lambda b, i: (0, 0))],
        out_specs=pl.BlockSpec((1, tm, D_MODEL), lambda b, i: (b, i, 0)),
        out_shape=jax.ShapeDtypeStruct((bsz, n, D_MODEL), F32),
        compiler_params=_params(("arbitrary", "arbitrary")),
        name="merge_out",
    )(ya, yb, x0, yc, zat, zat, zat, zmg, x, mod_rows, npost.reshape(1, D_MODEL), wb, wo)


def kernel(x, c, ctx, c_ctx, ada_w, ada_b, norm_pre, norm_post, w_in, da_lambda, da_subln, hy_short_w,
           hy_short_b, hy_f_w1, hy_f_b1, hy_f_w2, hy_f_b2, hy_f_w3, hy_f_freq, hy_bias, gm_ln_g, gm_ln_b,
           gm_ws, gm_bs, w_branch, w_out):
    bsz, n, _ = x.shape
    n_ctx = ctx.shape[1]
    assert bsz == 2 and 2 * n == DFT_R * DFT_R

    cond8 = jnp.zeros((8, D_MODEL), F32).at[0:bsz].set(c).at[bsz].set(c_ctx)
    mod = modulation_all(cond8, ada_w, ada_b)
    lat_row = lambda b: b
    ctx_row = lambda b: bsz

    cos_t, sin_t = rope_tables(n)
    feat = dft_order(hyena_positions(n))
    feat_c = hyena_positions(n_ctx)
    tables = dft_tables()
    tables_c = dft_small_tables(2 * n_ctx)
    vb, qb = ATC["V"] // LANES, ATC["Q"] // LANES

    xc = ctx
    for l in range(DEPTH):
        last = l == DEPTH - 1
        lam_init = 0.8 - 0.6 * math.exp(-0.3 * l)
        mod_rows = mod[l].reshape(8, 1, 3 * D_MODEL)
        w_l = w_in[l].astype(BF16)
        wb_l = w_branch[l].astype(BF16)
        wo_l = w_out[l].astype(BF16)
        filt_w = (hy_f_w1[l], hy_f_b1[l], hy_f_w2[l], hy_f_b2[l], hy_f_w3[l], hy_f_freq[l])

        gm_l = gmlp_params(gm_ln_g[l], gm_ln_b[l], gm_ws[l], gm_bs[l])
        sconv_l = (hy_short_w[l], hy_short_b[l].reshape(1, -1))
        z, qr, kr = in_projection(x, mod_rows, lat_row, norm_pre[l], w_l, gm_l, sconv_l, (cos_t, sin_t))
        zc = in_projection(xc, mod_rows, ctx_row, norm_pre[l], w_l, gm_l, sconv_l)

        taps, sumsq = hyena_filter_taps_dft(feat, *filt_w)
        y_a = diff_attention(qr, 0, zc["AT"], (kr, 0, z["AT"], vb), da_lambda[l], da_subln[l], lam_init, True)
        y_b = hyena_long_conv(z["UV"], taps, sumsq, hy_bias[l], tables)
        y_c = z["GM"]
        x_new = merge_out(y_a, y_b, z["X0"], y_c, z["AT"], z["MG"], x, mod_rows, lat_row, norm_post[l], wb_l,
                          wo_l)

        if not last:
            yc_a = diff_attention(zc["AT"], qb, zc["AT"], None, da_lambda[l], da_subln[l], lam_init, False)
            taps_c, sumsq_c = hyena_filter_taps(feat_c, *filt_w)
            yc_b = hyena_long_conv_small(zc["UV"], taps_c, sumsq_c, hy_bias[l], tables_c)
            yc_c = zc["GM"]
            xc = merge_out(yc_a, yc_b, zc["X0"], yc_c, zc["AT"], zc["MG"], xc, mod_rows, ctx_row, norm_post[l],
                           wb_l, wo_l)
        x = x_new
    return x
```

```python
import functools
import math

import jax
import jax.numpy as jnp
from jax import lax
from jax.experimental import pallas as pl
from jax.experimental.pallas import tpu as pltpu

F32 = jnp.float32
BF16 = jnp.bfloat16

D_MODEL = 1024
DEPTH = 4
GRID_W = 64
EPS = 1e-6
BR_WIDTH = 512
DA_SUB = 64
DA_VDIM = 128
DA_HEADS = 4
ROPE_BASE = 10000.0
ROPE_NF = 16
HY_WIDTH = 512
HY_BANDS = 16
HY_EMB = 33
HY_HIDDEN = 64
HY_MIN_DECAY = math.log(1e-2) / 1.5
HY_MAX_DECAY = math.log(1e-2) / 0.3
GM_GROUPS = 8
GM_CHUNK = 128

LANES = 128
DFT_R = 128
QK_SCALE = DA_SUB ** -0.5 * math.log2(math.e)

_REF_COLS = dict(K=(0, 512), V=(512, 512), Q=(1024, 512), GA=(1536, 512), HY=(2048, 1536),
                 GB=(3584, 512), GM=(4096, 1024), GC=(5120, 512), MG=(5632, 3072))
_GROUPS = (("MG", ("MG",)), ("HY", ("HY",)), ("GM", ("GM",)), ("AT", ("K", "V", "Q", "GA", "GB", "GC")))
GROUP_WIDTH = {g: sum(_REF_COLS[nm][1] for nm in names) for g, names in _GROUPS}
ATC = {}
_off = 0
for _name in _GROUPS[-1][1]:
    ATC[_name] = _off
    _off += _REF_COLS[_name][1]

VMEM_LIMIT = 48 * 1024 * 1024
INPROJ_VMEM_LIMIT = 56 * 1024 * 1024

ROW_TILE = 512
ATTN_Q_TILE = 512
ATTN_KEY_CHUNK = 512
DFT_COL_TILE = 8192
DFT_K1_TILE = 8
FILTER_ROW_TILE = 1024
FILTER_N2_TILE = 16
MOD_COL_TILE = 1024


def _params(sem):
    return pltpu.CompilerParams(dimension_semantics=sem, vmem_limit_bytes=VMEM_LIMIT)


def _dot(a, b):
    return jnp.dot(a, b, preferred_element_type=F32)


def _dot_hi(a, b):
    return jnp.dot(a, b, preferred_element_type=F32, precision=lax.Precision.HIGHEST)


def _mod_kernel(c_ref, w_ref, b_ref, o_ref):
    cond = c_ref[...]
    s = cond * jax.nn.sigmoid(cond)
    o_ref[0] = _dot_hi(s, w_ref[0]) + b_ref[0]


def modulation_all(cond8, ada_w, ada_b):
    tn = MOD_COL_TILE
    return pl.pallas_call(
        _mod_kernel,
        grid=(DEPTH, 3 * D_MODEL // tn),
        in_specs=[pl.BlockSpec((8, D_MODEL), lambda l, j: (0, 0)),
                  pl.BlockSpec((1, D_MODEL, tn), lambda l, j: (l, 0, j)),
                  pl.BlockSpec((1, 1, tn), lambda l, j: (l, 0, j))],
        out_specs=pl.BlockSpec((1, 8, tn), lambda l, j: (l, 0, j)),
        out_shape=jax.ShapeDtypeStruct((DEPTH, 8, 3 * D_MODEL), F32),
        compiler_params=_params(("arbitrary", "arbitrary")),
        name="modulation",
    )(cond8, ada_w, ada_b.reshape(DEPTH, 1, 3 * D_MODEL))


def _gmlp_math(zg, g_ref, b_ref, ws_ref, bs_ref):
    gl = 0.5 * zg * (1.0 + lax.erf(zg * (2.0 ** -0.5)))
    w = BR_WIDTH
    u = gl[:, :w]
    v = gl[:, w:]
    mu = jnp.mean(v, axis=-1, keepdims=True)
    var = jnp.mean(jnp.square(v - mu), axis=-1, keepdims=True)
    v = ((v - mu) * lax.rsqrt(var + EPS) * g_ref[...] + b_ref[...]).astype(BF16)
    gw = w // GM_GROUPS
    lane = lax.broadcasted_iota(jnp.int32, (GM_CHUNK, LANES), 1)
    first = lane < gw
    out = []
    for ci in range(zg.shape[0] // GM_CHUNK):
        rows = slice(ci * GM_CHUNK, (ci + 1) * GM_CHUNK)
        tiles = []
        for t in range(w // LANES):
            vt = v[rows, t * LANES:(t + 1) * LANES]
            tiles.append(jnp.where(first, _dot(ws_ref[2 * t], vt), _dot(ws_ref[2 * t + 1], vt)))
        vm = jnp.concatenate(tiles, axis=1) + bs_ref[...]
        out.append((u[rows] * vm).astype(BF16))
    return jnp.concatenate(out, axis=0)


def gmlp_params(ln_g, ln_b, ws, bs):
    w = BR_WIDTH
    bs_full = jnp.repeat(bs.T, w // GM_GROUPS, axis=1)
    return ln_g.reshape(1, w), ln_b.reshape(1, w), ws.astype(BF16), bs_full


_QK_WIDTH = DA_HEADS * LANES
MXU_TILE = 256
MAX_CHUNK = 2048
SUBLANES = 8
_INPROJ_CHUNKS = []
for _g, _names in _GROUPS:
    _dst = 0
    for _name in _names:
        _start, _width = _REF_COLS[_name]
        for _o in range(0, _width, MAX_CHUNK):
            _w = min(MAX_CHUNK, _width - _o)
            assert (_start + _o) % MXU_TILE == 0 and _w % MXU_TILE == 0
            _INPROJ_CHUNKS.append((_start + _o, _g, _dst + _o, _w))
        _dst += _width
W_IN_COLS = sum(w for _, w in _REF_COLS.values())
_PROJ_OUT = (("MG", GROUP_WIDTH["MG"]), ("AT", GROUP_WIDTH["AT"]), ("GM", BR_WIDTH), ("X0", HY_WIDTH),
             ("UV", HY_WIDTH))


def _inproj_kernel(*refs, rope):
    x_ref, xp_ref, xn_ref, mod_ref, g_ref, w_ref = refs[0:6]
    gm_refs = refs[6:10]
    sw_ref, sb_ref = refs[10:12]
    pos = 12
    if rope:
        cos_ref, sin_ref = refs[12:14]
        pos = 14
    o_of = {name: ref for (name, _), ref in zip(_PROJ_OUT, refs[pos:])}
    if rope:
        q_ref, k_ref = refs[pos + len(_PROJ_OUT):]
    tm = x_ref.shape[1]
    sh = mod_ref[0, :, 0:D_MODEL]
    sc = mod_ref[0, :, D_MODEL:2 * D_MODEL]

    def norm_mod(xv):
        y = xv * lax.rsqrt(jnp.mean(xv * xv, axis=-1, keepdims=True) + EPS) * g_ref[...]
        return (y * (1.0 + sc) + sh).astype(BF16)

    h = norm_mod(x_ref[0])
    h_halo = norm_mod(jnp.concatenate([xp_ref[0], xn_ref[0]], axis=0))

    if rope:
        lane = lax.broadcasted_iota(jnp.int32, cos_ref.shape, 1)
        low = (lane % (2 * ROPE_NF)) < ROPE_NF
        cs = cos_ref[...]
        sn = sin_ref[...]

        def rot(v):
            partner = jnp.where(low, pltpu.roll(v, LANES - ROPE_NF, axis=1), pltpu.roll(v, ROPE_NF, axis=1))
            return (v * cs + partner * sn).astype(BF16)

    for src, gname, dst, width in _INPROJ_CHUNKS:
        if gname == "HY":
            zc = _dot(jnp.concatenate([h, h_halo], axis=0), w_ref[:, src:src + width])
            i = pl.program_id(1)
            z = zc[0:tm]
            before = jnp.where(i == 0, 0.0, zc[tm + SUBLANES - 1:tm + SUBLANES])
            after = jnp.where(i == pl.num_programs(1) - 1, 0.0, zc[tm + SUBLANES:tm + SUBLANES + 1])
            row = lax.broadcasted_iota(jnp.int32, z.shape, 0)
            zm = jnp.where(row == 0, before, pltpu.roll(z, 1, axis=0))
            zp = jnp.where(row == tm - 1, after, pltpu.roll(z, tm - 1, axis=0))
            y = zm * sw_ref[0:1, :] + z * sw_ref[1:2, :] + zp * sw_ref[2:3, :] + sb_ref[...]
            o_of["X0"][0] = y[:, 0:HY_WIDTH].astype(BF16)
            o_of["UV"][0] = (y[:, HY_WIDTH:2 * HY_WIDTH] * y[:, 2 * HY_WIDTH:]).astype(BF16)
            continue
        zc = _dot(h, w_ref[:, src:src + width])
        if gname == "GM":
            o_of["GM"][0] = _gmlp_math(zc, *gm_refs)
            continue
        o_of[gname][0, :, dst:dst + width] = zc.astype(BF16)
        if rope and gname == "AT" and dst in (ATC["K"], ATC["Q"]):
            out, scale = (k_ref, 1.0) if dst == ATC["K"] else (q_ref, QK_SCALE)
            for hd in range(DA_HEADS):
                cols = slice(hd * LANES, (hd + 1) * LANES)
                out[0, :, cols] = rot(zc[:, cols] * scale)


def in_projection(x, mod_rows, row_of_batch, g, w, gm, sconv, rope_tabs=None):
    bsz, n, _ = x.shape
    tm = min(n, ROW_TILE)
    rope = rope_tabs is not None
    hb = tm // SUBLANES
    last_hb = n // SUBLANES - 1
    w_half = BR_WIDTH
    w3c = 3 * HY_WIDTH
    const2 = lambda b, i: (0, 0)
    in_specs = [pl.BlockSpec((1, tm, D_MODEL), lambda b, i: (b, i, 0)),
                pl.BlockSpec((1, SUBLANES, D_MODEL), lambda b, i: (b, jnp.maximum(i * hb - 1, 0), 0)),
                pl.BlockSpec((1, SUBLANES, D_MODEL), lambda b, i: (b, jnp.minimum((i + 1) * hb, last_hb), 0)),
                pl.BlockSpec((1, 1, 3 * D_MODEL), lambda b, i: (row_of_batch(b), 0, 0)),
                pl.BlockSpec((1, D_MODEL), const2),
                pl.BlockSpec((D_MODEL, W_IN_COLS), const2, pipeline_mode=pl.Buffered(1)),
                pl.BlockSpec((1, w_half), const2),
                pl.BlockSpec((1, w_half), const2),
                pl.BlockSpec((GM_GROUPS, GM_CHUNK, GM_CHUNK), lambda b, i: (0, 0, 0)),
                pl.BlockSpec((GM_CHUNK, w_half), const2),
                pl.BlockSpec((3, w3c), const2),
                pl.BlockSpec((1, w3c), const2)]
    args = [x, x, x, mod_rows, g.reshape(1, D_MODEL), w] + list(gm) + list(sconv)
    if rope:
        in_specs += [pl.BlockSpec((tm, LANES), lambda b, i: (i, 0))] * 2
        args += list(rope_tabs)
    out_specs = [pl.BlockSpec((1, tm, width), lambda b, i: (b, i, 0)) for _, width in _PROJ_OUT]
    out_shape = [jax.ShapeDtypeStruct((bsz, n, width), BF16) for _, width in _PROJ_OUT]
    if rope:
        out_specs += [pl.BlockSpec((1, tm, _QK_WIDTH), lambda b, i: (b, i, 0))] * 2
        out_shape += [jax.ShapeDtypeStruct((bsz, n, _QK_WIDTH), BF16)] * 2
    res = pl.pallas_call(
        functools.partial(_inproj_kernel, rope=rope),
        grid=(bsz, n // tm),
        in_specs=in_specs,
        out_specs=out_specs,
        out_shape=out_shape,
        compiler_params=pltpu.CompilerParams(dimension_semantics=("arbitrary", "arbitrary"),
                                             vmem_limit_bytes=INPROJ_VMEM_LIMIT),
        name="in_projection",
    )(*args)
    z = {name: r for (name, _), r in zip(_PROJ_OUT, res)}
    return (z, res[-2], res[-1]) if rope else z


def rope_tables(n):
    pos = jnp.arange(n)
    row = (pos // GRID_W).astype(F32)
    col = (pos % GRID_W).astype(F32)
    inv = ROPE_BASE ** (-jnp.arange(ROPE_NF, dtype=F32) / ROPE_NF)
    ar = row[:, None] * inv
    ac = col[:, None] * inv
    cos64 = jnp.concatenate([jnp.cos(ar), jnp.cos(ar), jnp.cos(ac), jnp.cos(ac)], axis=1)
    sin64 = jnp.concatenate([-jnp.sin(ar), jnp.sin(ar), -jnp.sin(ac), jnp.sin(ac)], axis=1)
    return jnp.tile(cos64, (1, 2)), jnp.tile(sin64, (1, 2))


def _attn_kernel(*refs, lam_init, n_lat, ck, prescaled):
    if n_lat:
        lam_ref, g_ref, q_ref, kc_ref, vc_ref, k_ref, v_ref, o_ref, vt_ref = refs
    else:
        lam_ref, g_ref, q_ref, kc_ref, vc_ref, o_ref, vt_ref = refs
    tq = q_ref.shape[1]
    n_ctx = kc_ref.shape[1]

    @pl.when(pl.program_id(2) == 0)
    def _():
        vt_ref[0:DA_VDIM, 0:n_ctx] = vc_ref[0].astype(F32).T.astype(BF16)
        for c in range(n_lat // ck):
            vt_ref[0:DA_VDIM, n_ctx + c * ck:n_ctx + (c + 1) * ck] = (
                v_ref[0, c * ck:(c + 1) * ck, :].astype(F32).T.astype(BF16))
        pad = vt_ref.shape[0] - DA_VDIM
        row = lax.broadcasted_iota(jnp.int32, (pad, vt_ref.shape[1]), 0)
        vt_ref[DA_VDIM:, :] = (row == 0).astype(BF16)

    lp = lam_ref[...]
    lam = (jnp.exp(jnp.sum(lp[0:1] * lp[1:2], axis=1, keepdims=True))
           - jnp.exp(jnp.sum(lp[2:3] * lp[3:4], axis=1, keepdims=True)) + lam_init)

    q = q_ref[0]
    if not prescaled:
        q = (q.astype(F32) * QK_SCALE).astype(BF16)
    lane = lax.broadcasted_iota(jnp.int32, q.shape, 1)
    zero = jnp.zeros_like(q)
    qq = jnp.concatenate([jnp.where(lane < DA_SUB, q, zero), jnp.where(lane >= DA_SUB, q, zero)], axis=0)

    chunks = [(lambda: kc_ref[0], 0, n_ctx)]
    for c in range(n_lat // ck):
        chunks.append((lambda c=c: k_ref[0, c * ck:(c + 1) * ck, :], n_ctx + c * ck, ck))

    def scores_t(c):
        return lax.dot_general(chunks[c][0](), qq, (((1,), (1,)), ((), ())), preferred_element_type=F32)

    def pv_t(p_t, c):
        _, off, width = chunks[c]
        return _dot(vt_ref[:, off:off + width], p_t)

    m = jnp.full((1, 2 * tq), -jnp.inf, F32)
    acc = jnp.zeros((vt_ref.shape[0], 2 * tq), F32)
    pending = None
    s_next = scores_t(0)
    for c in range(len(chunks)):
        s = s_next
        if c + 1 < len(chunks):
            s_next = scores_t(c + 1)
        part = s[0:64]
        for r in range(64, s.shape[0], 64):
            part = jnp.maximum(part, s[r:r + 64])
        m_new = jnp.maximum(m, jnp.max(part, axis=0, keepdims=True))
        alpha = jnp.exp2(m - m_new)
        p_t = jnp.exp2(s - m_new).astype(BF16)
        m = m_new
        if pending is not None:
            p_prev, alpha_prev, c_prev = pending
            acc = acc * alpha_prev + pv_t(p_prev, c_prev)
        pending = (p_t, alpha, c)
    p_prev, alpha_prev, c_prev = pending
    acc = acc * alpha_prev + pv_t(p_prev, c_prev)

    o_t = acc[0:DA_VDIM, :] / acc[DA_VDIM:DA_VDIM + 1, :]
    d = (o_t[:, 0:tq] - lam * o_t[:, tq:]).T
    y = d * lax.rsqrt(jnp.mean(d * d, axis=-1, keepdims=True) + EPS) * g_ref[...]
    o_ref[0] = (y * (1.0 - lam_init)).astype(BF16)


def diff_attention(q_arr, q_col, zc, lat, lam_p, subln, lam_init, prescaled):
    bsz, nq, _ = q_arr.shape
    n_ctx = zc.shape[1]
    tq = min(nq, ATTN_Q_TILE)
    kcb, vcb = ATC["K"] // LANES, ATC["V"] // LANES
    in_specs = [pl.BlockSpec((4, DA_SUB), lambda b, h, i: (0, 0)),
                pl.BlockSpec((1, DA_VDIM), lambda b, h, i: (0, 0)),
                pl.BlockSpec((1, tq, LANES), lambda b, h, i: (b, i, q_col + h)),
                pl.BlockSpec((1, n_ctx, LANES), lambda b, h, i: (b, 0, kcb + h)),
                pl.BlockSpec((1, n_ctx, LANES), lambda b, h, i: (b, 0, vcb + h))]
    args = [lam_p, subln.reshape(1, DA_VDIM), q_arr, zc, zc]
    n_lat = 0
    if lat is not None:
        k_arr, k_col, v_arr, v_col = lat
        n_lat = k_arr.shape[1]
        in_specs += [pl.BlockSpec((1, n_lat, LANES), lambda b, h, i: (b, 0, k_col + h)),
                     pl.BlockSpec((1, n_lat, LANES), lambda b, h, i: (b, 0, v_col + h))]
        args += [k_arr, v_arr]
    ones_rows = 16
    scratch = [pltpu.VMEM((DA_VDIM + ones_rows, n_ctx + n_lat), BF16)]
    return pl.pallas_call(
        functools.partial(_attn_kernel, lam_init=lam_init, n_lat=n_lat, ck=ATTN_KEY_CHUNK, prescaled=prescaled),
        grid=(bsz, DA_HEADS, nq // tq),
        in_specs=in_specs,
        out_specs=pl.BlockSpec((1, tq, LANES), lambda b, h, i: (b, i, h)),
        out_shape=jax.ShapeDtypeStruct((bsz, nq, BR_WIDTH), BF16),
        scratch_shapes=scratch,
        compiler_params=_params(("arbitrary", "arbitrary", "arbitrary")),
        name="diff_attention",
    )(*args)


def hyena_positions(L):
    t = jnp.linspace(0.0, 1.0, L, dtype=F32)[:, None]
    wpos = ((2.0 * math.pi / L) * jnp.arange(L, dtype=F32))[:, None]
    bands = jnp.linspace(1e-4, HY_BANDS - 1, HY_BANDS, dtype=F32)[None, :]
    fwd = jnp.concatenate([t, jnp.cos(bands * wpos), -jnp.sin(bands * wpos)], axis=-1)
    emb = jnp.concatenate([fwd, fwd[0:1], jnp.flip(fwd[1:], axis=0)], axis=0)
    mask = (jnp.arange(2 * L) != L).astype(F32)[:, None]
    pad = jnp.zeros((2 * L, LANES - HY_EMB - 1), F32)
    return jnp.concatenate([emb, pad, mask], axis=-1)


def _filter_kernel(feat_ref, w1_ref, b1_ref, w2_ref, b2_ref, w3_ref, fr_ref, dl_ref, k_ref, ss_ref):
    half = feat_ref.shape[0] // 2
    feat = feat_ref[...]
    f2 = jnp.concatenate([feat[0:half], feat[half:]], axis=1)
    hid = jnp.sin(fr_ref[0:1, :] * (_dot_hi(f2, w1_ref[...]) + b1_ref[...]))
    hid = jnp.sin(fr_ref[1:2, :] * (_dot_hi(hid, w2_ref[...]) + b2_ref[...]))
    w3 = w3_ref[...].astype(BF16)
    zero = jnp.zeros_like(w3)
    hid = hid.astype(BF16)
    h = jnp.concatenate([_dot(hid, jnp.concatenate([w3, zero], axis=0)),
                         _dot(hid, jnp.concatenate([zero, w3], axis=0))], axis=0)
    t = feat[:, 0:1]
    mask = feat[:, LANES - 1:LANES]
    k = h * jnp.exp(-t * dl_ref[...]) * mask
    k_ref[...] = k.astype(k_ref.dtype)

    @pl.when(pl.program_id(0) == 0)
    def _():
        ss_ref[...] = jnp.zeros_like(ss_ref)

    ss_ref[...] += jnp.sum(k * k, axis=0, keepdims=True)


def hyena_filter_taps(feat, w1, b1, w2, b2, w3, freq):
    two_l = feat.shape[0]
    tr = min(two_l // 2, FILTER_ROW_TILE)
    nb = two_l // tr
    hd = HY_HIDDEN
    w1d = jnp.zeros((2 * LANES, 2 * hd), F32).at[:HY_EMB, :hd].set(w1).at[LANES:LANES + HY_EMB, hd:].set(w1)
    w2d = jnp.zeros((2 * hd, 2 * hd), F32).at[:hd, :hd].set(w2).at[hd:, hd:].set(w2)
    twice = lambda v: jnp.tile(v.reshape(-1, hd), (1, 2))
    deltas = jnp.abs(jnp.linspace(HY_MIN_DECAY, HY_MAX_DECAY, HY_WIDTH, dtype=F32))[None, :]
    const = lambda i: (0, 0)
    return pl.pallas_call(
        _filter_kernel,
        grid=(nb,),
        in_specs=[pl.BlockSpec((tr, LANES), lambda i: (i, 0)),
                  pl.BlockSpec((2 * LANES, 2 * hd), const),
                  pl.BlockSpec((1, 2 * hd), const),
                  pl.BlockSpec((2 * hd, 2 * hd), const),
                  pl.BlockSpec((1, 2 * hd), const),
                  pl.BlockSpec((hd, HY_WIDTH), lambda i: (0, (2 * i) // nb)),
                  pl.BlockSpec((2, 2 * hd), const),
                  pl.BlockSpec((1, HY_WIDTH), const)],
        out_specs=[pl.BlockSpec((tr, HY_WIDTH), lambda i: (i, 0)),
                   pl.BlockSpec((1, HY_WIDTH), const)],
        out_shape=[jax.ShapeDtypeStruct((two_l, HY_WIDTH), BF16),
                   jax.ShapeDtypeStruct((1, HY_WIDTH), F32)],
        compiler_params=_params(("arbitrary",)),
        name="hyena_filter",
    )(feat, w1d, twice(b1), w2d, twice(b2), w3, twice(freq), deltas)


def _filter_dft_kernel(feat_ref, w1_ref, b1_ref, w2_ref, b2_ref, w3_ref, fr_ref, dl_ref, k_ref, ss_ref):
    r = DFT_R
    c = dl_ref.shape[-1]
    rows = feat_ref.shape[0]
    half = rows // 2
    feat = feat_ref[...]
    f2 = jnp.concatenate([feat[0:half], feat[half:]], axis=1)
    hid = jnp.sin(fr_ref[0:1, :] * (_dot_hi(f2, w1_ref[...]) + b1_ref[...]))
    hid = jnp.sin(fr_ref[1:2, :] * (_dot_hi(hid, w2_ref[...]) + b2_ref[...])).astype(BF16)
    w3 = w3_ref[...].astype(BF16)
    zero = jnp.zeros_like(w3)
    w_pad = (jnp.concatenate([w3, zero], axis=0), jnp.concatenate([zero, w3], axis=0))
    ss = jnp.zeros((1, c), F32)
    for j in range(rows // r):
        src = (j * r) % half
        hh = _dot(hid[src:src + r], w_pad[(j * r) // half])
        fj = feat[j * r:(j + 1) * r]
        k = jnp.concatenate([hh[0:r // 2, 0:c], hh[r // 2:, c:]], axis=0)
        k = k * jnp.exp(-fj[:, 0:1] * dl_ref[...]) * fj[:, LANES - 1:LANES]
        k_ref[:, j * c:(j + 1) * c] = k.astype(k_ref.dtype)
        ss = ss + jnp.sum(k * k, axis=0, keepdims=True)

    @pl.when(pl.program_id(0) == 0)
    def _():
        ss_ref[...] = jnp.zeros_like(ss_ref)

    ss_ref[...] += ss


def dft_order(feat):
    r = DFT_R
    return feat.reshape(r, r, LANES).transpose(1, 0, 2).reshape(r * r, LANES)


def hyena_filter_taps_dft(feat_p, w1, b1, w2, b2, w3, freq):
    r = DFT_R
    assert feat_p.shape[0] == r * r
    tb = FILTER_N2_TILE
    hd = HY_HIDDEN
    w1d = jnp.zeros((2 * LANES, 2 * hd), F32).at[:HY_EMB, :hd].set(w1).at[LANES:LANES + HY_EMB, hd:].set(w1)
    w2d = jnp.zeros((2 * hd, 2 * hd), F32).at[:hd, :hd].set(w2).at[hd:, hd:].set(w2)
    twice = lambda v: jnp.tile(v.reshape(-1, hd), (1, 2))
    deltas = jnp.abs(jnp.linspace(HY_MIN_DECAY, HY_MAX_DECAY, HY_WIDTH, dtype=F32))[None, :]
    const = lambda i: (0, 0)
    return pl.pallas_call(
        _filter_dft_kernel,
        grid=(r // tb,),
        in_specs=[pl.BlockSpec((tb * r, LANES), lambda i: (i, 0)),
                  pl.BlockSpec((2 * LANES, 2 * hd), const),
                  pl.BlockSpec((1, 2 * hd), const),
                  pl.BlockSpec((2 * hd, 2 * hd), const),
                  pl.BlockSpec((1, 2 * hd), const),
                  pl.BlockSpec((hd, 2 * HY_WIDTH), const),
                  pl.BlockSpec((2, 2 * hd), const),
                  pl.BlockSpec((1, HY_WIDTH), const)],
        out_specs=[pl.BlockSpec((r, tb * HY_WIDTH), lambda i: (0, i)),
                   pl.BlockSpec((1, HY_WIDTH), const)],
        out_shape=[jax.ShapeDtypeStruct((r, r * HY_WIDTH), BF16),
                   jax.ShapeDtypeStruct((1, HY_WIDTH), F32)],
        compiler_params=_params(("arbitrary",)),
        name="hyena_filter_dft",
    )(feat_p, w1d, twice(b1), w2d, twice(b2), w3, twice(freq), deltas)


def dft_tables():
    r = DFT_R
    n_fft = r * r
    idx = jnp.arange(r, dtype=jnp.int32)
    prod = idx[:, None] * idx[None, :]
    ang = (2.0 * math.pi / r) * (prod % r).astype(F32)
    f_re, f_im = jnp.cos(ang), -jnp.sin(ang)
    ang = (2.0 * math.pi / n_fft) * prod.astype(F32)
    t_re, t_im = jnp.cos(ang), -jnp.sin(ang)

    def cmul(a_re, a_im, b_re, b_im):
        return a_re * b_re - a_im * b_im, a_re * b_im + a_im * b_re

    g_re, g_im = cmul(f_re[None, :, :], f_im[None, :, :], t_re[:, None, :], t_im[:, None, :])
    def cplx(re, im):
        return jnp.concatenate([jnp.concatenate([re, -im], axis=-1),
                                jnp.concatenate([im, re], axis=-1)], axis=-2).astype(BF16)

    half = r // 2
    return dict(f_pad=cplx(f_re[:, :half], f_im[:, :half]),
                f_real=jnp.concatenate([f_re, f_im], axis=0).astype(BF16),
                g=cplx(g_re, g_im),
                f_inv=cplx(f_re[:half], -f_im[:half]))


def _fft1_kernel(x_ref, f_ref, are_ref, aim_ref):
    r = are_ref.shape[0]
    x = x_ref[...]
    p = _dot(f_ref[...], x.reshape(-1, x.shape[-1]))
    are_ref[...] = p[0:r].astype(BF16)
    aim_ref[...] = p[r:].astype(BF16)


def fft_level1(x, f):
    r = DFT_R
    cols = x.shape[-1]
    tc = DFT_COL_TILE
    if x.ndim == 3:
        x_spec = pl.BlockSpec((2, x.shape[1], tc), lambda j: (0, 0, j))
    else:
        x_spec = pl.BlockSpec((x.shape[0], tc), lambda j: (0, j))
    out = jax.ShapeDtypeStruct((r, cols), BF16)
    return pl.pallas_call(
        _fft1_kernel,
        grid=(cols // tc,),
        in_specs=[x_spec, pl.BlockSpec(f.shape, lambda j: (0, 0))],
        out_specs=[pl.BlockSpec((r, tc), lambda j: (0, j)), pl.BlockSpec((r, tc), lambda j: (0, j))],
        out_shape=[out, out],
        compiler_params=_params(("arbitrary",)),
        name="fft_level1",
    )(x, f)


def _fft2_conv_kernel(are_ref, aim_ref, kare_ref, kaim_ref, g_ref, sc_ref, bre_ref, bim_ref):
    r = are_ref.shape[1]
    c = are_ref.shape[-1]
    sc = sc_ref[...]
    for j in range(are_ref.shape[0]):
        rhs = jnp.concatenate([jnp.concatenate([are_ref[j], aim_ref[j]], axis=0),
                               jnp.concatenate([kare_ref[j], kaim_ref[j]], axis=0)], axis=1)
        xk = _dot(g_ref[j], rhs)
        x_re, x_im = xk[0:r, 0:c], xk[r:, 0:c]
        k_re, k_im = xk[0:r, c:] * sc, xk[r:, c:] * sc
        y = jnp.concatenate([(x_re * k_re - x_im * k_im).astype(BF16),
                             (x_re * k_im + x_im * k_re).astype(BF16)], axis=0)
        bm = lax.dot_general(g_ref[j], y, (((0,), (0,)), ((), ())), preferred_element_type=F32)
        bre_ref[j] = bm[0:r].astype(BF16)
        bim_ref[j] = bm[r:].astype(BF16)


def fft_level2_conv(a_re, a_im, ka_re, ka_im, g, scale):
    r = DFT_R
    c = a_re.shape[-1]
    tk = DFT_K1_TILE
    blk_a = pl.BlockSpec((tk, r, c), lambda i: (i, 0, 0))
    blk_g = pl.BlockSpec((tk, 2 * r, 2 * r), lambda i: (i, 0, 0))
    out = jax.ShapeDtypeStruct((r, r, c), BF16)
    return pl.pallas_call(
        _fft2_conv_kernel,
        grid=(r // tk,),
        in_specs=[blk_a, blk_a, blk_a, blk_a, blk_g, pl.BlockSpec((1, c), lambda i: (0, 0))],
        out_specs=[blk_a, blk_a],
        out_shape=[out, out],
        compiler_params=_params(("arbitrary",)),
        name="fft_level2_conv",
    )(a_re, a_im, ka_re, ka_im, g, scale)


def _ifft1_kernel(bre_ref, bim_ref, f_ref, uv_ref, bias_ref, o_ref):
    half = f_ref.shape[0] // 2
    c = o_ref.shape[-1]
    y = _dot(f_ref[...], jnp.concatenate([bre_ref[...], bim_ref[...]], axis=0))
    bias = bias_ref[...]
    y0 = y[0:half] + uv_ref[0].astype(F32) * bias
    y1 = y[half:] + uv_ref[1].astype(F32) * bias
    for t in range(o_ref.shape[2]):
        o_ref[0, :, t, :] = y0[:, t * c:(t + 1) * c]
        o_ref[1, :, t, :] = y1[:, t * c:(t + 1) * c]


def ifft_level1(b_re, b_im, f_inv, uv, bias_t):
    r = DFT_R
    half = r // 2
    cols = b_re.shape[-1]
    c = cols // r
    tc = DFT_COL_TILE
    blk_b = pl.BlockSpec((r, tc), lambda j: (0, j))
    blk_x = pl.BlockSpec((2, half, tc), lambda j: (0, 0, j))
    return pl.pallas_call(
        _ifft1_kernel,
        grid=(cols // tc,),
        in_specs=[blk_b, blk_b, pl.BlockSpec(f_inv.shape, lambda j: (0, 0)), blk_x,
                  pl.BlockSpec((1, tc), lambda j: (0, 0))],
        out_specs=pl.BlockSpec((2, half, tc // c, c), lambda j: (0, 0, j, 0)),
        out_shape=jax.ShapeDtypeStruct((2, half, r, c), F32),
        compiler_params=_params(("arbitrary",)),
        name="ifft_level1",
    )(b_re, b_im, f_inv, uv, bias_t)


def hyena_long_conv(uv, taps, sumsq, bias, tables):
    r = DFT_R
    bsz, L, c = uv.shape
    assert bsz == 2 and 2 * L == r * r
    scale = lax.rsqrt(sumsq) * (1.0 / (r * r))
    ka_re, ka_im = fft_level1(taps, tables["f_real"])
    uv2 = uv.reshape(2, r // 2, r * c)
    a_re, a_im = fft_level1(uv2, tables["f_pad"])
    b_re, b_im = fft_level2_conv(a_re.reshape(r, r, c), a_im.reshape(r, r, c), ka_re.reshape(r, r, c),
                                 ka_im.reshape(r, r, c), tables["g"], scale)
    bias_t = jnp.tile(bias.reshape(1, c), (1, DFT_COL_TILE // c))
    y = ifft_level1(b_re.reshape(r, r * c), b_im.reshape(r, r * c), tables["f_inv"], uv2, bias_t)
    return y.reshape(2, L, c)


def dft_small_tables(n_fft):
    idx = jnp.arange(n_fft, dtype=jnp.int32)
    ang = (2.0 * math.pi / n_fft) * ((idx[:, None] * idx[None, :]) % n_fft).astype(F32)
    return jnp.cos(ang).astype(BF16), (-jnp.sin(ang)).astype(BF16)


def _conv_small_kernel(taps_ref, ss_ref, fre_ref, fim_ref, uv_ref, bias_ref, o_ref):
    two_l = taps_ref.shape[0]
    L = two_l // 2
    fre, fim = fre_ref[...], fim_ref[...]
    scale = lax.rsqrt(ss_ref[...]) * (1.0 / two_l)
    taps = taps_ref[...]
    k_re = _dot(fre, taps) * scale
    k_im = _dot(fim, taps) * scale
    u0, u1 = uv_ref[0], uv_ref[1]
    fre_l, fim_l = fre[:, :L], fim[:, :L]
    x_re = _dot(fre_l, u0) - _dot(fim_l, u1)
    x_im = _dot(fre_l, u1) + _dot(fim_l, u0)
    y_re = (x_re * k_re - x_im * k_im).astype(BF16)
    y_im = (x_re * k_im + x_im * k_re).astype(BF16)
    fre_t, fim_t = fre[:L, :], fim[:L, :]
    y0 = _dot(fre_t, y_re) + _dot(fim_t, y_im)
    y1 = _dot(fre_t, y_im) - _dot(fim_t, y_re)
    bias = bias_ref[...]
    o_ref[0] = (y0 + u0.astype(F32) * bias).astype(BF16)
    o_ref[1] = (y1 + u1.astype(F32) * bias).astype(BF16)


def hyena_long_conv_small(uv, taps, sumsq, bias, tables):
    bsz, L, c = uv.shape
    assert bsz == 2
    f_re, f_im = tables
    full = lambda shape: pl.BlockSpec(shape, lambda i: (0,) * len(shape))
    return pl.pallas_call(
        _conv_small_kernel,
        grid=(1,),
        in_specs=[full((2 * L, c)), full((1, c)), full((2 * L, 2 * L)), full((2 * L, 2 * L)),
                  full((2, L, c)), full((1, c))],
        out_specs=full((2, L, c)),
        out_shape=jax.ShapeDtypeStruct((2, L, c), BF16),
        compiler_params=_params(("arbitrary",)),
        name="hyena_conv_small",
    )(taps, sumsq, f_re, f_im, uv, bias.reshape(1, c))


def _merge_kernel(ya_ref, yb_ref, x0_ref, yc_ref, ga_ref, gb_ref, gc_ref, mg_ref, x_ref, mod_ref, np_ref,
                  wb_ref, wo_ref, o_ref):
    acc = None
    for i, (y_ref, g_ref) in enumerate(((ya_ref, ga_ref), (yb_ref, gb_ref), (yc_ref, gc_ref))):
        g = g_ref[0]
        y = y_ref[0]
        if i == 1:
            y = (y * x0_ref[0]).astype(BF16)
        gated = y * (g * jax.nn.sigmoid(g))
        sel = jax.nn.sigmoid(mg_ref[0, :, i * D_MODEL:(i + 1) * D_MODEL])
        term = sel * _dot(gated, wb_ref[i]).astype(BF16)
        acc = term if acc is None else acc + term
    out = _dot(acc, wo_ref[...])
    r = out * lax.rsqrt(jnp.mean(out * out, axis=-1, keepdims=True) + EPS) * np_ref[...]
    gt = mod_ref[0, :, 2 * D_MODEL:]
    o_ref[0] = x_ref[0] + gt * r


def merge_out(ya, yb, x0, yc, zat, zmg, x, mod_rows, row_of_batch, npost, wb, wo):
    bsz, n, _ = x.shape
    tm = min(n, ROW_TILE)
    w = BR_WIDTH
    yspec = pl.BlockSpec((1, tm, w), lambda b, i: (b, i, 0))

    def zspec(name):
        cb = ATC[name] // w
        return pl.BlockSpec((1, tm, w), lambda b, i: (b, i, cb))

    return pl.pallas_call(
        _merge_kernel,
        grid=(bsz, n // tm),
        in_specs=[yspec, yspec, yspec, yspec, zspec("GA"), zspec("GB"), zspec("GC"),
                  pl.BlockSpec((1, tm, 3 * D_MODEL), lambda b, i: (b, i, 0)),
                  pl.BlockSpec((1, tm, D_MODEL), lambda b, i: (b, i, 0)),
                  pl.BlockSpec((1, 1, 3 * D_MODEL), lambda b, i: (row_of_batch(b), 0, 0)),
                  pl.BlockSpec((1, D_MODEL), lambda b, i: (0, 0)),
                  pl.BlockSpec((3, w, D_MODEL), lambda b, i: (0, 0, 0)),
                  pl.BlockSpec((D_MODEL, D_MODEL), lambda b, i: (0, 0))],
        out_specs=pl.BlockSpec((1, tm, D_MODEL), lambda b, i: (b, i, 0)),
        out_shape=jax.ShapeDtypeStruct((bsz, n, D_MODEL), F32),
        compiler_params=_params(("arbitrary", "arbitrary")),
        name="merge_out",
    )(ya, yb, x0, yc, zat, zat, zat, zmg, x, mod_rows, npost.reshape(1, D_MODEL), wb, wo)


def kernel(x, c, ctx, c_ctx, ada_w, ada_b, norm_pre, norm_post, w_in, da_lambda, da_subln, hy_short_w,
           hy_short_b, hy_f_w1, hy_f_b1, hy_f_w2, hy_f_b2, hy_f_w3, hy_f_freq, hy_bias, gm_ln_g, gm_ln_b,
           gm_ws, gm_bs, w_branch, w_out):
    bsz, n, _ = x.shape
    n_ctx = ctx.shape[1]
    assert bsz == 2 and 2 * n == DFT_R * DFT_R

    cond8 = jnp.zeros((8, D_MODEL), F32).at[0:bsz].set(c).at[bsz].set(c_ctx)
    mod = modulation_all(cond8, ada_w, ada_b)
    lat_row = lambda b: b
    ctx_row = lambda b: bsz

    cos_t, sin_t = rope_tables(n)
    feat = dft_order(hyena_positions(n))
    feat_c = hyena_positions(n_ctx)
    tables = dft_tables()
    tables_c = dft_small_tables(2 * n_ctx)
    vb, qb = ATC["V"] // LANES, ATC["Q"] // LANES

    xc = ctx
    for l in range(DEPTH):
        last = l == DEPTH - 1
        lam_init = 0.8 - 0.6 * math.exp(-0.3 * l)
        mod_rows = mod[l].reshape(8, 1, 3 * D_MODEL)
        w_l = w_in[l].astype(BF16)
        wb_l = w_branch[l].astype(BF16)
        wo_l = w_out[l].astype(BF16)
        filt_w = (hy_f_w1[l], hy_f_b1[l], hy_f_w2[l], hy_f_b2[l], hy_f_w3[l], hy_f_freq[l])

        gm_l = gmlp_params(gm_ln_g[l], gm_ln_b[l], gm_ws[l], gm_bs[l])
        sconv_l = (hy_short_w[l], hy_short_b[l].reshape(1, -1))
        z, qr, kr = in_projection(x, mod_rows, lat_row, norm_pre[l], w_l, gm_l, sconv_l, (cos_t, sin_t))
        zc = in_projection(xc, mod_rows, ctx_row, norm_pre[l], w_l, gm_l, sconv_l)

        taps, sumsq = hyena_filter_taps_dft(feat, *filt_w)
        y_a = diff_attention(qr, 0, zc["AT"], (kr, 0, z["AT"], vb), da_lambda[l], da_subln[l], lam_init, True)
        y_b = hyena_long_conv(z["UV"], taps, sumsq, hy_bias[l], tables)
        y_c = z["GM"]
        x_new = merge_out(y_a, y_b, z["X0"], y_c, z["AT"], z["MG"], x, mod_rows, lat_row, norm_post[l], wb_l,
                          wo_l)

        if not last:
            yc_a = diff_attention(zc["AT"], qb, zc["AT"], None, da_lambda[l], da_subln[l], lam_init, False)
            taps_c, sumsq_c = hyena_filter_taps(feat_c, *filt_w)
            yc_b = hyena_long_conv_small(zc["UV"], taps_c, sumsq_c, hy_bias[l], tables_c)
            yc_c = zc["GM"]
            xc = merge_out(yc_a, yc_b, zc["X0"], yc_c, zc["AT"], zc["MG"], xc, mod_rows, ctx_row, norm_post[l],
                           wb_l, wo_l)
        x = x_new
    return x
```

```python
import functools
import math

import jax
import jax.numpy as jnp
from jax import lax
from jax.experimental import pallas as pl
from jax.experimental.pallas import tpu as pltpu

F32 = jnp.float32
BF16 = jnp.bfloat16

D_MODEL = 1024
DEPTH = 4
GRID_W = 64
EPS = 1e-6
BR_WIDTH = 512
DA_SUB = 64
DA_VDIM = 128
DA_HEADS = 4
ROPE_BASE = 10000.0
ROPE_NF = 16
HY_WIDTH = 512
HY_BANDS = 16
HY_EMB = 33
HY_HIDDEN = 64
HY_MIN_DECAY = math.log(1e-2) / 1.5
HY_MAX_DECAY = math.log(1e-2) / 0.3
GM_GROUPS = 8
GM_CHUNK = 128

LANES = 128
DFT_R = 128
QK_SCALE = DA_SUB ** -0.5 * math.log2(math.e)

_REF_COLS = dict(K=(0, 512), V=(512, 512), Q=(1024, 512), GA=(1536, 512), HY=(2048, 1536),
                 GB=(3584, 512), GM=(4096, 1024), GC=(5120, 512), MG=(5632, 3072))
_GROUPS = (("MG", ("MG",)), ("HY", ("HY",)), ("GM", ("GM",)), ("AT", ("K", "V", "Q", "GA", "GB", "GC")))
GROUP_WIDTH = {g: sum(_REF_COLS[nm][1] for nm in names) for g, names in _GROUPS}
ATC = {}
_off = 0
for _name in _GROUPS[-1][1]:
    ATC[_name] = _off
    _off += _REF_COLS[_name][1]

VMEM_LIMIT = 48 * 1024 * 1024
INPROJ_VMEM_LIMIT = 56 * 1024 * 1024

ROW_TILE = 512
ATTN_Q_TILE = 512
ATTN_KEY_CHUNK = 512
DFT_COL_TILE = 8192
DFT_K1_TILE = 8
FILTER_ROW_TILE = 1024
FILTER_N2_TILE = 16
MOD_COL_TILE = 1024


def _params(sem):
    return pltpu.CompilerParams(dimension_semantics=sem, vmem_limit_bytes=VMEM_LIMIT)


def _dot(a, b):
    return jnp.dot(a, b, preferred_element_type=F32)


def _dot_hi(a, b):
    return jnp.dot(a, b, preferred_element_type=F32, precision=lax.Precision.HIGHEST)


def _mod_kernel(c_ref, w_ref, b_ref, o_ref):
    cond = c_ref[...]
    s = cond * jax.nn.sigmoid(cond)
    o_ref[0] = _dot_hi(s, w_ref[0]) + b_ref[0]


def modulation_all(cond8, ada_w, ada_b):
    tn = MOD_COL_TILE
    return pl.pallas_call(
        _mod_kernel,
        grid=(DEPTH, 3 * D_MODEL // tn),
        in_specs=[pl.BlockSpec((8, D_MODEL), lambda l, j: (0, 0)),
                  pl.BlockSpec((1, D_MODEL, tn), lambda l, j: (l, 0, j)),
                  pl.BlockSpec((1, 1, tn), lambda l, j: (l, 0, j))],
        out_specs=pl.BlockSpec((1, 8, tn), lambda l, j: (l, 0, j)),
        out_shape=jax.ShapeDtypeStruct((DEPTH, 8, 3 * D_MODEL), F32),
        compiler_params=_params(("arbitrary", "arbitrary")),
        name="modulation",
    )(cond8, ada_w, ada_b.reshape(DEPTH, 1, 3 * D_MODEL))


def _gmlp_math(zg, g_ref, b_ref, ws_ref, bs_ref):
    gl = 0.5 * zg * (1.0 + lax.erf(zg * (2.0 ** -0.5)))
    w = BR_WIDTH
    u = gl[:, :w]
    v = gl[:, w:]
    mu = jnp.mean(v, axis=-1, keepdims=True)
    var = jnp.mean(jnp.square(v - mu), axis=-1, keepdims=True)
    v = ((v - mu) * lax.rsqrt(var + EPS) * g_ref[...] + b_ref[...]).astype(BF16)
    gw = w // GM_GROUPS
    lane = lax.broadcasted_iota(jnp.int32, (GM_CHUNK, LANES), 1)
    first = lane < gw
    out = []
    for ci in range(zg.shape[0] // GM_CHUNK):
        rows = slice(ci * GM_CHUNK, (ci + 1) * GM_CHUNK)
        tiles = []
        for t in range(w // LANES):
            vt = v[rows, t * LANES:(t + 1) * LANES]
            tiles.append(jnp.where(first, _dot(ws_ref[2 * t], vt), _dot(ws_ref[2 * t + 1], vt)))
        vm = jnp.concatenate(tiles, axis=1) + bs_ref[...]
        out.append((u[rows] * vm).astype(BF16))
    return jnp.concatenate(out, axis=0)


def gmlp_params(ln_g, ln_b, ws, bs):
    w = BR_WIDTH
    bs_full = jnp.repeat(bs.T, w // GM_GROUPS, axis=1)
    return ln_g.reshape(1, w), ln_b.reshape(1, w), ws.astype(BF16), bs_full


_QK_WIDTH = DA_HEADS * LANES
MXU_TILE = 256
MAX_CHUNK = 2048
SUBLANES = 8
_INPROJ_CHUNKS = []
for _g, _names in _GROUPS:
    _dst = 0
    for _name in _names:
        _start, _width = _REF_COLS[_name]
        for _o in range(0, _width, MAX_CHUNK):
            _w = min(MAX_CHUNK, _width - _o)
            assert (_start + _o) % MXU_TILE == 0 and _w % MXU_TILE == 0
            _INPROJ_CHUNKS.append((_start + _o, _g, _dst + _o, _w))
        _dst += _width
W_IN_COLS = sum(w for _, w in _REF_COLS.values())
_PROJ_OUT = (("MG", GROUP_WIDTH["MG"], BF16), ("AT", GROUP_WIDTH["AT"], BF16), ("GM", BR_WIDTH, BF16),
             ("X0", HY_WIDTH, BF16), ("UV", HY_WIDTH, F32))


def _inproj_kernel(*refs, rope):
    x_ref, xp_ref, xn_ref, mod_ref, g_ref, w_ref = refs[0:6]
    gm_refs = refs[6:10]
    sw_ref, sb_ref = refs[10:12]
    pos = 12
    if rope:
        cos_ref, sin_ref = refs[12:14]
        pos = 14
    o_of = {name: ref for (name, _, _), ref in zip(_PROJ_OUT, refs[pos:])}
    if rope:
        q_ref, k_ref = refs[pos + len(_PROJ_OUT):]
    tm = x_ref.shape[1]
    sh = mod_ref[0, :, 0:D_MODEL]
    sc = mod_ref[0, :, D_MODEL:2 * D_MODEL]

    def norm_mod(xv):
        y = xv * lax.rsqrt(jnp.mean(xv * xv, axis=-1, keepdims=True) + EPS) * g_ref[...]
        return (y * (1.0 + sc) + sh).astype(BF16)

    h = norm_mod(x_ref[0])
    h_halo = norm_mod(jnp.concatenate([xp_ref[0], xn_ref[0]], axis=0))

    if rope:
        lane = lax.broadcasted_iota(jnp.int32, cos_ref.shape, 1)
        low = (lane % (2 * ROPE_NF)) < ROPE_NF
        cs = cos_ref[...]
        sn = sin_ref[...]

        def rot(v):
            partner = jnp.where(low, pltpu.roll(v, LANES - ROPE_NF, axis=1), pltpu.roll(v, ROPE_NF, axis=1))
            return (v * cs + partner * sn).astype(BF16)

    for src, gname, dst, width in _INPROJ_CHUNKS:
        if gname == "HY":
            zc = _dot(jnp.concatenate([h, h_halo], axis=0), w_ref[:, src:src + width])
            i = pl.program_id(1)
            z = zc[0:tm]
            before = jnp.where(i == 0, 0.0, zc[tm + SUBLANES - 1:tm + SUBLANES])
            after = jnp.where(i == pl.num_programs(1) - 1, 0.0, zc[tm + SUBLANES:tm + SUBLANES + 1])
            row = lax.broadcasted_iota(jnp.int32, z.shape, 0)
            zm = jnp.where(row == 0, before, pltpu.roll(z, 1, axis=0))
            zp = jnp.where(row == tm - 1, after, pltpu.roll(z, tm - 1, axis=0))
            y = zm * sw_ref[0:1, :] + z * sw_ref[1:2, :] + zp * sw_ref[2:3, :] + sb_ref[...]
            o_of["X0"][0] = y[:, 0:HY_WIDTH].astype(BF16)
            o_of["UV"][0] = y[:, HY_WIDTH:2 * HY_WIDTH] * y[:, 2 * HY_WIDTH:]
            continue
        zc = _dot(h, w_ref[:, src:src + width])
        if gname == "GM":
            o_of["GM"][0] = _gmlp_math(zc, *gm_refs)
            continue
        o_of[gname][0, :, dst:dst + width] = zc.astype(BF16)
        if rope and gname == "AT" and dst in (ATC["K"], ATC["Q"]):
            out, scale = (k_ref, 1.0) if dst == ATC["K"] else (q_ref, QK_SCALE)
            for hd in range(DA_HEADS):
                cols = slice(hd * LANES, (hd + 1) * LANES)
                out[0, :, cols] = rot(zc[:, cols] * scale)


def in_projection(x, mod_rows, row_of_batch, g, w, gm, sconv, rope_tabs=None):
    bsz, n, _ = x.shape
    tm = min(n, ROW_TILE)
    rope = rope_tabs is not None
    hb = tm // SUBLANES
    last_hb = n // SUBLANES - 1
    w_half = BR_WIDTH
    w3c = 3 * HY_WIDTH
    const2 = lambda b, i: (0, 0)
    in_specs = [pl.BlockSpec((1, tm, D_MODEL), lambda b, i: (b, i, 0)),
                pl.BlockSpec((1, SUBLANES, D_MODEL), lambda b, i: (b, jnp.maximum(i * hb - 1, 0), 0)),
                pl.BlockSpec((1, SUBLANES, D_MODEL), lambda b, i: (b, jnp.minimum((i + 1) * hb, last_hb), 0)),
                pl.BlockSpec((1, 1, 3 * D_MODEL), lambda b, i: (row_of_batch(b), 0, 0)),
                pl.BlockSpec((1, D_MODEL), const2),
                pl.BlockSpec((D_MODEL, W_IN_COLS), const2, pipeline_mode=pl.Buffered(1)),
                pl.BlockSpec((1, w_half), const2),
                pl.BlockSpec((1, w_half), const2),
                pl.BlockSpec((GM_GROUPS, GM_CHUNK, GM_CHUNK), lambda b, i: (0, 0, 0)),
                pl.BlockSpec((GM_CHUNK, w_half), const2),
                pl.BlockSpec((3, w3c), const2),
                pl.BlockSpec((1, w3c), const2)]
    args = [x, x, x, mod_rows, g.reshape(1, D_MODEL), w] + list(gm) + list(sconv)
    if rope:
        in_specs += [pl.BlockSpec((tm, LANES), lambda b, i: (i, 0))] * 2
        args += list(rope_tabs)
    out_specs = [pl.BlockSpec((1, tm, width), lambda b, i: (b, i, 0)) for _, width, _ in _PROJ_OUT]
    out_shape = [jax.ShapeDtypeStruct((bsz, n, width), dt) for _, width, dt in _PROJ_OUT]
    if rope:
        out_specs += [pl.BlockSpec((1, tm, _QK_WIDTH), lambda b, i: (b, i, 0))] * 2
        out_shape += [jax.ShapeDtypeStruct((bsz, n, _QK_WIDTH), BF16)] * 2
    res = pl.pallas_call(
        functools.partial(_inproj_kernel, rope=rope),
        grid=(bsz, n // tm),
        in_specs=in_specs,
        out_specs=out_specs,
        out_shape=out_shape,
        compiler_params=pltpu.CompilerParams(dimension_semantics=("arbitrary", "arbitrary"),
                                             vmem_limit_bytes=INPROJ_VMEM_LIMIT),
        name="in_projection",
    )(*args)
    z = {name: r for (name, _, _), r in zip(_PROJ_OUT, res)}
    return (z, res[-2], res[-1]) if rope else z


def rope_tables(n):
    pos = jnp.arange(n)
    row = (pos // GRID_W).astype(F32)
    col = (pos % GRID_W).astype(F32)
    inv = ROPE_BASE ** (-jnp.arange(ROPE_NF, dtype=F32) / ROPE_NF)
    ar = row[:, None] * inv
    ac = col[:, None] * inv
    cos64 = jnp.concatenate([jnp.cos(ar), jnp.cos(ar), jnp.cos(ac), jnp.cos(ac)], axis=1)
    sin64 = jnp.concatenate([-jnp.sin(ar), jnp.sin(ar), -jnp.sin(ac), jnp.sin(ac)], axis=1)
    return jnp.tile(cos64, (1, 2)), jnp.tile(sin64, (1, 2))


def _attn_kernel(*refs, lam_init, n_lat, ck, prescaled):
    if n_lat:
        lam_ref, g_ref, q_ref, kc_ref, vc_ref, k_ref, v_ref, o_ref, vt_ref = refs
    else:
        lam_ref, g_ref, q_ref, kc_ref, vc_ref, o_ref, vt_ref = refs
    tq = q_ref.shape[1]
    n_ctx = kc_ref.shape[1]

    @pl.when(pl.program_id(2) == 0)
    def _():
        vt_ref[0:DA_VDIM, 0:n_ctx] = vc_ref[0].astype(F32).T.astype(BF16)
        for c in range(n_lat // ck):
            vt_ref[0:DA_VDIM, n_ctx + c * ck:n_ctx + (c + 1) * ck] = (
                v_ref[0, c * ck:(c + 1) * ck, :].astype(F32).T.astype(BF16))
        pad = vt_ref.shape[0] - DA_VDIM
        row = lax.broadcasted_iota(jnp.int32, (pad, vt_ref.shape[1]), 0)
        vt_ref[DA_VDIM:, :] = (row == 0).astype(BF16)

    lp = lam_ref[...]
    lam = (jnp.exp(jnp.sum(lp[0:1] * lp[1:2], axis=1, keepdims=True))
           - jnp.exp(jnp.sum(lp[2:3] * lp[3:4], axis=1, keepdims=True)) + lam_init)

    q = q_ref[0]
    if not prescaled:
        q = (q.astype(F32) * QK_SCALE).astype(BF16)
    lane = lax.broadcasted_iota(jnp.int32, q.shape, 1)
    zero = jnp.zeros_like(q)
    qq = jnp.concatenate([jnp.where(lane < DA_SUB, q, zero), jnp.where(lane >= DA_SUB, q, zero)], axis=0)

    chunks = [(lambda: kc_ref[0], 0, n_ctx)]
    for c in range(n_lat // ck):
        chunks.append((lambda c=c: k_ref[0, c * ck:(c + 1) * ck, :], n_ctx + c * ck, ck))

    def scores_t(c):
        return lax.dot_general(chunks[c][0](), qq, (((1,), (1,)), ((), ())), preferred_element_type=F32)

    def pv_t(p_t, c):
        _, off, width = chunks[c]
        return _dot(vt_ref[:, off:off + width], p_t)

    m = jnp.full((1, 2 * tq), -jnp.inf, F32)
    acc = jnp.zeros((vt_ref.shape[0], 2 * tq), F32)
    pending = None
    s_next = scores_t(0)
    for c in range(len(chunks)):
        s = s_next
        if c + 1 < len(chunks):
            s_next = scores_t(c + 1)
        part = s[0:64]
        for r in range(64, s.shape[0], 64):
            part = jnp.maximum(part, s[r:r + 64])
        m_new = jnp.maximum(m, jnp.max(part, axis=0, keepdims=True))
        alpha = jnp.exp2(m - m_new)
        p_t = jnp.exp2(s - m_new).astype(BF16)
        m = m_new
        if pending is not None:
            p_prev, alpha_prev, c_prev = pending
            acc = acc * alpha_prev + pv_t(p_prev, c_prev)
        pending = (p_t, alpha, c)
    p_prev, alpha_prev, c_prev = pending
    acc = acc * alpha_prev + pv_t(p_prev, c_prev)

    o_t = acc[0:DA_VDIM, :] / acc[DA_VDIM:DA_VDIM + 1, :]
    d = (o_t[:, 0:tq] - lam * o_t[:, tq:]).T
    y = d * lax.rsqrt(jnp.mean(d * d, axis=-1, keepdims=True) + EPS) * g_ref[...]
    o_ref[0] = (y * (1.0 - lam_init)).astype(BF16)


def diff_attention(q_arr, q_col, zc, lat, lam_p, subln, lam_init, prescaled):
    bsz, nq, _ = q_arr.shape
    n_ctx = zc.shape[1]
    tq = min(nq, ATTN_Q_TILE)
    kcb, vcb = ATC["K"] // LANES, ATC["V"] // LANES
    in_specs = [pl.BlockSpec((4, DA_SUB), lambda b, h, i: (0, 0)),
                pl.BlockSpec((1, DA_VDIM), lambda b, h, i: (0, 0)),
                pl.BlockSpec((1, tq, LANES), lambda b, h, i: (b, i, q_col + h)),
                pl.BlockSpec((1, n_ctx, LANES), lambda b, h, i: (b, 0, kcb + h)),
                pl.BlockSpec((1, n_ctx, LANES), lambda b, h, i: (b, 0, vcb + h))]
    args = [lam_p, subln.reshape(1, DA_VDIM), q_arr, zc, zc]
    n_lat = 0
    if lat is not None:
        k_arr, k_col, v_arr, v_col = lat
        n_lat = k_arr.shape[1]
        in_specs += [pl.BlockSpec((1, n_lat, LANES), lambda b, h, i: (b, 0, k_col + h)),
                     pl.BlockSpec((1, n_lat, LANES), lambda b, h, i: (b, 0, v_col + h))]
        args += [k_arr, v_arr]
    ones_rows = 16
    scratch = [pltpu.VMEM((DA_VDIM + ones_rows, n_ctx + n_lat), BF16)]
    return pl.pallas_call(
        functools.partial(_attn_kernel, lam_init=lam_init, n_lat=n_lat, ck=ATTN_KEY_CHUNK, prescaled=prescaled),
        grid=(bsz, DA_HEADS, nq // tq),
        in_specs=in_specs,
        out_specs=pl.BlockSpec((1, tq, LANES), lambda b, h, i: (b, i, h)),
        out_shape=jax.ShapeDtypeStruct((bsz, nq, BR_WIDTH), BF16),
        scratch_shapes=scratch,
        compiler_params=_params(("arbitrary", "arbitrary", "arbitrary")),
        name="diff_attention",
    )(*args)


def hyena_positions(L):
    t = jnp.linspace(0.0, 1.0, L, dtype=F32)[:, None]
    wpos = ((2.0 * math.pi / L) * jnp.arange(L, dtype=F32))[:, None]
    bands = jnp.linspace(1e-4, HY_BANDS - 1, HY_BANDS, dtype=F32)[None, :]
    fwd = jnp.concatenate([t, jnp.cos(bands * wpos), -jnp.sin(bands * wpos)], axis=-1)
    emb = jnp.concatenate([fwd, fwd[0:1], jnp.flip(fwd[1:], axis=0)], axis=0)
    mask = (jnp.arange(2 * L) != L).astype(F32)[:, None]
    pad = jnp.zeros((2 * L, LANES - HY_EMB - 1), F32)
    return jnp.concatenate([emb, pad, mask], axis=-1)


def _filter_kernel(feat_ref, w1_ref, b1_ref, w2_ref, b2_ref, w3_ref, fr_ref, dl_ref, k_ref, ss_ref):
    half = feat_ref.shape[0] // 2
    feat = feat_ref[...]
    f2 = jnp.concatenate([feat[0:half], feat[half:]], axis=1)
    hid = jnp.sin(fr_ref[0:1, :] * (_dot_hi(f2, w1_ref[...]) + b1_ref[...]))
    hid = jnp.sin(fr_ref[1:2, :] * (_dot_hi(hid, w2_ref[...]) + b2_ref[...]))
    w3 = w3_ref[...].astype(BF16)
    zero = jnp.zeros_like(w3)
    hid = hid.astype(BF16)
    h = jnp.concatenate([_dot(hid, jnp.concatenate([w3, zero], axis=0)),
                         _dot(hid, jnp.concatenate([zero, w3], axis=0))], axis=0)
    t = feat[:, 0:1]
    mask = feat[:, LANES - 1:LANES]
    k = h * jnp.exp(-t * dl_ref[...]) * mask
    k_ref[...] = k.astype(k_ref.dtype)

    @pl.when(pl.program_id(0) == 0)
    def _():
        ss_ref[...] = jnp.zeros_like(ss_ref)

    ss_ref[...] += jnp.sum(k * k, axis=0, keepdims=True)


def hyena_filter_taps(feat, w1, b1, w2, b2, w3, freq):
    two_l = feat.shape[0]
    tr = min(two_l // 2, FILTER_ROW_TILE)
    nb = two_l // tr
    hd = HY_HIDDEN
    w1d = jnp.zeros((2 * LANES, 2 * hd), F32).at[:HY_EMB, :hd].set(w1).at[LANES:LANES + HY_EMB, hd:].set(w1)
    w2d = jnp.zeros((2 * hd, 2 * hd), F32).at[:hd, :hd].set(w2).at[hd:, hd:].set(w2)
    twice = lambda v: jnp.tile(v.reshape(-1, hd), (1, 2))
    deltas = jnp.abs(jnp.linspace(HY_MIN_DECAY, HY_MAX_DECAY, HY_WIDTH, dtype=F32))[None, :]
    const = lambda i: (0, 0)
    return pl.pallas_call(
        _filter_kernel,
        grid=(nb,),
        in_specs=[pl.BlockSpec((tr, LANES), lambda i: (i, 0)),
                  pl.BlockSpec((2 * LANES, 2 * hd), const),
                  pl.BlockSpec((1, 2 * hd), const),
                  pl.BlockSpec((2 * hd, 2 * hd), const),
                  pl.BlockSpec((1, 2 * hd), const),
                  pl.BlockSpec((hd, HY_WIDTH), lambda i: (0, (2 * i) // nb)),
                  pl.BlockSpec((2, 2 * hd), const),
                  pl.BlockSpec((1, HY_WIDTH), const)],
        out_specs=[pl.BlockSpec((tr, HY_WIDTH), lambda i: (i, 0)),
                   pl.BlockSpec((1, HY_WIDTH), const)],
        out_shape=[jax.ShapeDtypeStruct((two_l, HY_WIDTH), BF16),
                   jax.ShapeDtypeStruct((1, HY_WIDTH), F32)],
        compiler_params=_params(("arbitrary",)),
        name="hyena_filter",
    )(feat, w1d, twice(b1), w2d, twice(b2), w3, twice(freq), deltas)


def _filter_dft_kernel(feat_ref, w1_ref, b1_ref, w2_ref, b2_ref, w3_ref, fr_ref, dl_ref, k_ref, ss_ref):
    r = DFT_R
    c = dl_ref.shape[-1]
    rows = feat_ref.shape[0]
    half = rows // 2
    feat = feat_ref[...]
    f2 = jnp.concatenate([feat[0:half], feat[half:]], axis=1)
    hid = jnp.sin(fr_ref[0:1, :] * (_dot_hi(f2, w1_ref[...]) + b1_ref[...]))
    hid = jnp.sin(fr_ref[1:2, :] * (_dot_hi(hid, w2_ref[...]) + b2_ref[...])).astype(BF16)
    w3 = w3_ref[...].astype(BF16)
    zero = jnp.zeros_like(w3)
    w_pad = (jnp.concatenate([w3, zero], axis=0), jnp.concatenate([zero, w3], axis=0))
    ss = jnp.zeros((1, c), F32)
    for j in range(rows // r):
        src = (j * r) % half
        hh = _dot(hid[src:src + r], w_pad[(j * r) // half])
        fj = feat[j * r:(j + 1) * r]
        k = jnp.concatenate([hh[0:r // 2, 0:c], hh[r // 2:, c:]], axis=0)
        k = k * jnp.exp(-fj[:, 0:1] * dl_ref[...]) * fj[:, LANES - 1:LANES]
        k_ref[:, j * c:(j + 1) * c] = k.astype(k_ref.dtype)
        ss = ss + jnp.sum(k * k, axis=0, keepdims=True)

    @pl.when(pl.program_id(0) == 0)
    def _():
        ss_ref[...] = jnp.zeros_like(ss_ref)

    ss_ref[...] += ss


def dft_order(feat):
    r = DFT_R
    return feat.reshape(r, r, LANES).transpose(1, 0, 2).reshape(r * r, LANES)


def hyena_filter_taps_dft(feat_p, w1, b1, w2, b2, w3, freq):
    r = DFT_R
    assert feat_p.shape[0] == r * r
    tb = FILTER_N2_TILE
    hd = HY_HIDDEN
    w1d = jnp.zeros((2 * LANES, 2 * hd), F32).at[:HY_EMB, :hd].set(w1).at[LANES:LANES + HY_EMB, hd:].set(w1)
    w2d = jnp.zeros((2 * hd, 2 * hd), F32).at[:hd, :hd].set(w2).at[hd:, hd:].set(w2)
    twice = lambda v: jnp.tile(v.reshape(-1, hd), (1, 2))
    deltas = jnp.abs(jnp.linspace(HY_MIN_DECAY, HY_MAX_DECAY, HY_WIDTH, dtype=F32))[None, :]
    const = lambda i: (0, 0)
    return pl.pallas_call(
        _filter_dft_kernel,
        grid=(r // tb,),
        in_specs=[pl.BlockSpec((tb * r, LANES), lambda i: (i, 0)),
                  pl.BlockSpec((2 * LANES, 2 * hd), const),
                  pl.BlockSpec((1, 2 * hd), const),
                  pl.BlockSpec((2 * hd, 2 * hd), const),
                  pl.BlockSpec((1, 2 * hd), const),
                  pl.BlockSpec((hd, 2 * HY_WIDTH), const),
                  pl.BlockSpec((2, 2 * hd), const),
                  pl.BlockSpec((1, HY_WIDTH), const)],
        out_specs=[pl.BlockSpec((r, tb * HY_WIDTH), lambda i: (0, i)),
                   pl.BlockSpec((1, HY_WIDTH), const)],
        out_shape=[jax.ShapeDtypeStruct((r, r * HY_WIDTH), BF16),
                   jax.ShapeDtypeStruct((1, HY_WIDTH), F32)],
        compiler_params=_params(("arbitrary",)),
        name="hyena_filter_dft",
    )(feat_p, w1d, twice(b1), w2d, twice(b2), w3, twice(freq), deltas)


def dft_tables():
    r = DFT_R
    n_fft = r * r
    idx = jnp.arange(r, dtype=jnp.int32)
    prod = idx[:, None] * idx[None, :]
    ang = (2.0 * math.pi / r) * (prod % r).astype(F32)
    f_re, f_im = jnp.cos(ang), -jnp.sin(ang)
    ang = (2.0 * math.pi / n_fft) * prod.astype(F32)
    t_re, t_im = jnp.cos(ang), -jnp.sin(ang)

    def cmul(a_re, a_im, b_re, b_im):
        return a_re * b_re - a_im * b_im, a_re * b_im + a_im * b_re

    g_re, g_im = cmul(f_re[None, :, :], f_im[None, :, :], t_re[:, None, :], t_im[:, None, :])
    def cplx(re, im):
        return jnp.concatenate([jnp.concatenate([re, -im], axis=-1),
                                jnp.concatenate([im, re], axis=-1)], axis=-2).astype(BF16)

    half = r // 2
    return dict(f_pad=cplx(f_re[:, :half], f_im[:, :half]),
                f_real=jnp.concatenate([f_re, f_im], axis=0).astype(BF16),
                g=cplx(g_re, g_im),
                f_inv=cplx(f_re[:half], -f_im[:half]))


def _fft1_kernel(x_ref, f_ref, are_ref, aim_ref):
    r = are_ref.shape[0]
    if len(x_ref.shape) == 4:
        x = jnp.concatenate([jnp.concatenate([x_ref[b, :, t, :] for t in range(x_ref.shape[2])], axis=1)
                             for b in range(2)], axis=0).astype(BF16)
    else:
        x = x_ref[...]
    p = _dot(f_ref[...], x)
    are_ref[...] = p[0:r].astype(BF16)
    aim_ref[...] = p[r:].astype(BF16)


def fft_level1(x, f):
    r = DFT_R
    tc = DFT_COL_TILE
    if x.ndim == 4:
        c = x.shape[-1]
        cols = r * c
        x_spec = pl.BlockSpec((2, x.shape[1], tc // c, c), lambda j: (0, 0, j, 0))
    else:
        cols = x.shape[-1]
        x_spec = pl.BlockSpec((x.shape[0], tc), lambda j: (0, j))
    out = jax.ShapeDtypeStruct((r, cols), BF16)
    return pl.pallas_call(
        _fft1_kernel,
        grid=(cols // tc,),
        in_specs=[x_spec, pl.BlockSpec(f.shape, lambda j: (0, 0))],
        out_specs=[pl.BlockSpec((r, tc), lambda j: (0, j)), pl.BlockSpec((r, tc), lambda j: (0, j))],
        out_shape=[out, out],
        compiler_params=_params(("arbitrary",)),
        name="fft_level1",
    )(x, f)


def _fft2_conv_kernel(are_ref, aim_ref, kare_ref, kaim_ref, g_ref, sc_ref, bre_ref, bim_ref):
    r = are_ref.shape[1]
    c = are_ref.shape[-1]
    sc = sc_ref[...]
    for j in range(are_ref.shape[0]):
        rhs = jnp.concatenate([jnp.concatenate([are_ref[j], aim_ref[j]], axis=0),
                               jnp.concatenate([kare_ref[j], kaim_ref[j]], axis=0)], axis=1)
        xk = _dot(g_ref[j], rhs)
        x_re, x_im = xk[0:r, 0:c], xk[r:, 0:c]
        k_re, k_im = xk[0:r, c:] * sc, xk[r:, c:] * sc
        y = jnp.concatenate([(x_re * k_re - x_im * k_im).astype(BF16),
                             (x_re * k_im + x_im * k_re).astype(BF16)], axis=0)
        bm = lax.dot_general(g_ref[j], y, (((0,), (0,)), ((), ())), preferred_element_type=F32)
        bre_ref[j] = bm[0:r].astype(BF16)
        bim_ref[j] = bm[r:].astype(BF16)


def fft_level2_conv(a_re, a_im, ka_re, ka_im, g, scale):
    r = DFT_R
    c = a_re.shape[-1]
    tk = DFT_K1_TILE
    blk_a = pl.BlockSpec((tk, r, c), lambda i: (i, 0, 0))
    blk_g = pl.BlockSpec((tk, 2 * r, 2 * r), lambda i: (i, 0, 0))
    out = jax.ShapeDtypeStruct((r, r, c), BF16)
    return pl.pallas_call(
        _fft2_conv_kernel,
        grid=(r // tk,),
        in_specs=[blk_a, blk_a, blk_a, blk_a, blk_g, pl.BlockSpec((1, c), lambda i: (0, 0))],
        out_specs=[blk_a, blk_a],
        out_shape=[out, out],
        compiler_params=_params(("arbitrary",)),
        name="fft_level2_conv",
    )(a_re, a_im, ka_re, ka_im, g, scale)


def _ifft1_kernel(bre_ref, bim_ref, f_ref, uv_ref, bias_ref, o_ref):
    half = f_ref.shape[0] // 2
    c = o_ref.shape[-1]
    y = _dot(f_ref[...], jnp.concatenate([bre_ref[...], bim_ref[...]], axis=0))
    bias = bias_ref[...]
    for t in range(o_ref.shape[2]):
        o_ref[0, :, t, :] = y[0:half, t * c:(t + 1) * c] + uv_ref[0, :, t, :] * bias
        o_ref[1, :, t, :] = y[half:, t * c:(t + 1) * c] + uv_ref[1, :, t, :] * bias


def ifft_level1(b_re, b_im, f_inv, uv, bias):
    r = DFT_R
    half = r // 2
    cols = b_re.shape[-1]
    c = cols // r
    tc = DFT_COL_TILE
    blk_b = pl.BlockSpec((r, tc), lambda j: (0, j))
    blk_x = pl.BlockSpec((2, half, tc // c, c), lambda j: (0, 0, j, 0))
    return pl.pallas_call(
        _ifft1_kernel,
        grid=(cols // tc,),
        in_specs=[blk_b, blk_b, pl.BlockSpec(f_inv.shape, lambda j: (0, 0)), blk_x,
                  pl.BlockSpec((1, c), lambda j: (0, 0))],
        out_specs=blk_x,
        out_shape=jax.ShapeDtypeStruct((2, half, r, c), F32),
        compiler_params=_params(("arbitrary",)),
        name="ifft_level1",
    )(b_re, b_im, f_inv, uv, bias.reshape(1, c))


def hyena_long_conv(uv, taps, sumsq, bias, tables):
    r = DFT_R
    bsz, L, c = uv.shape
    assert bsz == 2 and 2 * L == r * r
    scale = lax.rsqrt(sumsq) * (1.0 / (r * r))
    ka_re, ka_im = fft_level1(taps, tables["f_real"])
    uv4 = uv.reshape(2, r // 2, r, c)
    a_re, a_im = fft_level1(uv4, tables["f_pad"])
    b_re, b_im = fft_level2_conv(a_re.reshape(r, r, c), a_im.reshape(r, r, c), ka_re.reshape(r, r, c),
                                 ka_im.reshape(r, r, c), tables["g"], scale)
    y = ifft_level1(b_re.reshape(r, r * c), b_im.reshape(r, r * c), tables["f_inv"], uv4, bias)
    return y.reshape(2, L, c)


def dft_small_tables(n_fft):
    idx = jnp.arange(n_fft, dtype=jnp.int32)
    ang = (2.0 * math.pi / n_fft) * ((idx[:, None] * idx[None, :]) % n_fft).astype(F32)
    return jnp.cos(ang).astype(BF16), (-jnp.sin(ang)).astype(BF16)


def _conv_small_kernel(taps_ref, ss_ref, fre_ref, fim_ref, uv_ref, bias_ref, o_ref):
    two_l = taps_ref.shape[0]
    L = two_l // 2
    fre, fim = fre_ref[...], fim_ref[...]
    scale = lax.rsqrt(ss_ref[...]) * (1.0 / two_l)
    taps = taps_ref[...]
    k_re = _dot(fre, taps) * scale
    k_im = _dot(fim, taps) * scale
    u0, u1 = uv_ref[0], uv_ref[1]
    ub0, ub1 = u0.astype(BF16), u1.astype(BF16)
    fre_l, fim_l = fre[:, :L], fim[:, :L]
    x_re = _dot(fre_l, ub0) - _dot(fim_l, ub1)
    x_im = _dot(fre_l, ub1) + _dot(fim_l, ub0)
    y_re = (x_re * k_re - x_im * k_im).astype(BF16)
    y_im = (x_re * k_im + x_im * k_re).astype(BF16)
    fre_t, fim_t = fre[:L, :], fim[:L, :]
    y0 = _dot(fre_t, y_re) + _dot(fim_t, y_im)
    y1 = _dot(fre_t, y_im) - _dot(fim_t, y_re)
    bias = bias_ref[...]
    o_ref[0] = (y0 + u0 * bias).astype(BF16)
    o_ref[1] = (y1 + u1 * bias).astype(BF16)


def hyena_long_conv_small(uv, taps, sumsq, bias, tables):
    bsz, L, c = uv.shape
    assert bsz == 2
    f_re, f_im = tables
    full = lambda shape: pl.BlockSpec(shape, lambda i: (0,) * len(shape))
    return pl.pallas_call(
        _conv_small_kernel,
        grid=(1,),
        in_specs=[full((2 * L, c)), full((1, c)), full((2 * L, 2 * L)), full((2 * L, 2 * L)),
                  full((2, L, c)), full((1, c))],
        out_specs=full((2, L, c)),
        out_shape=jax.ShapeDtypeStruct((2, L, c), BF16),
        compiler_params=_params(("arbitrary",)),
        name="hyena_conv_small",
    )(taps, sumsq, f_re, f_im, uv, bias.reshape(1, c))


def _merge_kernel(ya_ref, yb_ref, x0_ref, yc_ref, ga_ref, gb_ref, gc_ref, mg_ref, x_ref, mod_ref, np_ref,
                  wb_ref, wo_ref, o_ref):
    acc = None
    for i, (y_ref, g_ref) in enumerate(((ya_ref, ga_ref), (yb_ref, gb_ref), (yc_ref, gc_ref))):
        g = g_ref[0]
        y = y_ref[0]
        if i == 1:
            y = (y * x0_ref[0]).astype(BF16)
        gated = y * (g * jax.nn.sigmoid(g))
        sel = jax.nn.sigmoid(mg_ref[0, :, i * D_MODEL:(i + 1) * D_MODEL])
        term = sel * _dot(gated, wb_ref[i]).astype(BF16)
        acc = term if acc is None else acc + term
    out = _dot(acc, wo_ref[...])
    r = out * lax.rsqrt(jnp.mean(out * out, axis=-1, keepdims=True) + EPS) * np_ref[...]
    gt = mod_ref[0, :, 2 * D_MODEL:]
    o_ref[0] = x_ref[0] + gt * r


def merge_out(ya, yb, x0, yc, zat, zmg, x, mod_rows, row_of_batch, npost, wb, wo):
    bsz, n, _ = x.shape
    tm = min(n, ROW_TILE)
    w = BR_WIDTH
    yspec = pl.BlockSpec((1, tm, w), lambda b, i: (b, i, 0))

    def zspec(name):
        cb = ATC[name] // w
        return pl.BlockSpec((1, tm, w), lambda b, i: (b, i, cb))

    return pl.pallas_call(
        _merge_kernel,
        grid=(bsz, n // tm),
        in_specs=[yspec, yspec, yspec, yspec, zspec("GA"), zspec("GB"), zspec("GC"),
                  pl.BlockSpec((1, tm, 3 * D_MODEL), lambda b, i: (b, i, 0)),
                  pl.BlockSpec((1, tm, D_MODEL), lambda b, i: (b, i, 0)),
                  pl.BlockSpec((1, 1, 3 * D_MODEL), lambda b, i: (row_of_batch(b), 0, 0)),
                  pl.BlockSpec((1, D_MODEL), lambda b, i: (0, 0)),
                  pl.BlockSpec((3, w, D_MODEL), lambda b, i: (0, 0, 0)),
                  pl.BlockSpec((D_MODEL, D_MODEL), lambda b, i: (0, 0))],
        out_specs=pl.BlockSpec((1, tm, D_MODEL), lambda b, i: (b, i, 0)),
        out_shape=jax.ShapeDtypeStruct((bsz, n, D_MODEL), F32),
        compiler_params=_params(("arbitrary", "arbitrary")),
        name="merge_out",
    )(ya, yb, x0, yc, zat, zat, zat, zmg, x, mod_rows, npost.reshape(1, D_MODEL), wb, wo)


def kernel(x, c, ctx, c_ctx, ada_w, ada_b, norm_pre, norm_post, w_in, da_lambda, da_subln, hy_short_w,
           hy_short_b, hy_f_w1, hy_f_b1, hy_f_w2, hy_f_b2, hy_f_w3, hy_f_freq, hy_bias, gm_ln_g, gm_ln_b,
           gm_ws, gm_bs, w_branch, w_out):
    bsz, n, _ = x.shape
    n_ctx = ctx.shape[1]
    assert bsz == 2 and 2 * n == DFT_R * DFT_R

    cond8 = jnp.zeros((8, D_MODEL), F32).at[0:bsz].set(c).at[bsz].set(c_ctx)
    mod = modulation_all(cond8, ada_w, ada_b)
    lat_row = lambda b: b
    ctx_row = lambda b: bsz

    cos_t, sin_t = rope_tables(n)
    feat = dft_order(hyena_positions(n))
    feat_c = hyena_positions(n_ctx)
    tables = dft_tables()
    tables_c = dft_small_tables(2 * n_ctx)
    vb, qb = ATC["V"] // LANES, ATC["Q"] // LANES

    xc = ctx
    for l in range(DEPTH):
        last = l == DEPTH - 1
        lam_init = 0.8 - 0.6 * math.exp(-0.3 * l)
        mod_rows = mod[l].reshape(8, 1, 3 * D_MODEL)
        w_l = w_in[l].astype(BF16)
        wb_l = w_branch[l].astype(BF16)
        wo_l = w_out[l].astype(BF16)
        filt_w = (hy_f_w1[l], hy_f_b1[l], hy_f_w2[l], hy_f_b2[l], hy_f_w3[l], hy_f_freq[l])

        gm_l = gmlp_params(gm_ln_g[l], gm_ln_b[l], gm_ws[l], gm_bs[l])
        sconv_l = (hy_short_w[l], hy_short_b[l].reshape(1, -1))
        z, qr, kr = in_projection(x, mod_rows, lat_row, norm_pre[l], w_l, gm_l, sconv_l, (cos_t, sin_t))
        zc = in_projection(xc, mod_rows, ctx_row, norm_pre[l], w_l, gm_l, sconv_l)

        taps, sumsq = hyena_filter_taps_dft(feat, *filt_w)
        y_a = diff_attention(qr, 0, zc["AT"], (kr, 0, z["AT"], vb), da_lambda[l], da_subln[l], lam_init, True)
        y_b = hyena_long_conv(z["UV"], taps, sumsq, hy_bias[l], tables)
        y_c = z["GM"]
        x_new = merge_out(y_a, y_b, z["X0"], y_c, z["AT"], z["MG"], x, mod_rows, lat_row, norm_post[l], wb_l,
                          wo_l)

        if not last:
            yc_a = diff_attention(zc["AT"], qb, zc["AT"], None, da_lambda[l], da_subln[l], lam_init, False)
            taps_c, sumsq_c = hyena_filter_taps(feat_c, *filt_w)
            yc_b = hyena_long_conv_small(zc["UV"], taps_c, sumsq_c, hy_bias[l], tables_c)
            yc_c = zc["GM"]
            xc = merge_out(yc_a, yc_b, zc["X0"], yc_c, zc["AT"], zc["MG"], xc, mod_rows, ctx_row, norm_post[l],
                           wb_l, wo_l)
        x = x_new
    return x
```
